```python
import jax, jax.numpy as jnp
from jax import lax
import numpy as np

D_MODEL = 1024
BATCH = 8
SEQ = 8192
DEPTH = 2
DEC_BATCH = 16
DEC_SEQ = 64
PAST_LEN = 4096

CHUNK = 64
N_HEADS = 4
DV = D_MODEL // N_HEADS
DQK = DV // 2
QK_W = 2 * N_HEADS * DQK
V_W = N_HEADS * DV
CONV_W = 4
GMLP_CHUNK = 128
GMLP_GROUPS = 8
GMLP_CH = D_MODEL // GMLP_GROUPS
GMLP_W = GMLP_GROUPS * GMLP_CH
N_EXPERTS = 64
TOP_K = 8
N_GROUPS = 8
TOPK_GROUPS = 4
D_EXPERT = D_MODEL // 4
D_SHARED = D_MODEL // 4
ROUTE_SCALE = 2.5
EPS = 1e-6
PROJ_SPLITS = (QK_W, V_W, V_W, N_HEADS, N_HEADS, GMLP_W, GMLP_W, D_MODEL, D_MODEL)
PROJ_W = sum(PROJ_SPLITS)
PROJ_IDX = tuple(int(v) for v in np.cumsum(PROJ_SPLITS)[:-1])

kernel_name = 'hybrid_mlstm_gmlp_moe_stream_step'


def _rmsnorm(x, g):
    xf = x.astype(jnp.float32)
    y = xf * lax.rsqrt(jnp.mean(xf * xf, -1, keepdims=True) + EPS)
    return (y * g.astype(jnp.float32)).astype(x.dtype)


def _layernorm(x, g, b):
    xf = x.astype(jnp.float32)
    mu = jnp.mean(xf, -1, keepdims=True)
    xc = xf - mu
    y = xc * lax.rsqrt(jnp.mean(xc * xc, -1, keepdims=True) + EPS)
    return (y * g.astype(jnp.float32) + b.astype(jnp.float32)).astype(x.dtype)


def _causal_conv(x, state, w, b):
    T = x.shape[1]
    xp = jnp.concatenate([state.astype(x.dtype), x], axis=1)
    y = xp[:, 0:T] * w[0]
    for j in range(1, CONV_W):
        y = y + xp[:, j:j + T] * w[j]
    return y + b, xp[:, xp.shape[1] - (CONV_W - 1):]


def _mlstm_chunkwise(q, k, v, ig, lf, C0, n0, m0, L):
    B, T, H, _ = q.shape
    nc = T // L

    def to_chunks(a):
        a = a.astype(jnp.float32).reshape((B, nc, L) + a.shape[2:])
        return jnp.moveaxis(jnp.moveaxis(a, 1, 0), 3, 2)

    tril = jnp.tril(jnp.ones((L, L), bool))
    scale = DQK ** -0.5

    def step(carry, xs):
        C, n, m = carry
        qc, kc, vc, ic, fc = xs
        qc = qc * scale
        b = jnp.cumsum(fc, axis=-1)
        a = b + m[..., None]
        Dm = jnp.where(tril, b[..., :, None] - b[..., None, :] + ic[..., None, :], -jnp.inf)
        mt = jnp.maximum(a, jnp.max(Dm, -1))
        S = jnp.einsum('bhtd,bhsd->bhts', qc, kc) * jnp.exp(Dm - mt[..., None])
        aw = jnp.exp(a - mt)
        num = jnp.einsum('bhts,bhsv->bhtv', S, vc) + aw[..., None] * jnp.einsum('bhvd,bhtd->bhtv', C, qc)
        den = jnp.sum(S, -1) + aw * jnp.einsum('bhd,bhtd->bht', n, qc)
        h = num / jnp.maximum(jnp.abs(den), jnp.exp(-mt))[..., None]
        mL = mt[..., -1]
        decay = jnp.exp(b[..., -1] + m - mL)
        ws = jnp.exp(b[..., -1:] - b + ic - mL[..., None])
        C = decay[..., None, None] * C + jnp.einsum('bhs,bhsv,bhsd->bhvd', ws, vc, kc)
        n = decay[..., None] * n + jnp.einsum('bhs,bhsd->bhd', ws, kc)
        return (C, n, mL), h

    init = (C0.astype(jnp.float32), n0.astype(jnp.float32), m0.astype(jnp.float32))
    (C, n, m), h = lax.scan(step, init, tuple(to_chunks(a) for a in (q, k, v, ig, lf)))
    h = jnp.moveaxis(jnp.moveaxis(h, 2, 3), 0, 1).reshape(B, T, H, DV)
    return h, C, n, m


def _spatial_gate(vn, w_sp, b_sp, start, L):
    B, T, _ = vn.shape
    idx = jnp.arange(GMLP_CHUNK)
    mask = (idx[None, :] // CHUNK) <= (idx[:, None] // CHUNK)
    W = jnp.where(mask, w_sp, 0.0)[:, start:start + L, start:start + L]
    bias = b_sp[:, start:start + L]
    vr = vn.reshape(B, T // L, L, GMLP_GROUPS, GMLP_CH)
    out = jnp.einsum('gts,bnsgc->bntgc', W.astype(vn.dtype), vr) + bias.T[None, None, :, :, None]
    return out.reshape(B, T, GMLP_W)


def _token_block(n):
    for b in (1024, 512, 256, 128):
        if n % b == 0:
            return b
    return n


def _moe(h, w_r, b_r, w1, w3, w2, ws1, ws3, ws2):
    B, T, D = h.shape
    N = B * T
    blk = _token_block(N)
    xb = h.reshape(N // blk, blk, D)
    per_group = N_EXPERTS // N_GROUPS

    def one(xt):
        n = xt.shape[0]
        s = jax.nn.sigmoid((xt @ w_r).astype(jnp.float32))
        sel = s + b_r.astype(jnp.float32)
        gscore = lax.top_k(sel.reshape(n, N_GROUPS, per_group), 2)[0].sum(-1)
        _, gidx = lax.top_k(gscore, TOPK_GROUPS)
        gmask = jax.nn.one_hot(gidx, N_GROUPS, dtype=jnp.float32).sum(1) > 0
        sel = jnp.where(jnp.repeat(gmask, per_group, axis=1), sel, -jnp.inf)
        _, eidx = lax.top_k(sel, TOP_K)
        wk = jnp.take_along_axis(s, eidx, axis=1)
        wk = wk / jnp.sum(wk, -1, keepdims=True) * ROUTE_SCALE
        gates = jnp.sum(jax.nn.one_hot(eidx, N_EXPERTS, dtype=jnp.float32) * wk[..., None], axis=1)
        a = jnp.einsum('nd,edf->nef', xt, w1)
        g = jnp.einsum('nd,edf->nef', xt, w3)
        hm = jax.nn.silu(a) * g * gates[..., None].astype(xt.dtype)
        routed = jnp.einsum('nef,efd->nd', hm, w2)
        shared = (jax.nn.silu(xt @ ws1) * (xt @ ws3)) @ ws2
        return routed + shared

    return lax.map(one, xb).reshape(B, T, D)


def _mixer(h, w_in, b_igate, b_fgate, w_conv, b_conv, g_head, g_vnorm, b_vnorm, w_spatial, b_spatial, w_out,
           C0, n0, m0, conv0, mlstm_L, g_start, g_L):
    B, T, _ = h.shape
    P = h @ w_in
    qk_pre, v, o, ipre, fpre, u, vg, ga, gb = jnp.split(P, PROJ_IDX, axis=-1)
    qk, conv_new = _causal_conv(qk_pre, conv0, w_conv, b_conv)
    qk = jax.nn.silu(qk)
    q = qk[..., :N_HEADS * DQK].reshape(B, T, N_HEADS, DQK)
    k = qk[..., N_HEADS * DQK:].reshape(B, T, N_HEADS, DQK)
    v = v.reshape(B, T, N_HEADS, DV)
    ig = ipre.astype(jnp.float32) + b_igate.astype(jnp.float32)
    lf = jax.nn.log_sigmoid(fpre.astype(jnp.float32) + b_fgate.astype(jnp.float32))
    hm, C, n, m = _mlstm_chunkwise(q, k, v, ig, lf, C0, n0, m0, mlstm_L)
    hm = hm * lax.rsqrt(jnp.mean(hm * hm, -1, keepdims=True) + EPS)
    hm = (hm * g_head.reshape(N_HEADS, DV).astype(jnp.float32)).reshape(B, T, V_W).astype(h.dtype)
    h_a = jax.nn.sigmoid(o) * hm
    u = jax.nn.gelu(u)
    vn = _layernorm(jax.nn.gelu(vg), g_vnorm, b_vnorm)
    h_b = u * _spatial_gate(vn, w_spatial, b_spatial, g_start, g_L)
    y = (jax.nn.sigmoid(ga) * h_a + jax.nn.sigmoid(gb) * h_b) @ w_out
    return y, (C, n, m, conv_new, vn)


def _layer(x, c, prm, C0, n0, m0, conv0, mlstm_L, g_start, g_L):
    (w_ada, b_ada, g_pre_mix, g_post_mix, g_pre_ffn, g_post_ffn, w_in, b_igate, b_fgate, w_conv, b_conv,
     g_head, g_vnorm, b_vnorm, w_spatial, b_spatial, w_out, w_router, b_router, w_e1, w_e3, w_e2,
     w_s1, w_s3, w_s2) = prm
    mod = jax.nn.silu(c) @ w_ada + b_ada
    sh1, sc1, gt1, sh2, sc2, gt2 = jnp.split(mod[:, None, :], 6, axis=-1)
    h = _rmsnorm(x, g_pre_mix) * (1 + sc1) + sh1
    y, st = _mixer(h, w_in, b_igate, b_fgate, w_conv, b_conv, g_head, g_vnorm, b_vnorm, w_spatial, b_spatial,
                   w_out, C0, n0, m0, conv0, mlstm_L, g_start, g_L)
    x = x + gt1 * _rmsnorm(y, g_post_mix)
    h = _rmsnorm(x, g_pre_ffn) * (1 + sc2) + sh2
    x = x + gt2 * _rmsnorm(_moe(h, w_router, b_router, w_e1, w_e3, w_e2, w_s1, w_s3, w_s2), g_post_ffn)
    return x, st


def _trunk(x, c, C0, n0, m0, conv0, weights, mlstm_L, g_start, g_L, keep_v):
    outs = [[], [], [], []] + ([[]] if keep_v else [])
    for l in range(DEPTH):
        prm = tuple(w[l] for w in weights)
        x, (C, n, m, cv, vn) = _layer(x, c, prm, C0[l], n0[l], m0[l], conv0[l], mlstm_L, g_start, g_L)
        vals = [C, n, m, cv] + ([vn] if keep_v else [])
        for lst, val in zip(outs, vals):
            lst.append(val.astype(x.dtype))
    return x, tuple(jnp.stack(lst) for lst in outs)


def setup_inputs(seed: int = 0) -> dict:
    key = jax.random.key(seed)
    ks = jax.random.split(key, 48)
    cnt = [0]

    def nrm(shape, scale):
        kk = ks[cnt[0]]
        cnt[0] += 1
        return scale * jax.random.normal(kk, shape, jnp.float32)

    D = D_MODEL
    return {
        'x_prompt': nrm((BATCH, SEQ, D), 1.0),
        'x_sample': nrm((DEC_BATCH, DEC_SEQ, D), 1.0),
        'c_prompt': nrm((BATCH, D), 1.0),
        'c_sample': nrm((DEC_BATCH, D), 1.0),
        'state_mlstm_C': nrm((DEPTH, DEC_BATCH, N_HEADS, DV, DQK), 0.1),
        'state_mlstm_n': nrm((DEPTH, DEC_BATCH, N_HEADS, DQK), 0.1),
        'state_mlstm_m': nrm((DEPTH, DEC_BATCH, N_HEADS), 1.0),
        'state_conv': nrm((DEPTH, DEC_BATCH, CONV_W - 1, QK_W), 1.0),
        'w_ada': nrm((DEPTH, D, 6 * D), 0.5 * D ** -0.5),
        'b_ada': nrm((DEPTH, 6 * D), 0.02),
        'g_pre_mix': 1.0 + nrm((DEPTH, D), 0.05),
        'g_post_mix': 1.0 + nrm((DEPTH, D), 0.05),
        'g_pre_ffn': 1.0 + nrm((DEPTH, D), 0.05),
        'g_post_ffn': 1.0 + nrm((DEPTH, D), 0.05),
        'w_in': nrm((DEPTH, D, PROJ_W), D ** -0.5),
        'b_igate': nrm((DEPTH, N_HEADS), 0.1),
        'b_fgate': 3.0 + nrm((DEPTH, N_HEADS), 0.5),
        'w_conv': nrm((DEPTH, CONV_W, QK_W), CONV_W ** -0.5),
        'b_conv': nrm((DEPTH, QK_W), 0.02),
        'g_head': 1.0 + nrm((DEPTH, V_W), 0.05),
        'g_vnorm': 1.0 + nrm((DEPTH, GMLP_W), 0.05),
        'b_vnorm': nrm((DEPTH, GMLP_W), 0.02),
        'w_spatial': nrm((DEPTH, GMLP_GROUPS, GMLP_CHUNK, GMLP_CHUNK), 0.5 * GMLP_CHUNK ** -0.5),
        'b_spatial': 1.0 + nrm((DEPTH, GMLP_GROUPS, GMLP_CHUNK), 0.1),
        'w_out': nrm((DEPTH, D, D), D ** -0.5),
        'w_router': nrm((DEPTH, D, N_EXPERTS), D ** -0.5),
        'b_router': nrm((DEPTH, N_EXPERTS), 0.01),
        'w_e1': nrm((DEPTH, N_EXPERTS, D, D_EXPERT), D ** -0.5),
        'w_e3': nrm((DEPTH, N_EXPERTS, D, D_EXPERT), D ** -0.5),
        'w_e2': nrm((DEPTH, N_EXPERTS, D_EXPERT, D), D_EXPERT ** -0.5),
        'w_s1': nrm((DEPTH, D, D_SHARED), D ** -0.5),
        'w_s3': nrm((DEPTH, D, D_SHARED), D ** -0.5),
        'w_s2': nrm((DEPTH, D_SHARED, D), D_SHARED ** -0.5),
    }


def reference(x_prompt, x_sample, c_prompt, c_sample, state_mlstm_C, state_mlstm_n, state_mlstm_m, state_conv,
              w_ada, b_ada, g_pre_mix, g_post_mix, g_pre_ffn, g_post_ffn, w_in, b_igate, b_fgate, w_conv, b_conv,
              g_head, g_vnorm, b_vnorm, w_spatial, b_spatial, w_out, w_router, b_router, w_e1, w_e3, w_e2,
              w_s1, w_s3, w_s2):
    weights = (w_ada, b_ada, g_pre_mix, g_post_mix, g_pre_ffn, g_post_ffn, w_in, b_igate, b_fgate, w_conv, b_conv,
               g_head, g_vnorm, b_vnorm, w_spatial, b_spatial, w_out, w_router, b_router, w_e1, w_e3, w_e2,
               w_s1, w_s3, w_s2)
    B = x_prompt.shape[0]
    zC = jnp.zeros((DEPTH, B, N_HEADS, DV, DQK), jnp.float32)
    zn = jnp.zeros((DEPTH, B, N_HEADS, DQK), jnp.float32)
    zm = jnp.zeros((DEPTH, B, N_HEADS), jnp.float32)
    zconv = jnp.zeros((DEPTH, B, CONV_W - 1, QK_W), x_prompt.dtype)
    y_prompt, (p_C, p_n, p_m, p_conv) = _trunk(x_prompt, c_prompt, zC, zn, zm, zconv, weights,
                                               CHUNK, 0, GMLP_CHUNK, False)
    T = x_sample.shape[1]
    g_start = PAST_LEN % GMLP_CHUNK
    y_sample, (s_C, s_n, s_m, s_conv, s_v) = _trunk(x_sample, c_sample, state_mlstm_C, state_mlstm_n,
                                                     state_mlstm_m, state_conv, weights, T, g_start, T, True)
    return (y_prompt, y_sample, p_C, p_n, p_m, p_conv, s_C, s_n, s_m, s_conv, s_v)
```

```python
import functools

import jax
import jax.numpy as jnp
from jax import lax
from jax.experimental import pallas as pl
from jax.experimental.pallas import tpu as pltpu

F32 = jnp.float32
BF16 = jnp.bfloat16

EPS = 1e-6
N_HEADS = 4
CONV_W = 4
GMLP_GROUPS = 8
GMLP_CHUNK = 128
CHUNK = 64
N_GROUPS = 8
TOPK_GROUPS = 4
TOP_K = 8
ROUTE_SCALE = 2.5
PAST_LEN = 4096

LANES = 128
SUBLANES = 8
VMEM_LIMIT = 56 * 1024 * 1024

NT_DIMS = (((1,), (1,)), ((), ()))
TN_DIMS = (((0,), (0,)), ((), ()))


def _sigmoid(x):
    return 0.5 * (jnp.tanh(0.5 * x) + 1.0)


def _silu(x):
    return x * _sigmoid(x)


def _gelu_tanh(x):
    return x * (0.5 * (1.0 + jnp.tanh(0.7978845608028654 * (x + 0.044715 * (x * x * x)))))


def _rms(x, g):
    return x * lax.rsqrt(jnp.mean(x * x, -1, keepdims=True) + EPS) * g


def _params(sem):
    return pltpu.CompilerParams(dimension_semantics=sem, vmem_limit_bytes=VMEM_LIMIT)


def _ada_kernel(c_ref, w_ref, b_ref, o_ref):
    a = _silu(c_ref[...]).astype(BF16)
    o_ref[0] = jnp.dot(a, w_ref[0].astype(BF16), preferred_element_type=F32) + b_ref[0]


def _ada(c, w_ada, b_ada):
    depth, d, six_d = w_ada.shape
    nb = c.shape[0]
    tn = d
    return pl.pallas_call(
        _ada_kernel,
        grid=(depth, six_d // tn),
        in_specs=[
            pl.BlockSpec((nb, d), lambda l, j: (0, 0)),
            pl.BlockSpec((1, d, tn), lambda l, j: (l, 0, j)),
            pl.BlockSpec((1, 1, tn), lambda l, j: (l, 0, j)),
        ],
        out_specs=pl.BlockSpec((1, nb, tn), lambda l, j: (l, 0, j)),
        out_shape=jax.ShapeDtypeStruct((depth, nb, six_d), F32),
        compiler_params=_params(("parallel", "parallel")),
        name="adaln",
    )(c, w_ada, b_ada.reshape(depth, 1, six_d))


def _inproj_kernel(x_ref, sc_ref, sh_ref, g_ref, w_ref, wg_ref, p_ref, gates_ref, h_scr):
    @pl.when(pl.program_id(1) == 0)
    def _():
        h = _rms(x_ref[...], g_ref[...]) * (1.0 + sc_ref[0]) + sh_ref[0]
        hb = h.astype(BF16)
        h_scr[...] = hb
        gates_ref[...] = jnp.dot(hb, wg_ref[...], preferred_element_type=F32)

    p_ref[...] = jnp.dot(h_scr[...], w_ref[...], preferred_element_type=F32).astype(BF16)


def _mod_spec(mod, tm, tiles_per_seq, ngrid):
    d = mod.shape[-1]
    if mod.shape[1] == 1:
        if ngrid == 2:
            return pl.BlockSpec((1, 1, d), lambda i, j: (i // tiles_per_seq, 0, 0))
        return pl.BlockSpec((1, 1, d), lambda i: (i // tiles_per_seq, 0, 0))
    if ngrid == 2:
        return pl.BlockSpec((1, tm, d), lambda i, j: (0, i, 0))
    return pl.BlockSpec((1, tm, d), lambda i: (0, i, 0))


def _inproj(x, sc, sh, g, w_main, w_gate, tm, t_seq):
    n, d = x.shape
    ncol = w_main.shape[1] // d
    tps = max(t_seq // tm, 1)
    return pl.pallas_call(
        _inproj_kernel,
        grid=(n // tm, ncol),
        in_specs=[
            pl.BlockSpec((tm, d), lambda i, j: (i, 0)),
            _mod_spec(sc, tm, tps, 2),
            _mod_spec(sh, tm, tps, 2),
            pl.BlockSpec((1, d), lambda i, j: (0, 0)),
            pl.BlockSpec((d, d), lambda i, j: (0, j)),
            pl.BlockSpec((d, LANES), lambda i, j: (0, 0)),
        ],
        out_specs=[
            pl.BlockSpec((tm, d), lambda i, j: (i, j)),
            pl.BlockSpec((tm, LANES), lambda i, j: (i, 0)),
        ],
        out_shape=[
            jax.ShapeDtypeStruct((n, ncol * d), BF16),
            jax.ShapeDtypeStruct((n, LANES), F32),
        ],
        scratch_shapes=[pltpu.VMEM((tm, d), BF16)],
        compiler_params=_params(("parallel", "arbitrary")),
        name="inproj",
    )(x, sc, sh, g, w_main, w_gate)


def _mlstm_kernel(qk_ref, v_ref, gt_ref, c0_ref, n0_ref, m0_ref, conv0_ref, wconv_ref, bconv_ref,
                  gbias_ref, ghead_ref, hm_ref, c_ref, n_ref, m_ref, conv_ref, xbuf, *, L, H, dqk, dv):
    qkw = 2 * H * dqk
    pad = SUBLANES
    tail0 = pad - (CONV_W - 1)

    @pl.when(pl.program_id(1) == 0)
    def _():
        c_ref[...] = c0_ref[...]
        n_ref[...] = n0_ref[...]
        m_ref[...] = m0_ref[...]
        xbuf[0:pad, :] = jnp.zeros((pad, qkw), F32)
        xbuf[tail0:pad, :] = conv0_ref[0]

    xbuf[pad:pad + L, :] = qk_ref[...].astype(F32)
    w = wconv_ref[...]
    y = xbuf[tail0:tail0 + L, :] * w[0:1, :]
    for j in range(1, CONV_W):
        y = y + xbuf[tail0 + j:tail0 + j + L, :] * w[j:j + 1, :]
    y = y + bconv_ref[...]
    tail = xbuf[L + tail0:L + pad, :]
    conv_ref[0] = tail
    xbuf[tail0:pad, :] = tail
    qk = _silu(y)

    z = gt_ref[...] + gbias_ref[...]
    lf = jnp.minimum(z, 0.0) - jnp.log1p(jnp.exp(-jnp.abs(z)))
    row = lax.broadcasted_iota(jnp.int32, (L, L), 0)
    col = lax.broadcasted_iota(jnp.int32, (L, L), 1)
    tri = row >= col
    hi = lax.Precision.HIGHEST
    b_col = jnp.dot(tri.astype(F32), lf, precision=hi, preferred_element_type=F32)
    eye = (lax.broadcasted_iota(jnp.int32, (LANES, LANES), 0)
           == lax.broadcasted_iota(jnp.int32, (LANES, LANES), 1)).astype(F32)
    b_row = lax.dot_general(eye, b_col, NT_DIMS, precision=hi, preferred_element_type=F32)
    z_row = lax.dot_general(eye, z, NT_DIMS, precision=hi, preferred_element_type=F32)

    scale = dqk ** -0.5
    lane1 = lax.broadcasted_iota(jnp.int32, (1, LANES), 1)
    m_old = m_ref[0]
    m_new = m_old
    for h in range(H):
        q = qk[:, h * dqk:(h + 1) * dqk] * scale
        k = qk[:, (H + h) * dqk:(H + h + 1) * dqk]
        qb = q.astype(BF16)
        vb = v_ref[:, h * dv:(h + 1) * dv]
        b_c = b_col[:, H + h:H + h + 1]
        i_c = z[:, h:h + 1]
        b_r = b_row[H + h:H + h + 1, :]
        i_r = z_row[h:h + 1, :]
        m_prev = m_old[:, h:h + 1]
        a = b_c + m_prev
        dm = jnp.where(tri, b_c - b_r + i_r, -jnp.inf)
        mt = jnp.maximum(a, jnp.max(dm, axis=-1, keepdims=True))
        s = lax.dot_general(qb, k.astype(BF16), NT_DIMS, preferred_element_type=F32)
        sg = s * jnp.exp(dm - mt)
        aw = jnp.exp(a - mt)
        cm = c_ref[0, h]
        qc = lax.dot_general(qb, cm.astype(BF16), NT_DIMS, preferred_element_type=F32)
        num = jnp.dot(sg.astype(BF16), vb, preferred_element_type=F32) + aw * qc
        nrow = n_ref[0, h:h + 1, :]
        den = jnp.sum(sg, -1, keepdims=True) + aw * jnp.sum(q * nrow, -1, keepdims=True)
        hh = num / jnp.maximum(jnp.abs(den), jnp.exp(-mt))
        m_last = mt[L - 1:L, :]
        b_last = b_c[L - 1:L, :]
        decay = jnp.exp(b_last + m_prev - m_last)
        ws = jnp.exp(b_last - b_c + i_c - m_last)
        kw = ws * k
        c_ref[0, h] = decay * cm + lax.dot_general(vb, kw.astype(BF16), TN_DIMS, preferred_element_type=F32)
        n_ref[0, h:h + 1, :] = decay * nrow + jnp.sum(kw, axis=0, keepdims=True)
        m_new = jnp.where(lane1 == h, m_last, m_new)
        hm_ref[:, h * dv:(h + 1) * dv] = _rms(hh, ghead_ref[:, h * dv:(h + 1) * dv]).astype(BF16)
    m_ref[0] = m_new


def _mlstm(p, gates, c0, n0, m0, conv0, w_conv, b_conv, gbias, g_head, nb, L):
    n = p.shape[0]
    t_seq = n // nb
    nc = t_seq // L
    _, H, dv, dqk = c0.shape
    qkw = 2 * H * dqk
    vw = H * dv
    kern = functools.partial(_mlstm_kernel, L=L, H=H, dqk=dqk, dv=dv)
    return pl.pallas_call(
        kern,
        grid=(nb, nc),
        in_specs=[
            pl.BlockSpec((L, qkw), lambda b, c: (b * nc + c, 0)),
            pl.BlockSpec((L, vw), lambda b, c: (b * nc + c, 1)),
            pl.BlockSpec((L, LANES), lambda b, c: (b * nc + c, 0)),
            pl.BlockSpec((1, H, dv, dqk), lambda b, c: (b, 0, 0, 0)),
            pl.BlockSpec((1, H, dqk), lambda b, c: (b, 0, 0)),
            pl.BlockSpec((1, 1, LANES), lambda b, c: (b, 0, 0)),
            pl.BlockSpec((1, CONV_W - 1, qkw), lambda b, c: (b, 0, 0)),
            pl.BlockSpec((CONV_W, qkw), lambda b, c: (0, 0)),
            pl.BlockSpec((1, qkw), lambda b, c: (0, 0)),
            pl.BlockSpec((1, LANES), lambda b, c: (0, 0)),
            pl.BlockSpec((1, vw), lambda b, c: (0, 0)),
        ],
        out_specs=[
            pl.BlockSpec((L, vw), lambda b, c: (b * nc + c, 0)),
            pl.BlockSpec((1, H, dv, dqk), lambda b, c: (b, 0, 0, 0)),
            pl.BlockSpec((1, H, dqk), lambda b, c: (b, 0, 0)),
            pl.BlockSpec((1, 1, LANES), lambda b, c: (b, 0, 0)),
            pl.BlockSpec((1, CONV_W - 1, qkw), lambda b, c: (b, 0, 0)),
        ],
        out_shape=[
            jax.ShapeDtypeStruct((n, vw), BF16),
            jax.ShapeDtypeStruct((nb, H, dv, dqk), F32),
            jax.ShapeDtypeStruct((nb, H, dqk), F32),
            jax.ShapeDtypeStruct((nb, 1, LANES), F32),
            jax.ShapeDtypeStruct((nb, CONV_W - 1, qkw), F32),
        ],
        scratch_shapes=[pltpu.VMEM((L + SUBLANES, qkw), F32)],
        compiler_params=_params(("parallel", "arbitrary")),
        name="mlstm",
    )(p, p, gates, c0, n0, m0, conv0, w_conv, b_conv, gbias, g_head)


def _merge_kernel(o_ref, u_ref, vg_ref, ga_ref, gb_ref, hm_ref, x_ref, wsp_ref, bsp_ref, gvn_ref, bvn_ref,
                  wout_ref, gpost_ref, gt1_ref, gpre_ref, sc2_ref, sh2_ref, *rest, tm, gl, keep_v):
    if keep_v:
        x1_ref, h2_ref, vn_ref, sg_scr = rest
    else:
        x1_ref, h2_ref, sg_scr = rest
    groups = wsp_ref.shape[0]
    ch = vg_ref.shape[1] // groups
    vg = _gelu_tanh(vg_ref[...].astype(F32))
    mu = jnp.mean(vg, -1, keepdims=True)
    xc = vg - mu
    vn = xc * lax.rsqrt(jnp.mean(xc * xc, -1, keepdims=True) + EPS) * gvn_ref[...] + bvn_ref[...]
    if keep_v:
        vn_ref[...] = vn
    vnb = vn.astype(BF16)
    for ci in range(tm // gl):
        for g in range(groups):
            blk = jnp.dot(wsp_ref[g], vnb[ci * gl:(ci + 1) * gl, g * ch:(g + 1) * ch],
                          preferred_element_type=F32)
            sg_scr[ci * gl:(ci + 1) * gl, g * ch:(g + 1) * ch] = blk + bsp_ref[:, g:g + 1]
    h_b = _gelu_tanh(u_ref[...].astype(F32)) * sg_scr[...]
    h_a = _sigmoid(o_ref[...].astype(F32)) * hm_ref[...].astype(F32)
    merged = _sigmoid(ga_ref[...].astype(F32)) * h_a + _sigmoid(gb_ref[...].astype(F32)) * h_b
    y = jnp.dot(merged.astype(BF16), wout_ref[...], preferred_element_type=F32)
    x1 = x_ref[...] + gt1_ref[0] * _rms(y, gpost_ref[...])
    x1_ref[...] = x1
    h2_ref[...] = (_rms(x1, gpre_ref[...]) * (1.0 + sc2_ref[0]) + sh2_ref[0]).astype(BF16)


def _merge(p, hm, x, wsp, bsp_t, g_vn, b_vn, w_out, g_post, gt1, g_pre, sc2, sh2, tm, t_seq, keep_v):
    n, d = x.shape
    groups, gl, _ = wsp.shape
    tps = max(t_seq // tm, 1)
    row = lambda i: (0, 0)
    pcol = lambda c: pl.BlockSpec((tm, d), lambda i: (i, c))
    tok = pl.BlockSpec((tm, d), lambda i: (i, 0))
    vec = pl.BlockSpec((1, d), row)
    out_specs = [tok, tok]
    out_shape = [jax.ShapeDtypeStruct((n, d), F32), jax.ShapeDtypeStruct((n, d), BF16)]
    if keep_v:
        out_specs.append(tok)
        out_shape.append(jax.ShapeDtypeStruct((n, d), F32))
    kern = functools.partial(_merge_kernel, tm=tm, gl=gl, keep_v=keep_v)
    return pl.pallas_call(
        kern,
        grid=(n // tm,),
        in_specs=[
            pcol(2), pcol(3), pcol(4), pcol(5), pcol(6), tok, tok,
            pl.BlockSpec((groups, gl, gl), lambda i: (0, 0, 0)),
            pl.BlockSpec((gl, groups), row),
            vec, vec,
            pl.BlockSpec((d, d), row),
            vec,
            _mod_spec(gt1, tm, tps, 1),
            vec,
            _mod_spec(sc2, tm, tps, 1),
            _mod_spec(sh2, tm, tps, 1),
        ],
        out_specs=out_specs,
        out_shape=out_shape,
        scratch_shapes=[pltpu.VMEM((tm, d), F32)],
        compiler_params=_params(("parallel",)),
        name="merge",
    )(p, p, p, p, p, hm, x, wsp, bsp_t, g_vn, b_vn, w_out, g_post, gt1, g_pre, sc2, sh2)


def _route(hb, wrt_ref, br_ref, gt_scr):
    tm = hb.shape[0]
    n_exp = wrt_ref.shape[0]
    per = n_exp // N_GROUPS
    neg = -jnp.inf
    logits = lax.dot_general(wrt_ref[...], hb, NT_DIMS, preferred_element_type=F32)
    s = _sigmoid(logits)
    sel = s + br_ref[...]
    mem = [sel[r * N_GROUPS:(r + 1) * N_GROUPS, :] for r in range(per)]
    grow = lax.broadcasted_iota(jnp.int32, (N_GROUPS, tm), 0)
    m1 = functools.reduce(jnp.maximum, mem)
    idx1 = functools.reduce(jnp.minimum, [jnp.where(mem[r] == m1, r, per) for r in range(per)])
    m2 = functools.reduce(jnp.maximum, [jnp.where(idx1 == r, neg, mem[r]) for r in range(per)])
    gs = m1 + m2
    gmask = jnp.zeros((N_GROUPS, tm), jnp.bool_)
    for _ in range(TOPK_GROUPS):
        mx = jnp.max(gs, axis=0, keepdims=True)
        gi = jnp.min(jnp.where(gs == mx, grow, N_GROUPS), axis=0, keepdims=True)
        pick = grow == gi
        gmask = jnp.logical_or(gmask, pick)
        gs = jnp.where(pick, neg, gs)
    msk = [jnp.where(gmask, mem[r], neg) for r in range(per)]
    eidx = [grow * per + r for r in range(per)]
    chosen = [jnp.zeros((N_GROUPS, tm), jnp.bool_) for _ in range(per)]
    for _ in range(TOP_K):
        mx = jnp.max(functools.reduce(jnp.maximum, msk), axis=0, keepdims=True)
        cand = functools.reduce(jnp.minimum, [jnp.where(msk[r] == mx, eidx[r], n_exp) for r in range(per)])
        first = jnp.min(cand, axis=0, keepdims=True)
        for r in range(per):
            pick = eidx[r] == first
            chosen[r] = jnp.logical_or(chosen[r], pick)
            msk[r] = jnp.where(pick, neg, msk[r])
    wk = [jnp.where(chosen[r], s[r * N_GROUPS:(r + 1) * N_GROUPS, :], 0.0) for r in range(per)]
    denom = jnp.sum(functools.reduce(jnp.add, wk), axis=0, keepdims=True)
    gt_scr[n_exp:, :] = jnp.zeros((gt_scr.shape[0] - n_exp, tm), F32)
    for r in range(per):
        gt_scr[r * N_GROUPS:(r + 1) * N_GROUPS, :] = wk[r] / denom * ROUTE_SCALE


def _moe_kernel(h_ref, x1_ref, wrt_ref, br_ref, w1_ref, w3_ref, w2_ref, ws1_ref, ws3_ref, ws2_ref, gpost_ref,
                gt2_ref, out_ref, gt_scr, gates_scr, hm_scr, acc_scr, *, eb, f):
    j = pl.program_id(1)
    hb = h_ref[...]
    per = wrt_ref.shape[0] // N_GROUPS

    @pl.when(j == 0)
    def _():
        _route(hb, wrt_ref, br_ref, gt_scr)
        gates_scr[...] = gt_scr[...].T
        a = jnp.dot(hb, ws1_ref[...], preferred_element_type=F32)
        g = jnp.dot(hb, ws3_ref[...], preferred_element_type=F32)
        acc_scr[...] = jnp.dot((_silu(a) * g).astype(BF16), ws2_ref[...], preferred_element_type=F32)

    first = j * eb
    lane0 = lax.rem(first, per) * N_GROUPS + first // per
    gates = pltpu.roll(gates_scr[...], lax.rem(LANES - lane0, LANES), 1)
    for e in range(eb):
        a = jnp.dot(hb, w1_ref[:, e * f:(e + 1) * f], preferred_element_type=F32)
        g = jnp.dot(hb, w3_ref[:, e * f:(e + 1) * f], preferred_element_type=F32)
        gate = gates[:, e * N_GROUPS:e * N_GROUPS + 1]
        hm_scr[:, e * f:(e + 1) * f] = (_silu(a) * g * gate).astype(BF16)
    acc_scr[...] += jnp.dot(hm_scr[...], w2_ref[...], preferred_element_type=F32)

    @pl.when(j == pl.num_programs(1) - 1)
    def _():
        out_ref[...] = x1_ref[...] + gt2_ref[0] * _rms(acc_scr[...], gpost_ref[...])


def _moe(h2, x1, wrt, br, w1, w3, w2, ws1, ws3, ws2, g_post, gt2, tm, t_seq, eb):
    n, d = x1.shape
    n_exp = wrt.shape[0]
    f = w1.shape[1] // n_exp
    fs = ws1.shape[1]
    tps = max(t_seq // tm, 1)
    const = lambda i, j: (0, 0)
    kern = functools.partial(_moe_kernel, eb=eb, f=f)
    return pl.pallas_call(
        kern,
        grid=(n // tm, n_exp // eb),
        in_specs=[
            pl.BlockSpec((tm, d), lambda i, j: (i, 0)),
            pl.BlockSpec((tm, d), lambda i, j: (i, 0)),
            pl.BlockSpec((n_exp, d), const),
            pl.BlockSpec((n_exp, 1), const),
            pl.BlockSpec((d, eb * f), lambda i, j: (0, j)),
            pl.BlockSpec((d, eb * f), lambda i, j: (0, j)),
            pl.BlockSpec((eb * f, d), lambda i, j: (j, 0)),
            pl.BlockSpec((d, fs), const),
            pl.BlockSpec((d, fs), const),
            pl.BlockSpec((fs, d), const),
            pl.BlockSpec((1, d), const),
            _mod_spec(gt2, tm, tps, 2),
        ],
        out_specs=pl.BlockSpec((tm, d), lambda i, j: (i, 0)),
        out_shape=jax.ShapeDtypeStruct((n, d), F32),
        scratch_shapes=[
            pltpu.VMEM((LANES, tm), F32),
            pltpu.VMEM((tm, LANES), F32),
            pltpu.VMEM((tm, eb * f), BF16),
            pltpu.VMEM((tm, d), F32),
        ],
        compiler_params=_params(("parallel", "arbitrary")),
        name="moe",
    )(h2, x1, wrt, br, w1, w3, w2, ws1, ws3, ws2, g_post, gt2)


def _prep_layer(l, w):
    d = w["w_in"].shape[1]
    n_exp, _, f = w["w_e1"].shape[1:]
    H = N_HEADS
    w_in = w["w_in"][l]
    qkw = d
    g0 = 3 * d
    w_main = jnp.concatenate([w_in[:, :g0], w_in[:, g0 + 2 * H:]], axis=1).astype(BF16)
    w_gate = jnp.pad(w_in[:, g0:g0 + 2 * H], ((0, 0), (0, LANES - 2 * H))).astype(BF16)
    gbias = jnp.pad(jnp.concatenate([w["b_igate"][l], w["b_fgate"][l]]), (0, LANES - 2 * H)).reshape(1, LANES)
    return dict(
        w_main=w_main, w_gate=w_gate, gbias=gbias,
        g_pre_mix=w["g_pre_mix"][l].reshape(1, d), g_post_mix=w["g_post_mix"][l].reshape(1, d),
        g_pre_ffn=w["g_pre_ffn"][l].reshape(1, d), g_post_ffn=w["g_post_ffn"][l].reshape(1, d),
        w_conv=w["w_conv"][l], b_conv=w["b_conv"][l].reshape(1, qkw),
        g_head=w["g_head"][l].reshape(1, d), g_vnorm=w["g_vnorm"][l].reshape(1, d),
        b_vnorm=w["b_vnorm"][l].reshape(1, d),
        w_spatial=w["w_spatial"][l], b_spatial=w["b_spatial"][l],
        w_out=w["w_out"][l].astype(BF16),
        wrt=w["w_router"][l].T.reshape(N_GROUPS, n_exp // N_GROUPS, d).transpose(1, 0, 2)
        .reshape(n_exp, d).astype(BF16),
        br=w["b_router"][l].reshape(N_GROUPS, n_exp // N_GROUPS).T.reshape(n_exp, 1),
        w1=w["w_e1"][l].transpose(1, 0, 2).reshape(d, n_exp * f).astype(BF16),
        w3=w["w_e3"][l].transpose(1, 0, 2).reshape(d, n_exp * f).astype(BF16),
        w2=w["w_e2"][l].reshape(n_exp * f, d).astype(BF16),
        ws1=w["w_s1"][l].astype(BF16), ws3=w["w_s3"][l].astype(BF16), ws2=w["w_s2"][l].astype(BF16),
    )


def _spatial_weights(lw, start, gl):
    idx = jnp.arange(GMLP_CHUNK)
    mask = (idx[None, :] // CHUNK) <= (idx[:, None] // CHUNK)
    wsp = jnp.where(mask, lw["w_spatial"], 0.0)[:, start:start + gl, start:start + gl].astype(BF16)
    bsp_t = lw["b_spatial"][:, start:start + gl].T
    return wsp, bsp_t


def _tile(n, cap):
    t = cap
    while n % t:
        t //= 2
    return t


def _trunk(x, mods, c0, n0, m0, conv0, layers, mlstm_l, g_start, g_l, keep_v):
    nb, t_seq, d = x.shape
    n = nb * t_seq
    per_token = mods[0][0].shape[1] != 1
    tok_cap = n if per_token else t_seq
    tm_in = _tile(tok_cap, 1024)
    tm_mg = _tile(tok_cap, 512)
    tm_moe = _tile(tok_cap, 1024)
    xf = x.reshape(n, d)
    outs = [[] for _ in range(5 if keep_v else 4)]
    for l, lw in enumerate(layers):
        sh1, sc1, gt1, sh2, sc2, gt2 = mods[l]
        p, gates = _inproj(xf, sc1, sh1, lw["g_pre_mix"], lw["w_main"], lw["w_gate"], tm_in, t_seq)
        m0p = jnp.pad(m0[l], ((0, 0), (0, LANES - m0[l].shape[1]))).reshape(nb, 1, LANES)
        hm, c_new, n_new, m_new, conv_new = _mlstm(
            p, gates, c0[l], n0[l], m0p, conv0[l], lw["w_conv"], lw["b_conv"], lw["gbias"], lw["g_head"],
            nb, mlstm_l)
        wsp, bsp_t = _spatial_weights(lw, g_start, g_l)
        res = _merge(p, hm, xf, wsp, bsp_t, lw["g_vnorm"], lw["b_vnorm"], lw["w_out"], lw["g_post_mix"], gt1,
                     lw["g_pre_ffn"], sc2, sh2, tm_mg, t_seq, keep_v)
        x1, h2 = res[0], res[1]
        xf = _moe(h2, x1, lw["wrt"], lw["br"], lw["w1"], lw["w3"], lw["w2"], lw["ws1"], lw["ws3"], lw["ws2"],
                  lw["g_post_ffn"], gt2, tm_moe, t_seq, 4)
        vals = [c_new, n_new, m_new[:, 0, :N_HEADS], conv_new]
        if keep_v:
            vals.append(res[2].reshape(nb, t_seq, d))
        for lst, val in zip(outs, vals):
            lst.append(val)
    return xf.reshape(nb, t_seq, d), tuple(jnp.stack(lst) for lst in outs)


def kernel(x_prompt, x_sample, c_prompt, c_sample, state_mlstm_C, state_mlstm_n, state_mlstm_m, state_conv,
           w_ada, b_ada, g_pre_mix, g_post_mix, g_pre_ffn, g_post_ffn, w_in, b_igate, b_fgate, w_conv, b_conv,
           g_head, g_vnorm, b_vnorm, w_spatial, b_spatial, w_out, w_router, b_router, w_e1, w_e3, w_e2,
           w_s1, w_s3, w_s2):
    w = dict(w_in=w_in, b_igate=b_igate, b_fgate=b_fgate, g_pre_mix=g_pre_mix, g_post_mix=g_post_mix,
             g_pre_ffn=g_pre_ffn, g_post_ffn=g_post_ffn, w_conv=w_conv, b_conv=b_conv, g_head=g_head,
             g_vnorm=g_vnorm, b_vnorm=b_vnorm, w_spatial=w_spatial, b_spatial=b_spatial, w_out=w_out,
             w_router=w_router, b_router=b_router, w_e1=w_e1, w_e3=w_e3, w_e2=w_e2, w_s1=w_s1, w_s3=w_s3,
             w_s2=w_s2)
    depth = w_in.shape[0]
    bp, tp, d = x_prompt.shape
    bs, ts, _ = x_sample.shape
    H, dv, dqk = state_mlstm_C.shape[2:]
    qkw = state_conv.shape[-1]
    layers = [_prep_layer(l, w) for l in range(depth)]

    mod = _ada(jnp.concatenate([c_prompt, c_sample], axis=0), w_ada, b_ada)
    mods_p, mods_s = [], []
    for l in range(depth):
        parts = [mod[l][:, i * d:(i + 1) * d] for i in range(6)]
        mods_p.append([a[:bp].reshape(bp, 1, d) for a in parts])
        mods_s.append([jnp.repeat(a[bp:], ts, axis=0).reshape(1, bs * ts, d) for a in parts])

    zc = jnp.zeros((depth, bp, H, dv, dqk), F32)
    zn = jnp.zeros((depth, bp, H, dqk), F32)
    zm = jnp.zeros((depth, bp, H), F32)
    zconv = jnp.zeros((depth, bp, CONV_W - 1, qkw), F32)
    lp = 256 if tp % 256 == 0 else CHUNK
    y_p, (p_c, p_n, p_m, p_conv) = _trunk(x_prompt, mods_p, zc, zn, zm, zconv, layers, lp, 0, GMLP_CHUNK, False)
    g_start = PAST_LEN % GMLP_CHUNK
    y_s, (s_c, s_n, s_m, s_conv, s_v) = _trunk(x_sample, mods_s, state_mlstm_C, state_mlstm_n, state_mlstm_m,
                                               state_conv, layers, ts, g_start, ts, True)
    return (y_p, y_s, p_c, p_n, p_m, p_conv, s_c, s_n, s_m, s_conv, s_v)
```

```python
import functools

import jax
import jax.numpy as jnp
from jax import lax
from jax.experimental import pallas as pl
from jax.experimental.pallas import tpu as pltpu
from jax.experimental.pallas import tpu_sc as plsc

F32 = jnp.float32
BF16 = jnp.bfloat16

EPS = 1e-6
N_HEADS = 4
CONV_W = 4
GMLP_GROUPS = 8
GMLP_CHUNK = 128
CHUNK = 64
N_GROUPS = 8
TOPK_GROUPS = 4
TOP_K = 8
ROUTE_SCALE = 2.5
PAST_LEN = 4096

LANES = 128
SUBLANES = 8
VMEM_LIMIT = 56 * 1024 * 1024

NT_DIMS = (((1,), (1,)), ((), ()))
TN_DIMS = (((0,), (0,)), ((), ()))


def _sigmoid(x):
    return 0.5 * (jnp.tanh(0.5 * x) + 1.0)


def _silu(x):
    return x * _sigmoid(x)


def _gelu_tanh(x):
    return x * (0.5 * (1.0 + jnp.tanh(0.7978845608028654 * (x + 0.044715 * (x * x * x)))))


def _rms(x, g):
    return x * lax.rsqrt(jnp.mean(x * x, -1, keepdims=True) + EPS) * g


def _pack_pairs(x):
    c = x.shape[1] // 2
    lo = lax.bitcast_convert_type(x[:, :c].astype(BF16).astype(F32), jnp.uint32)
    hi = lax.bitcast_convert_type(x[:, c:].astype(BF16).astype(F32), jnp.uint32)
    return (lo >> 16) | (hi & jnp.uint32(0xFFFF0000))


def _unpack_pairs(u):
    lo = lax.bitcast_convert_type(u << 16, F32)
    hi = lax.bitcast_convert_type(u & jnp.uint32(0xFFFF0000), F32)
    return jnp.concatenate([lo, hi], axis=1)


def _params(sem):
    return pltpu.CompilerParams(dimension_semantics=sem, vmem_limit_bytes=VMEM_LIMIT)


def _ada_kernel(c_ref, w_ref, b_ref, o_ref):
    a = _silu(c_ref[...]).astype(BF16)
    o_ref[0] = jnp.dot(a, w_ref[0].astype(BF16), preferred_element_type=F32) + b_ref[0]


def _ada(c, w_ada, b_ada):
    depth, d, six_d = w_ada.shape
    nb = c.shape[0]
    tn = d
    return pl.pallas_call(
        _ada_kernel,
        grid=(depth, six_d // tn),
        in_specs=[
            pl.BlockSpec((nb, d), lambda l, j: (0, 0)),
            pl.BlockSpec((1, d, tn), lambda l, j: (l, 0, j)),
            pl.BlockSpec((1, 1, tn), lambda l, j: (l, 0, j)),
        ],
        out_specs=pl.BlockSpec((1, nb, tn), lambda l, j: (l, 0, j)),
        out_shape=jax.ShapeDtypeStruct((depth, nb, six_d), F32),
        compiler_params=_params(("parallel", "parallel")),
        name="adaln",
    )(c, w_ada, b_ada.reshape(depth, 1, six_d))


def _inproj_kernel(x_ref, sc_ref, sh_ref, g_ref, w_ref, wg_ref, p_ref, gates_ref, h_scr):
    @pl.when(pl.program_id(1) == 0)
    def _():
        h = _rms(x_ref[...], g_ref[...]) * (1.0 + sc_ref[0]) + sh_ref[0]
        h_scr[...] = h.astype(BF16)
        gates_ref[...] = jnp.dot(h, wg_ref[...], precision=lax.Precision.HIGHEST, preferred_element_type=F32)

    p_ref[...] = jnp.dot(h_scr[...], w_ref[...], preferred_element_type=F32).astype(BF16)


def _mod_spec(mod, tm, tiles_per_seq, ngrid):
    d = mod.shape[-1]
    if mod.shape[1] == 1:
        if ngrid == 2:
            return pl.BlockSpec((1, 1, d), lambda i, j: (i // tiles_per_seq, 0, 0))
        return pl.BlockSpec((1, 1, d), lambda i: (i // tiles_per_seq, 0, 0))
    if ngrid == 2:
        return pl.BlockSpec((1, tm, d), lambda i, j: (0, i, 0))
    return pl.BlockSpec((1, tm, d), lambda i: (0, i, 0))


def _inproj(x, sc, sh, g, w_main, w_gate, tm, t_seq):
    n, d = x.shape
    ncol = w_main.shape[1] // d
    tps = max(t_seq // tm, 1)
    return pl.pallas_call(
        _inproj_kernel,
        grid=(n // tm, ncol),
        in_specs=[
            pl.BlockSpec((tm, d), lambda i, j: (i, 0)),
            _mod_spec(sc, tm, tps, 2),
            _mod_spec(sh, tm, tps, 2),
            pl.BlockSpec((1, d), lambda i, j: (0, 0)),
            pl.BlockSpec((d, d), lambda i, j: (0, j)),
            pl.BlockSpec((d, LANES), lambda i, j: (0, 0)),
        ],
        out_specs=[
            pl.BlockSpec((tm, d), lambda i, j: (i, j)),
            pl.BlockSpec((tm, LANES), lambda i, j: (i, 0)),
        ],
        out_shape=[
            jax.ShapeDtypeStruct((n, ncol * d), BF16),
            jax.ShapeDtypeStruct((n, LANES), F32),
        ],
        scratch_shapes=[pltpu.VMEM((tm, d), BF16)],
        compiler_params=_params(("parallel", "arbitrary")),
        name="inproj",
    )(x, sc, sh, g, w_main, w_gate)


def _mlstm_kernel(qk_ref, v_ref, gt_ref, c0_ref, n0_ref, m0_ref, conv0_ref, wconv_ref, bconv_ref,
                  gbias_ref, ghead_ref, hm_ref, c_ref, n_ref, m_ref, conv_ref, xbuf, *, L, H, dqk, dv):
    qkw = 2 * H * dqk
    pad = SUBLANES
    tail0 = pad - (CONV_W - 1)

    @pl.when(pl.program_id(1) == 0)
    def _():
        c_ref[...] = c0_ref[...]
        n_ref[...] = n0_ref[...]
        m_ref[...] = m0_ref[...]
        xbuf[0:pad, :] = jnp.zeros((pad, qkw), F32)
        xbuf[tail0:pad, :] = conv0_ref[0]

    xbuf[pad:pad + L, :] = qk_ref[...].astype(F32)
    w = wconv_ref[...]
    y = xbuf[tail0:tail0 + L, :] * w[0:1, :]
    for j in range(1, CONV_W):
        y = y + xbuf[tail0 + j:tail0 + j + L, :] * w[j:j + 1, :]
    y = y + bconv_ref[...]
    tail = xbuf[L + tail0:L + pad, :]
    conv_ref[0] = tail
    xbuf[tail0:pad, :] = tail
    qk = _silu(y)

    z = gt_ref[...] + gbias_ref[...]
    lf = jnp.minimum(z, 0.0) - jnp.log1p(jnp.exp(-jnp.abs(z)))
    row = lax.broadcasted_iota(jnp.int32, (L, L), 0)
    col = lax.broadcasted_iota(jnp.int32, (L, L), 1)
    tri = row >= col
    hi = lax.Precision.HIGHEST
    b_col = jnp.dot(tri.astype(F32), lf, precision=hi, preferred_element_type=F32)
    eye = (lax.broadcasted_iota(jnp.int32, (LANES, LANES), 0)
           == lax.broadcasted_iota(jnp.int32, (LANES, LANES), 1)).astype(F32)
    b_row = lax.dot_general(eye, b_col, NT_DIMS, precision=hi, preferred_element_type=F32)
    z_row = lax.dot_general(eye, z, NT_DIMS, precision=hi, preferred_element_type=F32)

    scale = dqk ** -0.5
    lane1 = lax.broadcasted_iota(jnp.int32, (1, LANES), 1)
    m_old = m_ref[0]
    m_new = m_old
    for h in range(H):
        q = qk[:, h * dqk:(h + 1) * dqk] * scale
        k = qk[:, (H + h) * dqk:(H + h + 1) * dqk]
        qb = q.astype(BF16)
        vb = v_ref[:, h * dv:(h + 1) * dv]
        b_c = b_col[:, H + h:H + h + 1]
        i_c = z[:, h:h + 1]
        b_r = b_row[H + h:H + h + 1, :]
        i_r = z_row[h:h + 1, :]
        m_prev = m_old[:, h:h + 1]
        a = b_c + m_prev
        dm = jnp.where(tri, b_c - b_r + i_r, -jnp.inf)
        mt = jnp.maximum(a, jnp.max(dm, axis=-1, keepdims=True))
        s = lax.dot_general(qb, k.astype(BF16), NT_DIMS, preferred_element_type=F32)
        sg = s * jnp.exp(dm - mt)
        aw = jnp.exp(a - mt)
        cm = c_ref[0, h]
        qc = lax.dot_general(qb, cm.astype(BF16), NT_DIMS, preferred_element_type=F32)
        num = jnp.dot(sg.astype(BF16), vb, preferred_element_type=F32) + aw * qc
        nrow = n_ref[0, h:h + 1, :]
        den = jnp.sum(sg, -1, keepdims=True) + aw * jnp.sum(q * nrow, -1, keepdims=True)
        hh = num / jnp.maximum(jnp.abs(den), jnp.exp(-mt))
        m_last = mt[L - 1:L, :]
        b_last = b_c[L - 1:L, :]
        decay = jnp.exp(b_last + m_prev - m_last)
        ws = jnp.exp(b_last - b_c + i_c - m_last)
        kw = ws * k
        c_ref[0, h] = decay * cm + lax.dot_general(vb, kw.astype(BF16), TN_DIMS, preferred_element_type=F32)
        n_ref[0, h:h + 1, :] = decay * nrow + jnp.sum(kw, axis=0, keepdims=True)
        m_new = jnp.where(lane1 == h, m_last, m_new)
        hm_ref[:, h * dv:(h + 1) * dv] = _rms(hh, ghead_ref[:, h * dv:(h + 1) * dv]).astype(BF16)
    m_ref[0] = m_new


def _mlstm(p, gates, c0, n0, m0, conv0, w_conv, b_conv, gbias, g_head, nb, L):
    n = p.shape[0]
    t_seq = n // nb
    nc = t_seq // L
    _, H, dv, dqk = c0.shape
    qkw = 2 * H * dqk
    vw = H * dv
    kern = functools.partial(_mlstm_kernel, L=L, H=H, dqk=dqk, dv=dv)
    return pl.pallas_call(
        kern,
        grid=(nb, nc),
        in_specs=[
            pl.BlockSpec((L, qkw), lambda b, c: (b * nc + c, 0)),
            pl.BlockSpec((L, vw), lambda b, c: (b * nc + c, 1)),
            pl.BlockSpec((L, LANES), lambda b, c: (b * nc + c, 0)),
            pl.BlockSpec((1, H, dv, dqk), lambda b, c: (b, 0, 0, 0)),
            pl.BlockSpec((1, H, dqk), lambda b, c: (b, 0, 0)),
            pl.BlockSpec((1, 1, LANES), lambda b, c: (b, 0, 0)),
            pl.BlockSpec((1, CONV_W - 1, qkw), lambda b, c: (b, 0, 0)),
            pl.BlockSpec((CONV_W, qkw), lambda b, c: (0, 0)),
            pl.BlockSpec((1, qkw), lambda b, c: (0, 0)),
            pl.BlockSpec((1, LANES), lambda b, c: (0, 0)),
            pl.BlockSpec((1, vw), lambda b, c: (0, 0)),
        ],
        out_specs=[
            pl.BlockSpec((L, vw), lambda b, c: (b * nc + c, 0)),
            pl.BlockSpec((1, H, dv, dqk), lambda b, c: (b, 0, 0, 0)),
            pl.BlockSpec((1, H, dqk), lambda b, c: (b, 0, 0)),
            pl.BlockSpec((1, 1, LANES), lambda b, c: (b, 0, 0)),
            pl.BlockSpec((1, CONV_W - 1, qkw), lambda b, c: (b, 0, 0)),
        ],
        out_shape=[
            jax.ShapeDtypeStruct((n, vw), BF16),
            jax.ShapeDtypeStruct((nb, H, dv, dqk), F32),
            jax.ShapeDtypeStruct((nb, H, dqk), F32),
            jax.ShapeDtypeStruct((nb, 1, LANES), F32),
            jax.ShapeDtypeStruct((nb, CONV_W - 1, qkw), F32),
        ],
        scratch_shapes=[pltpu.VMEM((L + SUBLANES, qkw), F32)],
        compiler_params=_params(("parallel", "arbitrary")),
        name="mlstm",
    )(p, p, gates, c0, n0, m0, conv0, w_conv, b_conv, gbias, g_head)


def _merge_kernel(o_ref, u_ref, vg_ref, ga_ref, gb_ref, hm_ref, x_ref, wsp_ref, bsp_ref, gvn_ref, bvn_ref,
                  wout_ref, gpost_ref, gt1_ref, gpre_ref, sc2_ref, sh2_ref, *rest, tm, gl, keep_v):
    if keep_v:
        x1_ref, h2_ref, vn_ref, sg_scr = rest
    else:
        x1_ref, h2_ref, sg_scr = rest
    groups = wsp_ref.shape[0]
    ch = vg_ref.shape[1] // groups
    vg = _gelu_tanh(vg_ref[...].astype(F32))
    mu = jnp.mean(vg, -1, keepdims=True)
    xc = vg - mu
    vn = xc * lax.rsqrt(jnp.mean(xc * xc, -1, keepdims=True) + EPS) * gvn_ref[...] + bvn_ref[...]
    if keep_v:
        vn_ref[...] = vn
    vnb = vn.astype(BF16)
    for ci in range(tm // gl):
        for g in range(groups):
            blk = jnp.dot(wsp_ref[g], vnb[ci * gl:(ci + 1) * gl, g * ch:(g + 1) * ch],
                          preferred_element_type=F32)
            sg_scr[ci * gl:(ci + 1) * gl, g * ch:(g + 1) * ch] = blk + bsp_ref[:, g:g + 1]
    h_b = _gelu_tanh(u_ref[...].astype(F32)) * sg_scr[...]
    h_a = _sigmoid(o_ref[...].astype(F32)) * hm_ref[...].astype(F32)
    merged = _sigmoid(ga_ref[...].astype(F32)) * h_a + _sigmoid(gb_ref[...].astype(F32)) * h_b
    y = jnp.dot(merged.astype(BF16), wout_ref[...], preferred_element_type=F32)
    x1 = x_ref[...] + gt1_ref[0] * _rms(y, gpost_ref[...])
    x1_ref[...] = x1
    h2_ref[...] = _pack_pairs(_rms(x1, gpre_ref[...]) * (1.0 + sc2_ref[0]) + sh2_ref[0])


def _merge(p, hm, x, wsp, bsp_t, g_vn, b_vn, w_out, g_post, gt1, g_pre, sc2, sh2, tm, t_seq, keep_v):
    n, d = x.shape
    groups, gl, _ = wsp.shape
    tps = max(t_seq // tm, 1)
    row = lambda i: (0, 0)
    pcol = lambda c: pl.BlockSpec((tm, d), lambda i: (i, c))
    tok = pl.BlockSpec((tm, d), lambda i: (i, 0))
    vec = pl.BlockSpec((1, d), row)
    out_specs = [tok, pl.BlockSpec((tm, d // 2), lambda i: (i, 0))]
    out_shape = [jax.ShapeDtypeStruct((n, d), F32), jax.ShapeDtypeStruct((n, d // 2), jnp.uint32)]
    if keep_v:
        out_specs.append(tok)
        out_shape.append(jax.ShapeDtypeStruct((n, d), F32))
    kern = functools.partial(_merge_kernel, tm=tm, gl=gl, keep_v=keep_v)
    return pl.pallas_call(
        kern,
        grid=(n // tm,),
        in_specs=[
            pcol(2), pcol(3), pcol(4), pcol(5), pcol(6), tok, tok,
            pl.BlockSpec((groups, gl, gl), lambda i: (0, 0, 0)),
            pl.BlockSpec((gl, groups), row),
            vec, vec,
            pl.BlockSpec((d, d), row),
            vec,
            _mod_spec(gt1, tm, tps, 1),
            vec,
            _mod_spec(sc2, tm, tps, 1),
            _mod_spec(sh2, tm, tps, 1),
        ],
        out_specs=out_specs,
        out_shape=out_shape,
        scratch_shapes=[pltpu.VMEM((tm, d), F32)],
        compiler_params=_params(("parallel",)),
        name="merge",
    )(p, p, p, p, p, hm, x, wsp, bsp_t, g_vn, b_vn, w_out, g_post, gt1, g_pre, sc2, sh2)


def _select(hb, wrt_ref, br_ref):
    tm = hb.shape[0]
    n_exp = wrt_ref.shape[0]
    per = n_exp // N_GROUPS
    neg = -jnp.inf
    logits = lax.dot_general(wrt_ref[...], hb, NT_DIMS, preferred_element_type=F32)
    s = _sigmoid(logits)
    sel = s + br_ref[...]
    mem = [sel[r * N_GROUPS:(r + 1) * N_GROUPS, :] for r in range(per)]
    grow = lax.broadcasted_iota(jnp.int32, (N_GROUPS, tm), 0)
    m1 = functools.reduce(jnp.maximum, mem)
    idx1 = functools.reduce(jnp.minimum, [jnp.where(mem[r] == m1, r, per) for r in range(per)])
    m2 = functools.reduce(jnp.maximum, [jnp.where(idx1 == r, neg, mem[r]) for r in range(per)])
    gs = m1 + m2
    gmask = jnp.zeros((N_GROUPS, tm), jnp.bool_)
    for _ in range(TOPK_GROUPS):
        mx = jnp.max(gs, axis=0, keepdims=True)
        gi = jnp.min(jnp.where(gs == mx, grow, N_GROUPS), axis=0, keepdims=True)
        pick = grow == gi
        gmask = jnp.logical_or(gmask, pick)
        gs = jnp.where(pick, neg, gs)
    msk = [jnp.where(gmask, mem[r], neg) for r in range(per)]
    eidx = [grow * per + r for r in range(per)]
    chosen = [jnp.zeros((N_GROUPS, tm), jnp.bool_) for _ in range(per)]
    firsts = []
    for _ in range(TOP_K):
        mx = jnp.max(functools.reduce(jnp.maximum, msk), axis=0, keepdims=True)
        cand = functools.reduce(jnp.minimum, [jnp.where(msk[r] == mx, eidx[r], n_exp) for r in range(per)])
        first = jnp.min(cand, axis=0, keepdims=True)
        firsts.append(first)
        for r in range(per):
            pick = eidx[r] == first
            chosen[r] = jnp.logical_or(chosen[r], pick)
            msk[r] = jnp.where(pick, neg, msk[r])
    wk = [jnp.where(chosen[r], s[r * N_GROUPS:(r + 1) * N_GROUPS, :], 0.0) for r in range(per)]
    denom = jnp.sum(functools.reduce(jnp.add, wk), axis=0, keepdims=True)
    return s, chosen, firsts, eidx, denom


def _shared_expert(hb, ws1_ref, ws3_ref, ws2_ref):
    a = jnp.dot(hb, ws1_ref[...], preferred_element_type=F32)
    g = jnp.dot(hb, ws3_ref[...], preferred_element_type=F32)
    return jnp.dot((_silu(a) * g).astype(BF16), ws2_ref[...], preferred_element_type=F32)


def _moe_kernel(h_ref, x1_ref, wrt_ref, br_ref, w1_ref, w3_ref, w2_ref, ws1_ref, ws3_ref, ws2_ref, gpost_ref,
                gt2_ref, out_ref, gt_scr, gates_scr, acc_scr, *, eb):
    j = pl.program_id(1)
    hb = _unpack_pairs(h_ref[...]).astype(BF16)
    n_exp = wrt_ref.shape[0]
    per = n_exp // N_GROUPS

    @pl.when(j == 0)
    def _():
        s, chosen, _, _, denom = _select(hb, wrt_ref, br_ref)
        gt_scr[n_exp:, :] = jnp.zeros((gt_scr.shape[0] - n_exp, hb.shape[0]), F32)
        for r in range(per):
            wk = jnp.where(chosen[r], s[r * N_GROUPS:(r + 1) * N_GROUPS, :], 0.0)
            gt_scr[r * N_GROUPS:(r + 1) * N_GROUPS, :] = wk / denom * ROUTE_SCALE
        gates_scr[...] = gt_scr[...].T
        acc_scr[...] = _shared_expert(hb, ws1_ref, ws3_ref, ws2_ref)

    first = j * eb
    lane0 = lax.rem(first, per) * N_GROUPS + first // per
    gates = pltpu.roll(gates_scr[...], lax.rem(LANES - lane0, LANES), 1)
    acc = acc_scr[...]
    for e in range(eb):
        a = jnp.dot(hb, w1_ref[e], preferred_element_type=F32)
        g = jnp.dot(hb, w3_ref[e], preferred_element_type=F32)
        gate = gates[:, e * N_GROUPS:e * N_GROUPS + 1]
        acc = acc + jnp.dot((_silu(a) * g * gate).astype(BF16), w2_ref[e], preferred_element_type=F32)
    acc_scr[...] = acc

    @pl.when(j == pl.num_programs(1) - 1)
    def _():
        out_ref[...] = x1_ref[...] + gt2_ref[0] * _rms(acc_scr[...], gpost_ref[...])


def _moe_dense(h2, x1, lw, gt2, tm, t_seq, eb):
    n, d = x1.shape
    n_exp, _, f = lw["w1"].shape
    fs = lw["ws1"].shape[1]
    tps = max(t_seq // tm, 1)
    const = lambda i, j: (0, 0)
    return pl.pallas_call(
        functools.partial(_moe_kernel, eb=eb),
        grid=(n // tm, n_exp // eb),
        in_specs=[
            pl.BlockSpec((tm, d // 2), lambda i, j: (i, 0)),
            pl.BlockSpec((tm, d), lambda i, j: (i, 0)),
            pl.BlockSpec((n_exp, d), const),
            pl.BlockSpec((n_exp, 1), const),
            pl.BlockSpec((eb, d, f), lambda i, j: (j, 0, 0)),
            pl.BlockSpec((eb, d, f), lambda i, j: (j, 0, 0)),
            pl.BlockSpec((eb, f, d), lambda i, j: (j, 0, 0)),
            pl.BlockSpec((d, fs), const),
            pl.BlockSpec((d, fs), const),
            pl.BlockSpec((fs, d), const),
            pl.BlockSpec((1, d), const),
            _mod_spec(gt2, tm, tps, 2),
        ],
        out_specs=pl.BlockSpec((tm, d), lambda i, j: (i, 0)),
        out_shape=jax.ShapeDtypeStruct((n, d), F32),
        scratch_shapes=[
            pltpu.VMEM((LANES, tm), F32),
            pltpu.VMEM((tm, LANES), F32),
            pltpu.VMEM((tm, d), F32),
        ],
        compiler_params=_params(("parallel", "arbitrary")),
        name="moe_dense",
    )(h2, x1, lw["wrt"], lw["br"], lw["w1"], lw["w3"], lw["w2"], lw["ws1"], lw["ws3"], lw["ws2"],
      lw["g_post_ffn"], gt2)


def _router_kernel(h_ref, wrt_ref, br_ref, eidx_ref, rank_ref, gate_ref, cnt_ref, carry_scr):
    tm = h_ref.shape[0]
    n_exp = wrt_ref.shape[0]
    per = n_exp // N_GROUPS

    @pl.when(pl.program_id(0) == 0)
    def _():
        carry_scr[...] = jnp.zeros(carry_scr.shape, F32)

    hb = _unpack_pairs(h_ref[...]).astype(BF16)
    s, chosen, firsts, eidx, denom = _select(hb, wrt_ref, br_ref)
    sel01 = jnp.concatenate([c.astype(F32) for c in chosen], axis=0)
    before = (lax.broadcasted_iota(jnp.int32, (tm, tm), 0)
              < lax.broadcasted_iota(jnp.int32, (tm, tm), 1)).astype(BF16)
    rank = jnp.dot(sel01.astype(BF16), before, preferred_element_type=F32) + carry_scr[:, 0:1]
    carry_scr[...] = carry_scr[...] + jnp.sum(sel01, axis=1, keepdims=True)
    cnt_ref[...] = carry_scr[...]
    for k in range(TOP_K):
        s_k = jnp.zeros((1, tm), F32)
        r_k = jnp.zeros((1, tm), F32)
        for r in range(per):
            pick = eidx[r] == firsts[k]
            rows = slice(r * N_GROUPS, (r + 1) * N_GROUPS)
            s_k = s_k + jnp.sum(jnp.where(pick, s[rows, :], 0.0), axis=0, keepdims=True)
            r_k = r_k + jnp.sum(jnp.where(pick, rank[rows, :], 0.0), axis=0, keepdims=True)
        eidx_ref[k:k + 1, :] = firsts[k]
        rank_ref[k:k + 1, :] = r_k.astype(jnp.int32)
        gate_ref[k:k + 1, :] = s_k / denom * ROUTE_SCALE


def _router(h2, wrt, br, tm):
    n = h2.shape[0]
    n_exp, d = wrt.shape
    tok = pl.BlockSpec((TOP_K, tm), lambda i: (0, i))
    return pl.pallas_call(
        _router_kernel,
        grid=(n // tm,),
        in_specs=[
            pl.BlockSpec((tm, d // 2), lambda i: (i, 0)),
            pl.BlockSpec((n_exp, d), lambda i: (0, 0)),
            pl.BlockSpec((n_exp, 1), lambda i: (0, 0)),
        ],
        out_specs=[tok, tok, tok, pl.BlockSpec((n_exp, LANES), lambda i: (0, 0))],
        out_shape=[
            jax.ShapeDtypeStruct((TOP_K, n), jnp.int32),
            jax.ShapeDtypeStruct((TOP_K, n), jnp.int32),
            jax.ShapeDtypeStruct((TOP_K, n), F32),
            jax.ShapeDtypeStruct((n_exp, LANES), F32),
        ],
        scratch_shapes=[pltpu.VMEM((n_exp, LANES), F32)],
        compiler_params=_params(("arbitrary",)),
        name="router",
    )(h2, wrt, br)


SC_BATCH = 128
SC_WORKERS = 32


def _sc_mesh():
    return plsc.VectorSubcoreMesh(core_axis_name="c", subcore_axis_name="s")


def _sc_scatter(x, pos, n_rows):
    n, w = x.shape
    info = plsc.get_sparse_core_info()
    nc, nw = info.num_cores, info.num_cores * info.num_subcores
    steps = n // (nw * SC_BATCH)

    @functools.partial(
        pl.kernel, mesh=_sc_mesh(),
        out_type=jax.ShapeDtypeStruct((n_rows, w), x.dtype),
        scratch_types=[pltpu.VMEM((TOP_K, SC_BATCH), jnp.int32), pltpu.VMEM((SC_BATCH, w), x.dtype),
                       pltpu.SemaphoreType.DMA],
    )
    def scatter(x_hbm, pos_hbm, out_hbm, idx_v, rows_v, sem):
        wid = lax.axis_index("s") * nc + lax.axis_index("c")

        @pl.loop(0, steps)
        def _(s):
            blk = wid * steps + s
            pltpu.sync_copy(pos_hbm.at[blk], idx_v)
            pltpu.sync_copy(x_hbm.at[pl.ds(blk * SC_BATCH, SC_BATCH)], rows_v)
            copies = [pltpu.async_copy(rows_v, out_hbm.at[idx_v.at[k]], sem) for k in range(TOP_K)]
            for c in copies:
                c.wait()

    return scatter(x, pos)


def _sc_gather(y, pos):
    w = y.shape[1]
    n = pos.shape[0] * SC_BATCH
    info = plsc.get_sparse_core_info()
    nc, nw = info.num_cores, info.num_cores * info.num_subcores
    steps = n // (nw * SC_BATCH)

    @functools.partial(
        pl.kernel, mesh=_sc_mesh(),
        out_type=jax.ShapeDtypeStruct((TOP_K, n, w), y.dtype),
        scratch_types=[pltpu.VMEM((TOP_K, SC_BATCH), jnp.int32), pltpu.VMEM((SC_BATCH, w), y.dtype),
                       pltpu.SemaphoreType.DMA],
    )
    def gather(y_hbm, pos_hbm, out_hbm, idx_v, rows_v, sem):
        wid = lax.axis_index("s") * nc + lax.axis_index("c")

        @pl.loop(0, steps)
        def _(s):
            blk = wid * steps + s
            pltpu.sync_copy(pos_hbm.at[blk], idx_v)
            for k in range(TOP_K):
                pltpu.async_copy(y_hbm.at[idx_v.at[k]], rows_v, sem).wait()
                pltpu.sync_copy(rows_v, out_hbm.at[k, pl.ds(blk * SC_BATCH, SC_BATCH)])

    return gather(y, pos)


def _experts_kernel(te_ref, nu_ref, xs_ref, w1_ref, w3_ref, w2_ref, ys_ref):
    @pl.when(pl.program_id(0) < nu_ref[0])
    def _():
        x = _unpack_pairs(xs_ref[...]).astype(BF16)
        a = jnp.dot(x, w1_ref[0], preferred_element_type=F32)
        g = jnp.dot(x, w3_ref[0], preferred_element_type=F32)
        y = jnp.dot((_silu(a) * g).astype(BF16), w2_ref[0], preferred_element_type=F32)
        ys_ref[...] = _pack_pairs(y)


def _experts(xs, tile_expert, n_used, w1, w3, w2, rt):
    rows, half = xs.shape
    n_exp, d, f = w1.shape
    grid_spec = pltpu.PrefetchScalarGridSpec(
        num_scalar_prefetch=2,
        grid=(rows // rt,),
        in_specs=[
            pl.BlockSpec((rt, half), lambda i, te, nu: (i, 0)),
            pl.BlockSpec((1, d, f), lambda i, te, nu: (te[i], 0, 0)),
            pl.BlockSpec((1, d, f), lambda i, te, nu: (te[i], 0, 0)),
            pl.BlockSpec((1, f, d), lambda i, te, nu: (te[i], 0, 0)),
        ],
        out_specs=pl.BlockSpec((rt, half), lambda i, te, nu: (i, 0)),
    )
    return pl.pallas_call(
        _experts_kernel,
        grid_spec=grid_spec,
        out_shape=jax.ShapeDtypeStruct((rows, half), jnp.uint32),
        compiler_params=_params(("arbitrary",)),
        name="experts",
    )(tile_expert, n_used, xs, w1, w3, w2)


def _combine_kernel(yk_ref, g_ref, h_ref, x1_ref, ws1_ref, ws3_ref, ws2_ref, gpost_ref, gt2_ref, out_ref):
    hb = _unpack_pairs(h_ref[...]).astype(BF16)
    acc = _shared_expert(hb, ws1_ref, ws3_ref, ws2_ref)
    gates = g_ref[...]
    for k in range(TOP_K):
        acc = acc + gates[:, k:k + 1] * _unpack_pairs(yk_ref[k])
    out_ref[...] = x1_ref[...] + gt2_ref[0] * _rms(acc, gpost_ref[...])


def _combine(yk, gates_t, h2, x1, lw, gt2, tm, t_seq):
    n, d = x1.shape
    fs = lw["ws1"].shape[1]
    tps = max(t_seq // tm, 1)
    const = lambda i: (0, 0)
    return pl.pallas_call(
        _combine_kernel,
        grid=(n // tm,),
        in_specs=[
            pl.BlockSpec((TOP_K, tm, d // 2), lambda i: (0, i, 0)),
            pl.BlockSpec((tm, TOP_K), lambda i: (i, 0)),
            pl.BlockSpec((tm, d // 2), lambda i: (i, 0)),
            pl.BlockSpec((tm, d), lambda i: (i, 0)),
            pl.BlockSpec((d, fs), const),
            pl.BlockSpec((d, fs), const),
            pl.BlockSpec((fs, d), const),
            pl.BlockSpec((1, d), const),
            _mod_spec(gt2, tm, tps, 1),
        ],
        out_specs=pl.BlockSpec((tm, d), lambda i: (i, 0)),
        out_shape=jax.ShapeDtypeStruct((n, d), F32),
        compiler_params=_params(("parallel",)),
        name="combine",
    )(yk, gates_t, h2, x1, lw["ws1"], lw["ws3"], lw["ws2"], lw["g_post_ffn"], gt2)


EXPERT_ROWS = 512


def _moe_dispatched(h2, x1, lw, gt2, t_seq):
    n, d = x1.shape
    n_exp = lw["wrt"].shape[0]
    per = n_exp // N_GROUPS
    rt = EXPERT_ROWS
    n_tiles = n * TOP_K // rt + n_exp
    eidx, rank, gate, cnt = _router(h2, lw["wrt"], lw["br"], _tile(n, 1024))
    counts = cnt[:, 0].astype(jnp.int32).reshape(per, N_GROUPS).T.reshape(n_exp)
    padded = (counts + rt - 1) // rt * rt
    ends = jnp.cumsum(padded)
    starts = ends - padded
    base = jnp.zeros_like(eidx)
    for e in range(n_exp):
        base = base + jnp.where(eidx == e, starts[e], 0)
    pos = (base + rank).reshape(TOP_K, n // SC_BATCH, SC_BATCH).transpose(1, 0, 2)
    tile_start = jnp.arange(n_tiles, dtype=jnp.int32) * rt
    tile_expert = jnp.minimum(jnp.sum(ends[None, :] <= tile_start[:, None], axis=1), n_exp - 1).astype(jnp.int32)
    n_used = (ends[-1:] // rt).astype(jnp.int32)
    xs = _sc_scatter(h2, pos, n_tiles * rt)
    ys = _experts(xs, tile_expert, n_used, lw["w1"], lw["w3"], lw["w2"], rt)
    yk = _sc_gather(ys, pos)
    return _combine(yk, gate.T, h2, x1, lw, gt2, _tile(t_seq, 512), t_seq)


def _prep_layer(l, w):
    d = w["w_in"].shape[1]
    n_exp, _, f = w["w_e1"].shape[1:]
    H = N_HEADS
    w_in = w["w_in"][l]
    qkw = d
    g0 = 3 * d
    w_main = jnp.concatenate([w_in[:, :g0], w_in[:, g0 + 2 * H:]], axis=1).astype(BF16)
    w_gate = jnp.pad(w_in[:, g0:g0 + 2 * H], ((0, 0), (0, LANES - 2 * H)))
    gbias = jnp.pad(jnp.concatenate([w["b_igate"][l], w["b_fgate"][l]]), (0, LANES - 2 * H)).reshape(1, LANES)
    return dict(
        w_main=w_main, w_gate=w_gate, gbias=gbias,
        g_pre_mix=w["g_pre_mix"][l].reshape(1, d), g_post_mix=w["g_post_mix"][l].reshape(1, d),
        g_pre_ffn=w["g_pre_ffn"][l].reshape(1, d), g_post_ffn=w["g_post_ffn"][l].reshape(1, d),
        w_conv=w["w_conv"][l], b_conv=w["b_conv"][l].reshape(1, qkw),
        g_head=w["g_head"][l].reshape(1, d), g_vnorm=w["g_vnorm"][l].reshape(1, d),
        b_vnorm=w["b_vnorm"][l].reshape(1, d),
        w_spatial=w["w_spatial"][l], b_spatial=w["b_spatial"][l],
        w_out=w["w_out"][l].astype(BF16),
        wrt=w["w_router"][l].T.reshape(N_GROUPS, n_exp // N_GROUPS, d).transpose(1, 0, 2)
        .reshape(n_exp, d).astype(BF16),
        br=w["b_router"][l].reshape(N_GROUPS, n_exp // N_GROUPS).T.reshape(n_exp, 1),
        w1=w["w_e1"][l].astype(BF16), w3=w["w_e3"][l].astype(BF16), w2=w["w_e2"][l].astype(BF16),
        ws1=w["w_s1"][l].astype(BF16), ws3=w["w_s3"][l].astype(BF16), ws2=w["w_s2"][l].astype(BF16),
    )


def _spatial_weights(lw, start, gl):
    idx = jnp.arange(GMLP_CHUNK)
    mask = (idx[None, :] // CHUNK) <= (idx[:, None] // CHUNK)
    wsp = jnp.where(mask, lw["w_spatial"], 0.0)[:, start:start + gl, start:start + gl].astype(BF16)
    bsp_t = lw["b_spatial"][:, start:start + gl].T
    return wsp, bsp_t


def _tile(n, cap):
    t = cap
    while n % t:
        t //= 2
    return t


def _trunk(x, mods, c0, n0, m0, conv0, layers, mlstm_l, g_start, g_l, keep_v):
    nb, t_seq, d = x.shape
    n = nb * t_seq
    per_token = mods[0][0].shape[1] != 1
    tok_cap = n if per_token else t_seq
    tm_in = _tile(tok_cap, 1024)
    tm_mg = _tile(tok_cap, 512)
    tm_moe = _tile(tok_cap, 1024)
    xf = x.reshape(n, d)
    outs = [[] for _ in range(5 if keep_v else 4)]
    for l, lw in enumerate(layers):
        sh1, sc1, gt1, sh2, sc2, gt2 = mods[l]
        p, gates = _inproj(xf, sc1, sh1, lw["g_pre_mix"], lw["w_main"], lw["w_gate"], tm_in, t_seq)
        m0p = jnp.pad(m0[l], ((0, 0), (0, LANES - m0[l].shape[1]))).reshape(nb, 1, LANES)
        hm, c_new, n_new, m_new, conv_new = _mlstm(
            p, gates, c0[l], n0[l], m0p, conv0[l], lw["w_conv"], lw["b_conv"], lw["gbias"], lw["g_head"],
            nb, mlstm_l)
        wsp, bsp_t = _spatial_weights(lw, g_start, g_l)
        res = _merge(p, hm, xf, wsp, bsp_t, lw["g_vnorm"], lw["b_vnorm"], lw["w_out"], lw["g_post_mix"], gt1,
                     lw["g_pre_ffn"], sc2, sh2, tm_mg, t_seq, keep_v)
        x1, h2 = res[0], res[1]
        if n % (SC_WORKERS * SC_BATCH) == 0:
            xf = _moe_dispatched(h2, x1, lw, gt2, t_seq)
        else:
            xf = _moe_dense(h2, x1, lw, gt2, tm_moe, t_seq, 4)
        vals = [c_new, n_new, m_new[:, 0, :N_HEADS], conv_new]
        if keep_v:
            vals.append(res[2].reshape(nb, t_seq, d))
        for lst, val in zip(outs, vals):
            lst.append(val)
    return xf.reshape(nb, t_seq, d), tuple(jnp.stack(lst) for lst in outs)


def kernel(x_prompt, x_sample, c_prompt, c_sample, state_mlstm_C, state_mlstm_n, state_mlstm_m, state_conv,
           w_ada, b_ada, g_pre_mix, g_post_mix, g_pre_ffn, g_post_ffn, w_in, b_igate, b_fgate, w_conv, b_conv,
           g_head, g_vnorm, b_vnorm, w_spatial, b_spatial, w_out, w_router, b_router, w_e1, w_e3, w_e2,
           w_s1, w_s3, w_s2):
    w = dict(w_in=w_in, b_igate=b_igate, b_fgate=b_fgate, g_pre_mix=g_pre_mix, g_post_mix=g_post_mix,
             g_pre_ffn=g_pre_ffn, g_post_ffn=g_post_ffn, w_conv=w_conv, b_conv=b_conv, g_head=g_head,
             g_vnorm=g_vnorm, b_vnorm=b_vnorm, w_spatial=w_spatial, b_spatial=b_spatial, w_out=w_out,
             w_router=w_router, b_router=b_router, w_e1=w_e1, w_e3=w_e3, w_e2=w_e2, w_s1=w_s1, w_s3=w_s3,
             w_s2=w_s2)
    depth = w_in.shape[0]
    bp, tp, d = x_prompt.shape
    bs, ts, _ = x_sample.shape
    H, dv, dqk = state_mlstm_C.shape[2:]
    qkw = state_conv.shape[-1]
    layers = [_prep_layer(l, w) for l in range(depth)]

    mod = _ada(jnp.concatenate([c_prompt, c_sample], axis=0), w_ada, b_ada)
    mods_p, mods_s = [], []
    for l in range(depth):
        parts = [mod[l][:, i * d:(i + 1) * d] for i in range(6)]
        mods_p.append([a[:bp].reshape(bp, 1, d) for a in parts])
        mods_s.append([jnp.repeat(a[bp:], ts, axis=0).reshape(1, bs * ts, d) for a in parts])

    zc = jnp.zeros((depth, bp, H, dv, dqk), F32)
    zn = jnp.zeros((depth, bp, H, dqk), F32)
    zm = jnp.zeros((depth, bp, H), F32)
    zconv = jnp.zeros((depth, bp, CONV_W - 1, qkw), F32)
    lp = 256 if tp % 256 == 0 else CHUNK
    y_p, (p_c, p_n, p_m, p_conv) = _trunk(x_prompt, mods_p, zc, zn, zm, zconv, layers, lp, 0, GMLP_CHUNK, False)
    g_start = PAST_LEN % GMLP_CHUNK
    y_s, (s_c, s_n, s_m, s_conv, s_v) = _trunk(x_sample, mods_s, state_mlstm_C, state_mlstm_n, state_mlstm_m,
                                               state_conv, layers, ts, g_start, ts, True)
    return (y_p, y_s, p_c, p_n, p_m, p_conv, s_c, s_n, s_m, s_conv, s_v)
```

```python
import functools

import jax
import jax.numpy as jnp
from jax import lax
from jax.experimental import pallas as pl
from jax.experimental.pallas import tpu as pltpu
from jax.experimental.pallas import tpu_sc as plsc

F32 = jnp.float32
BF16 = jnp.bfloat16

EPS = 1e-6
N_HEADS = 4
CONV_W = 4
GMLP_GROUPS = 8
GMLP_CHUNK = 128
CHUNK = 64
N_GROUPS = 8
TOPK_GROUPS = 4
TOP_K = 8
ROUTE_SCALE = 2.5
PAST_LEN = 4096

LANES = 128
SUBLANES = 8
VMEM_LIMIT = 56 * 1024 * 1024

NT_DIMS = (((1,), (1,)), ((), ()))
TN_DIMS = (((0,), (0,)), ((), ()))


def _sigmoid(x):
    return 0.5 * (jnp.tanh(0.5 * x) + 1.0)


def _silu(x):
    return x * _sigmoid(x)


def _gelu_tanh(x):
    return x * (0.5 * (1.0 + jnp.tanh(0.7978845608028654 * (x + 0.044715 * (x * x * x)))))


def _rms(x, g):
    return x * lax.rsqrt(jnp.mean(x * x, -1, keepdims=True) + EPS) * g


def _pack_pairs(x):
    c = x.shape[1] // 2
    lo = lax.bitcast_convert_type(x[:, :c].astype(BF16).astype(F32), jnp.uint32)
    hi = lax.bitcast_convert_type(x[:, c:].astype(BF16).astype(F32), jnp.uint32)
    return (lo >> 16) | (hi & jnp.uint32(0xFFFF0000))


def _unpack_pairs(u):
    lo = lax.bitcast_convert_type(u << 16, F32)
    hi = lax.bitcast_convert_type(u & jnp.uint32(0xFFFF0000), F32)
    return jnp.concatenate([lo, hi], axis=1)


def _params(sem):
    return pltpu.CompilerParams(dimension_semantics=sem, vmem_limit_bytes=VMEM_LIMIT)


def _ada_kernel(c_ref, w_ref, b_ref, o_ref):
    a = _silu(c_ref[...]).astype(BF16)
    o_ref[0] = jnp.dot(a, w_ref[0].astype(BF16), preferred_element_type=F32) + b_ref[0]


def _ada(c, w_ada, b_ada):
    depth, d, six_d = w_ada.shape
    nb = c.shape[0]
    tn = d
    return pl.pallas_call(
        _ada_kernel,
        grid=(depth, six_d // tn),
        in_specs=[
            pl.BlockSpec((nb, d), lambda l, j: (0, 0)),
            pl.BlockSpec((1, d, tn), lambda l, j: (l, 0, j)),
            pl.BlockSpec((1, 1, tn), lambda l, j: (l, 0, j)),
        ],
        out_specs=pl.BlockSpec((1, nb, tn), lambda l, j: (l, 0, j)),
        out_shape=jax.ShapeDtypeStruct((depth, nb, six_d), F32),
        compiler_params=_params(("parallel", "parallel")),
        name="adaln",
    )(c, w_ada, b_ada.reshape(depth, 1, six_d))


def _inproj_kernel(x_ref, sc_ref, sh_ref, g_ref, w_ref, wg_ref, p_ref, gates_ref, h_scr):
    @pl.when(pl.program_id(1) == 0)
    def _():
        h = _rms(x_ref[...], g_ref[...]) * (1.0 + sc_ref[0]) + sh_ref[0]
        hb = h.astype(BF16)
        h_scr[...] = hb
        h_low = (h - hb.astype(F32)).astype(BF16)
        gg = jnp.dot(hb, wg_ref[...], preferred_element_type=F32)
        gates_ref[...] = (gg[:, :LANES] + gg[:, LANES:]
                          + jnp.dot(h_low, wg_ref[:, :LANES], preferred_element_type=F32))

    p_ref[...] = jnp.dot(h_scr[...], w_ref[...], preferred_element_type=F32).astype(BF16)


def _mod_spec(mod, tm, tiles_per_seq, ngrid):
    d = mod.shape[-1]
    if mod.shape[1] == 1:
        if ngrid == 2:
            return pl.BlockSpec((1, 1, d), lambda i, j: (i // tiles_per_seq, 0, 0))
        return pl.BlockSpec((1, 1, d), lambda i: (i // tiles_per_seq, 0, 0))
    if ngrid == 2:
        return pl.BlockSpec((1, tm, d), lambda i, j: (0, i, 0))
    return pl.BlockSpec((1, tm, d), lambda i: (0, i, 0))


def _inproj(x, sc, sh, g, w_main, w_gate, tm, t_seq):
    n, d = x.shape
    ncol = w_main.shape[1] // d
    tps = max(t_seq // tm, 1)
    return pl.pallas_call(
        _inproj_kernel,
        grid=(n // tm, ncol),
        in_specs=[
            pl.BlockSpec((tm, d), lambda i, j: (i, 0)),
            _mod_spec(sc, tm, tps, 2),
            _mod_spec(sh, tm, tps, 2),
            pl.BlockSpec((1, d), lambda i, j: (0, 0)),
            pl.BlockSpec((d, d), lambda i, j: (0, j)),
            pl.BlockSpec((d, 2 * LANES), lambda i, j: (0, 0)),
        ],
        out_specs=[
            pl.BlockSpec((tm, d), lambda i, j: (i, j)),
            pl.BlockSpec((tm, LANES), lambda i, j: (i, 0)),
        ],
        out_shape=[
            jax.ShapeDtypeStruct((n, ncol * d), BF16),
            jax.ShapeDtypeStruct((n, LANES), F32),
        ],
        scratch_shapes=[pltpu.VMEM((tm, d), BF16)],
        compiler_params=_params(("parallel", "arbitrary")),
        name="inproj",
    )(x, sc, sh, g, w_main, w_gate)


def _mlstm_kernel(qk_ref, v_ref, gt_ref, c0_ref, n0_ref, m0_ref, conv0_ref, wconv_ref, bconv_ref,
                  gbias_ref, ghead_ref, hm_ref, c_ref, n_ref, m_ref, conv_ref, xbuf, *, L, H, dqk, dv):
    qkw = 2 * H * dqk
    pad = SUBLANES
    tail0 = pad - (CONV_W - 1)

    @pl.when(pl.program_id(1) == 0)
    def _():
        c_ref[...] = c0_ref[...]
        n_ref[...] = n0_ref[...]
        m_ref[...] = m0_ref[...]
        xbuf[0:pad, :] = jnp.zeros((pad, qkw), F32)
        xbuf[tail0:pad, :] = conv0_ref[0]

    xbuf[pad:pad + L, :] = qk_ref[...].astype(F32)
    w = wconv_ref[...]
    y = xbuf[tail0:tail0 + L, :] * w[0:1, :]
    for j in range(1, CONV_W):
        y = y + xbuf[tail0 + j:tail0 + j + L, :] * w[j:j + 1, :]
    y = y + bconv_ref[...]
    tail = xbuf[L + tail0:L + pad, :]
    conv_ref[0] = tail
    xbuf[tail0:pad, :] = tail
    qk = _silu(y)

    z = gt_ref[...] + gbias_ref[...]
    lf = jnp.minimum(z, 0.0) - jnp.log(1.0 + jnp.exp(-jnp.abs(z)))
    row = lax.broadcasted_iota(jnp.int32, (L, L), 0)
    col = lax.broadcasted_iota(jnp.int32, (L, L), 1)
    tri = row >= col
    hi = lax.Precision.HIGHEST
    b_col = jnp.dot(tri.astype(F32), lf, precision=hi, preferred_element_type=F32)
    eye = (lax.broadcasted_iota(jnp.int32, (LANES, LANES), 0)
           == lax.broadcasted_iota(jnp.int32, (LANES, LANES), 1)).astype(F32)
    b_row = lax.dot_general(eye, b_col, NT_DIMS, precision=hi, preferred_element_type=F32)
    z_row = lax.dot_general(eye, z, NT_DIMS, precision=hi, preferred_element_type=F32)

    scale = dqk ** -0.5
    lane1 = lax.broadcasted_iota(jnp.int32, (1, LANES), 1)
    m_old = m_ref[0]
    m_new = m_old
    for h in range(H):
        q = qk[:, h * dqk:(h + 1) * dqk] * scale
        k = qk[:, (H + h) * dqk:(H + h + 1) * dqk]
        qb = q.astype(BF16)
        vb = v_ref[:, h * dv:(h + 1) * dv]
        b_c = b_col[:, H + h:H + h + 1]
        i_c = z[:, h:h + 1]
        b_r = b_row[H + h:H + h + 1, :]
        i_r = z_row[h:h + 1, :]
        m_prev = m_old[:, h:h + 1]
        a = b_c + m_prev
        dm = jnp.where(tri, b_c - b_r + i_r, -jnp.inf)
        mt = jnp.maximum(a, jnp.max(dm, axis=-1, keepdims=True))
        s = lax.dot_general(qb, k.astype(BF16), NT_DIMS, preferred_element_type=F32)
        sg = s * jnp.exp(dm - mt)
        aw = jnp.exp(a - mt)
        cm = c_ref[0, h]
        qc = lax.dot_general(qb, cm.astype(BF16), NT_DIMS, preferred_element_type=F32)
        num = jnp.dot(sg.astype(BF16), vb, preferred_element_type=F32) + aw * qc
        nrow = n_ref[0, h:h + 1, :]
        den = jnp.sum(sg, -1, keepdims=True) + aw * jnp.sum(q * nrow, -1, keepdims=True)
        hh = num / jnp.maximum(jnp.abs(den), jnp.exp(-mt))
        m_last = mt[L - 1:L, :]
        b_last = b_c[L - 1:L, :]
        decay = jnp.exp(b_last + m_prev - m_last)
        ws = jnp.exp(b_last - b_c + i_c - m_last)
        kw = ws * k
        c_ref[0, h] = decay * cm + lax.dot_general(vb, kw.astype(BF16), TN_DIMS, preferred_element_type=F32)
        n_ref[0, h:h + 1, :] = decay * nrow + jnp.sum(kw, axis=0, keepdims=True)
        m_new = jnp.where(lane1 == h, m_last, m_new)
        hm_ref[:, h * dv:(h + 1) * dv] = _rms(hh, ghead_ref[:, h * dv:(h + 1) * dv]).astype(BF16)
    m_ref[0] = m_new


def _mlstm(p, gates, c0, n0, m0, conv0, w_conv, b_conv, gbias, g_head, nb, L):
    n = p.shape[0]
    t_seq = n // nb
    nc = t_seq // L
    _, H, dv, dqk = c0.shape
    qkw = 2 * H * dqk
    vw = H * dv
    kern = functools.partial(_mlstm_kernel, L=L, H=H, dqk=dqk, dv=dv)
    return pl.pallas_call(
        kern,
        grid=(nb, nc),
        in_specs=[
            pl.BlockSpec((L, qkw), lambda b, c: (b * nc + c, 0)),
            pl.BlockSpec((L, vw), lambda b, c: (b * nc + c, 1)),
            pl.BlockSpec((L, LANES), lambda b, c: (b * nc + c, 0)),
            pl.BlockSpec((1, H, dv, dqk), lambda b, c: (b, 0, 0, 0)),
            pl.BlockSpec((1, H, dqk), lambda b, c: (b, 0, 0)),
            pl.BlockSpec((1, 1, LANES), lambda b, c: (b, 0, 0)),
            pl.BlockSpec((1, CONV_W - 1, qkw), lambda b, c: (b, 0, 0)),
            pl.BlockSpec((CONV_W, qkw), lambda b, c: (0, 0)),
            pl.BlockSpec((1, qkw), lambda b, c: (0, 0)),
            pl.BlockSpec((1, LANES), lambda b, c: (0, 0)),
            pl.BlockSpec((1, vw), lambda b, c: (0, 0)),
        ],
        out_specs=[
            pl.BlockSpec((L, vw), lambda b, c: (b * nc + c, 0)),
            pl.BlockSpec((1, H, dv, dqk), lambda b, c: (b, 0, 0, 0)),
            pl.BlockSpec((1, H, dqk), lambda b, c: (b, 0, 0)),
            pl.BlockSpec((1, 1, LANES), lambda b, c: (b, 0, 0)),
            pl.BlockSpec((1, CONV_W - 1, qkw), lambda b, c: (b, 0, 0)),
        ],
        out_shape=[
            jax.ShapeDtypeStruct((n, vw), BF16),
            jax.ShapeDtypeStruct((nb, H, dv, dqk), F32),
            jax.ShapeDtypeStruct((nb, H, dqk), F32),
            jax.ShapeDtypeStruct((nb, 1, LANES), F32),
            jax.ShapeDtypeStruct((nb, CONV_W - 1, qkw), F32),
        ],
        scratch_shapes=[pltpu.VMEM((L + SUBLANES, qkw), F32)],
        compiler_params=_params(("parallel", "arbitrary")),
        name="mlstm",
    )(p, p, gates, c0, n0, m0, conv0, w_conv, b_conv, gbias, g_head)


def _merge_kernel(o_ref, u_ref, vg_ref, ga_ref, gb_ref, hm_ref, x_ref, wsp_ref, bsp_ref, gvn_ref, bvn_ref,
                  wout_ref, gpost_ref, gt1_ref, gpre_ref, sc2_ref, sh2_ref, *rest, tm, gl, keep_v):
    if keep_v:
        x1_ref, h2_ref, vn_ref, sg_scr = rest
    else:
        x1_ref, h2_ref, sg_scr = rest
    groups = wsp_ref.shape[0]
    ch = vg_ref.shape[1] // groups
    vg = _gelu_tanh(vg_ref[...].astype(F32))
    mu = jnp.mean(vg, -1, keepdims=True)
    xc = vg - mu
    vn = xc * lax.rsqrt(jnp.mean(xc * xc, -1, keepdims=True) + EPS) * gvn_ref[...] + bvn_ref[...]
    if keep_v:
        vn_ref[...] = vn
    vnb = vn.astype(BF16)
    for ci in range(tm // gl):
        for g in range(groups):
            blk = jnp.dot(wsp_ref[g], vnb[ci * gl:(ci + 1) * gl, g * ch:(g + 1) * ch],
                          preferred_element_type=F32)
            sg_scr[ci * gl:(ci + 1) * gl, g * ch:(g + 1) * ch] = blk + bsp_ref[:, g:g + 1]
    h_b = _gelu_tanh(u_ref[...].astype(F32)) * sg_scr[...]
    h_a = _sigmoid(o_ref[...].astype(F32)) * hm_ref[...].astype(F32)
    merged = _sigmoid(ga_ref[...].astype(F32)) * h_a + _sigmoid(gb_ref[...].astype(F32)) * h_b
    y = jnp.dot(merged.astype(BF16), wout_ref[...], preferred_element_type=F32)
    x1 = x_ref[...] + gt1_ref[0] * _rms(y, gpost_ref[...])
    x1_ref[...] = x1
    h2_ref[...] = _pack_pairs(_rms(x1, gpre_ref[...]) * (1.0 + sc2_ref[0]) + sh2_ref[0])


def _merge(p, hm, x, wsp, bsp_t, g_vn, b_vn, w_out, g_post, gt1, g_pre, sc2, sh2, tm, t_seq, keep_v):
    n, d = x.shape
    groups, gl, _ = wsp.shape
    tps = max(t_seq // tm, 1)
    row = lambda i: (0, 0)
    pcol = lambda c: pl.BlockSpec((tm, d), lambda i: (i, c))
    tok = pl.BlockSpec((tm, d), lambda i: (i, 0))
    vec = pl.BlockSpec((1, d), row)
    out_specs = [tok, pl.BlockSpec((tm, d // 2), lambda i: (i, 0))]
    out_shape = [jax.ShapeDtypeStruct((n, d), F32), jax.ShapeDtypeStruct((n, d // 2), jnp.uint32)]
    if keep_v:
        out_specs.append(tok)
        out_shape.append(jax.ShapeDtypeStruct((n, d), F32))
    kern = functools.partial(_merge_kernel, tm=tm, gl=gl, keep_v=keep_v)
    return pl.pallas_call(
        kern,
        grid=(n // tm,),
        in_specs=[
            pcol(2), pcol(3), pcol(4), pcol(5), pcol(6), tok, tok,
            pl.BlockSpec((groups, gl, gl), lambda i: (0, 0, 0)),
            pl.BlockSpec((gl, groups), row),
            vec, vec,
            pl.BlockSpec((d, d), row),
            vec,
            _mod_spec(gt1, tm, tps, 1),
            vec,
            _mod_spec(sc2, tm, tps, 1),
            _mod_spec(sh2, tm, tps, 1),
        ],
        out_specs=out_specs,
        out_shape=out_shape,
        scratch_shapes=[pltpu.VMEM((tm, d), F32)],
        compiler_params=_params(("parallel",)),
        name="merge",
    )(p, p, p, p, p, hm, x, wsp, bsp_t, g_vn, b_vn, w_out, g_post, gt1, g_pre, sc2, sh2)


def _select(hb, wrt_ref, br_ref):
    tm = hb.shape[0]
    n_exp = wrt_ref.shape[0]
    per = n_exp // N_GROUPS
    neg = -jnp.inf
    logits = lax.dot_general(wrt_ref[...], hb, NT_DIMS, preferred_element_type=F32)
    s = _sigmoid(logits)
    sel = s + br_ref[...]
    mem = [sel[r * N_GROUPS:(r + 1) * N_GROUPS, :] for r in range(per)]
    grow = lax.broadcasted_iota(jnp.int32, (N_GROUPS, tm), 0)
    m1 = functools.reduce(jnp.maximum, mem)
    idx1 = functools.reduce(jnp.minimum, [jnp.where(mem[r] == m1, r, per) for r in range(per)])
    m2 = functools.reduce(jnp.maximum, [jnp.where(idx1 == r, neg, mem[r]) for r in range(per)])
    gs = m1 + m2
    gmask = jnp.zeros((N_GROUPS, tm), jnp.bool_)
    for _ in range(TOPK_GROUPS):
        mx = jnp.max(gs, axis=0, keepdims=True)
        gi = jnp.min(jnp.where(gs == mx, grow, N_GROUPS), axis=0, keepdims=True)
        pick = grow == gi
        gmask = jnp.logical_or(gmask, pick)
        gs = jnp.where(pick, neg, gs)
    msk = [jnp.where(gmask, mem[r], neg) for r in range(per)]
    eidx = [grow * per + r for r in range(per)]
    chosen = [jnp.zeros((N_GROUPS, tm), jnp.bool_) for _ in range(per)]
    firsts = []
    for _ in range(TOP_K):
        mx = jnp.max(functools.reduce(jnp.maximum, msk), axis=0, keepdims=True)
        cand = functools.reduce(jnp.minimum, [jnp.where(msk[r] == mx, eidx[r], n_exp) for r in range(per)])
        first = jnp.min(cand, axis=0, keepdims=True)
        firsts.append(first)
        for r in range(per):
            pick = eidx[r] == first
            chosen[r] = jnp.logical_or(chosen[r], pick)
            msk[r] = jnp.where(pick, neg, msk[r])
    wk = [jnp.where(chosen[r], s[r * N_GROUPS:(r + 1) * N_GROUPS, :], 0.0) for r in range(per)]
    denom = jnp.sum(functools.reduce(jnp.add, wk), axis=0, keepdims=True)
    return s, chosen, firsts, eidx, denom


def _shared_expert(hb, ws1_ref, ws3_ref, ws2_ref):
    a = jnp.dot(hb, ws1_ref[...], preferred_element_type=F32)
    g = jnp.dot(hb, ws3_ref[...], preferred_element_type=F32)
    return jnp.dot((_silu(a) * g).astype(BF16), ws2_ref[...], preferred_element_type=F32)


def _moe_kernel(h_ref, x1_ref, wrt_ref, br_ref, w1_ref, w3_ref, w2_ref, ws1_ref, ws3_ref, ws2_ref, gpost_ref,
                gt2_ref, out_ref, gt_scr, gates_scr, acc_scr, *, eb):
    j = pl.program_id(1)
    hb = _unpack_pairs(h_ref[...]).astype(BF16)
    n_exp = wrt_ref.shape[0]
    per = n_exp // N_GROUPS

    @pl.when(j == 0)
    def _():
        s, chosen, _, _, denom = _select(hb, wrt_ref, br_ref)
        gt_scr[n_exp:, :] = jnp.zeros((gt_scr.shape[0] - n_exp, hb.shape[0]), F32)
        for r in range(per):
            wk = jnp.where(chosen[r], s[r * N_GROUPS:(r + 1) * N_GROUPS, :], 0.0)
            gt_scr[r * N_GROUPS:(r + 1) * N_GROUPS, :] = wk / denom * ROUTE_SCALE
        gates_scr[...] = gt_scr[...].T
        acc_scr[...] = _shared_expert(hb, ws1_ref, ws3_ref, ws2_ref)

    first = j * eb
    lane0 = lax.rem(first, per) * N_GROUPS + first // per
    gates = pltpu.roll(gates_scr[...], lax.rem(LANES - lane0, LANES), 1)
    acc = acc_scr[...]
    for e in range(eb):
        a = jnp.dot(hb, w1_ref[e].astype(BF16), preferred_element_type=F32)
        g = jnp.dot(hb, w3_ref[e].astype(BF16), preferred_element_type=F32)
        gate = gates[:, e * N_GROUPS:e * N_GROUPS + 1]
        acc = acc + jnp.dot((_silu(a) * g * gate).astype(BF16), w2_ref[e].astype(BF16),
                            preferred_element_type=F32)
    acc_scr[...] = acc

    @pl.when(j == pl.num_programs(1) - 1)
    def _():
        out_ref[...] = x1_ref[...] + gt2_ref[0] * _rms(acc_scr[...], gpost_ref[...])


def _moe_dense(h2, x1, lw, gt2, tm, t_seq, eb):
    n, d = x1.shape
    _, n_exp, _, f = lw["w1"].shape
    layer = lw["layer"]
    fs = lw["ws1"].shape[1]
    tps = max(t_seq // tm, 1)
    const = lambda i, j: (0, 0)
    return pl.pallas_call(
        functools.partial(_moe_kernel, eb=eb),
        grid=(n // tm, n_exp // eb),
        in_specs=[
            pl.BlockSpec((tm, d // 2), lambda i, j: (i, 0)),
            pl.BlockSpec((tm, d), lambda i, j: (i, 0)),
            pl.BlockSpec((n_exp, d), const),
            pl.BlockSpec((n_exp, 1), const),
            pl.BlockSpec((None, eb, d, f), lambda i, j: (layer, j, 0, 0)),
            pl.BlockSpec((None, eb, d, f), lambda i, j: (layer, j, 0, 0)),
            pl.BlockSpec((None, eb, f, d), lambda i, j: (layer, j, 0, 0)),
            pl.BlockSpec((d, fs), const),
            pl.BlockSpec((d, fs), const),
            pl.BlockSpec((fs, d), const),
            pl.BlockSpec((1, d), const),
            _mod_spec(gt2, tm, tps, 2),
        ],
        out_specs=pl.BlockSpec((tm, d), lambda i, j: (i, 0)),
        out_shape=jax.ShapeDtypeStruct((n, d), F32),
        scratch_shapes=[
            pltpu.VMEM((LANES, tm), F32),
            pltpu.VMEM((tm, LANES), F32),
            pltpu.VMEM((tm, d), F32),
        ],
        compiler_params=_params(("parallel", "arbitrary")),
        name="moe_dense",
    )(h2, x1, lw["wrt"], lw["br"], lw["w1"], lw["w3"], lw["w2"], lw["ws1"], lw["ws3"], lw["ws2"],
      lw["g_post_ffn"], gt2)


def _router_kernel(h_ref, wrt_ref, br_ref, eidx_ref, rank_ref, gate_ref, cnt_ref, carry_scr):
    tm = h_ref.shape[0]
    n_exp = wrt_ref.shape[0]
    per = n_exp // N_GROUPS

    @pl.when(pl.program_id(0) == 0)
    def _():
        carry_scr[...] = jnp.zeros(carry_scr.shape, F32)

    hb = _unpack_pairs(h_ref[...]).astype(BF16)
    s, chosen, firsts, eidx, denom = _select(hb, wrt_ref, br_ref)
    sel01 = jnp.concatenate([c.astype(F32) for c in chosen], axis=0)
    before = (lax.broadcasted_iota(jnp.int32, (tm, tm), 0)
              < lax.broadcasted_iota(jnp.int32, (tm, tm), 1)).astype(BF16)
    rank = jnp.dot(sel01.astype(BF16), before, preferred_element_type=F32) + carry_scr[:, 0:1]
    carry_scr[...] = carry_scr[...] + jnp.sum(sel01, axis=1, keepdims=True)
    cnt_ref[...] = carry_scr[...]
    for k in range(TOP_K):
        s_k = jnp.zeros((1, tm), F32)
        r_k = jnp.zeros((1, tm), F32)
        for r in range(per):
            pick = eidx[r] == firsts[k]
            rows = slice(r * N_GROUPS, (r + 1) * N_GROUPS)
            s_k = s_k + jnp.sum(jnp.where(pick, s[rows, :], 0.0), axis=0, keepdims=True)
            r_k = r_k + jnp.sum(jnp.where(pick, rank[rows, :], 0.0), axis=0, keepdims=True)
        eidx_ref[k:k + 1, :] = firsts[k]
        rank_ref[k:k + 1, :] = r_k.astype(jnp.int32)
        gate_ref[k:k + 1, :] = s_k / denom * ROUTE_SCALE


def _router(h2, wrt, br, tm, row0, n):
    n_exp, d = wrt.shape
    tok = pl.BlockSpec((TOP_K, tm), lambda i: (0, i))
    blk0 = row0 // tm
    return pl.pallas_call(
        _router_kernel,
        grid=(n // tm,),
        in_specs=[
            pl.BlockSpec((tm, d // 2), lambda i: (i + blk0, 0)),
            pl.BlockSpec((n_exp, d), lambda i: (0, 0)),
            pl.BlockSpec((n_exp, 1), lambda i: (0, 0)),
        ],
        out_specs=[tok, tok, tok, pl.BlockSpec((n_exp, LANES), lambda i: (0, 0))],
        out_shape=[
            jax.ShapeDtypeStruct((TOP_K, n), jnp.int32),
            jax.ShapeDtypeStruct((TOP_K, n), jnp.int32),
            jax.ShapeDtypeStruct((TOP_K, n), F32),
            jax.ShapeDtypeStruct((n_exp, LANES), F32),
        ],
        scratch_shapes=[pltpu.VMEM((n_exp, LANES), F32)],
        compiler_params=_params(("arbitrary",)),
        name="router",
    )(h2, wrt, br)


SC_BATCH = 128
SC_WORKERS = 32


def _sc_mesh():
    return plsc.VectorSubcoreMesh(core_axis_name="c", subcore_axis_name="s")


def _sc_scatter(x, pos, n_rows, row0):
    w = x.shape[1]
    n = pos.shape[0] * SC_BATCH
    info = plsc.get_sparse_core_info()
    nc, nw = info.num_cores, info.num_cores * info.num_subcores
    steps = n // (nw * SC_BATCH)

    @functools.partial(
        pl.kernel, mesh=_sc_mesh(),
        out_type=jax.ShapeDtypeStruct((n_rows, w), x.dtype),
        scratch_types=[pltpu.VMEM((TOP_K, SC_BATCH), jnp.int32), pltpu.VMEM((SC_BATCH, w), x.dtype),
                       pltpu.SemaphoreType.DMA],
    )
    def scatter(x_hbm, pos_hbm, out_hbm, idx_v, rows_v, sem):
        wid = lax.axis_index("s") * nc + lax.axis_index("c")

        @pl.loop(0, steps)
        def _(s):
            blk = wid * steps + s
            pltpu.sync_copy(pos_hbm.at[blk], idx_v)
            pltpu.sync_copy(x_hbm.at[pl.ds(row0 + blk * SC_BATCH, SC_BATCH)], rows_v)
            copies = [pltpu.async_copy(rows_v, out_hbm.at[idx_v.at[k]], sem) for k in range(TOP_K)]
            for c in copies:
                c.wait()

    return scatter(x, pos)


def _sc_gather(y, pos):
    w = y.shape[1]
    n = pos.shape[0] * SC_BATCH
    info = plsc.get_sparse_core_info()
    nc, nw = info.num_cores, info.num_cores * info.num_subcores
    steps = n // (nw * SC_BATCH)

    @functools.partial(
        pl.kernel, mesh=_sc_mesh(),
        out_type=jax.ShapeDtypeStruct((TOP_K, n, w), y.dtype),
        scratch_types=[pltpu.VMEM((TOP_K, SC_BATCH), jnp.int32), pltpu.VMEM((SC_BATCH, w), y.dtype),
                       pltpu.SemaphoreType.DMA],
    )
    def gather(y_hbm, pos_hbm, out_hbm, idx_v, rows_v, sem):
        wid = lax.axis_index("s") * nc + lax.axis_index("c")

        @pl.loop(0, steps)
        def _(s):
            blk = wid * steps + s
            pltpu.sync_copy(pos_hbm.at[blk], idx_v)
            for k in range(TOP_K):
                pltpu.async_copy(y_hbm.at[idx_v.at[k]], rows_v, sem).wait()
                pltpu.sync_copy(rows_v, out_hbm.at[k, pl.ds(blk * SC_BATCH, SC_BATCH)])

    return gather(y, pos)


def _experts_kernel(te_ref, nu_ref, xs_ref, w1_ref, w3_ref, w2_ref, ys_ref, w1_scr, w3_scr, w2_scr):
    i = pl.program_id(0)

    @pl.when(jnp.logical_or(i == 0, te_ref[i] != te_ref[jnp.maximum(i - 1, 0)]))
    def _():
        w1_scr[...] = w1_ref[...].astype(BF16)
        w3_scr[...] = w3_ref[...].astype(BF16)
        w2_scr[...] = w2_ref[...].astype(BF16)

    @pl.when(i < nu_ref[0])
    def _():
        x = _unpack_pairs(xs_ref[...]).astype(BF16)
        a = jnp.dot(x, w1_scr[...], preferred_element_type=F32)
        g = jnp.dot(x, w3_scr[...], preferred_element_type=F32)
        y = jnp.dot((_silu(a) * g).astype(BF16), w2_scr[...], preferred_element_type=F32)
        ys_ref[...] = _pack_pairs(y)


def _experts(xs, tile_expert, n_used, w1, w3, w2, layer, rt):
    rows, half = xs.shape
    _, n_exp, d, f = w1.shape
    scratch = [pltpu.VMEM((d, f), BF16), pltpu.VMEM((d, f), BF16), pltpu.VMEM((f, d), BF16)]
    grid_spec = pltpu.PrefetchScalarGridSpec(
        num_scalar_prefetch=2,
        grid=(rows // rt,),
        in_specs=[
            pl.BlockSpec((rt, half), lambda i, te, nu: (i, 0)),
            pl.BlockSpec((None, None, d, f), lambda i, te, nu: (layer, te[i], 0, 0)),
            pl.BlockSpec((None, None, d, f), lambda i, te, nu: (layer, te[i], 0, 0)),
            pl.BlockSpec((None, None, f, d), lambda i, te, nu: (layer, te[i], 0, 0)),
        ],
        out_specs=pl.BlockSpec((rt, half), lambda i, te, nu: (i, 0)),
        scratch_shapes=scratch,
    )
    return pl.pallas_call(
        _experts_kernel,
        grid_spec=grid_spec,
        out_shape=jax.ShapeDtypeStruct((rows, half), jnp.uint32),
        compiler_params=_params(("arbitrary",)),
        name="experts",
    )(tile_expert, n_used, xs, w1, w3, w2)


def _combine_kernel(yk_ref, g_ref, h_ref, x1_ref, ws1_ref, ws3_ref, ws2_ref, gpost_ref, gt2_ref, *rest):
    out_ref = rest[-1]
    hb = _unpack_pairs(h_ref[...]).astype(BF16)
    acc = _shared_expert(hb, ws1_ref, ws3_ref, ws2_ref)
    gates = g_ref[...]
    for k in range(TOP_K):
        acc = acc + gates[:, k:k + 1] * _unpack_pairs(yk_ref[k])
    out_ref[...] = x1_ref[...] + gt2_ref[0] * _rms(acc, gpost_ref[...])


def _combine(yk, gates_t, h2, x1, lw, gt2, tm, t_seq, row0, out_prev):
    n, d = x1.shape
    nc = yk.shape[1]
    fs = lw["ws1"].shape[1]
    tps = max(t_seq // tm, 1)
    blk0 = row0 // tm
    const = lambda i: (0, 0)
    in_specs = [
        pl.BlockSpec((TOP_K, tm, d // 2), lambda i: (0, i, 0)),
        pl.BlockSpec((tm, TOP_K), lambda i: (i, 0)),
        pl.BlockSpec((tm, d // 2), lambda i: (i + blk0, 0)),
        pl.BlockSpec((tm, d), lambda i: (i + blk0, 0)),
        pl.BlockSpec((d, fs), const),
        pl.BlockSpec((d, fs), const),
        pl.BlockSpec((fs, d), const),
        pl.BlockSpec((1, d), const),
        pl.BlockSpec((1, 1, d), lambda i: ((i + blk0) // tps, 0, 0)),
    ]
    args = [yk, gates_t, h2, x1, lw["ws1"], lw["ws3"], lw["ws2"], lw["g_post_ffn"], gt2]
    aliases = {}
    if out_prev is not None:
        in_specs.append(pl.BlockSpec(memory_space=pl.ANY))
        aliases = {len(args): 0}
        args.append(out_prev)
    return pl.pallas_call(
        _combine_kernel,
        grid=(nc // tm,),
        in_specs=in_specs,
        out_specs=pl.BlockSpec((tm, d), lambda i: (i + blk0, 0)),
        out_shape=jax.ShapeDtypeStruct((n, d), F32),
        input_output_aliases=aliases,
        compiler_params=_params(("parallel",)),
        name="combine",
    )(*args)


EXPERT_ROWS = 512
MOE_CHUNKS = 2


def _moe_dispatched(h2, x1, lw, gt2, t_seq):
    n, d = x1.shape
    n_exp = lw["wrt"].shape[0]
    per = n_exp // N_GROUPS
    rt = EXPERT_ROWS
    chunks = MOE_CHUNKS
    while n % (chunks * SC_WORKERS * SC_BATCH):
        chunks //= 2
    nc = n // chunks
    n_tiles = nc * TOP_K // rt + n_exp
    tm = _tile(min(nc, t_seq), 512)
    routed = [_router(h2, lw["wrt"], lw["br"], _tile(nc, 1024), c * nc, nc) for c in range(chunks)]
    out = None
    for c, (eidx, rank, gate, cnt) in enumerate(routed):
        counts = cnt[:, 0].astype(jnp.int32).reshape(per, N_GROUPS).T.reshape(n_exp)
        padded = (counts + rt - 1) // rt * rt
        ends = jnp.cumsum(padded)
        starts = ends - padded
        base = jnp.zeros_like(eidx)
        for e in range(n_exp):
            base = base + jnp.where(eidx == e, starts[e], 0)
        pos = (base + rank).reshape(TOP_K, nc // SC_BATCH, SC_BATCH).transpose(1, 0, 2)
        tile_start = jnp.arange(n_tiles, dtype=jnp.int32) * rt
        tile_expert = jnp.minimum(jnp.sum(ends[None, :] <= tile_start[:, None], axis=1), n_exp - 1).astype(jnp.int32)
        n_used = (ends[-1:] // rt).astype(jnp.int32)
        xs = _sc_scatter(h2, pos, n_tiles * rt, c * nc)
        ys = _experts(xs, tile_expert, n_used, lw["w1"], lw["w3"], lw["w2"], lw["layer"], rt)
        yk = _sc_gather(ys, pos)
        out = _combine(yk, gate.T, h2, x1, lw, gt2, tm, t_seq, c * nc, out)
    return out


def _prep_layer(l, w):
    d = w["w_in"].shape[1]
    n_exp = w["w_e1"].shape[1]
    H = N_HEADS
    w_in = w["w_in"][l]
    qkw = d
    g0 = 3 * d
    w_main = jnp.concatenate([w_in[:, :g0], w_in[:, g0 + 2 * H:]], axis=1).astype(BF16)
    wg = jnp.pad(w_in[:, g0:g0 + 2 * H], ((0, 0), (0, LANES - 2 * H)))
    wg_high = wg.astype(BF16)
    w_gate = jnp.concatenate([wg_high, (wg - wg_high.astype(F32)).astype(BF16)], axis=1)
    gbias = jnp.pad(jnp.concatenate([w["b_igate"][l], w["b_fgate"][l]]), (0, LANES - 2 * H)).reshape(1, LANES)
    return dict(
        w_main=w_main, w_gate=w_gate, gbias=gbias,
        g_pre_mix=w["g_pre_mix"][l].reshape(1, d), g_post_mix=w["g_post_mix"][l].reshape(1, d),
        g_pre_ffn=w["g_pre_ffn"][l].reshape(1, d), g_post_ffn=w["g_post_ffn"][l].reshape(1, d),
        w_conv=w["w_conv"][l], b_conv=w["b_conv"][l].reshape(1, qkw),
        g_head=w["g_head"][l].reshape(1, d), g_vnorm=w["g_vnorm"][l].reshape(1, d),
        b_vnorm=w["b_vnorm"][l].reshape(1, d),
        w_spatial=w["w_spatial"][l], b_spatial=w["b_spatial"][l],
        w_out=w["w_out"][l].astype(BF16),
        wrt=w["w_router"][l].T.reshape(N_GROUPS, n_exp // N_GROUPS, d).transpose(1, 0, 2)
        .reshape(n_exp, d).astype(BF16),
        br=w["b_router"][l].reshape(N_GROUPS, n_exp // N_GROUPS).T.reshape(n_exp, 1),
        w1=w["w_e1"], w3=w["w_e3"], w2=w["w_e2"], layer=l,
        ws1=w["w_s1"][l].astype(BF16), ws3=w["w_s3"][l].astype(BF16), ws2=w["w_s2"][l].astype(BF16),
    )


def _spatial_weights(lw, start, gl):
    idx = jnp.arange(GMLP_CHUNK)
    mask = (idx[None, :] // CHUNK) <= (idx[:, None] // CHUNK)
    wsp = jnp.where(mask, lw["w_spatial"], 0.0)[:, start:start + gl, start:start + gl].astype(BF16)
    bsp_t = lw["b_spatial"][:, start:start + gl].T
    return wsp, bsp_t


def _tile(n, cap):
    t = cap
    while n % t:
        t //= 2
    return t


def _trunk(x, mods, c0, n0, m0, conv0, layers, mlstm_l, g_start, g_l, keep_v):
    nb, t_seq, d = x.shape
    n = nb * t_seq
    per_token = mods[0][0].shape[1] != 1
    tok_cap = n if per_token else t_seq
    tm_in = _tile(tok_cap, 1024)
    tm_mg = _tile(tok_cap, 512)
    tm_moe = _tile(tok_cap, 1024)
    xf = x.reshape(n, d)
    outs = [[] for _ in range(5 if keep_v else 4)]
    for l, lw in enumerate(layers):
        sh1, sc1, gt1, sh2, sc2, gt2 = mods[l]
        p, gates = _inproj(xf, sc1, sh1, lw["g_pre_mix"], lw["w_main"], lw["w_gate"], tm_in, t_seq)
        m0p = jnp.pad(m0[l], ((0, 0), (0, LANES - m0[l].shape[1]))).reshape(nb, 1, LANES)
        hm, c_new, n_new, m_new, conv_new = _mlstm(
            p, gates, c0[l], n0[l], m0p, conv0[l], lw["w_conv"], lw["b_conv"], lw["gbias"], lw["g_head"],
            nb, mlstm_l)
        wsp, bsp_t = _spatial_weights(lw, g_start, g_l)
        res = _merge(p, hm, xf, wsp, bsp_t, lw["g_vnorm"], lw["b_vnorm"], lw["w_out"], lw["g_post_mix"], gt1,
                     lw["g_pre_ffn"], sc2, sh2, tm_mg, t_seq, keep_v)
        x1, h2 = res[0], res[1]
        if n % (SC_WORKERS * SC_BATCH) == 0:
            xf = _moe_dispatched(h2, x1, lw, gt2, t_seq)
        else:
            xf = _moe_dense(h2, x1, lw, gt2, tm_moe, t_seq, 4)
        vals = [c_new, n_new, m_new[:, 0, :N_HEADS], conv_new]
        if keep_v:
            vals.append(res[2].reshape(nb, t_seq, d))
        for lst, val in zip(outs, vals):
            lst.append(val)
    return xf.reshape(nb, t_seq, d), tuple(jnp.stack(lst) for lst in outs)


def kernel(x_prompt, x_sample, c_prompt, c_sample, state_mlstm_C, state_mlstm_n, state_mlstm_m, state_conv,
           w_ada, b_ada, g_pre_mix, g_post_mix, g_pre_ffn, g_post_ffn, w_in, b_igate, b_fgate, w_conv, b_conv,
           g_head, g_vnorm, b_vnorm, w_spatial, b_spatial, w_out, w_router, b_router, w_e1, w_e3, w_e2,
           w_s1, w_s3, w_s2):
    w = dict(w_in=w_in, b_igate=b_igate, b_fgate=b_fgate, g_pre_mix=g_pre_mix, g_post_mix=g_post_mix,
             g_pre_ffn=g_pre_ffn, g_post_ffn=g_post_ffn, w_conv=w_conv, b_conv=b_conv, g_head=g_head,
             g_vnorm=g_vnorm, b_vnorm=b_vnorm, w_spatial=w_spatial, b_spatial=b_spatial, w_out=w_out,
             w_router=w_router, b_router=b_router, w_e1=w_e1, w_e3=w_e3, w_e2=w_e2, w_s1=w_s1, w_s3=w_s3,
             w_s2=w_s2)
    depth = w_in.shape[0]
    bp, tp, d = x_prompt.shape
    bs, ts, _ = x_sample.shape
    H, dv, dqk = state_mlstm_C.shape[2:]
    qkw = state_conv.shape[-1]
    layers = [_prep_layer(l, w) for l in range(depth)]

    mod = _ada(jnp.concatenate([c_prompt, c_sample], axis=0), w_ada, b_ada)
    mods_p, mods_s = [], []
    for l in range(depth):
        parts = [mod[l][:, i * d:(i + 1) * d] for i in range(6)]
        mods_p.append([a[:bp].reshape(bp, 1, d) for a in parts])
        mods_s.append([jnp.repeat(a[bp:], ts, axis=0).reshape(1, bs * ts, d) for a in parts])

    zc = jnp.zeros((depth, bp, H, dv, dqk), F32)
    zn = jnp.zeros((depth, bp, H, dqk), F32)
    zm = jnp.zeros((depth, bp, H), F32)
    zconv = jnp.zeros((depth, bp, CONV_W - 1, qkw), F32)
    lp = 256 if tp % 256 == 0 else CHUNK
    y_p, (p_c, p_n, p_m, p_conv) = _trunk(x_prompt, mods_p, zc, zn, zm, zconv, layers, lp, 0, GMLP_CHUNK, False)
    g_start = PAST_LEN % GMLP_CHUNK
    y_s, (s_c, s_n, s_m, s_conv, s_v) = _trunk(x_sample, mods_s, state_mlstm_C, state_mlstm_n, state_mlstm_m,
                                               state_conv, layers, ts, g_start, ts, True)
    return (y_p, y_s, p_c, p_n, p_m, p_conv, s_c, s_n, s_m, s_conv, s_v)
```

```python
import functools

import jax
import jax.numpy as jnp
from jax import lax
from jax.experimental import pallas as pl
from jax.experimental.pallas import tpu as pltpu
from jax.experimental.pallas import tpu_sc as plsc

F32 = jnp.float32
BF16 = jnp.bfloat16

EPS = 1e-6
N_HEADS = 4
CONV_W = 4
GMLP_GROUPS = 8
GMLP_CHUNK = 128
CHUNK = 64
N_GROUPS = 8
TOPK_GROUPS = 4
TOP_K = 8
ROUTE_SCALE = 2.5
PAST_LEN = 4096

LANES = 128
SUBLANES = 8
VMEM_LIMIT = 56 * 1024 * 1024

NT_DIMS = (((1,), (1,)), ((), ()))
TN_DIMS = (((0,), (0,)), ((), ()))


def _sigmoid(x):
    return 0.5 * (jnp.tanh(0.5 * x) + 1.0)


def _silu(x):
    return x * _sigmoid(x)


def _gelu_tanh(x):
    return x * (0.5 * (1.0 + jnp.tanh(0.7978845608028654 * (x + 0.044715 * (x * x * x)))))


def _rms(x, g):
    return x * lax.rsqrt(jnp.mean(x * x, -1, keepdims=True) + EPS) * g


def _pack_pairs(x):
    c = x.shape[1] // 2
    return pltpu.pack_elementwise([x[:, :c], x[:, c:]], packed_dtype=BF16)


def _unpack_pairs(u):
    lo = pltpu.unpack_elementwise(u, index=0, packed_dtype=BF16, unpacked_dtype=F32)
    hi = pltpu.unpack_elementwise(u, index=1, packed_dtype=BF16, unpacked_dtype=F32)
    return jnp.concatenate([lo, hi], axis=1)


def _params(sem):
    return pltpu.CompilerParams(dimension_semantics=sem, vmem_limit_bytes=VMEM_LIMIT)


def _ada_kernel(c_ref, w_ref, b_ref, o_ref):
    a = _silu(c_ref[...]).astype(BF16)
    o_ref[0] = jnp.dot(a, w_ref[0].astype(BF16), preferred_element_type=F32) + b_ref[0]


def _ada(c, w_ada, b_ada):
    depth, d, six_d = w_ada.shape
    nb = c.shape[0]
    tn = d
    return pl.pallas_call(
        _ada_kernel,
        grid=(depth, six_d // tn),
        in_specs=[
            pl.BlockSpec((nb, d), lambda l, j: (0, 0)),
            pl.BlockSpec((1, d, tn), lambda l, j: (l, 0, j)),
            pl.BlockSpec((1, 1, tn), lambda l, j: (l, 0, j)),
        ],
        out_specs=pl.BlockSpec((1, nb, tn), lambda l, j: (l, 0, j)),
        out_shape=jax.ShapeDtypeStruct((depth, nb, six_d), F32),
        compiler_params=_params(("parallel", "parallel")),
        name="adaln",
    )(c, w_ada, b_ada.reshape(depth, 1, six_d))


def _inproj_kernel(x_ref, sc_ref, sh_ref, g_ref, w_ref, wg_ref, p_ref, gates_ref, h_scr):
    @pl.when(pl.program_id(1) == 0)
    def _():
        h = _rms(x_ref[...], g_ref[...]) * (1.0 + sc_ref[0]) + sh_ref[0]
        hb = h.astype(BF16)
        h_scr[...] = hb
        h_low = (h - hb.astype(F32)).astype(BF16)
        gg = jnp.dot(hb, wg_ref[...], preferred_element_type=F32)
        gates_ref[...] = (gg[:, :LANES] + gg[:, LANES:]
                          + jnp.dot(h_low, wg_ref[:, :LANES], preferred_element_type=F32))

    p_ref[...] = jnp.dot(h_scr[...], w_ref[...], preferred_element_type=F32).astype(BF16)


def _mod_spec(mod, tm, tiles_per_seq, ngrid):
    d = mod.shape[-1]
    if mod.shape[1] == 1:
        if ngrid == 2:
            return pl.BlockSpec((1, 1, d), lambda i, j: (i // tiles_per_seq, 0, 0))
        return pl.BlockSpec((1, 1, d), lambda i: (i // tiles_per_seq, 0, 0))
    if ngrid == 2:
        return pl.BlockSpec((1, tm, d), lambda i, j: (0, i, 0))
    return pl.BlockSpec((1, tm, d), lambda i: (0, i, 0))


def _inproj(x, sc, sh, g, w_main, w_gate, tm, t_seq):
    n, d = x.shape
    ncol = w_main.shape[1] // d
    tps = max(t_seq // tm, 1)
    return pl.pallas_call(
        _inproj_kernel,
        grid=(n // tm, ncol),
        in_specs=[
            pl.BlockSpec((tm, d), lambda i, j: (i, 0)),
            _mod_spec(sc, tm, tps, 2),
            _mod_spec(sh, tm, tps, 2),
            pl.BlockSpec((1, d), lambda i, j: (0, 0)),
            pl.BlockSpec((d, d), lambda i, j: (0, j)),
            pl.BlockSpec((d, 2 * LANES), lambda i, j: (0, 0)),
        ],
        out_specs=[
            pl.BlockSpec((tm, d), lambda i, j: (i, j)),
            pl.BlockSpec((tm, LANES), lambda i, j: (i, 0)),
        ],
        out_shape=[
            jax.ShapeDtypeStruct((n, ncol * d), BF16),
            jax.ShapeDtypeStruct((n, LANES), F32),
        ],
        scratch_shapes=[pltpu.VMEM((tm, d), BF16)],
        compiler_params=_params(("parallel", "arbitrary")),
        name="inproj",
    )(x, sc, sh, g, w_main, w_gate)


def _mlstm_kernel(qk_ref, v_ref, gt_ref, c0_ref, n0_ref, m0_ref, conv0_ref, wconv_ref, bconv_ref,
                  gbias_ref, ghead_ref, hm_ref, c_ref, n_ref, m_ref, conv_ref, xbuf, *, L, H, dqk, dv):
    qkw = 2 * H * dqk
    pad = SUBLANES
    tail0 = pad - (CONV_W - 1)

    @pl.when(pl.program_id(1) == 0)
    def _():
        c_ref[...] = c0_ref[...]
        n_ref[...] = n0_ref[...]
        m_ref[...] = m0_ref[...]
        xbuf[0:pad, :] = jnp.zeros((pad, qkw), F32)
        xbuf[tail0:pad, :] = conv0_ref[0]

    xbuf[pad:pad + L, :] = qk_ref[...].astype(F32)
    w = wconv_ref[...]
    y = xbuf[tail0:tail0 + L, :] * w[0:1, :]
    for j in range(1, CONV_W):
        y = y + xbuf[tail0 + j:tail0 + j + L, :] * w[j:j + 1, :]
    y = y + bconv_ref[...]
    tail = xbuf[L + tail0:L + pad, :]
    conv_ref[0] = tail
    xbuf[tail0:pad, :] = tail
    qk = _silu(y)

    z = gt_ref[...] + gbias_ref[...]
    lf = jnp.minimum(z, 0.0) - jnp.log(1.0 + jnp.exp(-jnp.abs(z)))
    row = lax.broadcasted_iota(jnp.int32, (L, L), 0)
    col = lax.broadcasted_iota(jnp.int32, (L, L), 1)
    tri = row >= col
    hi = lax.Precision.HIGHEST
    b_col = jnp.dot(tri.astype(F32), lf, precision=hi, preferred_element_type=F32)
    eye = (lax.broadcasted_iota(jnp.int32, (LANES, LANES), 0)
           == lax.broadcasted_iota(jnp.int32, (LANES, LANES), 1)).astype(F32)
    b_row = lax.dot_general(eye, b_col, NT_DIMS, precision=hi, preferred_element_type=F32)
    z_row = lax.dot_general(eye, z, NT_DIMS, precision=hi, preferred_element_type=F32)

    scale = dqk ** -0.5
    lane1 = lax.broadcasted_iota(jnp.int32, (1, LANES), 1)
    m_old = m_ref[0]
    m_new = m_old
    for h in range(H):
        q = qk[:, h * dqk:(h + 1) * dqk] * scale
        k = qk[:, (H + h) * dqk:(H + h + 1) * dqk]
        qb = q.astype(BF16)
        vb = v_ref[:, h * dv:(h + 1) * dv]
        b_c = b_col[:, H + h:H + h + 1]
        i_c = z[:, h:h + 1]
        b_r = b_row[H + h:H + h + 1, :]
        i_r = z_row[h:h + 1, :]
        m_prev = m_old[:, h:h + 1]
        a = b_c + m_prev
        dm = jnp.where(tri, b_c - b_r + i_r, -jnp.inf)
        mt = jnp.maximum(a, jnp.max(dm, axis=-1, keepdims=True))
        s = lax.dot_general(qb, k.astype(BF16), NT_DIMS, preferred_element_type=F32)
        sg = s * jnp.exp(dm - mt)
        aw = jnp.exp(a - mt)
        cm = c_ref[0, h]
        qc = lax.dot_general(qb, cm.astype(BF16), NT_DIMS, preferred_element_type=F32)
        num = jnp.dot(sg.astype(BF16), vb, preferred_element_type=F32) + aw * qc
        nrow = n_ref[0, h:h + 1, :]
        den = jnp.sum(sg, -1, keepdims=True) + aw * jnp.sum(q * nrow, -1, keepdims=True)
        hh = num / jnp.maximum(jnp.abs(den), jnp.exp(-mt))
        m_last = mt[L - 1:L, :]
        b_last = b_c[L - 1:L, :]
        decay = jnp.exp(b_last + m_prev - m_last)
        ws = jnp.exp(b_last - b_c + i_c - m_last)
        kw = ws * k
        c_ref[0, h] = decay * cm + lax.dot_general(vb, kw.astype(BF16), TN_DIMS, preferred_element_type=F32)
        n_ref[0, h:h + 1, :] = decay * nrow + jnp.sum(kw, axis=0, keepdims=True)
        m_new = jnp.where(lane1 == h, m_last, m_new)
        hm_ref[:, h * dv:(h + 1) * dv] = _rms(hh, ghead_ref[:, h * dv:(h + 1) * dv]).astype(BF16)
    m_ref[0] = m_new


def _mlstm(p, gates, c0, n0, m0, conv0, w_conv, b_conv, gbias, g_head, nb, L):
    n = p.shape[0]
    t_seq = n // nb
    nc = t_seq // L
    _, H, dv, dqk = c0.shape
    qkw = 2 * H * dqk
    vw = H * dv
    kern = functools.partial(_mlstm_kernel, L=L, H=H, dqk=dqk, dv=dv)
    return pl.pallas_call(
        kern,
        grid=(nb, nc),
        in_specs=[
            pl.BlockSpec((L, qkw), lambda b, c: (b * nc + c, 0)),
            pl.BlockSpec((L, vw), lambda b, c: (b * nc + c, 1)),
            pl.BlockSpec((L, LANES), lambda b, c: (b * nc + c, 0)),
            pl.BlockSpec((1, H, dv, dqk), lambda b, c: (b, 0, 0, 0)),
            pl.BlockSpec((1, H, dqk), lambda b, c: (b, 0, 0)),
            pl.BlockSpec((1, 1, LANES), lambda b, c: (b, 0, 0)),
            pl.BlockSpec((1, CONV_W - 1, qkw), lambda b, c: (b, 0, 0)),
            pl.BlockSpec((CONV_W, qkw), lambda b, c: (0, 0)),
            pl.BlockSpec((1, qkw), lambda b, c: (0, 0)),
            pl.BlockSpec((1, LANES), lambda b, c: (0, 0)),
            pl.BlockSpec((1, vw), lambda b, c: (0, 0)),
        ],
        out_specs=[
            pl.BlockSpec((L, vw), lambda b, c: (b * nc + c, 0)),
            pl.BlockSpec((1, H, dv, dqk), lambda b, c: (b, 0, 0, 0)),
            pl.BlockSpec((1, H, dqk), lambda b, c: (b, 0, 0)),
            pl.BlockSpec((1, 1, LANES), lambda b, c: (b, 0, 0)),
            pl.BlockSpec((1, CONV_W - 1, qkw), lambda b, c: (b, 0, 0)),
        ],
        out_shape=[
            jax.ShapeDtypeStruct((n, vw), BF16),
            jax.ShapeDtypeStruct((nb, H, dv, dqk), F32),
            jax.ShapeDtypeStruct((nb, H, dqk), F32),
            jax.ShapeDtypeStruct((nb, 1, LANES), F32),
            jax.ShapeDtypeStruct((nb, CONV_W - 1, qkw), F32),
        ],
        scratch_shapes=[pltpu.VMEM((L + SUBLANES, qkw), F32)],
        compiler_params=_params(("parallel", "arbitrary")),
        name="mlstm",
    )(p, p, gates, c0, n0, m0, conv0, w_conv, b_conv, gbias, g_head)


def _merge_kernel(o_ref, u_ref, vg_ref, ga_ref, gb_ref, hm_ref, x_ref, wsp_ref, bsp_ref, gvn_ref, bvn_ref,
                  wout_ref, gpost_ref, gt1_ref, gpre_ref, sc2_ref, sh2_ref, wr_ref, *rest, tm, gl, keep_v):
    if keep_v:
        x1_ref, h2_ref, lg_ref, vn_ref, sg_scr = rest
    else:
        x1_ref, h2_ref, lg_ref, sg_scr = rest
    groups = wsp_ref.shape[0]
    ch = vg_ref.shape[1] // groups
    vg = _gelu_tanh(vg_ref[...].astype(F32))
    mu = jnp.mean(vg, -1, keepdims=True)
    xc = vg - mu
    vn = xc * lax.rsqrt(jnp.mean(xc * xc, -1, keepdims=True) + EPS) * gvn_ref[...] + bvn_ref[...]
    if keep_v:
        vn_ref[...] = vn
    vnb = vn.astype(BF16)
    for ci in range(tm // gl):
        for g in range(groups):
            blk = jnp.dot(wsp_ref[g], vnb[ci * gl:(ci + 1) * gl, g * ch:(g + 1) * ch],
                          preferred_element_type=F32)
            sg_scr[ci * gl:(ci + 1) * gl, g * ch:(g + 1) * ch] = blk + bsp_ref[:, g:g + 1]
    h_b = _gelu_tanh(u_ref[...].astype(F32)) * sg_scr[...]
    h_a = _sigmoid(o_ref[...].astype(F32)) * hm_ref[...].astype(F32)
    merged = _sigmoid(ga_ref[...].astype(F32)) * h_a + _sigmoid(gb_ref[...].astype(F32)) * h_b
    y = jnp.dot(merged.astype(BF16), wout_ref[...], preferred_element_type=F32)
    x1 = x_ref[...] + gt1_ref[0] * _rms(y, gpost_ref[...])
    x1_ref[...] = x1
    h2 = _rms(x1, gpre_ref[...]) * (1.0 + sc2_ref[0]) + sh2_ref[0]
    h2_ref[...] = _pack_pairs(h2)
    n_exp = lg_ref.shape[0]
    h2_high = h2.astype(BF16)
    h2_low = (h2 - h2_high.astype(F32)).astype(BF16)
    lg = lax.dot_general(wr_ref[...], h2_high, NT_DIMS, preferred_element_type=F32)
    lg_ref[...] = (lg[:n_exp] + lg[n_exp:]
                   + lax.dot_general(wr_ref[:n_exp, :], h2_low, NT_DIMS, preferred_element_type=F32))


def _merge(p, hm, x, wsp, bsp_t, g_vn, b_vn, w_out, g_post, gt1, g_pre, sc2, sh2, wr, tm, t_seq, keep_v):
    n, d = x.shape
    groups, gl, _ = wsp.shape
    n_exp = wr.shape[0] // 2
    tps = max(t_seq // tm, 1)
    row = lambda i: (0, 0)
    pcol = lambda c: pl.BlockSpec((tm, d), lambda i: (i, c))
    tok = pl.BlockSpec((tm, d), lambda i: (i, 0))
    vec = pl.BlockSpec((1, d), row)
    out_specs = [tok, pl.BlockSpec((tm, d // 2), lambda i: (i, 0)), pl.BlockSpec((n_exp, tm), lambda i: (0, i))]
    out_shape = [jax.ShapeDtypeStruct((n, d), F32), jax.ShapeDtypeStruct((n, d // 2), jnp.uint32),
                 jax.ShapeDtypeStruct((n_exp, n), F32)]
    if keep_v:
        out_specs.append(tok)
        out_shape.append(jax.ShapeDtypeStruct((n, d), F32))
    kern = functools.partial(_merge_kernel, tm=tm, gl=gl, keep_v=keep_v)
    return pl.pallas_call(
        kern,
        grid=(n // tm,),
        in_specs=[
            pcol(2), pcol(3), pcol(4), pcol(5), pcol(6), tok, tok,
            pl.BlockSpec((groups, gl, gl), lambda i: (0, 0, 0)),
            pl.BlockSpec((gl, groups), row),
            vec, vec,
            pl.BlockSpec((d, d), row),
            vec,
            _mod_spec(gt1, tm, tps, 1),
            vec,
            _mod_spec(sc2, tm, tps, 1),
            _mod_spec(sh2, tm, tps, 1),
            pl.BlockSpec((2 * n_exp, d), row),
        ],
        out_specs=out_specs,
        out_shape=out_shape,
        scratch_shapes=[pltpu.VMEM((tm, d), F32)],
        compiler_params=_params(("parallel",)),
        name="merge",
    )(p, p, p, p, p, hm, x, wsp, bsp_t, g_vn, b_vn, w_out, g_post, gt1, g_pre, sc2, sh2, wr)


def _select(logits, br_ref):
    n_exp, tm = logits.shape
    per = n_exp // N_GROUPS
    neg = -jnp.inf
    s = _sigmoid(logits)
    sel = s + br_ref[...]
    mem = [sel[r * N_GROUPS:(r + 1) * N_GROUPS, :] for r in range(per)]
    grow = lax.broadcasted_iota(jnp.int32, (N_GROUPS, tm), 0)
    m1 = functools.reduce(jnp.maximum, mem)
    idx1 = functools.reduce(jnp.minimum, [jnp.where(mem[r] == m1, r, per) for r in range(per)])
    m2 = functools.reduce(jnp.maximum, [jnp.where(idx1 == r, neg, mem[r]) for r in range(per)])
    gs = m1 + m2
    gmask = jnp.zeros((N_GROUPS, tm), jnp.bool_)
    for _ in range(TOPK_GROUPS):
        mx = jnp.max(gs, axis=0, keepdims=True)
        gi = jnp.min(jnp.where(gs == mx, grow, N_GROUPS), axis=0, keepdims=True)
        pick = grow == gi
        gmask = jnp.logical_or(gmask, pick)
        gs = jnp.where(pick, neg, gs)
    msk = [jnp.where(gmask, mem[r], neg) for r in range(per)]
    eidx = [grow * per + r for r in range(per)]
    chosen = [jnp.zeros((N_GROUPS, tm), jnp.bool_) for _ in range(per)]
    firsts = []
    for _ in range(TOP_K):
        mx = jnp.max(functools.reduce(jnp.maximum, msk), axis=0, keepdims=True)
        cand = functools.reduce(jnp.minimum, [jnp.where(msk[r] == mx, eidx[r], n_exp) for r in range(per)])
        first = jnp.min(cand, axis=0, keepdims=True)
        firsts.append(first)
        for r in range(per):
            pick = eidx[r] == first
            chosen[r] = jnp.logical_or(chosen[r], pick)
            msk[r] = jnp.where(pick, neg, msk[r])
    wk = [jnp.where(chosen[r], s[r * N_GROUPS:(r + 1) * N_GROUPS, :], 0.0) for r in range(per)]
    denom = jnp.sum(functools.reduce(jnp.add, wk), axis=0, keepdims=True)
    return s, chosen, firsts, eidx, denom


def _shared_expert(hb, ws1_ref, ws3_ref, ws2_ref):
    a = jnp.dot(hb, ws1_ref[...], preferred_element_type=F32)
    g = jnp.dot(hb, ws3_ref[...], preferred_element_type=F32)
    return jnp.dot((_silu(a) * g).astype(BF16), ws2_ref[...], preferred_element_type=F32)


def _moe_kernel(h_ref, x1_ref, lg_ref, br_ref, w1_ref, w3_ref, w2_ref, ws1_ref, ws3_ref, ws2_ref, gpost_ref,
                gt2_ref, out_ref, gt_scr, gates_scr, acc_scr, *, eb):
    j = pl.program_id(1)
    hb = _unpack_pairs(h_ref[...]).astype(BF16)
    n_exp = lg_ref.shape[0]
    per = n_exp // N_GROUPS

    @pl.when(j == 0)
    def _():
        s, chosen, _, _, denom = _select(lg_ref[...], br_ref)
        gt_scr[n_exp:, :] = jnp.zeros((gt_scr.shape[0] - n_exp, hb.shape[0]), F32)
        for r in range(per):
            wk = jnp.where(chosen[r], s[r * N_GROUPS:(r + 1) * N_GROUPS, :], 0.0)
            gt_scr[r * N_GROUPS:(r + 1) * N_GROUPS, :] = wk / denom * ROUTE_SCALE
        gates_scr[...] = gt_scr[...].T
        acc_scr[...] = _shared_expert(hb, ws1_ref, ws3_ref, ws2_ref)

    first = j * eb
    lane0 = lax.rem(first, per) * N_GROUPS + first // per
    gates = pltpu.roll(gates_scr[...], lax.rem(LANES - lane0, LANES), 1)
    acc = acc_scr[...]
    for e in range(eb):
        a = jnp.dot(hb, w1_ref[e].astype(BF16), preferred_element_type=F32)
        g = jnp.dot(hb, w3_ref[e].astype(BF16), preferred_element_type=F32)
        gate = gates[:, e * N_GROUPS:e * N_GROUPS + 1]
        acc = acc + jnp.dot((_silu(a) * g * gate).astype(BF16), w2_ref[e].astype(BF16),
                            preferred_element_type=F32)
    acc_scr[...] = acc

    @pl.when(j == pl.num_programs(1) - 1)
    def _():
        out_ref[...] = x1_ref[...] + gt2_ref[0] * _rms(acc_scr[...], gpost_ref[...])


def _moe_dense(h2, x1, logits, lw, gt2, tm, t_seq, eb):
    n, d = x1.shape
    _, n_exp, _, f = lw["w1"].shape
    layer = lw["layer"]
    fs = lw["ws1"].shape[1]
    tps = max(t_seq // tm, 1)
    const = lambda i, j: (0, 0)
    return pl.pallas_call(
        functools.partial(_moe_kernel, eb=eb),
        grid=(n // tm, n_exp // eb),
        in_specs=[
            pl.BlockSpec((tm, d // 2), lambda i, j: (i, 0)),
            pl.BlockSpec((tm, d), lambda i, j: (i, 0)),
            pl.BlockSpec((n_exp, tm), lambda i, j: (0, i)),
            pl.BlockSpec((n_exp, 1), const),
            pl.BlockSpec((None, eb, d, f), lambda i, j: (layer, j, 0, 0)),
            pl.BlockSpec((None, eb, d, f), lambda i, j: (layer, j, 0, 0)),
            pl.BlockSpec((None, eb, f, d), lambda i, j: (layer, j, 0, 0)),
            pl.BlockSpec((d, fs), const),
            pl.BlockSpec((d, fs), const),
            pl.BlockSpec((fs, d), const),
            pl.BlockSpec((1, d), const),
            _mod_spec(gt2, tm, tps, 2),
        ],
        out_specs=pl.BlockSpec((tm, d), lambda i, j: (i, 0)),
        out_shape=jax.ShapeDtypeStruct((n, d), F32),
        scratch_shapes=[
            pltpu.VMEM((LANES, tm), F32),
            pltpu.VMEM((tm, LANES), F32),
            pltpu.VMEM((tm, d), F32),
        ],
        compiler_params=_params(("parallel", "arbitrary")),
        name="moe_dense",
    )(h2, x1, logits, lw["br"], lw["w1"], lw["w3"], lw["w2"], lw["ws1"], lw["ws3"], lw["ws2"],
      lw["g_post_ffn"], gt2)


def _router_kernel(lg_ref, br_ref, eidx_ref, rank_ref, gate_ref, cnt_ref, carry_scr):
    n_exp, tm = lg_ref.shape
    per = n_exp // N_GROUPS

    @pl.when(pl.program_id(0) == 0)
    def _():
        carry_scr[...] = jnp.zeros(carry_scr.shape, F32)

    s, chosen, firsts, eidx, denom = _select(lg_ref[...], br_ref)
    sel01 = jnp.concatenate([c.astype(F32) for c in chosen], axis=0)
    before = (lax.broadcasted_iota(jnp.int32, (tm, tm), 0)
              < lax.broadcasted_iota(jnp.int32, (tm, tm), 1)).astype(BF16)
    rank = jnp.dot(sel01.astype(BF16), before, preferred_element_type=F32) + carry_scr[:, 0:1]
    carry_scr[...] = carry_scr[...] + jnp.sum(sel01, axis=1, keepdims=True)
    cnt_ref[...] = carry_scr[...]
    for k in range(TOP_K):
        s_k = jnp.zeros((1, tm), F32)
        r_k = jnp.zeros((1, tm), F32)
        for r in range(per):
            pick = eidx[r] == firsts[k]
            rows = slice(r * N_GROUPS, (r + 1) * N_GROUPS)
            s_k = s_k + jnp.sum(jnp.where(pick, s[rows, :], 0.0), axis=0, keepdims=True)
            r_k = r_k + jnp.sum(jnp.where(pick, rank[rows, :], 0.0), axis=0, keepdims=True)
        eidx_ref[k:k + 1, :] = firsts[k]
        rank_ref[k:k + 1, :] = r_k.astype(jnp.int32)
        gate_ref[k:k + 1, :] = s_k / denom * ROUTE_SCALE


def _router(logits, br, tm, row0, n):
    n_exp = logits.shape[0]
    tok = pl.BlockSpec((TOP_K, tm), lambda i: (0, i))
    blk0 = row0 // tm
    return pl.pallas_call(
        _router_kernel,
        grid=(n // tm,),
        in_specs=[
            pl.BlockSpec((n_exp, tm), lambda i: (0, i + blk0)),
            pl.BlockSpec((n_exp, 1), lambda i: (0, 0)),
        ],
        out_specs=[tok, tok, tok, pl.BlockSpec((n_exp, LANES), lambda i: (0, 0))],
        out_shape=[
            jax.ShapeDtypeStruct((TOP_K, n), jnp.int32),
            jax.ShapeDtypeStruct((TOP_K, n), jnp.int32),
            jax.ShapeDtypeStruct((TOP_K, n), F32),
            jax.ShapeDtypeStruct((n_exp, LANES), F32),
        ],
        scratch_shapes=[pltpu.VMEM((n_exp, LANES), F32)],
        compiler_params=_params(("arbitrary",)),
        name="router",
    )(logits, br)


SC_BATCH = 128
SC_WORKERS = 32


def _sc_mesh():
    return plsc.VectorSubcoreMesh(core_axis_name="c", subcore_axis_name="s")


def _sc_scatter(x, pos, n_rows, row0):
    w = x.shape[1]
    n = pos.shape[0] * SC_BATCH
    info = plsc.get_sparse_core_info()
    nc, nw = info.num_cores, info.num_cores * info.num_subcores
    steps = n // (nw * SC_BATCH)

    @functools.partial(
        pl.kernel, mesh=_sc_mesh(),
        out_type=jax.ShapeDtypeStruct((n_rows, w), x.dtype),
        scratch_types=[pltpu.VMEM((TOP_K, SC_BATCH), jnp.int32), pltpu.VMEM((SC_BATCH, w), x.dtype),
                       pltpu.SemaphoreType.DMA],
    )
    def scatter(x_hbm, pos_hbm, out_hbm, idx_v, rows_v, sem):
        wid = lax.axis_index("s") * nc + lax.axis_index("c")

        @pl.loop(0, steps)
        def _(s):
            blk = wid * steps + s
            pltpu.sync_copy(pos_hbm.at[blk], idx_v)
            pltpu.sync_copy(x_hbm.at[pl.ds(row0 + blk * SC_BATCH, SC_BATCH)], rows_v)
            copies = [pltpu.async_copy(rows_v, out_hbm.at[idx_v.at[k]], sem) for k in range(TOP_K)]
            for c in copies:
                c.wait()

    return scatter(x, pos)


def _sc_gather(y, pos):
    w = y.shape[1]
    n = pos.shape[0] * SC_BATCH
    info = plsc.get_sparse_core_info()
    nc, nw = info.num_cores, info.num_cores * info.num_subcores
    steps = n // (nw * SC_BATCH)

    @functools.partial(
        pl.kernel, mesh=_sc_mesh(),
        out_type=jax.ShapeDtypeStruct((TOP_K, n, w), y.dtype),
        scratch_types=[pltpu.VMEM((TOP_K, SC_BATCH), jnp.int32), pltpu.VMEM((SC_BATCH, w), y.dtype),
                       pltpu.SemaphoreType.DMA],
    )
    def gather(y_hbm, pos_hbm, out_hbm, idx_v, rows_v, sem):
        wid = lax.axis_index("s") * nc + lax.axis_index("c")

        @pl.loop(0, steps)
        def _(s):
            blk = wid * steps + s
            pltpu.sync_copy(pos_hbm.at[blk], idx_v)
            for k in range(TOP_K):
                pltpu.async_copy(y_hbm.at[idx_v.at[k]], rows_v, sem).wait()
                pltpu.sync_copy(rows_v, out_hbm.at[k, pl.ds(blk * SC_BATCH, SC_BATCH)])

    return gather(y, pos)


def _experts_kernel(te_ref, nu_ref, xs_ref, w1_ref, w3_ref, w2_ref, ys_ref, w1_scr, w3_scr, w2_scr):
    i = pl.program_id(0)

    @pl.when(jnp.logical_or(i == 0, te_ref[i] != te_ref[jnp.maximum(i - 1, 0)]))
    def _():
        w1_scr[...] = w1_ref[...].astype(BF16)
        w3_scr[...] = w3_ref[...].astype(BF16)
        w2_scr[...] = w2_ref[...].astype(BF16)

    @pl.when(i < nu_ref[0])
    def _():
        x = _unpack_pairs(xs_ref[...]).astype(BF16)
        a = jnp.dot(x, w1_scr[...], preferred_element_type=F32)
        g = jnp.dot(x, w3_scr[...], preferred_element_type=F32)
        y = jnp.dot((_silu(a) * g).astype(BF16), w2_scr[...], preferred_element_type=F32)
        ys_ref[...] = _pack_pairs(y)


def _experts(xs, tile_expert, n_used, w1, w3, w2, layer, rt):
    rows, half = xs.shape
    _, n_exp, d, f = w1.shape
    scratch = [pltpu.VMEM((d, f), BF16), pltpu.VMEM((d, f), BF16), pltpu.VMEM((f, d), BF16)]
    grid_spec = pltpu.PrefetchScalarGridSpec(
        num_scalar_prefetch=2,
        grid=(rows // rt,),
        in_specs=[
            pl.BlockSpec((rt, half), lambda i, te, nu: (i, 0)),
            pl.BlockSpec((None, None, d, f), lambda i, te, nu: (layer, te[i], 0, 0)),
            pl.BlockSpec((None, None, d, f), lambda i, te, nu: (layer, te[i], 0, 0)),
            pl.BlockSpec((None, None, f, d), lambda i, te, nu: (layer, te[i], 0, 0)),
        ],
        out_specs=pl.BlockSpec((rt, half), lambda i, te, nu: (i, 0)),
        scratch_shapes=scratch,
    )
    return pl.pallas_call(
        _experts_kernel,
        grid_spec=grid_spec,
        out_shape=jax.ShapeDtypeStruct((rows, half), jnp.uint32),
        compiler_params=_params(("arbitrary",)),
        name="experts",
    )(tile_expert, n_used, xs, w1, w3, w2)


def _combine_kernel(yk_ref, g_ref, h_ref, x1_ref, ws1_ref, ws3_ref, ws2_ref, gpost_ref, gt2_ref, *rest):
    out_ref = rest[-1]
    hb = _unpack_pairs(h_ref[...]).astype(BF16)
    acc = _shared_expert(hb, ws1_ref, ws3_ref, ws2_ref)
    gates = g_ref[...]
    for k in range(TOP_K):
        acc = acc + gates[:, k:k + 1] * _unpack_pairs(yk_ref[k])
    out_ref[...] = x1_ref[...] + gt2_ref[0] * _rms(acc, gpost_ref[...])


def _combine(yk, gates_t, h2, x1, lw, gt2, tm, t_seq, row0, out_prev):
    n, d = x1.shape
    nc = yk.shape[1]
    fs = lw["ws1"].shape[1]
    tps = max(t_seq // tm, 1)
    blk0 = row0 // tm
    const = lambda i: (0, 0)
    in_specs = [
        pl.BlockSpec((TOP_K, tm, d // 2), lambda i: (0, i, 0)),
        pl.BlockSpec((tm, TOP_K), lambda i: (i, 0)),
        pl.BlockSpec((tm, d // 2), lambda i: (i + blk0, 0)),
        pl.BlockSpec((tm, d), lambda i: (i + blk0, 0)),
        pl.BlockSpec((d, fs), const),
        pl.BlockSpec((d, fs), const),
        pl.BlockSpec((fs, d), const),
        pl.BlockSpec((1, d), const),
        pl.BlockSpec((1, 1, d), lambda i: ((i + blk0) // tps, 0, 0)),
    ]
    args = [yk, gates_t, h2, x1, lw["ws1"], lw["ws3"], lw["ws2"], lw["g_post_ffn"], gt2]
    aliases = {}
    if out_prev is not None:
        in_specs.append(pl.BlockSpec(memory_space=pl.ANY))
        aliases = {len(args): 0}
        args.append(out_prev)
    return pl.pallas_call(
        _combine_kernel,
        grid=(nc // tm,),
        in_specs=in_specs,
        out_specs=pl.BlockSpec((tm, d), lambda i: (i + blk0, 0)),
        out_shape=jax.ShapeDtypeStruct((n, d), F32),
        input_output_aliases=aliases,
        compiler_params=_params(("parallel",)),
        name="combine",
    )(*args)


EXPERT_ROWS = 512
MOE_CHUNKS = 2


def _moe_dispatched(h2, x1, logits, lw, gt2, t_seq):
    n, d = x1.shape
    n_exp = logits.shape[0]
    per = n_exp // N_GROUPS
    rt = EXPERT_ROWS
    chunks = MOE_CHUNKS
    while n % (chunks * SC_WORKERS * SC_BATCH):
        chunks //= 2
    nc = n // chunks
    n_tiles = nc * TOP_K // rt + n_exp
    tm = _tile(min(nc, t_seq), 512)
    routed = [_router(logits, lw["br"], _tile(nc, 1024), c * nc, nc) for c in range(chunks)]
    out = None
    for c, (eidx, rank, gate, cnt) in enumerate(routed):
        counts = cnt[:, 0].astype(jnp.int32).reshape(per, N_GROUPS).T.reshape(n_exp)
        padded = (counts + rt - 1) // rt * rt
        ends = jnp.cumsum(padded)
        starts = ends - padded
        base = jnp.zeros_like(eidx)
        for e in range(n_exp):
            base = base + jnp.where(eidx == e, starts[e], 0)
        pos = (base + rank).reshape(TOP_K, nc // SC_BATCH, SC_BATCH).transpose(1, 0, 2)
        tile_start = jnp.arange(n_tiles, dtype=jnp.int32) * rt
        tile_expert = jnp.minimum(jnp.sum(ends[None, :] <= tile_start[:, None], axis=1), n_exp - 1).astype(jnp.int32)
        n_used = (ends[-1:] // rt).astype(jnp.int32)
        xs = _sc_scatter(h2, pos, n_tiles * rt, c * nc)
        ys = _experts(xs, tile_expert, n_used, lw["w1"], lw["w3"], lw["w2"], lw["layer"], rt)
        yk = _sc_gather(ys, pos)
        out = _combine(yk, gate.T, h2, x1, lw, gt2, tm, t_seq, c * nc, out)
    return out


def _split_router(w_router, n_exp, d):
    wrt = w_router.T.reshape(N_GROUPS, n_exp // N_GROUPS, d).transpose(1, 0, 2).reshape(n_exp, d)
    high = wrt.astype(BF16)
    return jnp.concatenate([high, (wrt - high.astype(F32)).astype(BF16)], axis=0)


def _prep_layer(l, w):
    d = w["w_in"].shape[1]
    n_exp = w["w_e1"].shape[1]
    H = N_HEADS
    w_in = w["w_in"][l]
    qkw = d
    g0 = 3 * d
    w_main = jnp.concatenate([w_in[:, :g0], w_in[:, g0 + 2 * H:]], axis=1).astype(BF16)
    wg = jnp.pad(w_in[:, g0:g0 + 2 * H], ((0, 0), (0, LANES - 2 * H)))
    wg_high = wg.astype(BF16)
    w_gate = jnp.concatenate([wg_high, (wg - wg_high.astype(F32)).astype(BF16)], axis=1)
    gbias = jnp.pad(jnp.concatenate([w["b_igate"][l], w["b_fgate"][l]]), (0, LANES - 2 * H)).reshape(1, LANES)
    return dict(
        w_main=w_main, w_gate=w_gate, gbias=gbias,
        g_pre_mix=w["g_pre_mix"][l].reshape(1, d), g_post_mix=w["g_post_mix"][l].reshape(1, d),
        g_pre_ffn=w["g_pre_ffn"][l].reshape(1, d), g_post_ffn=w["g_post_ffn"][l].reshape(1, d),
        w_conv=w["w_conv"][l], b_conv=w["b_conv"][l].reshape(1, qkw),
        g_head=w["g_head"][l].reshape(1, d), g_vnorm=w["g_vnorm"][l].reshape(1, d),
        b_vnorm=w["b_vnorm"][l].reshape(1, d),
        w_spatial=w["w_spatial"][l], b_spatial=w["b_spatial"][l],
        w_out=w["w_out"][l].astype(BF16),
        wr=_split_router(w["w_router"][l], n_exp, d),
        br=w["b_router"][l].reshape(N_GROUPS, n_exp // N_GROUPS).T.reshape(n_exp, 1),
        w1=w["w_e1"], w3=w["w_e3"], w2=w["w_e2"], layer=l,
        ws1=w["w_s1"][l].astype(BF16), ws3=w["w_s3"][l].astype(BF16), ws2=w["w_s2"][l].astype(BF16),
    )


def _spatial_weights(lw, start, gl):
    idx = jnp.arange(GMLP_CHUNK)
    mask = (idx[None, :] // CHUNK) <= (idx[:, None] // CHUNK)
    wsp = jnp.where(mask, lw["w_spatial"], 0.0)[:, start:start + gl, start:start + gl].astype(BF16)
    bsp_t = lw["b_spatial"][:, start:start + gl].T
    return wsp, bsp_t


def _tile(n, cap):
    t = cap
    while n % t:
        t //= 2
    return t


def _trunk(x, mods, c0, n0, m0, conv0, layers, mlstm_l, g_start, g_l, keep_v):
    nb, t_seq, d = x.shape
    n = nb * t_seq
    per_token = mods[0][0].shape[1] != 1
    tok_cap = n if per_token else t_seq
    tm_in = _tile(tok_cap, 1024)
    tm_mg = _tile(tok_cap, 512)
    tm_moe = _tile(tok_cap, 1024)
    xf = x.reshape(n, d)
    outs = [[] for _ in range(5 if keep_v else 4)]
    for l, lw in enumerate(layers):
        sh1, sc1, gt1, sh2, sc2, gt2 = mods[l]
        p, gates = _inproj(xf, sc1, sh1, lw["g_pre_mix"], lw["w_main"], lw["w_gate"], tm_in, t_seq)
        m0p = jnp.pad(m0[l], ((0, 0), (0, LANES - m0[l].shape[1]))).reshape(nb, 1, LANES)
        hm, c_new, n_new, m_new, conv_new = _mlstm(
            p, gates, c0[l], n0[l], m0p, conv0[l], lw["w_conv"], lw["b_conv"], lw["gbias"], lw["g_head"],
            nb, mlstm_l)
        wsp, bsp_t = _spatial_weights(lw, g_start, g_l)
        res = _merge(p, hm, xf, wsp, bsp_t, lw["g_vnorm"], lw["b_vnorm"], lw["w_out"], lw["g_post_mix"], gt1,
                     lw["g_pre_ffn"], sc2, sh2, lw["wr"], tm_mg, t_seq, keep_v)
        x1, h2, logits = res[0], res[1], res[2]
        if n % (SC_WORKERS * SC_BATCH) == 0 and not per_token:
            xf = _moe_dispatched(h2, x1, logits, lw, gt2, t_seq)
        else:
            xf = _moe_dense(h2, x1, logits, lw, gt2, tm_moe, t_seq, 4)
        vals = [c_new, n_new, m_new[:, 0, :N_HEADS], conv_new]
        if keep_v:
            vals.append(res[3].reshape(nb, t_seq, d))
        for lst, val in zip(outs, vals):
            lst.append(val)
    return xf.reshape(nb, t_seq, d), tuple(jnp.stack(lst) for lst in outs)


def kernel(x_prompt, x_sample, c_prompt, c_sample, state_mlstm_C, state_mlstm_n, state_mlstm_m, state_conv,
           w_ada, b_ada, g_pre_mix, g_post_mix, g_pre_ffn, g_post_ffn, w_in, b_igate, b_fgate, w_conv, b_conv,
           g_head, g_vnorm, b_vnorm, w_spatial, b_spatial, w_out, w_router, b_router, w_e1, w_e3, w_e2,
           w_s1, w_s3, w_s2):
    w = dict(w_in=w_in, b_igate=b_igate, b_fgate=b_fgate, g_pre_mix=g_pre_mix, g_post_mix=g_post_mix,
             g_pre_ffn=g_pre_ffn, g_post_ffn=g_post_ffn, w_conv=w_conv, b_conv=b_conv, g_head=g_head,
             g_vnorm=g_vnorm, b_vnorm=b_vnorm, w_spatial=w_spatial, b_spatial=b_spatial, w_out=w_out,
             w_router=w_router, b_router=b_router, w_e1=w_e1, w_e3=w_e3, w_e2=w_e2, w_s1=w_s1, w_s3=w_s3,
             w_s2=w_s2)
    depth = w_in.shape[0]
    bp, tp, d = x_prompt.shape
    bs, ts, _ = x_sample.shape
    H, dv, dqk = state_mlstm_C.shape[2:]
    qkw = state_conv.shape[-1]
    layers = [_prep_layer(l, w) for l in range(depth)]

    mod = _ada(jnp.concatenate([c_prompt, c_sample], axis=0), w_ada, b_ada)
    mods_p, mods_s = [], []
    for l in range(depth):
        parts = [mod[l][:, i * d:(i + 1) * d] for i in range(6)]
        mods_p.append([a[:bp].reshape(bp, 1, d) for a in parts])
        mods_s.append([jnp.repeat(a[bp:], ts, axis=0).reshape(1, bs * ts, d) for a in parts])

    zc = jnp.zeros((depth, bp, H, dv, dqk), F32)
    zn = jnp.zeros((depth, bp, H, dqk), F32)
    zm = jnp.zeros((depth, bp, H), F32)
    zconv = jnp.zeros((depth, bp, CONV_W - 1, qkw), F32)
    lp = 256 if tp % 256 == 0 else CHUNK
    y_p, (p_c, p_n, p_m, p_conv) = _trunk(x_prompt, mods_p, zc, zn, zm, zconv, layers, lp, 0, GMLP_CHUNK, False)
    g_start = PAST_LEN % GMLP_CHUNK
    y_s, (s_c, s_n, s_m, s_conv, s_v) = _trunk(x_sample, mods_s, state_mlstm_C, state_mlstm_n, state_mlstm_m,
                                               state_conv, layers, ts, g_start, ts, True)
    return (y_p, y_s, p_c, p_n, p_m, p_conv, s_c, s_n, s_m, s_conv, s_v)
```

```python
import functools

import jax
import jax.numpy as jnp
from jax import lax
from jax.experimental import pallas as pl
from jax.experimental.pallas import tpu as pltpu
from jax.experimental.pallas import tpu_sc as plsc

F32 = jnp.float32
BF16 = jnp.bfloat16

EPS = 1e-6
N_HEADS = 4
CONV_W = 4
GMLP_GROUPS = 8
GMLP_CHUNK = 128
CHUNK = 64
N_GROUPS = 8
TOPK_GROUPS = 4
TOP_K = 8
ROUTE_SCALE = 2.5
PAST_LEN = 4096

LANES = 128
SUBLANES = 8
VMEM_LIMIT = 56 * 1024 * 1024

NT_DIMS = (((1,), (1,)), ((), ()))
TN_DIMS = (((0,), (0,)), ((), ()))


def _sigmoid(x):
    return 0.5 * (jnp.tanh(0.5 * x) + 1.0)


def _silu(x):
    return x * _sigmoid(x)


def _gelu_tanh(x):
    return x * (0.5 * (1.0 + jnp.tanh(0.7978845608028654 * (x + 0.044715 * (x * x * x)))))


def _rms(x, g):
    return x * lax.rsqrt(jnp.mean(x * x, -1, keepdims=True) + EPS) * g


def _pack_pairs(x):
    c = x.shape[1] // 2
    return pltpu.pack_elementwise([x[:, :c], x[:, c:]], packed_dtype=BF16)


def _unpack_pairs(u):
    lo = pltpu.unpack_elementwise(u, index=0, packed_dtype=BF16, unpacked_dtype=F32)
    hi = pltpu.unpack_elementwise(u, index=1, packed_dtype=BF16, unpacked_dtype=F32)
    return jnp.concatenate([lo, hi], axis=1)


def _params(sem):
    return pltpu.CompilerParams(dimension_semantics=sem, vmem_limit_bytes=VMEM_LIMIT)


def _ada_kernel(c_ref, w_ref, b_ref, o_ref):
    a = _silu(c_ref[...]).astype(BF16)
    o_ref[0] = jnp.dot(a, w_ref[0].astype(BF16), preferred_element_type=F32) + b_ref[0]


def _ada(c, w_ada, b_ada):
    depth, d, six_d = w_ada.shape
    nb = c.shape[0]
    tn = d
    return pl.pallas_call(
        _ada_kernel,
        grid=(depth, six_d // tn),
        in_specs=[
            pl.BlockSpec((nb, d), lambda l, j: (0, 0)),
            pl.BlockSpec((1, d, tn), lambda l, j: (l, 0, j)),
            pl.BlockSpec((1, 1, tn), lambda l, j: (l, 0, j)),
        ],
        out_specs=pl.BlockSpec((1, nb, tn), lambda l, j: (l, 0, j)),
        out_shape=jax.ShapeDtypeStruct((depth, nb, six_d), F32),
        compiler_params=_params(("parallel", "parallel")),
        name="adaln",
    )(c, w_ada, b_ada.reshape(depth, 1, six_d))


def _inproj_kernel(x_ref, sc_ref, sh_ref, g_ref, w_ref, wg_ref, p_ref, gates_ref, h_scr):
    @pl.when(pl.program_id(1) == 0)
    def _():
        h = _rms(x_ref[...], g_ref[...]) * (1.0 + sc_ref[0]) + sh_ref[0]
        hb = h.astype(BF16)
        h_scr[...] = hb
        h_low = (h - hb.astype(F32)).astype(BF16)
        gg = jnp.dot(hb, wg_ref[...], preferred_element_type=F32)
        gates_ref[...] = (gg[:, :LANES] + gg[:, LANES:]
                          + jnp.dot(h_low, wg_ref[:, :LANES], preferred_element_type=F32))

    p_ref[...] = jnp.dot(h_scr[...], w_ref[...], preferred_element_type=F32).astype(BF16)


def _mod_spec(mod, tm, tiles_per_seq, ngrid):
    d = mod.shape[-1]
    if mod.shape[1] == 1:
        if ngrid == 2:
            return pl.BlockSpec((1, 1, d), lambda i, j: (i // tiles_per_seq, 0, 0))
        return pl.BlockSpec((1, 1, d), lambda i: (i // tiles_per_seq, 0, 0))
    if ngrid == 2:
        return pl.BlockSpec((1, tm, d), lambda i, j: (0, i, 0))
    return pl.BlockSpec((1, tm, d), lambda i: (0, i, 0))


def _inproj(x, sc, sh, g, w_main, w_gate, tm, t_seq):
    n, d = x.shape
    width = w_main.shape[1]
    ncol = 2 if (width // 2) % LANES == 0 else width // d
    wc = width // ncol
    tps = max(t_seq // tm, 1)
    return pl.pallas_call(
        _inproj_kernel,
        grid=(n // tm, ncol),
        in_specs=[
            pl.BlockSpec((tm, d), lambda i, j: (i, 0)),
            _mod_spec(sc, tm, tps, 2),
            _mod_spec(sh, tm, tps, 2),
            pl.BlockSpec((1, d), lambda i, j: (0, 0)),
            pl.BlockSpec((d, wc), lambda i, j: (0, j)),
            pl.BlockSpec((d, 2 * LANES), lambda i, j: (0, 0)),
        ],
        out_specs=[
            pl.BlockSpec((tm, wc), lambda i, j: (i, j)),
            pl.BlockSpec((tm, LANES), lambda i, j: (i, 0)),
        ],
        out_shape=[
            jax.ShapeDtypeStruct((n, width), BF16),
            jax.ShapeDtypeStruct((n, LANES), F32),
        ],
        scratch_shapes=[pltpu.VMEM((tm, d), BF16)],
        compiler_params=_params(("parallel", "arbitrary")),
        name="inproj",
    )(x, sc, sh, g, w_main, w_gate)


def _mlstm_kernel(qk_ref, v_ref, gt_ref, c0_ref, n0_ref, m0_ref, conv0_ref, wconv_ref, bconv_ref,
                  gbias_ref, ghead_ref, hm_ref, c_ref, n_ref, m_ref, conv_ref, xbuf, *, L, H, dqk, dv):
    qkw = 2 * H * dqk
    pad = SUBLANES
    tail0 = pad - (CONV_W - 1)

    @pl.when(pl.program_id(1) == 0)
    def _():
        c_ref[...] = c0_ref[...]
        n_ref[...] = n0_ref[...]
        m_ref[...] = m0_ref[...]
        xbuf[0:pad, :] = jnp.zeros((pad, qkw), F32)
        xbuf[tail0:pad, :] = conv0_ref[0]

    xbuf[pad:pad + L, :] = qk_ref[...].astype(F32)
    w = wconv_ref[...]
    y = xbuf[tail0:tail0 + L, :] * w[0:1, :]
    for j in range(1, CONV_W):
        y = y + xbuf[tail0 + j:tail0 + j + L, :] * w[j:j + 1, :]
    y = y + bconv_ref[...]
    tail = xbuf[L + tail0:L + pad, :]
    conv_ref[0] = tail
    xbuf[tail0:pad, :] = tail
    qk = _silu(y)

    z = gt_ref[...] + gbias_ref[...]
    lf = jnp.minimum(z, 0.0) - jnp.log(1.0 + jnp.exp(-jnp.abs(z)))
    row = lax.broadcasted_iota(jnp.int32, (L, L), 0)
    col = lax.broadcasted_iota(jnp.int32, (L, L), 1)
    tri = row >= col
    hi = lax.Precision.HIGHEST
    b_col = jnp.dot(tri.astype(F32), lf, precision=hi, preferred_element_type=F32)
    eye = (lax.broadcasted_iota(jnp.int32, (LANES, LANES), 0)
           == lax.broadcasted_iota(jnp.int32, (LANES, LANES), 1)).astype(F32)
    b_row = lax.dot_general(eye, b_col, NT_DIMS, precision=hi, preferred_element_type=F32)
    z_row = lax.dot_general(eye, z, NT_DIMS, precision=hi, preferred_element_type=F32)

    scale = dqk ** -0.5
    lane1 = lax.broadcasted_iota(jnp.int32, (1, LANES), 1)
    m_old = m_ref[0]
    m_new = m_old
    for h in range(H):
        q = qk[:, h * dqk:(h + 1) * dqk] * scale
        k = qk[:, (H + h) * dqk:(H + h + 1) * dqk]
        qb = q.astype(BF16)
        vb = v_ref[:, h * dv:(h + 1) * dv]
        b_c = b_col[:, H + h:H + h + 1]
        i_c = z[:, h:h + 1]
        b_r = b_row[H + h:H + h + 1, :]
        i_r = z_row[h:h + 1, :]
        m_prev = m_old[:, h:h + 1]
        a = b_c + m_prev
        dm = jnp.where(tri, b_c - b_r + i_r, -jnp.inf)
        mt = jnp.maximum(a, jnp.max(dm, axis=-1, keepdims=True))
        s = lax.dot_general(qb, k.astype(BF16), NT_DIMS, preferred_element_type=F32)
        sg = s * jnp.exp(dm - mt)
        aw = jnp.exp(a - mt)
        cm = c_ref[0, h]
        qc = lax.dot_general(qb, cm.astype(BF16), NT_DIMS, preferred_element_type=F32)
        num = jnp.dot(sg.astype(BF16), vb, preferred_element_type=F32) + aw * qc
        nrow = n_ref[0, h:h + 1, :]
        den = jnp.sum(sg, -1, keepdims=True) + aw * jnp.sum(q * nrow, -1, keepdims=True)
        hh = num / jnp.maximum(jnp.abs(den), jnp.exp(-mt))
        m_last = mt[L - 1:L, :]
        b_last = b_c[L - 1:L, :]
        decay = jnp.exp(b_last + m_prev - m_last)
        ws = jnp.exp(b_last - b_c + i_c - m_last)
        kw = ws * k
        c_ref[0, h] = decay * cm + lax.dot_general(vb, kw.astype(BF16), TN_DIMS, preferred_element_type=F32)
        n_ref[0, h:h + 1, :] = decay * nrow + jnp.sum(kw, axis=0, keepdims=True)
        m_new = jnp.where(lane1 == h, m_last, m_new)
        hm_ref[:, h * dv:(h + 1) * dv] = _rms(hh, ghead_ref[:, h * dv:(h + 1) * dv]).astype(BF16)
    m_ref[0] = m_new


def _mlstm(p, gates, c0, n0, m0, conv0, w_conv, b_conv, gbias, g_head, nb, L):
    n = p.shape[0]
    t_seq = n // nb
    nc = t_seq // L
    _, H, dv, dqk = c0.shape
    qkw = 2 * H * dqk
    vw = H * dv
    kern = functools.partial(_mlstm_kernel, L=L, H=H, dqk=dqk, dv=dv)
    return pl.pallas_call(
        kern,
        grid=(nb, nc),
        in_specs=[
            pl.BlockSpec((L, qkw), lambda b, c: (b * nc + c, 0)),
            pl.BlockSpec((L, vw), lambda b, c: (b * nc + c, 1)),
            pl.BlockSpec((L, LANES), lambda b, c: (b * nc + c, 0)),
            pl.BlockSpec((1, H, dv, dqk), lambda b, c: (b, 0, 0, 0)),
            pl.BlockSpec((1, H, dqk), lambda b, c: (b, 0, 0)),
            pl.BlockSpec((1, 1, LANES), lambda b, c: (b, 0, 0)),
            pl.BlockSpec((1, CONV_W - 1, qkw), lambda b, c: (b, 0, 0)),
            pl.BlockSpec((CONV_W, qkw), lambda b, c: (0, 0)),
            pl.BlockSpec((1, qkw), lambda b, c: (0, 0)),
            pl.BlockSpec((1, LANES), lambda b, c: (0, 0)),
            pl.BlockSpec((1, vw), lambda b, c: (0, 0)),
        ],
        out_specs=[
            pl.BlockSpec((L, vw), lambda b, c: (b * nc + c, 0)),
            pl.BlockSpec((1, H, dv, dqk), lambda b, c: (b, 0, 0, 0)),
            pl.BlockSpec((1, H, dqk), lambda b, c: (b, 0, 0)),
            pl.BlockSpec((1, 1, LANES), lambda b, c: (b, 0, 0)),
            pl.BlockSpec((1, CONV_W - 1, qkw), lambda b, c: (b, 0, 0)),
        ],
        out_shape=[
            jax.ShapeDtypeStruct((n, vw), BF16),
            jax.ShapeDtypeStruct((nb, H, dv, dqk), F32),
            jax.ShapeDtypeStruct((nb, H, dqk), F32),
            jax.ShapeDtypeStruct((nb, 1, LANES), F32),
            jax.ShapeDtypeStruct((nb, CONV_W - 1, qkw), F32),
        ],
        scratch_shapes=[pltpu.VMEM((L + SUBLANES, qkw), F32)],
        compiler_params=_params(("parallel", "arbitrary")),
        name="mlstm",
    )(p, p, gates, c0, n0, m0, conv0, w_conv, b_conv, gbias, g_head)


def _merge_kernel(o_ref, u_ref, vg_ref, ga_ref, gb_ref, hm_ref, x_ref, wsp_ref, bsp_ref, gvn_ref, bvn_ref,
                  wout_ref, gpost_ref, gt1_ref, gpre_ref, sc2_ref, sh2_ref, wr_ref, *rest, tm, gl, keep_v):
    if keep_v:
        x1_ref, h2_ref, lg_ref, vn_ref, sg_scr = rest
    else:
        x1_ref, h2_ref, lg_ref, sg_scr = rest
    groups = wsp_ref.shape[0]
    ch = vg_ref.shape[1] // groups
    vg = _gelu_tanh(vg_ref[...].astype(F32))
    mu = jnp.mean(vg, -1, keepdims=True)
    xc = vg - mu
    vn = xc * lax.rsqrt(jnp.mean(xc * xc, -1, keepdims=True) + EPS) * gvn_ref[...] + bvn_ref[...]
    if keep_v:
        vn_ref[...] = vn
    vnb = vn.astype(BF16)
    for ci in range(tm // gl):
        for g in range(groups):
            blk = jnp.dot(wsp_ref[g], vnb[ci * gl:(ci + 1) * gl, g * ch:(g + 1) * ch],
                          preferred_element_type=F32)
            sg_scr[ci * gl:(ci + 1) * gl, g * ch:(g + 1) * ch] = blk + bsp_ref[:, g:g + 1]
    h_b = _gelu_tanh(u_ref[...].astype(F32)) * sg_scr[...]
    h_a = _sigmoid(o_ref[...].astype(F32)) * hm_ref[...].astype(F32)
    merged = _sigmoid(ga_ref[...].astype(F32)) * h_a + _sigmoid(gb_ref[...].astype(F32)) * h_b
    y = jnp.dot(merged.astype(BF16), wout_ref[...], preferred_element_type=F32)
    x1 = x_ref[...] + gt1_ref[0] * _rms(y, gpost_ref[...])
    x1_ref[...] = x1
    h2 = _rms(x1, gpre_ref[...]) * (1.0 + sc2_ref[0]) + sh2_ref[0]
    h2_ref[...] = _pack_pairs(h2)
    n_exp = lg_ref.shape[0]
    h2_high = h2.astype(BF16)
    h2_low = (h2 - h2_high.astype(F32)).astype(BF16)
    lg = lax.dot_general(wr_ref[...], h2_high, NT_DIMS, preferred_element_type=F32)
    lg_ref[...] = (lg[:n_exp] + lg[n_exp:]
                   + lax.dot_general(wr_ref[:n_exp, :], h2_low, NT_DIMS, preferred_element_type=F32))


def _merge(p, hm, x, wsp, bsp_t, g_vn, b_vn, w_out, g_post, gt1, g_pre, sc2, sh2, wr, tm, t_seq, keep_v):
    n, d = x.shape
    groups, gl, _ = wsp.shape
    n_exp = wr.shape[0] // 2
    tps = max(t_seq // tm, 1)
    row = lambda i: (0, 0)
    pcol = lambda c: pl.BlockSpec((tm, d), lambda i: (i, c))
    tok = pl.BlockSpec((tm, d), lambda i: (i, 0))
    vec = pl.BlockSpec((1, d), row)
    out_specs = [tok, pl.BlockSpec((tm, d // 2), lambda i: (i, 0)), pl.BlockSpec((n_exp, tm), lambda i: (0, i))]
    out_shape = [jax.ShapeDtypeStruct((n, d), F32), jax.ShapeDtypeStruct((n, d // 2), jnp.uint32),
                 jax.ShapeDtypeStruct((n_exp, n), F32)]
    if keep_v:
        out_specs.append(tok)
        out_shape.append(jax.ShapeDtypeStruct((n, d), F32))
    kern = functools.partial(_merge_kernel, tm=tm, gl=gl, keep_v=keep_v)
    return pl.pallas_call(
        kern,
        grid=(n // tm,),
        in_specs=[
            pcol(2), pcol(3), pcol(4), pcol(5), pcol(6), tok, tok,
            pl.BlockSpec((groups, gl, gl), lambda i: (0, 0, 0)),
            pl.BlockSpec((gl, groups), row),
            vec, vec,
            pl.BlockSpec((d, d), row),
            vec,
            _mod_spec(gt1, tm, tps, 1),
            vec,
            _mod_spec(sc2, tm, tps, 1),
            _mod_spec(sh2, tm, tps, 1),
            pl.BlockSpec((2 * n_exp, d), row),
        ],
        out_specs=out_specs,
        out_shape=out_shape,
        scratch_shapes=[pltpu.VMEM((tm, d), F32)],
        compiler_params=_params(("parallel",)),
        name="merge",
    )(p, p, p, p, p, hm, x, wsp, bsp_t, g_vn, b_vn, w_out, g_post, gt1, g_pre, sc2, sh2, wr)


def _select(logits, br_ref):
    n_exp, tm = logits.shape
    per = n_exp // N_GROUPS
    neg = -jnp.inf
    s = _sigmoid(logits)
    sel = s + br_ref[...]
    mem = [sel[r * N_GROUPS:(r + 1) * N_GROUPS, :] for r in range(per)]
    grow = lax.broadcasted_iota(jnp.int32, (N_GROUPS, tm), 0)
    m1 = functools.reduce(jnp.maximum, mem)
    idx1 = functools.reduce(jnp.minimum, [jnp.where(mem[r] == m1, r, per) for r in range(per)])
    m2 = functools.reduce(jnp.maximum, [jnp.where(idx1 == r, neg, mem[r]) for r in range(per)])
    gs = m1 + m2
    gmask = jnp.zeros((N_GROUPS, tm), jnp.bool_)
    for _ in range(TOPK_GROUPS):
        mx = jnp.max(gs, axis=0, keepdims=True)
        gi = jnp.min(jnp.where(gs == mx, grow, N_GROUPS), axis=0, keepdims=True)
        pick = grow == gi
        gmask = jnp.logical_or(gmask, pick)
        gs = jnp.where(pick, neg, gs)
    msk = [jnp.where(gmask, mem[r], neg) for r in range(per)]
    eidx = [grow * per + r for r in range(per)]
    chosen = [jnp.zeros((N_GROUPS, tm), jnp.bool_) for _ in range(per)]
    firsts = []
    for _ in range(TOP_K):
        mx = jnp.max(functools.reduce(jnp.maximum, msk), axis=0, keepdims=True)
        cand = functools.reduce(jnp.minimum, [jnp.where(msk[r] == mx, eidx[r], n_exp) for r in range(per)])
        first = jnp.min(cand, axis=0, keepdims=True)
        firsts.append(first)
        for r in range(per):
            pick = eidx[r] == first
            chosen[r] = jnp.logical_or(chosen[r], pick)
            msk[r] = jnp.where(pick, neg, msk[r])
    wk = [jnp.where(chosen[r], s[r * N_GROUPS:(r + 1) * N_GROUPS, :], 0.0) for r in range(per)]
    denom = jnp.sum(functools.reduce(jnp.add, wk), axis=0, keepdims=True)
    return s, chosen, firsts, eidx, denom


def _shared_expert(hb, ws1_ref, ws3_ref, ws2_ref):
    a = jnp.dot(hb, ws1_ref[...], preferred_element_type=F32)
    g = jnp.dot(hb, ws3_ref[...], preferred_element_type=F32)
    return jnp.dot((_silu(a) * g).astype(BF16), ws2_ref[...], preferred_element_type=F32)


def _moe_kernel(h_ref, x1_ref, lg_ref, br_ref, w1_ref, w3_ref, w2_ref, ws1_ref, ws3_ref, ws2_ref, gpost_ref,
                gt2_ref, out_ref, gt_scr, gates_scr, acc_scr, *, eb):
    j = pl.program_id(1)
    hb = _unpack_pairs(h_ref[...]).astype(BF16)
    n_exp = lg_ref.shape[0]
    per = n_exp // N_GROUPS

    @pl.when(j == 0)
    def _():
        s, chosen, _, _, denom = _select(lg_ref[...], br_ref)
        gt_scr[n_exp:, :] = jnp.zeros((gt_scr.shape[0] - n_exp, hb.shape[0]), F32)
        for r in range(per):
            wk = jnp.where(chosen[r], s[r * N_GROUPS:(r + 1) * N_GROUPS, :], 0.0)
            gt_scr[r * N_GROUPS:(r + 1) * N_GROUPS, :] = wk / denom * ROUTE_SCALE
        gates_scr[...] = gt_scr[...].T
        acc_scr[...] = _shared_expert(hb, ws1_ref, ws3_ref, ws2_ref)

    first = j * eb
    lane0 = lax.rem(first, per) * N_GROUPS + first // per
    gates = pltpu.roll(gates_scr[...], lax.rem(LANES - lane0, LANES), 1)
    acc = acc_scr[...]
    for e in range(eb):
        a = jnp.dot(hb, w1_ref[e].astype(BF16), preferred_element_type=F32)
        g = jnp.dot(hb, w3_ref[e].astype(BF16), preferred_element_type=F32)
        gate = gates[:, e * N_GROUPS:e * N_GROUPS + 1]
        acc = acc + jnp.dot((_silu(a) * g * gate).astype(BF16), w2_ref[e].astype(BF16),
                            preferred_element_type=F32)
    acc_scr[...] = acc

    @pl.when(j == pl.num_programs(1) - 1)
    def _():
        out_ref[...] = x1_ref[...] + gt2_ref[0] * _rms(acc_scr[...], gpost_ref[...])


def _moe_dense(h2, x1, logits, lw, gt2, tm, t_seq, eb):
    n, d = x1.shape
    _, n_exp, _, f = lw["w1"].shape
    layer = lw["layer"]
    fs = lw["ws1"].shape[1]
    tps = max(t_seq // tm, 1)
    const = lambda i, j: (0, 0)
    return pl.pallas_call(
        functools.partial(_moe_kernel, eb=eb),
        grid=(n // tm, n_exp // eb),
        in_specs=[
            pl.BlockSpec((tm, d // 2), lambda i, j: (i, 0)),
            pl.BlockSpec((tm, d), lambda i, j: (i, 0)),
            pl.BlockSpec((n_exp, tm), lambda i, j: (0, i)),
            pl.BlockSpec((n_exp, 1), const),
            pl.BlockSpec((None, eb, d, f), lambda i, j: (layer, j, 0, 0)),
            pl.BlockSpec((None, eb, d, f), lambda i, j: (layer, j, 0, 0)),
            pl.BlockSpec((None, eb, f, d), lambda i, j: (layer, j, 0, 0)),
            pl.BlockSpec((d, fs), const),
            pl.BlockSpec((d, fs), const),
            pl.BlockSpec((fs, d), const),
            pl.BlockSpec((1, d), const),
            _mod_spec(gt2, tm, tps, 2),
        ],
        out_specs=pl.BlockSpec((tm, d), lambda i, j: (i, 0)),
        out_shape=jax.ShapeDtypeStruct((n, d), F32),
        scratch_shapes=[
            pltpu.VMEM((LANES, tm), F32),
            pltpu.VMEM((tm, LANES), F32),
            pltpu.VMEM((tm, d), F32),
        ],
        compiler_params=_params(("parallel", "arbitrary")),
        name="moe_dense",
    )(h2, x1, logits, lw["br"], lw["w1"], lw["w3"], lw["w2"], lw["ws1"], lw["ws3"], lw["ws2"],
      lw["g_post_ffn"], gt2)


def _router_kernel(lg_ref, br_ref, eidx_ref, rank_ref, gate_ref, cnt_ref, carry_scr):
    n_exp, tm = lg_ref.shape
    per = n_exp // N_GROUPS

    @pl.when(pl.program_id(0) == 0)
    def _():
        carry_scr[...] = jnp.zeros(carry_scr.shape, F32)

    s, chosen, firsts, eidx, denom = _select(lg_ref[...], br_ref)
    sel01 = jnp.concatenate([c.astype(F32) for c in chosen], axis=0)
    before = (lax.broadcasted_iota(jnp.int32, (tm, tm), 0)
              < lax.broadcasted_iota(jnp.int32, (tm, tm), 1)).astype(BF16)
    rank = jnp.dot(sel01.astype(BF16), before, preferred_element_type=F32) + carry_scr[:, 0:1]
    carry_scr[...] = carry_scr[...] + jnp.sum(sel01, axis=1, keepdims=True)
    cnt_ref[...] = carry_scr[...]
    for k in range(TOP_K):
        s_k = jnp.zeros((1, tm), F32)
        r_k = jnp.zeros((1, tm), F32)
        for r in range(per):
            pick = eidx[r] == firsts[k]
            rows = slice(r * N_GROUPS, (r + 1) * N_GROUPS)
            s_k = s_k + jnp.sum(jnp.where(pick, s[rows, :], 0.0), axis=0, keepdims=True)
            r_k = r_k + jnp.sum(jnp.where(pick, rank[rows, :], 0.0), axis=0, keepdims=True)
        eidx_ref[k:k + 1, :] = firsts[k]
        rank_ref[k:k + 1, :] = r_k.astype(jnp.int32)
        gate_ref[k:k + 1, :] = s_k / denom * ROUTE_SCALE


def _router(logits, br, tm, row0, n):
    n_exp = logits.shape[0]
    tok = pl.BlockSpec((TOP_K, tm), lambda i: (0, i))
    blk0 = row0 // tm
    return pl.pallas_call(
        _router_kernel,
        grid=(n // tm,),
        in_specs=[
            pl.BlockSpec((n_exp, tm), lambda i: (0, i + blk0)),
            pl.BlockSpec((n_exp, 1), lambda i: (0, 0)),
        ],
        out_specs=[tok, tok, tok, pl.BlockSpec((n_exp, LANES), lambda i: (0, 0))],
        out_shape=[
            jax.ShapeDtypeStruct((TOP_K, n), jnp.int32),
            jax.ShapeDtypeStruct((TOP_K, n), jnp.int32),
            jax.ShapeDtypeStruct((TOP_K, n), F32),
            jax.ShapeDtypeStruct((n_exp, LANES), F32),
        ],
        scratch_shapes=[pltpu.VMEM((n_exp, LANES), F32)],
        compiler_params=_params(("arbitrary",)),
        name="router",
    )(logits, br)


SC_BATCH = 128
SC_WORKERS = 32


def _sc_mesh():
    return plsc.VectorSubcoreMesh(core_axis_name="c", subcore_axis_name="s")


def _sc_scatter(x, pos, n_rows, row0):
    w = x.shape[1]
    n = pos.shape[0] * SC_BATCH
    info = plsc.get_sparse_core_info()
    nc, nw = info.num_cores, info.num_cores * info.num_subcores
    steps = n // (nw * SC_BATCH)

    @functools.partial(
        pl.kernel, mesh=_sc_mesh(),
        out_type=jax.ShapeDtypeStruct((n_rows, w), x.dtype),
        scratch_types=[pltpu.VMEM((TOP_K, SC_BATCH), jnp.int32), pltpu.VMEM((SC_BATCH, w), x.dtype),
                       pltpu.SemaphoreType.DMA],
    )
    def scatter(x_hbm, pos_hbm, out_hbm, idx_v, rows_v, sem):
        wid = lax.axis_index("s") * nc + lax.axis_index("c")

        @pl.loop(0, steps)
        def _(s):
            blk = wid * steps + s
            pltpu.sync_copy(pos_hbm.at[blk], idx_v)
            pltpu.sync_copy(x_hbm.at[pl.ds(row0 + blk * SC_BATCH, SC_BATCH)], rows_v)
            copies = [pltpu.async_copy(rows_v, out_hbm.at[idx_v.at[k]], sem) for k in range(TOP_K)]
            for c in copies:
                c.wait()

    return scatter(x, pos)


def _sc_gather(y, pos):
    w = y.shape[1]
    n = pos.shape[0] * SC_BATCH
    info = plsc.get_sparse_core_info()
    nc, nw = info.num_cores, info.num_cores * info.num_subcores
    steps = n // (nw * SC_BATCH)

    @functools.partial(
        pl.kernel, mesh=_sc_mesh(),
        out_type=jax.ShapeDtypeStruct((TOP_K, n, w), y.dtype),
        scratch_types=[pltpu.VMEM((TOP_K, SC_BATCH), jnp.int32), pltpu.VMEM((SC_BATCH, w), y.dtype),
                       pltpu.SemaphoreType.DMA],
    )
    def gather(y_hbm, pos_hbm, out_hbm, idx_v, rows_v, sem):
        wid = lax.axis_index("s") * nc + lax.axis_index("c")

        @pl.loop(0, steps)
        def _(s):
            blk = wid * steps + s
            pltpu.sync_copy(pos_hbm.at[blk], idx_v)
            for k in range(TOP_K):
                pltpu.async_copy(y_hbm.at[idx_v.at[k]], rows_v, sem).wait()
                pltpu.sync_copy(rows_v, out_hbm.at[k, pl.ds(blk * SC_BATCH, SC_BATCH)])

    return gather(y, pos)


def _experts_kernel(te_ref, nu_ref, xs_ref, w1_ref, w3_ref, w2_ref, ys_ref, w1_scr, w3_scr, w2_scr):
    i = pl.program_id(0)

    @pl.when(jnp.logical_or(i == 0, te_ref[i] != te_ref[jnp.maximum(i - 1, 0)]))
    def _():
        w1_scr[...] = w1_ref[...].astype(BF16)
        w3_scr[...] = w3_ref[...].astype(BF16)
        w2_scr[...] = w2_ref[...].astype(BF16)

    @pl.when(i < nu_ref[0])
    def _():
        x = _unpack_pairs(xs_ref[...]).astype(BF16)
        a = jnp.dot(x, w1_scr[...], preferred_element_type=F32)
        g = jnp.dot(x, w3_scr[...], preferred_element_type=F32)
        y = jnp.dot((_silu(a) * g).astype(BF16), w2_scr[...], preferred_element_type=F32)
        ys_ref[...] = _pack_pairs(y)


def _experts(xs, tile_expert, n_used, w1, w3, w2, layer, rt):
    rows, half = xs.shape
    _, n_exp, d, f = w1.shape
    scratch = [pltpu.VMEM((d, f), BF16), pltpu.VMEM((d, f), BF16), pltpu.VMEM((f, d), BF16)]
    grid_spec = pltpu.PrefetchScalarGridSpec(
        num_scalar_prefetch=2,
        grid=(rows // rt,),
        in_specs=[
            pl.BlockSpec((rt, half), lambda i, te, nu: (i, 0)),
            pl.BlockSpec((None, None, d, f), lambda i, te, nu: (layer, te[i], 0, 0)),
            pl.BlockSpec((None, None, d, f), lambda i, te, nu: (layer, te[i], 0, 0)),
            pl.BlockSpec((None, None, f, d), lambda i, te, nu: (layer, te[i], 0, 0)),
        ],
        out_specs=pl.BlockSpec((rt, half), lambda i, te, nu: (i, 0)),
        scratch_shapes=scratch,
    )
    return pl.pallas_call(
        _experts_kernel,
        grid_spec=grid_spec,
        out_shape=jax.ShapeDtypeStruct((rows, half), jnp.uint32),
        compiler_params=_params(("arbitrary",)),
        name="experts",
    )(tile_expert, n_used, xs, w1, w3, w2)


def _combine_kernel(yk_ref, g_ref, h_ref, x1_ref, ws1_ref, ws3_ref, ws2_ref, gpost_ref, gt2_ref, *rest):
    out_ref = rest[-1]
    hb = _unpack_pairs(h_ref[...]).astype(BF16)
    acc = _shared_expert(hb, ws1_ref, ws3_ref, ws2_ref)
    gates = g_ref[...]
    for k in range(TOP_K):
        acc = acc + gates[:, k:k + 1] * _unpack_pairs(yk_ref[k])
    out_ref[...] = x1_ref[...] + gt2_ref[0] * _rms(acc, gpost_ref[...])


def _combine(yk, gates_t, h2, x1, lw, gt2, tm, t_seq, row0, out_prev):
    n, d = x1.shape
    nc = yk.shape[1]
    fs = lw["ws1"].shape[1]
    tps = max(t_seq // tm, 1)
    blk0 = row0 // tm
    const = lambda i: (0, 0)
    in_specs = [
        pl.BlockSpec((TOP_K, tm, d // 2), lambda i: (0, i, 0)),
        pl.BlockSpec((tm, TOP_K), lambda i: (i, 0)),
        pl.BlockSpec((tm, d // 2), lambda i: (i + blk0, 0)),
        pl.BlockSpec((tm, d), lambda i: (i + blk0, 0)),
        pl.BlockSpec((d, fs), const),
        pl.BlockSpec((d, fs), const),
        pl.BlockSpec((fs, d), const),
        pl.BlockSpec((1, d), const),
        pl.BlockSpec((1, 1, d), lambda i: ((i + blk0) // tps, 0, 0)),
    ]
    args = [yk, gates_t, h2, x1, lw["ws1"], lw["ws3"], lw["ws2"], lw["g_post_ffn"], gt2]
    aliases = {}
    if out_prev is not None:
        in_specs.append(pl.BlockSpec(memory_space=pl.ANY))
        aliases = {len(args): 0}
        args.append(out_prev)
    return pl.pallas_call(
        _combine_kernel,
        grid=(nc // tm,),
        in_specs=in_specs,
        out_specs=pl.BlockSpec((tm, d), lambda i: (i + blk0, 0)),
        out_shape=jax.ShapeDtypeStruct((n, d), F32),
        input_output_aliases=aliases,
        compiler_params=_params(("parallel",)),
        name="combine",
    )(*args)


EXPERT_ROWS = 512
MOE_CHUNKS = 1
PROMPT_CHAINS = 2


def _moe_dispatched(h2, x1, logits, lw, gt2, t_seq):
    n, d = x1.shape
    n_exp = logits.shape[0]
    per = n_exp // N_GROUPS
    rt = EXPERT_ROWS
    chunks = MOE_CHUNKS
    while n % (chunks * SC_WORKERS * SC_BATCH):
        chunks //= 2
    nc = n // chunks
    n_tiles = nc * TOP_K // rt + n_exp
    tm = _tile(min(nc, t_seq), 512)
    routed = [_router(logits, lw["br"], _tile(nc, 1024), c * nc, nc) for c in range(chunks)]
    out = None
    for c, (eidx, rank, gate, cnt) in enumerate(routed):
        counts = cnt[:, 0].astype(jnp.int32).reshape(per, N_GROUPS).T.reshape(n_exp)
        padded = (counts + rt - 1) // rt * rt
        ends = jnp.cumsum(padded)
        starts = ends - padded
        base = jnp.zeros_like(eidx)
        for e in range(n_exp):
            base = base + jnp.where(eidx == e, starts[e], 0)
        pos = (base + rank).reshape(TOP_K, nc // SC_BATCH, SC_BATCH).transpose(1, 0, 2)
        tile_start = jnp.arange(n_tiles, dtype=jnp.int32) * rt
        tile_expert = jnp.minimum(jnp.sum(ends[None, :] <= tile_start[:, None], axis=1), n_exp - 1).astype(jnp.int32)
        n_used = (ends[-1:] // rt).astype(jnp.int32)
        xs = _sc_scatter(h2, pos, n_tiles * rt, c * nc)
        ys = _experts(xs, tile_expert, n_used, lw["w1"], lw["w3"], lw["w2"], lw["layer"], rt)
        yk = _sc_gather(ys, pos)
        out = _combine(yk, gate.T, h2, x1, lw, gt2, tm, t_seq, c * nc, out)
    return out


def _split_router(w_router, n_exp, d):
    wrt = w_router.T.reshape(N_GROUPS, n_exp // N_GROUPS, d).transpose(1, 0, 2).reshape(n_exp, d)
    high = wrt.astype(BF16)
    return jnp.concatenate([high, (wrt - high.astype(F32)).astype(BF16)], axis=0)


def _prep_layer(l, w):
    d = w["w_in"].shape[1]
    n_exp = w["w_e1"].shape[1]
    H = N_HEADS
    w_in = w["w_in"][l]
    qkw = d
    g0 = 3 * d
    w_main = jnp.concatenate([w_in[:, :g0], w_in[:, g0 + 2 * H:]], axis=1).astype(BF16)
    wg = jnp.pad(w_in[:, g0:g0 + 2 * H], ((0, 0), (0, LANES - 2 * H)))
    wg_high = wg.astype(BF16)
    w_gate = jnp.concatenate([wg_high, (wg - wg_high.astype(F32)).astype(BF16)], axis=1)
    gbias = jnp.pad(jnp.concatenate([w["b_igate"][l], w["b_fgate"][l]]), (0, LANES - 2 * H)).reshape(1, LANES)
    return dict(
        w_main=w_main, w_gate=w_gate, gbias=gbias,
        g_pre_mix=w["g_pre_mix"][l].reshape(1, d), g_post_mix=w["g_post_mix"][l].reshape(1, d),
        g_pre_ffn=w["g_pre_ffn"][l].reshape(1, d), g_post_ffn=w["g_post_ffn"][l].reshape(1, d),
        w_conv=w["w_conv"][l], b_conv=w["b_conv"][l].reshape(1, qkw),
        g_head=w["g_head"][l].reshape(1, d), g_vnorm=w["g_vnorm"][l].reshape(1, d),
        b_vnorm=w["b_vnorm"][l].reshape(1, d),
        w_spatial=w["w_spatial"][l], b_spatial=w["b_spatial"][l],
        w_out=w["w_out"][l].astype(BF16),
        wr=_split_router(w["w_router"][l], n_exp, d),
        br=w["b_router"][l].reshape(N_GROUPS, n_exp // N_GROUPS).T.reshape(n_exp, 1),
        w1=w["w_e1"], w3=w["w_e3"], w2=w["w_e2"], layer=l,
        ws1=w["w_s1"][l].astype(BF16), ws3=w["w_s3"][l].astype(BF16), ws2=w["w_s2"][l].astype(BF16),
    )


def _spatial_weights(lw, start, gl):
    idx = jnp.arange(GMLP_CHUNK)
    mask = (idx[None, :] // CHUNK) <= (idx[:, None] // CHUNK)
    wsp = jnp.where(mask, lw["w_spatial"], 0.0)[:, start:start + gl, start:start + gl].astype(BF16)
    bsp_t = lw["b_spatial"][:, start:start + gl].T
    return wsp, bsp_t


def _tile(n, cap):
    t = cap
    while n % t:
        t //= 2
    return t


def _trunk(x, mods, c0, n0, m0, conv0, layers, mlstm_l, g_start, g_l, keep_v):
    nb, t_seq, d = x.shape
    n = nb * t_seq
    per_token = mods[0][0].shape[1] != 1
    tok_cap = n if per_token else t_seq
    tm_in = _tile(tok_cap, 1024)
    tm_mg = _tile(tok_cap, 512)
    tm_moe = _tile(tok_cap, 1024)
    xf = x.reshape(n, d)
    outs = [[] for _ in range(5 if keep_v else 4)]
    for l, lw in enumerate(layers):
        sh1, sc1, gt1, sh2, sc2, gt2 = mods[l]
        p, gates = _inproj(xf, sc1, sh1, lw["g_pre_mix"], lw["w_main"], lw["w_gate"], tm_in, t_seq)
        m0p = jnp.pad(m0[l], ((0, 0), (0, LANES - m0[l].shape[1]))).reshape(nb, 1, LANES)
        hm, c_new, n_new, m_new, conv_new = _mlstm(
            p, gates, c0[l], n0[l], m0p, conv0[l], lw["w_conv"], lw["b_conv"], lw["gbias"], lw["g_head"],
            nb, mlstm_l)
        wsp, bsp_t = _spatial_weights(lw, g_start, g_l)
        res = _merge(p, hm, xf, wsp, bsp_t, lw["g_vnorm"], lw["b_vnorm"], lw["w_out"], lw["g_post_mix"], gt1,
                     lw["g_pre_ffn"], sc2, sh2, lw["wr"], tm_mg, t_seq, keep_v)
        x1, h2, logits = res[0], res[1], res[2]
        if n % (SC_WORKERS * SC_BATCH) == 0 and not per_token:
            xf = _moe_dispatched(h2, x1, logits, lw, gt2, t_seq)
        else:
            xf = _moe_dense(h2, x1, logits, lw, gt2, tm_moe, t_seq, 4)
        vals = [c_new, n_new, m_new[:, 0, :N_HEADS], conv_new]
        if keep_v:
            vals.append(res[3].reshape(nb, t_seq, d))
        for lst, val in zip(outs, vals):
            lst.append(val)
    return xf.reshape(nb, t_seq, d), tuple(jnp.stack(lst) for lst in outs)


def kernel(x_prompt, x_sample, c_prompt, c_sample, state_mlstm_C, state_mlstm_n, state_mlstm_m, state_conv,
           w_ada, b_ada, g_pre_mix, g_post_mix, g_pre_ffn, g_post_ffn, w_in, b_igate, b_fgate, w_conv, b_conv,
           g_head, g_vnorm, b_vnorm, w_spatial, b_spatial, w_out, w_router, b_router, w_e1, w_e3, w_e2,
           w_s1, w_s3, w_s2):
    w = dict(w_in=w_in, b_igate=b_igate, b_fgate=b_fgate, g_pre_mix=g_pre_mix, g_post_mix=g_post_mix,
             g_pre_ffn=g_pre_ffn, g_post_ffn=g_post_ffn, w_conv=w_conv, b_conv=b_conv, g_head=g_head,
             g_vnorm=g_vnorm, b_vnorm=b_vnorm, w_spatial=w_spatial, b_spatial=b_spatial, w_out=w_out,
             w_router=w_router, b_router=b_router, w_e1=w_e1, w_e3=w_e3, w_e2=w_e2, w_s1=w_s1, w_s3=w_s3,
             w_s2=w_s2)
    depth = w_in.shape[0]
    bp, tp, d = x_prompt.shape
    bs, ts, _ = x_sample.shape
    H, dv, dqk = state_mlstm_C.shape[2:]
    qkw = state_conv.shape[-1]
    layers = [_prep_layer(l, w) for l in range(depth)]

    mod = _ada(jnp.concatenate([c_prompt, c_sample], axis=0), w_ada, b_ada)
    mods_p, mods_s = [], []
    for l in range(depth):
        parts = [mod[l][:, i * d:(i + 1) * d] for i in range(6)]
        mods_p.append([a[:bp].reshape(bp, 1, d) for a in parts])
        mods_s.append([jnp.repeat(a[bp:], ts, axis=0).reshape(1, bs * ts, d) for a in parts])

    lp = 256 if tp % 256 == 0 else CHUNK
    chains = PROMPT_CHAINS if bp % PROMPT_CHAINS == 0 else 1
    bc = bp // chains
    zc = jnp.zeros((depth, bc, H, dv, dqk), F32)
    zn = jnp.zeros((depth, bc, H, dqk), F32)
    zm = jnp.zeros((depth, bc, H), F32)
    zconv = jnp.zeros((depth, bc, CONV_W - 1, qkw), F32)
    parts = []
    for ci in range(chains):
        rows = slice(ci * bc, (ci + 1) * bc)
        mods_c = [[a[rows] for a in layer_mods] for layer_mods in mods_p]
        parts.append(_trunk(x_prompt[rows], mods_c, zc, zn, zm, zconv, layers, lp, 0, GMLP_CHUNK, False))
    y_p = jnp.concatenate([p[0] for p in parts], axis=0)
    p_c, p_n, p_m, p_conv = (jnp.concatenate([p[1][i] for p in parts], axis=1) for i in range(4))
    g_start = PAST_LEN % GMLP_CHUNK
    y_s, (s_c, s_n, s_m, s_conv, s_v) = _trunk(x_sample, mods_s, state_mlstm_C, state_mlstm_n, state_mlstm_m,
                                               state_conv, layers, ts, g_start, ts, True)
    return (y_p, y_s, p_c, p_n, p_m, p_conv, s_c, s_n, s_m, s_conv, s_v)
```

```python
import functools

import jax
import jax.numpy as jnp
from jax import lax
from jax.experimental import pallas as pl
from jax.experimental.pallas import tpu as pltpu
from jax.experimental.pallas import tpu_sc as plsc

F32 = jnp.float32
BF16 = jnp.bfloat16

EPS = 1e-6
N_HEADS = 4
CONV_W = 4
GMLP_GROUPS = 8
GMLP_CHUNK = 128
CHUNK = 64
N_GROUPS = 8
TOPK_GROUPS = 4
TOP_K = 8
ROUTE_SCALE = 2.5
PAST_LEN = 4096

LANES = 128
SUBLANES = 8
VMEM_LIMIT = 56 * 1024 * 1024

NT_DIMS = (((1,), (1,)), ((), ()))
TN_DIMS = (((0,), (0,)), ((), ()))


def _sigmoid(x):
    return 0.5 * (jnp.tanh(0.5 * x) + 1.0)


def _silu(x):
    return x * _sigmoid(x)


def _gelu_tanh(x):
    return x * (0.5 * (1.0 + jnp.tanh(0.7978845608028654 * (x + 0.044715 * (x * x * x)))))


def _rms(x, g):
    return x * lax.rsqrt(jnp.mean(x * x, -1, keepdims=True) + EPS) * g


def _pack_pairs(x):
    c = x.shape[1] // 2
    return pltpu.pack_elementwise([x[:, :c], x[:, c:]], packed_dtype=BF16)


def _unpack_pairs(u):
    lo = pltpu.unpack_elementwise(u, index=0, packed_dtype=BF16, unpacked_dtype=F32)
    hi = pltpu.unpack_elementwise(u, index=1, packed_dtype=BF16, unpacked_dtype=F32)
    return jnp.concatenate([lo, hi], axis=1)


def _params(sem):
    return pltpu.CompilerParams(dimension_semantics=sem, vmem_limit_bytes=VMEM_LIMIT)


def _ada_kernel(c_ref, w_ref, b_ref, o_ref):
    a = _silu(c_ref[...]).astype(BF16)
    o_ref[0] = jnp.dot(a, w_ref[0].astype(BF16), preferred_element_type=F32) + b_ref[0]


def _ada(c, w_ada, b_ada):
    depth, d, six_d = w_ada.shape
    nb = c.shape[0]
    tn = d
    return pl.pallas_call(
        _ada_kernel,
        grid=(depth, six_d // tn),
        in_specs=[
            pl.BlockSpec((nb, d), lambda l, j: (0, 0)),
            pl.BlockSpec((1, d, tn), lambda l, j: (l, 0, j)),
            pl.BlockSpec((1, 1, tn), lambda l, j: (l, 0, j)),
        ],
        out_specs=pl.BlockSpec((1, nb, tn), lambda l, j: (l, 0, j)),
        out_shape=jax.ShapeDtypeStruct((depth, nb, six_d), F32),
        compiler_params=_params(("parallel", "parallel")),
        name="adaln",
    )(c, w_ada, b_ada.reshape(depth, 1, six_d))


def _inproj_kernel(x_ref, sc_ref, sh_ref, g_ref, w_ref, wg_ref, p_ref, gates_ref, h_scr):
    @pl.when(pl.program_id(1) == 0)
    def _():
        h = _rms(x_ref[...], g_ref[...]) * (1.0 + sc_ref[0]) + sh_ref[0]
        hb = h.astype(BF16)
        h_scr[...] = hb
        h_low = (h - hb.astype(F32)).astype(BF16)
        gg = jnp.dot(hb, wg_ref[...], preferred_element_type=F32)
        gates_ref[...] = (gg[:, :LANES] + gg[:, LANES:]
                          + jnp.dot(h_low, wg_ref[:, :LANES], preferred_element_type=F32))

    p_ref[...] = jnp.dot(h_scr[...], w_ref[...], preferred_element_type=F32).astype(BF16)


def _mod_spec(mod, tm, tiles_per_seq, ngrid):
    d = mod.shape[-1]
    if mod.shape[1] == 1:
        if ngrid == 2:
            return pl.BlockSpec((1, 1, d), lambda i, j: (i // tiles_per_seq, 0, 0))
        return pl.BlockSpec((1, 1, d), lambda i: (i // tiles_per_seq, 0, 0))
    if ngrid == 2:
        return pl.BlockSpec((1, tm, d), lambda i, j: (0, i, 0))
    return pl.BlockSpec((1, tm, d), lambda i: (0, i, 0))


def _inproj(x, sc, sh, g, w_main, w_gate, tm, t_seq, row0, n):
    d = x.shape[1]
    blk0 = row0 // tm
    width = w_main.shape[1]
    ncol = 2 if (width // 2) % LANES == 0 else width // d
    wc = width // ncol
    tps = max(t_seq // tm, 1)
    return pl.pallas_call(
        _inproj_kernel,
        grid=(n // tm, ncol),
        in_specs=[
            pl.BlockSpec((tm, d), lambda i, j: (i + blk0, 0)),
            _mod_spec(sc, tm, tps, 2),
            _mod_spec(sh, tm, tps, 2),
            pl.BlockSpec((1, d), lambda i, j: (0, 0)),
            pl.BlockSpec((d, wc), lambda i, j: (0, j)),
            pl.BlockSpec((d, 2 * LANES), lambda i, j: (0, 0)),
        ],
        out_specs=[
            pl.BlockSpec((tm, wc), lambda i, j: (i, j)),
            pl.BlockSpec((tm, LANES), lambda i, j: (i, 0)),
        ],
        out_shape=[
            jax.ShapeDtypeStruct((n, width), BF16),
            jax.ShapeDtypeStruct((n, LANES), F32),
        ],
        scratch_shapes=[pltpu.VMEM((tm, d), BF16)],
        compiler_params=_params(("parallel", "arbitrary")),
        name="inproj",
    )(x, sc, sh, g, w_main, w_gate)


def _mlstm_kernel(qk_ref, v_ref, gt_ref, c0_ref, n0_ref, m0_ref, conv0_ref, wconv_ref, bconv_ref,
                  gbias_ref, ghead_ref, hm_ref, c_ref, n_ref, m_ref, conv_ref, xbuf, *, L, H, dqk, dv):
    qkw = 2 * H * dqk
    pad = SUBLANES
    tail0 = pad - (CONV_W - 1)

    @pl.when(pl.program_id(1) == 0)
    def _():
        c_ref[...] = c0_ref[...]
        n_ref[...] = n0_ref[...]
        m_ref[...] = m0_ref[...]
        xbuf[0:pad, :] = jnp.zeros((pad, qkw), F32)
        xbuf[tail0:pad, :] = conv0_ref[0]

    xbuf[pad:pad + L, :] = qk_ref[...].astype(F32)
    w = wconv_ref[...]
    y = xbuf[tail0:tail0 + L, :] * w[0:1, :]
    for j in range(1, CONV_W):
        y = y + xbuf[tail0 + j:tail0 + j + L, :] * w[j:j + 1, :]
    y = y + bconv_ref[...]
    tail = xbuf[L + tail0:L + pad, :]
    conv_ref[0] = tail
    xbuf[tail0:pad, :] = tail
    qk = _silu(y)

    z = gt_ref[...] + gbias_ref[...]
    lf = jnp.minimum(z, 0.0) - jnp.log(1.0 + jnp.exp(-jnp.abs(z)))
    row = lax.broadcasted_iota(jnp.int32, (L, L), 0)
    col = lax.broadcasted_iota(jnp.int32, (L, L), 1)
    tri = row >= col
    hi = lax.Precision.HIGHEST
    b_col = jnp.dot(tri.astype(F32), lf, precision=hi, preferred_element_type=F32)
    eye = (lax.broadcasted_iota(jnp.int32, (LANES, LANES), 0)
           == lax.broadcasted_iota(jnp.int32, (LANES, LANES), 1)).astype(F32)
    b_row = lax.dot_general(eye, b_col, NT_DIMS, precision=hi, preferred_element_type=F32)
    z_row = lax.dot_general(eye, z, NT_DIMS, precision=hi, preferred_element_type=F32)

    scale = dqk ** -0.5
    lane1 = lax.broadcasted_iota(jnp.int32, (1, LANES), 1)
    m_old = m_ref[0]
    m_new = m_old
    for h in range(H):
        q = qk[:, h * dqk:(h + 1) * dqk] * scale
        k = qk[:, (H + h) * dqk:(H + h + 1) * dqk]
        qb = q.astype(BF16)
        vb = v_ref[:, h * dv:(h + 1) * dv]
        b_c = b_col[:, H + h:H + h + 1]
        i_c = z[:, h:h + 1]
        b_r = b_row[H + h:H + h + 1, :]
        i_r = z_row[h:h + 1, :]
        m_prev = m_old[:, h:h + 1]
        a = b_c + m_prev
        dm = jnp.where(tri, b_c - b_r + i_r, -jnp.inf)
        mt = jnp.maximum(a, jnp.max(dm, axis=-1, keepdims=True))
        s = lax.dot_general(qb, k.astype(BF16), NT_DIMS, preferred_element_type=F32)
        sg = s * jnp.exp(dm - mt)
        aw = jnp.exp(a - mt)
        cm = c_ref[0, h]
        qc = lax.dot_general(qb, cm.astype(BF16), NT_DIMS, preferred_element_type=F32)
        num = jnp.dot(sg.astype(BF16), vb, preferred_element_type=F32) + aw * qc
        nrow = n_ref[0, h:h + 1, :]
        den = jnp.sum(sg, -1, keepdims=True) + aw * jnp.sum(q * nrow, -1, keepdims=True)
        hh = num / jnp.maximum(jnp.abs(den), jnp.exp(-mt))
        m_last = mt[L - 1:L, :]
        b_last = b_c[L - 1:L, :]
        decay = jnp.exp(b_last + m_prev - m_last)
        ws = jnp.exp(b_last - b_c + i_c - m_last)
        kw = ws * k
        c_ref[0, h] = decay * cm + lax.dot_general(vb, kw.astype(BF16), TN_DIMS, preferred_element_type=F32)
        n_ref[0, h:h + 1, :] = decay * nrow + jnp.sum(kw, axis=0, keepdims=True)
        m_new = jnp.where(lane1 == h, m_last, m_new)
        hm_ref[:, h * dv:(h + 1) * dv] = _rms(hh, ghead_ref[:, h * dv:(h + 1) * dv]).astype(BF16)
    m_ref[0] = m_new


def _mlstm(p, gates, c0, n0, m0, conv0, w_conv, b_conv, gbias, g_head, nb, L):
    n = p.shape[0]
    t_seq = n // nb
    nc = t_seq // L
    _, H, dv, dqk = c0.shape
    qkw = 2 * H * dqk
    vw = H * dv
    kern = functools.partial(_mlstm_kernel, L=L, H=H, dqk=dqk, dv=dv)
    return pl.pallas_call(
        kern,
        grid=(nb, nc),
        in_specs=[
            pl.BlockSpec((L, qkw), lambda b, c: (b * nc + c, 0)),
            pl.BlockSpec((L, vw), lambda b, c: (b * nc + c, 1)),
            pl.BlockSpec((L, LANES), lambda b, c: (b * nc + c, 0)),
            pl.BlockSpec((1, H, dv, dqk), lambda b, c: (b, 0, 0, 0)),
            pl.BlockSpec((1, H, dqk), lambda b, c: (b, 0, 0)),
            pl.BlockSpec((1, 1, LANES), lambda b, c: (b, 0, 0)),
            pl.BlockSpec((1, CONV_W - 1, qkw), lambda b, c: (b, 0, 0)),
            pl.BlockSpec((CONV_W, qkw), lambda b, c: (0, 0)),
            pl.BlockSpec((1, qkw), lambda b, c: (0, 0)),
            pl.BlockSpec((1, LANES), lambda b, c: (0, 0)),
            pl.BlockSpec((1, vw), lambda b, c: (0, 0)),
        ],
        out_specs=[
            pl.BlockSpec((L, vw), lambda b, c: (b * nc + c, 0)),
            pl.BlockSpec((1, H, dv, dqk), lambda b, c: (b, 0, 0, 0)),
            pl.BlockSpec((1, H, dqk), lambda b, c: (b, 0, 0)),
            pl.BlockSpec((1, 1, LANES), lambda b, c: (b, 0, 0)),
            pl.BlockSpec((1, CONV_W - 1, qkw), lambda b, c: (b, 0, 0)),
        ],
        out_shape=[
            jax.ShapeDtypeStruct((n, vw), BF16),
            jax.ShapeDtypeStruct((nb, H, dv, dqk), F32),
            jax.ShapeDtypeStruct((nb, H, dqk), F32),
            jax.ShapeDtypeStruct((nb, 1, LANES), F32),
            jax.ShapeDtypeStruct((nb, CONV_W - 1, qkw), F32),
        ],
        scratch_shapes=[pltpu.VMEM((L + SUBLANES, qkw), F32)],
        compiler_params=_params(("parallel", "arbitrary")),
        name="mlstm",
    )(p, p, gates, c0, n0, m0, conv0, w_conv, b_conv, gbias, g_head)


def _merge_kernel(o_ref, u_ref, vg_ref, ga_ref, gb_ref, hm_ref, x_ref, wsp_ref, bsp_ref, gvn_ref, bvn_ref,
                  wout_ref, gpost_ref, gt1_ref, gpre_ref, sc2_ref, sh2_ref, wr_ref, *rest, tm, gl, keep_v):
    if keep_v:
        x1_ref, h2_ref, lg_ref, vn_ref, sg_scr = rest
    else:
        x1_ref, h2_ref, lg_ref, sg_scr = rest
    groups = wsp_ref.shape[0]
    ch = vg_ref.shape[1] // groups
    vg = _gelu_tanh(vg_ref[...].astype(F32))
    mu = jnp.mean(vg, -1, keepdims=True)
    xc = vg - mu
    vn = xc * lax.rsqrt(jnp.mean(xc * xc, -1, keepdims=True) + EPS) * gvn_ref[...] + bvn_ref[...]
    if keep_v:
        vn_ref[...] = vn
    vnb = vn.astype(BF16)
    for ci in range(tm // gl):
        for g in range(groups):
            blk = jnp.dot(wsp_ref[g], vnb[ci * gl:(ci + 1) * gl, g * ch:(g + 1) * ch],
                          preferred_element_type=F32)
            sg_scr[ci * gl:(ci + 1) * gl, g * ch:(g + 1) * ch] = blk + bsp_ref[:, g:g + 1]
    h_b = _gelu_tanh(u_ref[...].astype(F32)) * sg_scr[...]
    h_a = _sigmoid(o_ref[...].astype(F32)) * hm_ref[...].astype(F32)
    merged = _sigmoid(ga_ref[...].astype(F32)) * h_a + _sigmoid(gb_ref[...].astype(F32)) * h_b
    y = jnp.dot(merged.astype(BF16), wout_ref[...], preferred_element_type=F32)
    x1 = x_ref[...] + gt1_ref[0] * _rms(y, gpost_ref[...])
    x1_ref[...] = x1
    h2 = _rms(x1, gpre_ref[...]) * (1.0 + sc2_ref[0]) + sh2_ref[0]
    h2_ref[...] = _pack_pairs(h2)
    n_exp = lg_ref.shape[0]
    h2_high = h2.astype(BF16)
    h2_low = (h2 - h2_high.astype(F32)).astype(BF16)
    lg = lax.dot_general(wr_ref[...], h2_high, NT_DIMS, preferred_element_type=F32)
    lg_ref[...] = (lg[:n_exp] + lg[n_exp:]
                   + lax.dot_general(wr_ref[:n_exp, :], h2_low, NT_DIMS, preferred_element_type=F32))


def _merge(p, hm, x, wsp, bsp_t, g_vn, b_vn, w_out, g_post, gt1, g_pre, sc2, sh2, wr, tm, t_seq, keep_v, x_row0):
    n, d = hm.shape
    x_blk0 = x_row0 // tm
    groups, gl, _ = wsp.shape
    n_exp = wr.shape[0] // 2
    tps = max(t_seq // tm, 1)
    row = lambda i: (0, 0)
    pcol = lambda c: pl.BlockSpec((tm, d), lambda i: (i, c))
    tok = pl.BlockSpec((tm, d), lambda i: (i, 0))
    vec = pl.BlockSpec((1, d), row)
    out_specs = [tok, pl.BlockSpec((tm, d // 2), lambda i: (i, 0)), pl.BlockSpec((n_exp, tm), lambda i: (0, i))]
    out_shape = [jax.ShapeDtypeStruct((n, d), F32), jax.ShapeDtypeStruct((n, d // 2), jnp.uint32),
                 jax.ShapeDtypeStruct((n_exp, n), F32)]
    if keep_v:
        out_specs.append(tok)
        out_shape.append(jax.ShapeDtypeStruct((n, d), F32))
    kern = functools.partial(_merge_kernel, tm=tm, gl=gl, keep_v=keep_v)
    return pl.pallas_call(
        kern,
        grid=(n // tm,),
        in_specs=[
            pcol(2), pcol(3), pcol(4), pcol(5), pcol(6), tok,
            pl.BlockSpec((tm, d), lambda i: (i + x_blk0, 0)),
            pl.BlockSpec((groups, gl, gl), lambda i: (0, 0, 0)),
            pl.BlockSpec((gl, groups), row),
            vec, vec,
            pl.BlockSpec((d, d), row),
            vec,
            _mod_spec(gt1, tm, tps, 1),
            vec,
            _mod_spec(sc2, tm, tps, 1),
            _mod_spec(sh2, tm, tps, 1),
            pl.BlockSpec((2 * n_exp, d), row),
        ],
        out_specs=out_specs,
        out_shape=out_shape,
        scratch_shapes=[pltpu.VMEM((tm, d), F32)],
        compiler_params=_params(("parallel",)),
        name="merge",
    )(p, p, p, p, p, hm, x, wsp, bsp_t, g_vn, b_vn, w_out, g_post, gt1, g_pre, sc2, sh2, wr)


def _select(logits, br_ref):
    n_exp, tm = logits.shape
    per = n_exp // N_GROUPS
    neg = -jnp.inf
    s = _sigmoid(logits)
    sel = s + br_ref[...]
    mem = [sel[r * N_GROUPS:(r + 1) * N_GROUPS, :] for r in range(per)]
    grow = lax.broadcasted_iota(jnp.int32, (N_GROUPS, tm), 0)
    m1 = functools.reduce(jnp.maximum, mem)
    idx1 = functools.reduce(jnp.minimum, [jnp.where(mem[r] == m1, r, per) for r in range(per)])
    m2 = functools.reduce(jnp.maximum, [jnp.where(idx1 == r, neg, mem[r]) for r in range(per)])
    gs = m1 + m2
    gmask = jnp.zeros((N_GROUPS, tm), jnp.bool_)
    for _ in range(TOPK_GROUPS):
        mx = jnp.max(gs, axis=0, keepdims=True)
        gi = jnp.min(jnp.where(gs == mx, grow, N_GROUPS), axis=0, keepdims=True)
        pick = grow == gi
        gmask = jnp.logical_or(gmask, pick)
        gs = jnp.where(pick, neg, gs)
    msk = [jnp.where(gmask, mem[r], neg) for r in range(per)]
    eidx = [grow * per + r for r in range(per)]
    chosen = [jnp.zeros((N_GROUPS, tm), jnp.bool_) for _ in range(per)]
    firsts = []
    for _ in range(TOP_K):
        mx = jnp.max(functools.reduce(jnp.maximum, msk), axis=0, keepdims=True)
        cand = functools.reduce(jnp.minimum, [jnp.where(msk[r] == mx, eidx[r], n_exp) for r in range(per)])
        first = jnp.min(cand, axis=0, keepdims=True)
        firsts.append(first)
        for r in range(per):
            pick = eidx[r] == first
            chosen[r] = jnp.logical_or(chosen[r], pick)
            msk[r] = jnp.where(pick, neg, msk[r])
    wk = [jnp.where(chosen[r], s[r * N_GROUPS:(r + 1) * N_GROUPS, :], 0.0) for r in range(per)]
    denom = jnp.sum(functools.reduce(jnp.add, wk), axis=0, keepdims=True)
    return s, chosen, firsts, eidx, denom


def _shared_expert(hb, ws1_ref, ws3_ref, ws2_ref):
    a = jnp.dot(hb, ws1_ref[...], preferred_element_type=F32)
    g = jnp.dot(hb, ws3_ref[...], preferred_element_type=F32)
    return jnp.dot((_silu(a) * g).astype(BF16), ws2_ref[...], preferred_element_type=F32)


def _moe_kernel(h_ref, x1_ref, lg_ref, br_ref, w1_ref, w3_ref, w2_ref, ws1_ref, ws3_ref, ws2_ref, gpost_ref,
                gt2_ref, out_ref, gt_scr, gates_scr, acc_scr, *, eb):
    j = pl.program_id(1)
    hb = _unpack_pairs(h_ref[...]).astype(BF16)
    n_exp = lg_ref.shape[0]
    per = n_exp // N_GROUPS

    @pl.when(j == 0)
    def _():
        s, chosen, _, _, denom = _select(lg_ref[...], br_ref)
        gt_scr[n_exp:, :] = jnp.zeros((gt_scr.shape[0] - n_exp, hb.shape[0]), F32)
        for r in range(per):
            wk = jnp.where(chosen[r], s[r * N_GROUPS:(r + 1) * N_GROUPS, :], 0.0)
            gt_scr[r * N_GROUPS:(r + 1) * N_GROUPS, :] = wk / denom * ROUTE_SCALE
        gates_scr[...] = gt_scr[...].T
        acc_scr[...] = _shared_expert(hb, ws1_ref, ws3_ref, ws2_ref)

    first = j * eb
    lane0 = lax.rem(first, per) * N_GROUPS + first // per
    gates = pltpu.roll(gates_scr[...], lax.rem(LANES - lane0, LANES), 1)
    acc = acc_scr[...]
    for e in range(eb):
        a = jnp.dot(hb, w1_ref[e].astype(BF16), preferred_element_type=F32)
        g = jnp.dot(hb, w3_ref[e].astype(BF16), preferred_element_type=F32)
        gate = gates[:, e * N_GROUPS:e * N_GROUPS + 1]
        acc = acc + jnp.dot((_silu(a) * g * gate).astype(BF16), w2_ref[e].astype(BF16),
                            preferred_element_type=F32)
    acc_scr[...] = acc

    @pl.when(j == pl.num_programs(1) - 1)
    def _():
        out_ref[...] = x1_ref[...] + gt2_ref[0] * _rms(acc_scr[...], gpost_ref[...])


def _moe_dense(h2, x1, logits, lw, gt2, tm, t_seq, eb):
    n, d = x1.shape
    _, n_exp, _, f = lw["w1"].shape
    layer = lw["layer"]
    fs = lw["ws1"].shape[1]
    tps = max(t_seq // tm, 1)
    const = lambda i, j: (0, 0)
    return pl.pallas_call(
        functools.partial(_moe_kernel, eb=eb),
        grid=(n // tm, n_exp // eb),
        in_specs=[
            pl.BlockSpec((tm, d // 2), lambda i, j: (i, 0)),
            pl.BlockSpec((tm, d), lambda i, j: (i, 0)),
            pl.BlockSpec((n_exp, tm), lambda i, j: (0, i)),
            pl.BlockSpec((n_exp, 1), const),
            pl.BlockSpec((None, eb, d, f), lambda i, j: (layer, j, 0, 0)),
            pl.BlockSpec((None, eb, d, f), lambda i, j: (layer, j, 0, 0)),
            pl.BlockSpec((None, eb, f, d), lambda i, j: (layer, j, 0, 0)),
            pl.BlockSpec((d, fs), const),
            pl.BlockSpec((d, fs), const),
            pl.BlockSpec((fs, d), const),
            pl.BlockSpec((1, d), const),
            _mod_spec(gt2, tm, tps, 2),
        ],
        out_specs=pl.BlockSpec((tm, d), lambda i, j: (i, 0)),
        out_shape=jax.ShapeDtypeStruct((n, d), F32),
        scratch_shapes=[
            pltpu.VMEM((LANES, tm), F32),
            pltpu.VMEM((tm, LANES), F32),
            pltpu.VMEM((tm, d), F32),
        ],
        compiler_params=_params(("parallel", "arbitrary")),
        name="moe_dense",
    )(h2, x1, logits, lw["br"], lw["w1"], lw["w3"], lw["w2"], lw["ws1"], lw["ws3"], lw["ws2"],
      lw["g_post_ffn"], gt2)


def _router_kernel(lg_ref, br_ref, eidx_ref, rank_ref, gate_ref, cnt_ref, carry_scr):
    n_exp, tm = lg_ref.shape
    per = n_exp // N_GROUPS

    @pl.when(pl.program_id(0) == 0)
    def _():
        carry_scr[...] = jnp.zeros(carry_scr.shape, F32)

    s, chosen, firsts, eidx, denom = _select(lg_ref[...], br_ref)
    sel01 = jnp.concatenate([c.astype(F32) for c in chosen], axis=0)
    before = (lax.broadcasted_iota(jnp.int32, (tm, tm), 0)
              < lax.broadcasted_iota(jnp.int32, (tm, tm), 1)).astype(BF16)
    rank = jnp.dot(sel01.astype(BF16), before, preferred_element_type=F32) + carry_scr[:, 0:1]
    carry_scr[...] = carry_scr[...] + jnp.sum(sel01, axis=1, keepdims=True)
    cnt_ref[...] = carry_scr[...]
    for k in range(TOP_K):
        s_k = jnp.zeros((1, tm), F32)
        r_k = jnp.zeros((1, tm), F32)
        for r in range(per):
            pick = eidx[r] == firsts[k]
            rows = slice(r * N_GROUPS, (r + 1) * N_GROUPS)
            s_k = s_k + jnp.sum(jnp.where(pick, s[rows, :], 0.0), axis=0, keepdims=True)
            r_k = r_k + jnp.sum(jnp.where(pick, rank[rows, :], 0.0), axis=0, keepdims=True)
        eidx_ref[k:k + 1, :] = firsts[k]
        rank_ref[k:k + 1, :] = r_k.astype(jnp.int32)
        gate_ref[k:k + 1, :] = s_k / denom * ROUTE_SCALE


def _router(logits, br, tm, row0, n):
    n_exp = logits.shape[0]
    tok = pl.BlockSpec((TOP_K, tm), lambda i: (0, i))
    blk0 = row0 // tm
    return pl.pallas_call(
        _router_kernel,
        grid=(n // tm,),
        in_specs=[
            pl.BlockSpec((n_exp, tm), lambda i: (0, i + blk0)),
            pl.BlockSpec((n_exp, 1), lambda i: (0, 0)),
        ],
        out_specs=[tok, tok, tok, pl.BlockSpec((n_exp, LANES), lambda i: (0, 0))],
        out_shape=[
            jax.ShapeDtypeStruct((TOP_K, n), jnp.int32),
            jax.ShapeDtypeStruct((TOP_K, n), jnp.int32),
            jax.ShapeDtypeStruct((TOP_K, n), F32),
            jax.ShapeDtypeStruct((n_exp, LANES), F32),
        ],
        scratch_shapes=[pltpu.VMEM((n_exp, LANES), F32)],
        compiler_params=_params(("arbitrary",)),
        name="router",
    )(logits, br)


SC_BATCH = 128
SC_WORKERS = 32


def _sc_mesh():
    return plsc.VectorSubcoreMesh(core_axis_name="c", subcore_axis_name="s")


def _sc_scatter(x, pos, n_rows, row0):
    w = x.shape[1]
    n = pos.shape[0] * SC_BATCH
    info = plsc.get_sparse_core_info()
    nc, nw = info.num_cores, info.num_cores * info.num_subcores
    steps = n // (nw * SC_BATCH)

    @functools.partial(
        pl.kernel, mesh=_sc_mesh(),
        out_type=jax.ShapeDtypeStruct((n_rows, w), x.dtype),
        scratch_types=[pltpu.VMEM((TOP_K, SC_BATCH), jnp.int32), pltpu.VMEM((SC_BATCH, w), x.dtype),
                       pltpu.SemaphoreType.DMA],
    )
    def scatter(x_hbm, pos_hbm, out_hbm, idx_v, rows_v, sem):
        wid = lax.axis_index("s") * nc + lax.axis_index("c")

        @pl.loop(0, steps)
        def _(s):
            blk = wid * steps + s
            pltpu.sync_copy(pos_hbm.at[blk], idx_v)
            pltpu.sync_copy(x_hbm.at[pl.ds(row0 + blk * SC_BATCH, SC_BATCH)], rows_v)
            copies = [pltpu.async_copy(rows_v, out_hbm.at[idx_v.at[k]], sem) for k in range(TOP_K)]
            for c in copies:
                c.wait()

    return scatter(x, pos)


def _sc_gather(y, pos):
    w = y.shape[1]
    n = pos.shape[0] * SC_BATCH
    info = plsc.get_sparse_core_info()
    nc, nw = info.num_cores, info.num_cores * info.num_subcores
    steps = n // (nw * SC_BATCH)

    @functools.partial(
        pl.kernel, mesh=_sc_mesh(),
        out_type=jax.ShapeDtypeStruct((TOP_K, n, w), y.dtype),
        scratch_types=[pltpu.VMEM((TOP_K, SC_BATCH), jnp.int32), pltpu.VMEM((SC_BATCH, w), y.dtype),
                       pltpu.SemaphoreType.DMA],
    )
    def gather(y_hbm, pos_hbm, out_hbm, idx_v, rows_v, sem):
        wid = lax.axis_index("s") * nc + lax.axis_index("c")

        @pl.loop(0, steps)
        def _(s):
            blk = wid * steps + s
            pltpu.sync_copy(pos_hbm.at[blk], idx_v)
            for k in range(TOP_K):
                pltpu.async_copy(y_hbm.at[idx_v.at[k]], rows_v, sem).wait()
                pltpu.sync_copy(rows_v, out_hbm.at[k, pl.ds(blk * SC_BATCH, SC_BATCH)])

    return gather(y, pos)


def _experts_kernel(te_ref, nu_ref, xs_ref, w1_ref, w3_ref, w2_ref, ys_ref, w1_scr, w3_scr, w2_scr):
    i = pl.program_id(0)

    @pl.when(jnp.logical_or(i == 0, te_ref[i] != te_ref[jnp.maximum(i - 1, 0)]))
    def _():
        w1_scr[...] = w1_ref[...].astype(BF16)
        w3_scr[...] = w3_ref[...].astype(BF16)
        w2_scr[...] = w2_ref[...].astype(BF16)

    @pl.when(i < nu_ref[0])
    def _():
        x = _unpack_pairs(xs_ref[...]).astype(BF16)
        a = jnp.dot(x, w1_scr[...], preferred_element_type=F32)
        g = jnp.dot(x, w3_scr[...], preferred_element_type=F32)
        y = jnp.dot((_silu(a) * g).astype(BF16), w2_scr[...], preferred_element_type=F32)
        ys_ref[...] = _pack_pairs(y)


def _experts(xs, tile_expert, n_used, w1, w3, w2, layer, rt):
    rows, half = xs.shape
    _, n_exp, d, f = w1.shape
    scratch = [pltpu.VMEM((d, f), BF16), pltpu.VMEM((d, f), BF16), pltpu.VMEM((f, d), BF16)]
    grid_spec = pltpu.PrefetchScalarGridSpec(
        num_scalar_prefetch=2,
        grid=(rows // rt,),
        in_specs=[
            pl.BlockSpec((rt, half), lambda i, te, nu: (i, 0)),
            pl.BlockSpec((None, None, d, f), lambda i, te, nu: (layer, te[i], 0, 0)),
            pl.BlockSpec((None, None, d, f), lambda i, te, nu: (layer, te[i], 0, 0)),
            pl.BlockSpec((None, None, f, d), lambda i, te, nu: (layer, te[i], 0, 0)),
        ],
        out_specs=pl.BlockSpec((rt, half), lambda i, te, nu: (i, 0)),
        scratch_shapes=scratch,
    )
    return pl.pallas_call(
        _experts_kernel,
        grid_spec=grid_spec,
        out_shape=jax.ShapeDtypeStruct((rows, half), jnp.uint32),
        compiler_params=_params(("arbitrary",)),
        name="experts",
    )(tile_expert, n_used, xs, w1, w3, w2)


def _combine_kernel(yk_ref, g_ref, h_ref, x1_ref, ws1_ref, ws3_ref, ws2_ref, gpost_ref, gt2_ref, *rest):
    out_ref = rest[-1]
    hb = _unpack_pairs(h_ref[...]).astype(BF16)
    acc = _shared_expert(hb, ws1_ref, ws3_ref, ws2_ref)
    gates = g_ref[...]
    for k in range(TOP_K):
        acc = acc + gates[:, k:k + 1] * _unpack_pairs(yk_ref[k])
    out_ref[...] = x1_ref[...] + gt2_ref[0] * _rms(acc, gpost_ref[...])


def _combine(yk, gates_t, h2, x1, lw, gt2, tm, t_seq, row0, out_rows, out_row0, out_prev):
    d = x1.shape[1]
    nc = yk.shape[1]
    fs = lw["ws1"].shape[1]
    tps = max(t_seq // tm, 1)
    blk0 = row0 // tm
    out_blk0 = (out_row0 + row0) // tm
    const = lambda i: (0, 0)
    in_specs = [
        pl.BlockSpec((TOP_K, tm, d // 2), lambda i: (0, i, 0)),
        pl.BlockSpec((tm, TOP_K), lambda i: (i, 0)),
        pl.BlockSpec((tm, d // 2), lambda i: (i + blk0, 0)),
        pl.BlockSpec((tm, d), lambda i: (i + blk0, 0)),
        pl.BlockSpec((d, fs), const),
        pl.BlockSpec((d, fs), const),
        pl.BlockSpec((fs, d), const),
        pl.BlockSpec((1, d), const),
        pl.BlockSpec((1, 1, d), lambda i: ((i + blk0) // tps, 0, 0)),
    ]
    args = [yk, gates_t, h2, x1, lw["ws1"], lw["ws3"], lw["ws2"], lw["g_post_ffn"], gt2]
    aliases = {}
    if out_prev is not None:
        in_specs.append(pl.BlockSpec(memory_space=pl.ANY))
        aliases = {len(args): 0}
        args.append(out_prev)
    return pl.pallas_call(
        _combine_kernel,
        grid=(nc // tm,),
        in_specs=in_specs,
        out_specs=pl.BlockSpec((tm, d), lambda i: (i + out_blk0, 0)),
        out_shape=jax.ShapeDtypeStruct((out_rows, d), F32),
        input_output_aliases=aliases,
        compiler_params=_params(("parallel",)),
        name="combine",
    )(*args)


EXPERT_ROWS = 1024
MOE_CHUNKS = 1
PROMPT_CHAINS = 2


def _moe_dispatched(h2, x1, logits, lw, gt2, t_seq, dest):
    n, d = x1.shape
    out_rows, out_row0, out = dest if dest is not None else (n, 0, None)
    n_exp = logits.shape[0]
    per = n_exp // N_GROUPS
    rt = EXPERT_ROWS
    chunks = MOE_CHUNKS
    while n % (chunks * SC_WORKERS * SC_BATCH):
        chunks //= 2
    nc = n // chunks
    n_tiles = nc * TOP_K // rt + n_exp
    tm = _tile(min(nc, t_seq), 512)
    routed = [_router(logits, lw["br"], _tile(nc, 1024), c * nc, nc) for c in range(chunks)]
    for c, (eidx, rank, gate, cnt) in enumerate(routed):
        counts = cnt[:, 0].astype(jnp.int32).reshape(per, N_GROUPS).T.reshape(n_exp)
        padded = (counts + rt - 1) // rt * rt
        ends = jnp.cumsum(padded)
        starts = ends - padded
        base = jnp.zeros_like(eidx)
        for e in range(n_exp):
            base = base + jnp.where(eidx == e, starts[e], 0)
        pos = (base + rank).reshape(TOP_K, nc // SC_BATCH, SC_BATCH).transpose(1, 0, 2)
        tile_start = jnp.arange(n_tiles, dtype=jnp.int32) * rt
        tile_expert = jnp.minimum(jnp.sum(ends[None, :] <= tile_start[:, None], axis=1), n_exp - 1).astype(jnp.int32)
        n_used = (ends[-1:] // rt).astype(jnp.int32)
        xs = _sc_scatter(h2, pos, n_tiles * rt, c * nc)
        ys = _experts(xs, tile_expert, n_used, lw["w1"], lw["w3"], lw["w2"], lw["layer"], rt)
        yk = _sc_gather(ys, pos)
        out = _combine(yk, gate.T, h2, x1, lw, gt2, tm, t_seq, c * nc, out_rows, out_row0, out)
    return out


def _split_router(w_router, n_exp, d):
    wrt = w_router.T.reshape(N_GROUPS, n_exp // N_GROUPS, d).transpose(1, 0, 2).reshape(n_exp, d)
    high = wrt.astype(BF16)
    return jnp.concatenate([high, (wrt - high.astype(F32)).astype(BF16)], axis=0)


def _prep_layer(l, w):
    d = w["w_in"].shape[1]
    n_exp = w["w_e1"].shape[1]
    H = N_HEADS
    w_in = w["w_in"][l]
    qkw = d
    g0 = 3 * d
    w_main = jnp.concatenate([w_in[:, :g0], w_in[:, g0 + 2 * H:]], axis=1).astype(BF16)
    wg = jnp.pad(w_in[:, g0:g0 + 2 * H], ((0, 0), (0, LANES - 2 * H)))
    wg_high = wg.astype(BF16)
    w_gate = jnp.concatenate([wg_high, (wg - wg_high.astype(F32)).astype(BF16)], axis=1)
    gbias = jnp.pad(jnp.concatenate([w["b_igate"][l], w["b_fgate"][l]]), (0, LANES - 2 * H)).reshape(1, LANES)
    return dict(
        w_main=w_main, w_gate=w_gate, gbias=gbias,
        g_pre_mix=w["g_pre_mix"][l].reshape(1, d), g_post_mix=w["g_post_mix"][l].reshape(1, d),
        g_pre_ffn=w["g_pre_ffn"][l].reshape(1, d), g_post_ffn=w["g_post_ffn"][l].reshape(1, d),
        w_conv=w["w_conv"][l], b_conv=w["b_conv"][l].reshape(1, qkw),
        g_head=w["g_head"][l].reshape(1, d), g_vnorm=w["g_vnorm"][l].reshape(1, d),
        b_vnorm=w["b_vnorm"][l].reshape(1, d),
        w_spatial=w["w_spatial"][l], b_spatial=w["b_spatial"][l],
        w_out=w["w_out"][l].astype(BF16),
        wr=_split_router(w["w_router"][l], n_exp, d),
        br=w["b_router"][l].reshape(N_GROUPS, n_exp // N_GROUPS).T.reshape(n_exp, 1),
        w1=w["w_e1"], w3=w["w_e3"], w2=w["w_e2"], layer=l,
        ws1=w["w_s1"][l].astype(BF16), ws3=w["w_s3"][l].astype(BF16), ws2=w["w_s2"][l].astype(BF16),
    )


def _spatial_weights(lw, start, gl):
    idx = jnp.arange(GMLP_CHUNK)
    mask = (idx[None, :] // CHUNK) <= (idx[:, None] // CHUNK)
    wsp = jnp.where(mask, lw["w_spatial"], 0.0)[:, start:start + gl, start:start + gl].astype(BF16)
    bsp_t = lw["b_spatial"][:, start:start + gl].T
    return wsp, bsp_t


def _tile(n, cap):
    t = cap
    while n % t:
        t //= 2
    return t


def _trunk(x, mods, c0, n0, m0, conv0, layers, mlstm_l, g_start, g_l, keep_v, seq0=0, nb=None, dest=None):
    b_all, t_seq, d = x.shape
    nb = b_all if nb is None else nb
    n = nb * t_seq
    per_token = mods[0][0].shape[1] != 1
    tok_cap = n if per_token else t_seq
    tm_in = _tile(tok_cap, 1024)
    tm_mg = _tile(tok_cap, 512)
    tm_moe = _tile(tok_cap, 1024)
    xf = x.reshape(b_all * t_seq, d)
    x_row0 = seq0 * t_seq
    outs = [[] for _ in range(5 if keep_v else 4)]
    for l, lw in enumerate(layers):
        sh1, sc1, gt1, sh2, sc2, gt2 = mods[l]
        p, gates = _inproj(xf, sc1, sh1, lw["g_pre_mix"], lw["w_main"], lw["w_gate"], tm_in, t_seq, x_row0, n)
        m0p = jnp.pad(m0[l], ((0, 0), (0, LANES - m0[l].shape[1]))).reshape(nb, 1, LANES)
        hm, c_new, n_new, m_new, conv_new = _mlstm(
            p, gates, c0[l], n0[l], m0p, conv0[l], lw["w_conv"], lw["b_conv"], lw["gbias"], lw["g_head"],
            nb, mlstm_l)
        wsp, bsp_t = _spatial_weights(lw, g_start, g_l)
        res = _merge(p, hm, xf, wsp, bsp_t, lw["g_vnorm"], lw["b_vnorm"], lw["w_out"], lw["g_post_mix"], gt1,
                     lw["g_pre_ffn"], sc2, sh2, lw["wr"], tm_mg, t_seq, keep_v, x_row0)
        x1, h2, logits = res[0], res[1], res[2]
        last = l == len(layers) - 1
        if n % (SC_WORKERS * SC_BATCH) == 0 and not per_token:
            shared = None
            if last and dest is not None:
                buf = None if dest[0] is None else dest[0].reshape(b_all * t_seq, d)
                shared = (b_all * t_seq, seq0 * t_seq, buf)
            xf = _moe_dispatched(h2, x1, logits, lw, gt2, t_seq, shared)
        else:
            xf = _moe_dense(h2, x1, logits, lw, gt2, tm_moe, t_seq, 4)
        x_row0 = 0
        vals = [c_new, n_new, m_new[:, 0, :N_HEADS], conv_new]
        if keep_v:
            vals.append(res[3].reshape(nb, t_seq, d))
        for lst, val in zip(outs, vals):
            lst.append(val)
    return xf.reshape(-1, t_seq, d), tuple(jnp.stack(lst) for lst in outs)


def kernel(x_prompt, x_sample, c_prompt, c_sample, state_mlstm_C, state_mlstm_n, state_mlstm_m, state_conv,
           w_ada, b_ada, g_pre_mix, g_post_mix, g_pre_ffn, g_post_ffn, w_in, b_igate, b_fgate, w_conv, b_conv,
           g_head, g_vnorm, b_vnorm, w_spatial, b_spatial, w_out, w_router, b_router, w_e1, w_e3, w_e2,
           w_s1, w_s3, w_s2):
    w = dict(w_in=w_in, b_igate=b_igate, b_fgate=b_fgate, g_pre_mix=g_pre_mix, g_post_mix=g_post_mix,
             g_pre_ffn=g_pre_ffn, g_post_ffn=g_post_ffn, w_conv=w_conv, b_conv=b_conv, g_head=g_head,
             g_vnorm=g_vnorm, b_vnorm=b_vnorm, w_spatial=w_spatial, b_spatial=b_spatial, w_out=w_out,
             w_router=w_router, b_router=b_router, w_e1=w_e1, w_e3=w_e3, w_e2=w_e2, w_s1=w_s1, w_s3=w_s3,
             w_s2=w_s2)
    depth = w_in.shape[0]
    bp, tp, d = x_prompt.shape
    bs, ts, _ = x_sample.shape
    H, dv, dqk = state_mlstm_C.shape[2:]
    qkw = state_conv.shape[-1]
    layers = [_prep_layer(l, w) for l in range(depth)]

    mod = _ada(jnp.concatenate([c_prompt, c_sample], axis=0), w_ada, b_ada)
    mods_p, mods_s = [], []
    for l in range(depth):
        parts = [mod[l][:, i * d:(i + 1) * d] for i in range(6)]
        mods_p.append([a[:bp].reshape(bp, 1, d) for a in parts])
        mods_s.append([jnp.repeat(a[bp:], ts, axis=0).reshape(1, bs * ts, d) for a in parts])

    lp = 256 if tp % 256 == 0 else CHUNK
    chains = PROMPT_CHAINS if bp % PROMPT_CHAINS == 0 else 1
    bc = bp // chains
    zc = jnp.zeros((depth, bc, H, dv, dqk), F32)
    zn = jnp.zeros((depth, bc, H, dqk), F32)
    zm = jnp.zeros((depth, bc, H), F32)
    zconv = jnp.zeros((depth, bc, CONV_W - 1, qkw), F32)
    shared = chains > 1 and (bc * tp) % (SC_WORKERS * SC_BATCH) == 0
    parts, y_p = [], None
    for ci in range(chains):
        rows = slice(ci * bc, (ci + 1) * bc)
        mods_c = [[a[rows] for a in layer_mods] for layer_mods in mods_p]
        if shared:
            y_p, states = _trunk(x_prompt, mods_c, zc, zn, zm, zconv, layers, lp, 0, GMLP_CHUNK, False,
                                 seq0=ci * bc, nb=bc, dest=(y_p,))
        else:
            y_c, states = _trunk(x_prompt[rows], mods_c, zc, zn, zm, zconv, layers, lp, 0, GMLP_CHUNK, False)
            y_p = y_c if y_p is None else jnp.concatenate([y_p, y_c], axis=0)
        parts.append(states)
    p_c, p_n, p_m, p_conv = (jnp.concatenate([s[i] for s in parts], axis=1) for i in range(4))
    g_start = PAST_LEN % GMLP_CHUNK
    y_s, (s_c, s_n, s_m, s_conv, s_v) = _trunk(x_sample, mods_s, state_mlstm_C, state_mlstm_n, state_mlstm_m,
                                               state_conv, layers, ts, g_start, ts, True)
    return (y_p, y_s, p_c, p_n, p_m, p_conv, s_c, s_n, s_m, s_conv, s_v)
```

```python
import functools

import jax
import jax.numpy as jnp
from jax import lax
from jax.experimental import pallas as pl
from jax.experimental.pallas import tpu as pltpu
from jax.experimental.pallas import tpu_sc as plsc

F32 = jnp.float32
BF16 = jnp.bfloat16

EPS = 1e-6
N_HEADS = 4
CONV_W = 4
GMLP_GROUPS = 8
GMLP_CHUNK = 128
CHUNK = 64
N_GROUPS = 8
TOPK_GROUPS = 4
TOP_K = 8
ROUTE_SCALE = 2.5
PAST_LEN = 4096

LANES = 128
SUBLANES = 8
VMEM_LIMIT = 56 * 1024 * 1024

NT_DIMS = (((1,), (1,)), ((), ()))
TN_DIMS = (((0,), (0,)), ((), ()))


def _sigmoid(x):
    return 0.5 * (jnp.tanh(0.5 * x) + 1.0)


def _silu(x):
    return x * _sigmoid(x)


def _gelu_tanh(x):
    return x * (0.5 * (1.0 + jnp.tanh(0.7978845608028654 * (x + 0.044715 * (x * x * x)))))


def _rms(x, g):
    return x * lax.rsqrt(jnp.mean(x * x, -1, keepdims=True) + EPS) * g


def _pack_pairs(x):
    c = x.shape[1] // 2
    return pltpu.pack_elementwise([x[:, :c], x[:, c:]], packed_dtype=BF16)


def _unpack_pairs(u):
    lo = pltpu.unpack_elementwise(u, index=0, packed_dtype=BF16, unpacked_dtype=F32)
    hi = pltpu.unpack_elementwise(u, index=1, packed_dtype=BF16, unpacked_dtype=F32)
    return jnp.concatenate([lo, hi], axis=1)


def _params(sem):
    return pltpu.CompilerParams(dimension_semantics=sem, vmem_limit_bytes=VMEM_LIMIT)


def _ada_kernel(c_ref, w_ref, b_ref, o_ref):
    a = _silu(c_ref[...]).astype(BF16)
    o_ref[0] = jnp.dot(a, w_ref[0].astype(BF16), preferred_element_type=F32) + b_ref[0]


def _ada(c, w_ada, b_ada):
    depth, d, six_d = w_ada.shape
    nb = c.shape[0]
    tn = d
    return pl.pallas_call(
        _ada_kernel,
        grid=(depth, six_d // tn),
        in_specs=[
            pl.BlockSpec((nb, d), lambda l, j: (0, 0)),
            pl.BlockSpec((1, d, tn), lambda l, j: (l, 0, j)),
            pl.BlockSpec((1, 1, tn), lambda l, j: (l, 0, j)),
        ],
        out_specs=pl.BlockSpec((1, nb, tn), lambda l, j: (l, 0, j)),
        out_shape=jax.ShapeDtypeStruct((depth, nb, six_d), F32),
        compiler_params=_params(("parallel", "parallel")),
        name="adaln",
    )(c, w_ada, b_ada.reshape(depth, 1, six_d))


def _inproj_kernel(x_ref, sc_ref, sh_ref, g_ref, w_ref, wg_ref, p_ref, gates_ref, h_scr):
    @pl.when(pl.program_id(1) == 0)
    def _():
        h = _rms(x_ref[...], g_ref[...]) * (1.0 + sc_ref[0]) + sh_ref[0]
        hb = h.astype(BF16)
        h_scr[...] = hb
        h_low = (h - hb.astype(F32)).astype(BF16)
        gg = jnp.dot(hb, wg_ref[...], preferred_element_type=F32)
        gates_ref[...] = (gg[:, :LANES] + gg[:, LANES:]
                          + jnp.dot(h_low, wg_ref[:, :LANES], preferred_element_type=F32))

    p_ref[...] = jnp.dot(h_scr[...], w_ref[...], preferred_element_type=F32).astype(BF16)


def _mod_spec(mod, tm, tiles_per_seq, ngrid):
    d = mod.shape[-1]
    if mod.shape[1] == 1:
        if ngrid == 2:
            return pl.BlockSpec((1, 1, d), lambda i, j: (i // tiles_per_seq, 0, 0))
        return pl.BlockSpec((1, 1, d), lambda i: (i // tiles_per_seq, 0, 0))
    if ngrid == 2:
        return pl.BlockSpec((1, tm, d), lambda i, j: (0, i, 0))
    return pl.BlockSpec((1, tm, d), lambda i: (0, i, 0))


def _inproj(x, sc, sh, g, w_main, w_gate, tm, t_seq, row0, n):
    d = x.shape[1]
    blk0 = row0 // tm
    width = w_main.shape[1]
    ncol = 2 if (width // 2) % LANES == 0 else width // d
    wc = width // ncol
    tps = max(t_seq // tm, 1)
    return pl.pallas_call(
        _inproj_kernel,
        grid=(n // tm, ncol),
        in_specs=[
            pl.BlockSpec((tm, d), lambda i, j: (i + blk0, 0)),
            _mod_spec(sc, tm, tps, 2),
            _mod_spec(sh, tm, tps, 2),
            pl.BlockSpec((1, d), lambda i, j: (0, 0)),
            pl.BlockSpec((d, wc), lambda i, j: (0, j)),
            pl.BlockSpec((d, 2 * LANES), lambda i, j: (0, 0)),
        ],
        out_specs=[
            pl.BlockSpec((tm, wc), lambda i, j: (i, j)),
            pl.BlockSpec((tm, LANES), lambda i, j: (i, 0)),
        ],
        out_shape=[
            jax.ShapeDtypeStruct((n, width), BF16),
            jax.ShapeDtypeStruct((n, LANES), F32),
        ],
        scratch_shapes=[pltpu.VMEM((tm, d), BF16)],
        compiler_params=_params(("parallel", "arbitrary")),
        name="inproj",
    )(x, sc, sh, g, w_main, w_gate)


def _mlstm_kernel(qk_ref, v_ref, gt_ref, c0_ref, n0_ref, m0_ref, conv0_ref, wconv_ref, bconv_ref,
                  gbias_ref, ghead_ref, hm_ref, c_ref, n_ref, m_ref, conv_ref, xbuf, *, L, H, dqk, dv):
    qkw = 2 * H * dqk
    pad = SUBLANES
    tail0 = pad - (CONV_W - 1)

    @pl.when(pl.program_id(1) == 0)
    def _():
        c_ref[...] = c0_ref[...]
        n_ref[...] = n0_ref[...]
        m_ref[...] = m0_ref[...]
        xbuf[0:pad, :] = jnp.zeros((pad, qkw), F32)
        xbuf[tail0:pad, :] = conv0_ref[0]

    xbuf[pad:pad + L, :] = qk_ref[...].astype(F32)
    w = wconv_ref[...]
    y = xbuf[tail0:tail0 + L, :] * w[0:1, :]
    for j in range(1, CONV_W):
        y = y + xbuf[tail0 + j:tail0 + j + L, :] * w[j:j + 1, :]
    y = y + bconv_ref[...]
    tail = xbuf[L + tail0:L + pad, :]
    conv_ref[0] = tail
    xbuf[tail0:pad, :] = tail
    qk = _silu(y)

    z = gt_ref[...] + gbias_ref[...]
    lf = jnp.minimum(z, 0.0) - jnp.log(1.0 + jnp.exp(-jnp.abs(z)))
    row = lax.broadcasted_iota(jnp.int32, (L, L), 0)
    col = lax.broadcasted_iota(jnp.int32, (L, L), 1)
    tri = row >= col
    hi = lax.Precision.HIGHEST
    b_col = jnp.dot(tri.astype(F32), lf, precision=hi, preferred_element_type=F32)
    eye = (lax.broadcasted_iota(jnp.int32, (LANES, LANES), 0)
           == lax.broadcasted_iota(jnp.int32, (LANES, LANES), 1)).astype(F32)
    b_row = lax.dot_general(eye, b_col, NT_DIMS, precision=hi, preferred_element_type=F32)
    z_row = lax.dot_general(eye, z, NT_DIMS, precision=hi, preferred_element_type=F32)

    scale = dqk ** -0.5
    lane1 = lax.broadcasted_iota(jnp.int32, (1, LANES), 1)
    m_old = m_ref[0]
    m_new = m_old
    for h in range(H):
        q = qk[:, h * dqk:(h + 1) * dqk] * scale
        k = qk[:, (H + h) * dqk:(H + h + 1) * dqk]
        qb = q.astype(BF16)
        vb = v_ref[:, h * dv:(h + 1) * dv]
        b_c = b_col[:, H + h:H + h + 1]
        i_c = z[:, h:h + 1]
        b_r = b_row[H + h:H + h + 1, :]
        i_r = z_row[h:h + 1, :]
        m_prev = m_old[:, h:h + 1]
        a = b_c + m_prev
        dm = jnp.where(tri, b_c - b_r + i_r, -jnp.inf)
        mt = jnp.maximum(a, jnp.max(dm, axis=-1, keepdims=True))
        s = lax.dot_general(qb, k.astype(BF16), NT_DIMS, preferred_element_type=F32)
        sg = s * jnp.exp(dm - mt)
        aw = jnp.exp(a - mt)
        cm = c_ref[0, h]
        qc = lax.dot_general(qb, cm.astype(BF16), NT_DIMS, preferred_element_type=F32)
        num = jnp.dot(sg.astype(BF16), vb, preferred_element_type=F32) + aw * qc
        nrow = n_ref[0, h:h + 1, :]
        den = jnp.sum(sg, -1, keepdims=True) + aw * jnp.sum(q * nrow, -1, keepdims=True)
        hh = num / jnp.maximum(jnp.abs(den), jnp.exp(-mt))
        m_last = mt[L - 1:L, :]
        b_last = b_c[L - 1:L, :]
        decay = jnp.exp(b_last + m_prev - m_last)
        ws = jnp.exp(b_last - b_c + i_c - m_last)
        kw = ws * k
        c_ref[0, h] = decay * cm + lax.dot_general(vb, kw.astype(BF16), TN_DIMS, preferred_element_type=F32)
        n_ref[0, h:h + 1, :] = decay * nrow + jnp.sum(kw, axis=0, keepdims=True)
        m_new = jnp.where(lane1 == h, m_last, m_new)
        hm_ref[:, h * dv:(h + 1) * dv] = _rms(hh, ghead_ref[:, h * dv:(h + 1) * dv]).astype(BF16)
    m_ref[0] = m_new


def _mlstm(p, gates, c0, n0, m0, conv0, w_conv, b_conv, gbias, g_head, nb, L):
    n = p.shape[0]
    t_seq = n // nb
    nc = t_seq // L
    _, H, dv, dqk = c0.shape
    qkw = 2 * H * dqk
    vw = H * dv
    kern = functools.partial(_mlstm_kernel, L=L, H=H, dqk=dqk, dv=dv)
    return pl.pallas_call(
        kern,
        grid=(nb, nc),
        in_specs=[
            pl.BlockSpec((L, qkw), lambda b, c: (b * nc + c, 0)),
            pl.BlockSpec((L, vw), lambda b, c: (b * nc + c, 1)),
            pl.BlockSpec((L, LANES), lambda b, c: (b * nc + c, 0)),
            pl.BlockSpec((1, H, dv, dqk), lambda b, c: (b, 0, 0, 0)),
            pl.BlockSpec((1, H, dqk), lambda b, c: (b, 0, 0)),
            pl.BlockSpec((1, 1, LANES), lambda b, c: (b, 0, 0)),
            pl.BlockSpec((1, CONV_W - 1, qkw), lambda b, c: (b, 0, 0)),
            pl.BlockSpec((CONV_W, qkw), lambda b, c: (0, 0)),
            pl.BlockSpec((1, qkw), lambda b, c: (0, 0)),
            pl.BlockSpec((1, LANES), lambda b, c: (0, 0)),
            pl.BlockSpec((1, vw), lambda b, c: (0, 0)),
        ],
        out_specs=[
            pl.BlockSpec((L, vw), lambda b, c: (b * nc + c, 0)),
            pl.BlockSpec((1, H, dv, dqk), lambda b, c: (b, 0, 0, 0)),
            pl.BlockSpec((1, H, dqk), lambda b, c: (b, 0, 0)),
            pl.BlockSpec((1, 1, LANES), lambda b, c: (b, 0, 0)),
            pl.BlockSpec((1, CONV_W - 1, qkw), lambda b, c: (b, 0, 0)),
        ],
        out_shape=[
            jax.ShapeDtypeStruct((n, vw), BF16),
            jax.ShapeDtypeStruct((nb, H, dv, dqk), F32),
            jax.ShapeDtypeStruct((nb, H, dqk), F32),
            jax.ShapeDtypeStruct((nb, 1, LANES), F32),
            jax.ShapeDtypeStruct((nb, CONV_W - 1, qkw), F32),
        ],
        scratch_shapes=[pltpu.VMEM((L + SUBLANES, qkw), F32)],
        compiler_params=_params(("parallel", "arbitrary")),
        name="mlstm",
    )(p, p, gates, c0, n0, m0, conv0, w_conv, b_conv, gbias, g_head)


def _merge_kernel(o_ref, u_ref, vg_ref, ga_ref, gb_ref, hm_ref, x_ref, wsp_ref, bsp_ref, gvn_ref, bvn_ref,
                  wout_ref, gpost_ref, gt1_ref, gpre_ref, sc2_ref, sh2_ref, wr_ref, *rest, tm, gl, keep_v):
    if keep_v:
        x1_ref, h2_ref, lg_ref, vn_ref, sg_scr = rest
    else:
        x1_ref, h2_ref, lg_ref, sg_scr = rest
    groups = wsp_ref.shape[0]
    ch = vg_ref.shape[1] // groups
    vg = _gelu_tanh(vg_ref[...].astype(F32))
    mu = jnp.mean(vg, -1, keepdims=True)
    xc = vg - mu
    vn = xc * lax.rsqrt(jnp.mean(xc * xc, -1, keepdims=True) + EPS) * gvn_ref[...] + bvn_ref[...]
    if keep_v:
        vn_ref[...] = vn
    vnb = vn.astype(BF16)
    for ci in range(tm // gl):
        for g in range(groups):
            blk = jnp.dot(wsp_ref[g], vnb[ci * gl:(ci + 1) * gl, g * ch:(g + 1) * ch],
                          preferred_element_type=F32)
            sg_scr[ci * gl:(ci + 1) * gl, g * ch:(g + 1) * ch] = blk + bsp_ref[:, g:g + 1]
    h_b = _gelu_tanh(u_ref[...].astype(F32)) * sg_scr[...]
    h_a = _sigmoid(o_ref[...].astype(F32)) * hm_ref[...].astype(F32)
    merged = _sigmoid(ga_ref[...].astype(F32)) * h_a + _sigmoid(gb_ref[...].astype(F32)) * h_b
    y = jnp.dot(merged.astype(BF16), wout_ref[...], preferred_element_type=F32)
    x1 = x_ref[...] + gt1_ref[0] * _rms(y, gpost_ref[...])
    x1_ref[...] = x1
    h2 = _rms(x1, gpre_ref[...]) * (1.0 + sc2_ref[0]) + sh2_ref[0]
    h2_ref[...] = _pack_pairs(h2)
    n_exp = lg_ref.shape[0]
    h2_high = h2.astype(BF16)
    h2_low = (h2 - h2_high.astype(F32)).astype(BF16)
    lg = lax.dot_general(wr_ref[...], h2_high, NT_DIMS, preferred_element_type=F32)
    lg_ref[...] = (lg[:n_exp] + lg[n_exp:]
                   + lax.dot_general(wr_ref[:n_exp, :], h2_low, NT_DIMS, preferred_element_type=F32))


def _merge(p, hm, x, wsp, bsp_t, g_vn, b_vn, w_out, g_post, gt1, g_pre, sc2, sh2, wr, tm, t_seq, keep_v, x_row0):
    n, d = hm.shape
    x_blk0 = x_row0 // tm
    groups, gl, _ = wsp.shape
    n_exp = wr.shape[0] // 2
    tps = max(t_seq // tm, 1)
    row = lambda i: (0, 0)
    pcol = lambda c: pl.BlockSpec((tm, d), lambda i: (i, c))
    tok = pl.BlockSpec((tm, d), lambda i: (i, 0))
    vec = pl.BlockSpec((1, d), row)
    out_specs = [tok, pl.BlockSpec((tm, d // 2), lambda i: (i, 0)), pl.BlockSpec((n_exp, tm), lambda i: (0, i))]
    out_shape = [jax.ShapeDtypeStruct((n, d), F32), jax.ShapeDtypeStruct((n, d // 2), jnp.uint32),
                 jax.ShapeDtypeStruct((n_exp, n), F32)]
    if keep_v:
        out_specs.append(tok)
        out_shape.append(jax.ShapeDtypeStruct((n, d), F32))
    kern = functools.partial(_merge_kernel, tm=tm, gl=gl, keep_v=keep_v)
    return pl.pallas_call(
        kern,
        grid=(n // tm,),
        in_specs=[
            pcol(2), pcol(3), pcol(4), pcol(5), pcol(6), tok,
            pl.BlockSpec((tm, d), lambda i: (i + x_blk0, 0)),
            pl.BlockSpec((groups, gl, gl), lambda i: (0, 0, 0)),
            pl.BlockSpec((gl, groups), row),
            vec, vec,
            pl.BlockSpec((d, d), row),
            vec,
            _mod_spec(gt1, tm, tps, 1),
            vec,
            _mod_spec(sc2, tm, tps, 1),
            _mod_spec(sh2, tm, tps, 1),
            pl.BlockSpec((2 * n_exp, d), row),
        ],
        out_specs=out_specs,
        out_shape=out_shape,
        scratch_shapes=[pltpu.VMEM((tm, d), F32)],
        compiler_params=_params(("parallel",)),
        name="merge",
    )(p, p, p, p, p, hm, x, wsp, bsp_t, g_vn, b_vn, w_out, g_post, gt1, g_pre, sc2, sh2, wr)


def _select(logits, br_ref):
    n_exp, tm = logits.shape
    per = n_exp // N_GROUPS
    neg = -jnp.inf
    s = _sigmoid(logits)
    sel = s + br_ref[...]
    mem = [sel[r * N_GROUPS:(r + 1) * N_GROUPS, :] for r in range(per)]
    grow = lax.broadcasted_iota(jnp.int32, (N_GROUPS, tm), 0)
    m1 = functools.reduce(jnp.maximum, mem)
    idx1 = functools.reduce(jnp.minimum, [jnp.where(mem[r] == m1, r, per) for r in range(per)])
    m2 = functools.reduce(jnp.maximum, [jnp.where(idx1 == r, neg, mem[r]) for r in range(per)])
    gs = m1 + m2
    gmask = jnp.zeros((N_GROUPS, tm), jnp.bool_)
    for _ in range(TOPK_GROUPS):
        mx = jnp.max(gs, axis=0, keepdims=True)
        gi = jnp.min(jnp.where(gs == mx, grow, N_GROUPS), axis=0, keepdims=True)
        pick = grow == gi
        gmask = jnp.logical_or(gmask, pick)
        gs = jnp.where(pick, neg, gs)
    msk = [jnp.where(gmask, mem[r], neg) for r in range(per)]
    eidx = [grow * per + r for r in range(per)]
    chosen = [jnp.zeros((N_GROUPS, tm), jnp.bool_) for _ in range(per)]
    firsts = []
    for _ in range(TOP_K):
        mx = jnp.max(functools.reduce(jnp.maximum, msk), axis=0, keepdims=True)
        cand = functools.reduce(jnp.minimum, [jnp.where(msk[r] == mx, eidx[r], n_exp) for r in range(per)])
        first = jnp.min(cand, axis=0, keepdims=True)
        firsts.append(first)
        for r in range(per):
            pick = eidx[r] == first
            chosen[r] = jnp.logical_or(chosen[r], pick)
            msk[r] = jnp.where(pick, neg, msk[r])
    wk = [jnp.where(chosen[r], s[r * N_GROUPS:(r + 1) * N_GROUPS, :], 0.0) for r in range(per)]
    denom = jnp.sum(functools.reduce(jnp.add, wk), axis=0, keepdims=True)
    return s, chosen, firsts, eidx, denom


def _shared_expert(hb, ws1_ref, ws3_ref, ws2_ref):
    a = jnp.dot(hb, ws1_ref[...], preferred_element_type=F32)
    g = jnp.dot(hb, ws3_ref[...], preferred_element_type=F32)
    return jnp.dot((_silu(a) * g).astype(BF16), ws2_ref[...], preferred_element_type=F32)


def _moe_kernel(h_ref, x1_ref, lg_ref, br_ref, w1_ref, w3_ref, w2_ref, ws1_ref, ws3_ref, ws2_ref, gpost_ref,
                gt2_ref, out_ref, gt_scr, gates_scr, acc_scr, *, eb):
    j = pl.program_id(1)
    hb = _unpack_pairs(h_ref[...]).astype(BF16)
    n_exp = lg_ref.shape[0]
    per = n_exp // N_GROUPS

    @pl.when(j == 0)
    def _():
        s, chosen, _, _, denom = _select(lg_ref[...], br_ref)
        gt_scr[n_exp:, :] = jnp.zeros((gt_scr.shape[0] - n_exp, hb.shape[0]), F32)
        for r in range(per):
            wk = jnp.where(chosen[r], s[r * N_GROUPS:(r + 1) * N_GROUPS, :], 0.0)
            gt_scr[r * N_GROUPS:(r + 1) * N_GROUPS, :] = wk / denom * ROUTE_SCALE
        gates_scr[...] = gt_scr[...].T
        acc_scr[...] = _shared_expert(hb, ws1_ref, ws3_ref, ws2_ref)

    first = j * eb
    lane0 = lax.rem(first, per) * N_GROUPS + first // per
    gates = pltpu.roll(gates_scr[...], lax.rem(LANES - lane0, LANES), 1)
    acc = acc_scr[...]
    for e in range(eb):
        a = jnp.dot(hb, w1_ref[e].astype(BF16), preferred_element_type=F32)
        g = jnp.dot(hb, w3_ref[e].astype(BF16), preferred_element_type=F32)
        gate = gates[:, e * N_GROUPS:e * N_GROUPS + 1]
        acc = acc + jnp.dot((_silu(a) * g * gate).astype(BF16), w2_ref[e].astype(BF16),
                            preferred_element_type=F32)
    acc_scr[...] = acc

    @pl.when(j == pl.num_programs(1) - 1)
    def _():
        out_ref[...] = x1_ref[...] + gt2_ref[0] * _rms(acc_scr[...], gpost_ref[...])


def _moe_dense(h2, x1, logits, lw, gt2, tm, t_seq, eb):
    n, d = x1.shape
    _, n_exp, _, f = lw["w1"].shape
    layer = lw["layer"]
    fs = lw["ws1"].shape[1]
    tps = max(t_seq // tm, 1)
    const = lambda i, j: (0, 0)
    return pl.pallas_call(
        functools.partial(_moe_kernel, eb=eb),
        grid=(n // tm, n_exp // eb),
        in_specs=[
            pl.BlockSpec((tm, d // 2), lambda i, j: (i, 0)),
            pl.BlockSpec((tm, d), lambda i, j: (i, 0)),
            pl.BlockSpec((n_exp, tm), lambda i, j: (0, i)),
            pl.BlockSpec((n_exp, 1), const),
            pl.BlockSpec((None, eb, d, f), lambda i, j: (layer, j, 0, 0)),
            pl.BlockSpec((None, eb, d, f), lambda i, j: (layer, j, 0, 0)),
            pl.BlockSpec((None, eb, f, d), lambda i, j: (layer, j, 0, 0)),
            pl.BlockSpec((d, fs), const),
            pl.BlockSpec((d, fs), const),
            pl.BlockSpec((fs, d), const),
            pl.BlockSpec((1, d), const),
            _mod_spec(gt2, tm, tps, 2),
        ],
        out_specs=pl.BlockSpec((tm, d), lambda i, j: (i, 0)),
        out_shape=jax.ShapeDtypeStruct((n, d), F32),
        scratch_shapes=[
            pltpu.VMEM((LANES, tm), F32),
            pltpu.VMEM((tm, LANES), F32),
            pltpu.VMEM((tm, d), F32),
        ],
        compiler_params=_params(("parallel", "arbitrary")),
        name="moe_dense",
    )(h2, x1, logits, lw["br"], lw["w1"], lw["w3"], lw["w2"], lw["ws1"], lw["ws3"], lw["ws2"],
      lw["g_post_ffn"], gt2)


def _router_kernel(lg_ref, br_ref, eidx_ref, rank_ref, gate_ref, cnt_ref, carry_scr):
    n_exp, tm = lg_ref.shape
    per = n_exp // N_GROUPS

    @pl.when(pl.program_id(0) == 0)
    def _():
        carry_scr[...] = jnp.zeros(carry_scr.shape, F32)

    s, chosen, firsts, eidx, denom = _select(lg_ref[...], br_ref)
    sel01 = jnp.concatenate([c.astype(F32) for c in chosen], axis=0)
    before = (lax.broadcasted_iota(jnp.int32, (tm, tm), 0)
              < lax.broadcasted_iota(jnp.int32, (tm, tm), 1)).astype(BF16)
    rank = jnp.dot(sel01.astype(BF16), before, preferred_element_type=F32) + carry_scr[:, 0:1]
    carry_scr[...] = carry_scr[...] + jnp.sum(sel01, axis=1, keepdims=True)
    cnt_ref[...] = carry_scr[...]
    for k in range(TOP_K):
        s_k = jnp.zeros((1, tm), F32)
        r_k = jnp.zeros((1, tm), F32)
        for r in range(per):
            pick = eidx[r] == firsts[k]
            rows = slice(r * N_GROUPS, (r + 1) * N_GROUPS)
            s_k = s_k + jnp.sum(jnp.where(pick, s[rows, :], 0.0), axis=0, keepdims=True)
            r_k = r_k + jnp.sum(jnp.where(pick, rank[rows, :], 0.0), axis=0, keepdims=True)
        eidx_ref[k:k + 1, :] = firsts[k]
        rank_ref[k:k + 1, :] = r_k.astype(jnp.int32)
        gate_ref[k:k + 1, :] = s_k / denom * ROUTE_SCALE


def _router(logits, br, tm, row0, n):
    n_exp = logits.shape[0]
    tok = pl.BlockSpec((TOP_K, tm), lambda i: (0, i))
    blk0 = row0 // tm
    return pl.pallas_call(
        _router_kernel,
        grid=(n // tm,),
        in_specs=[
            pl.BlockSpec((n_exp, tm), lambda i: (0, i + blk0)),
            pl.BlockSpec((n_exp, 1), lambda i: (0, 0)),
        ],
        out_specs=[tok, tok, tok, pl.BlockSpec((n_exp, LANES), lambda i: (0, 0))],
        out_shape=[
            jax.ShapeDtypeStruct((TOP_K, n), jnp.int32),
            jax.ShapeDtypeStruct((TOP_K, n), jnp.int32),
            jax.ShapeDtypeStruct((TOP_K, n), F32),
            jax.ShapeDtypeStruct((n_exp, LANES), F32),
        ],
        scratch_shapes=[pltpu.VMEM((n_exp, LANES), F32)],
        compiler_params=_params(("arbitrary",)),
        name="router",
    )(logits, br)


def _pos_kernel(starts_ref, eidx_ref, rank_ref, pos_ref):
    e = eidx_ref[...]
    base = jnp.zeros_like(e)
    for x in range(starts_ref.shape[0]):
        base = base + jnp.where(e == x, starts_ref[x], 0)
    pos = base + rank_ref[...]
    for c in range(pos_ref.shape[0]):
        pos_ref[c] = pos[:, c * SC_BATCH:(c + 1) * SC_BATCH]


def _positions(starts, eidx, rank, tm):
    k, n = eidx.shape
    grid_spec = pltpu.PrefetchScalarGridSpec(
        num_scalar_prefetch=1,
        grid=(n // tm,),
        in_specs=[pl.BlockSpec((k, tm), lambda i, s: (0, i)), pl.BlockSpec((k, tm), lambda i, s: (0, i))],
        out_specs=pl.BlockSpec((tm // SC_BATCH, k, SC_BATCH), lambda i, s: (i, 0, 0)),
    )
    return pl.pallas_call(
        _pos_kernel,
        grid_spec=grid_spec,
        out_shape=jax.ShapeDtypeStruct((n // SC_BATCH, k, SC_BATCH), jnp.int32),
        compiler_params=_params(("parallel",)),
        name="positions",
    )(starts, eidx, rank)


SC_BATCH = 128
SC_WORKERS = 32


def _sc_mesh():
    return plsc.VectorSubcoreMesh(core_axis_name="c", subcore_axis_name="s")


def _sc_scatter(x, pos, n_rows, row0):
    w = x.shape[1]
    n = pos.shape[0] * SC_BATCH
    info = plsc.get_sparse_core_info()
    nc, nw = info.num_cores, info.num_cores * info.num_subcores
    steps = n // (nw * SC_BATCH)

    @functools.partial(
        pl.kernel, mesh=_sc_mesh(),
        out_type=jax.ShapeDtypeStruct((n_rows, w), x.dtype),
        scratch_types=[pltpu.VMEM((TOP_K, SC_BATCH), jnp.int32), pltpu.VMEM((SC_BATCH, w), x.dtype),
                       pltpu.SemaphoreType.DMA],
    )
    def scatter(x_hbm, pos_hbm, out_hbm, idx_v, rows_v, sem):
        wid = lax.axis_index("s") * nc + lax.axis_index("c")

        @pl.loop(0, steps)
        def _(s):
            blk = wid * steps + s
            pltpu.sync_copy(pos_hbm.at[blk], idx_v)
            pltpu.sync_copy(x_hbm.at[pl.ds(row0 + blk * SC_BATCH, SC_BATCH)], rows_v)
            copies = [pltpu.async_copy(rows_v, out_hbm.at[idx_v.at[k]], sem) for k in range(TOP_K)]
            for c in copies:
                c.wait()

    return scatter(x, pos)


def _sc_gather(y, pos):
    w = y.shape[1]
    n = pos.shape[0] * SC_BATCH
    info = plsc.get_sparse_core_info()
    nc, nw = info.num_cores, info.num_cores * info.num_subcores
    steps = n // (nw * SC_BATCH)

    @functools.partial(
        pl.kernel, mesh=_sc_mesh(),
        out_type=jax.ShapeDtypeStruct((TOP_K, n, w), y.dtype),
        scratch_types=[pltpu.VMEM((TOP_K, SC_BATCH), jnp.int32), pltpu.VMEM((SC_BATCH, w), y.dtype),
                       pltpu.SemaphoreType.DMA],
    )
    def gather(y_hbm, pos_hbm, out_hbm, idx_v, rows_v, sem):
        wid = lax.axis_index("s") * nc + lax.axis_index("c")

        @pl.loop(0, steps)
        def _(s):
            blk = wid * steps + s
            pltpu.sync_copy(pos_hbm.at[blk], idx_v)
            for k in range(TOP_K):
                pltpu.async_copy(y_hbm.at[idx_v.at[k]], rows_v, sem).wait()
                pltpu.sync_copy(rows_v, out_hbm.at[k, pl.ds(blk * SC_BATCH, SC_BATCH)])

    return gather(y, pos)


def _experts_kernel(te_ref, nu_ref, xs_ref, w1_ref, w3_ref, w2_ref, ys_ref, w1_scr, w3_scr, w2_scr):
    i = pl.program_id(0)

    @pl.when(jnp.logical_or(i == 0, te_ref[i] != te_ref[jnp.maximum(i - 1, 0)]))
    def _():
        w1_scr[...] = w1_ref[...].astype(BF16)
        w3_scr[...] = w3_ref[...].astype(BF16)
        w2_scr[...] = w2_ref[...].astype(BF16)

    @pl.when(i < nu_ref[0])
    def _():
        x = _unpack_pairs(xs_ref[...]).astype(BF16)
        a = jnp.dot(x, w1_scr[...], preferred_element_type=F32)
        g = jnp.dot(x, w3_scr[...], preferred_element_type=F32)
        y = jnp.dot((_silu(a) * g).astype(BF16), w2_scr[...], preferred_element_type=F32)
        ys_ref[...] = _pack_pairs(y)


def _experts(xs, tile_expert, n_used, w1, w3, w2, layer, rt):
    rows, half = xs.shape
    _, n_exp, d, f = w1.shape
    scratch = [pltpu.VMEM((d, f), BF16), pltpu.VMEM((d, f), BF16), pltpu.VMEM((f, d), BF16)]
    grid_spec = pltpu.PrefetchScalarGridSpec(
        num_scalar_prefetch=2,
        grid=(rows // rt,),
        in_specs=[
            pl.BlockSpec((rt, half), lambda i, te, nu: (i, 0)),
            pl.BlockSpec((None, None, d, f), lambda i, te, nu: (layer, te[i], 0, 0)),
            pl.BlockSpec((None, None, d, f), lambda i, te, nu: (layer, te[i], 0, 0)),
            pl.BlockSpec((None, None, f, d), lambda i, te, nu: (layer, te[i], 0, 0)),
        ],
        out_specs=pl.BlockSpec((rt, half), lambda i, te, nu: (i, 0)),
        scratch_shapes=scratch,
    )
    return pl.pallas_call(
        _experts_kernel,
        grid_spec=grid_spec,
        out_shape=jax.ShapeDtypeStruct((rows, half), jnp.uint32),
        compiler_params=_params(("arbitrary",)),
        name="experts",
    )(tile_expert, n_used, xs, w1, w3, w2)


def _combine_kernel(yk_ref, g_ref, h_ref, x1_ref, ws1_ref, ws3_ref, ws2_ref, gpost_ref, gt2_ref, *rest):
    out_ref = rest[-1]
    hb = _unpack_pairs(h_ref[...]).astype(BF16)
    acc = _shared_expert(hb, ws1_ref, ws3_ref, ws2_ref)
    gates = g_ref[...]
    for k in range(TOP_K):
        acc = acc + gates[:, k:k + 1] * _unpack_pairs(yk_ref[k])
    out_ref[...] = x1_ref[...] + gt2_ref[0] * _rms(acc, gpost_ref[...])


def _combine(yk, gates_t, h2, x1, lw, gt2, tm, t_seq, row0, out_rows, out_row0, out_prev):
    d = x1.shape[1]
    nc = yk.shape[1]
    fs = lw["ws1"].shape[1]
    tps = max(t_seq // tm, 1)
    blk0 = row0 // tm
    out_blk0 = (out_row0 + row0) // tm
    const = lambda i: (0, 0)
    in_specs = [
        pl.BlockSpec((TOP_K, tm, d // 2), lambda i: (0, i, 0)),
        pl.BlockSpec((tm, TOP_K), lambda i: (i, 0)),
        pl.BlockSpec((tm, d // 2), lambda i: (i + blk0, 0)),
        pl.BlockSpec((tm, d), lambda i: (i + blk0, 0)),
        pl.BlockSpec((d, fs), const),
        pl.BlockSpec((d, fs), const),
        pl.BlockSpec((fs, d), const),
        pl.BlockSpec((1, d), const),
        pl.BlockSpec((1, 1, d), lambda i: ((i + blk0) // tps, 0, 0)),
    ]
    args = [yk, gates_t, h2, x1, lw["ws1"], lw["ws3"], lw["ws2"], lw["g_post_ffn"], gt2]
    aliases = {}
    if out_prev is not None:
        in_specs.append(pl.BlockSpec(memory_space=pl.ANY))
        aliases = {len(args): 0}
        args.append(out_prev)
    return pl.pallas_call(
        _combine_kernel,
        grid=(nc // tm,),
        in_specs=in_specs,
        out_specs=pl.BlockSpec((tm, d), lambda i: (i + out_blk0, 0)),
        out_shape=jax.ShapeDtypeStruct((out_rows, d), F32),
        input_output_aliases=aliases,
        compiler_params=_params(("parallel",)),
        name="combine",
    )(*args)


EXPERT_ROWS = 512
MOE_CHUNKS = 1
PROMPT_CHAINS = 4


def _moe_dispatched(h2, x1, logits, lw, gt2, t_seq, dest):
    n, d = x1.shape
    out_rows, out_row0, out = dest if dest is not None else (n, 0, None)
    n_exp = logits.shape[0]
    per = n_exp // N_GROUPS
    rt = EXPERT_ROWS
    chunks = MOE_CHUNKS
    while n % (chunks * SC_WORKERS * SC_BATCH):
        chunks //= 2
    nc = n // chunks
    n_tiles = nc * TOP_K // rt + n_exp
    tm = _tile(min(nc, t_seq), 512)
    routed = [_router(logits, lw["br"], _tile(nc, 1024), c * nc, nc) for c in range(chunks)]
    for c, (eidx, rank, gate, cnt) in enumerate(routed):
        counts = cnt[:, 0].astype(jnp.int32).reshape(per, N_GROUPS).T.reshape(n_exp)
        padded = (counts + rt - 1) // rt * rt
        ends = jnp.cumsum(padded)
        starts = ends - padded
        pos = _positions(starts, eidx, rank, _tile(nc, 2048))
        tile_start = jnp.arange(n_tiles, dtype=jnp.int32) * rt
        tile_expert = jnp.minimum(jnp.sum(ends[None, :] <= tile_start[:, None], axis=1), n_exp - 1).astype(jnp.int32)
        n_used = (ends[-1:] // rt).astype(jnp.int32)
        xs = _sc_scatter(h2, pos, n_tiles * rt, c * nc)
        ys = _experts(xs, tile_expert, n_used, lw["w1"], lw["w3"], lw["w2"], lw["layer"], rt)
        yk = _sc_gather(ys, pos)
        out = _combine(yk, gate.T, h2, x1, lw, gt2, tm, t_seq, c * nc, out_rows, out_row0, out)
    return out


def _split_router(w_router, n_exp, d):
    wrt = w_router.T.reshape(N_GROUPS, n_exp // N_GROUPS, d).transpose(1, 0, 2).reshape(n_exp, d)
    high = wrt.astype(BF16)
    return jnp.concatenate([high, (wrt - high.astype(F32)).astype(BF16)], axis=0)


def _prep_layer(l, w):
    d = w["w_in"].shape[1]
    n_exp = w["w_e1"].shape[1]
    H = N_HEADS
    w_in = w["w_in"][l]
    qkw = d
    g0 = 3 * d
    w_main = jnp.concatenate([w_in[:, :g0], w_in[:, g0 + 2 * H:]], axis=1).astype(BF16)
    wg = jnp.pad(w_in[:, g0:g0 + 2 * H], ((0, 0), (0, LANES - 2 * H)))
    wg_high = wg.astype(BF16)
    w_gate = jnp.concatenate([wg_high, (wg - wg_high.astype(F32)).astype(BF16)], axis=1)
    gbias = jnp.pad(jnp.concatenate([w["b_igate"][l], w["b_fgate"][l]]), (0, LANES - 2 * H)).reshape(1, LANES)
    return dict(
        w_main=w_main, w_gate=w_gate, gbias=gbias,
        g_pre_mix=w["g_pre_mix"][l].reshape(1, d), g_post_mix=w["g_post_mix"][l].reshape(1, d),
        g_pre_ffn=w["g_pre_ffn"][l].reshape(1, d), g_post_ffn=w["g_post_ffn"][l].reshape(1, d),
        w_conv=w["w_conv"][l], b_conv=w["b_conv"][l].reshape(1, qkw),
        g_head=w["g_head"][l].reshape(1, d), g_vnorm=w["g_vnorm"][l].reshape(1, d),
        b_vnorm=w["b_vnorm"][l].reshape(1, d),
        w_spatial=w["w_spatial"][l], b_spatial=w["b_spatial"][l],
        w_out=w["w_out"][l].astype(BF16),
        wr=_split_router(w["w_router"][l], n_exp, d),
        br=w["b_router"][l].reshape(N_GROUPS, n_exp // N_GROUPS).T.reshape(n_exp, 1),
        w1=w["w_e1"], w3=w["w_e3"], w2=w["w_e2"], layer=l,
        ws1=w["w_s1"][l].astype(BF16), ws3=w["w_s3"][l].astype(BF16), ws2=w["w_s2"][l].astype(BF16),
    )


def _spatial_weights(lw, start, gl):
    idx = jnp.arange(GMLP_CHUNK)
    mask = (idx[None, :] // CHUNK) <= (idx[:, None] // CHUNK)
    wsp = jnp.where(mask, lw["w_spatial"], 0.0)[:, start:start + gl, start:start + gl].astype(BF16)
    bsp_t = lw["b_spatial"][:, start:start + gl].T
    return wsp, bsp_t


def _tile(n, cap):
    t = cap
    while n % t:
        t //= 2
    return t


def _trunk(x, mods, c0, n0, m0, conv0, layers, mlstm_l, g_start, g_l, keep_v, seq0=0, nb=None, dest=None):
    b_all, t_seq, d = x.shape
    nb = b_all if nb is None else nb
    n = nb * t_seq
    per_token = mods[0][0].shape[1] != 1
    tok_cap = n if per_token else t_seq
    tm_in = _tile(tok_cap, 1024)
    tm_mg = _tile(tok_cap, 512)
    tm_moe = _tile(tok_cap, 1024)
    xf = x.reshape(b_all * t_seq, d)
    x_row0 = seq0 * t_seq
    outs = [[] for _ in range(5 if keep_v else 4)]
    for l, lw in enumerate(layers):
        sh1, sc1, gt1, sh2, sc2, gt2 = mods[l]
        p, gates = _inproj(xf, sc1, sh1, lw["g_pre_mix"], lw["w_main"], lw["w_gate"], tm_in, t_seq, x_row0, n)
        m0p = jnp.pad(m0[l], ((0, 0), (0, LANES - m0[l].shape[1]))).reshape(nb, 1, LANES)
        hm, c_new, n_new, m_new, conv_new = _mlstm(
            p, gates, c0[l], n0[l], m0p, conv0[l], lw["w_conv"], lw["b_conv"], lw["gbias"], lw["g_head"],
            nb, mlstm_l)
        wsp, bsp_t = _spatial_weights(lw, g_start, g_l)
        res = _merge(p, hm, xf, wsp, bsp_t, lw["g_vnorm"], lw["b_vnorm"], lw["w_out"], lw["g_post_mix"], gt1,
                     lw["g_pre_ffn"], sc2, sh2, lw["wr"], tm_mg, t_seq, keep_v, x_row0)
        x1, h2, logits = res[0], res[1], res[2]
        last = l == len(layers) - 1
        if n % (SC_WORKERS * SC_BATCH) == 0 and not per_token:
            shared = None
            if last and dest is not None:
                buf = None if dest[0] is None else dest[0].reshape(b_all * t_seq, d)
                shared = (b_all * t_seq, seq0 * t_seq, buf)
            xf = _moe_dispatched(h2, x1, logits, lw, gt2, t_seq, shared)
        else:
            xf = _moe_dense(h2, x1, logits, lw, gt2, tm_moe, t_seq, 4)
        x_row0 = 0
        vals = [c_new, n_new, m_new[:, 0, :N_HEADS], conv_new]
        if keep_v:
            vals.append(res[3].reshape(nb, t_seq, d))
        for lst, val in zip(outs, vals):
            lst.append(val)
    return xf.reshape(-1, t_seq, d), tuple(jnp.stack(lst) for lst in outs)


def kernel(x_prompt, x_sample, c_prompt, c_sample, state_mlstm_C, state_mlstm_n, state_mlstm_m, state_conv,
           w_ada, b_ada, g_pre_mix, g_post_mix, g_pre_ffn, g_post_ffn, w_in, b_igate, b_fgate, w_conv, b_conv,
           g_head, g_vnorm, b_vnorm, w_spatial, b_spatial, w_out, w_router, b_router, w_e1, w_e3, w_e2,
           w_s1, w_s3, w_s2):
    w = dict(w_in=w_in, b_igate=b_igate, b_fgate=b_fgate, g_pre_mix=g_pre_mix, g_post_mix=g_post_mix,
             g_pre_ffn=g_pre_ffn, g_post_ffn=g_post_ffn, w_conv=w_conv, b_conv=b_conv, g_head=g_head,
             g_vnorm=g_vnorm, b_vnorm=b_vnorm, w_spatial=w_spatial, b_spatial=b_spatial, w_out=w_out,
             w_router=w_router, b_router=b_router, w_e1=w_e1, w_e3=w_e3, w_e2=w_e2, w_s1=w_s1, w_s3=w_s3,
             w_s2=w_s2)
    depth = w_in.shape[0]
    bp, tp, d = x_prompt.shape
    bs, ts, _ = x_sample.shape
    H, dv, dqk = state_mlstm_C.shape[2:]
    qkw = state_conv.shape[-1]
    layers = [_prep_layer(l, w) for l in range(depth)]

    mod = _ada(jnp.concatenate([c_prompt, c_sample], axis=0), w_ada, b_ada)
    mods_p, mods_s = [], []
    for l in range(depth):
        parts = [mod[l][:, i * d:(i + 1) * d] for i in range(6)]
        mods_p.append([a[:bp].reshape(bp, 1, d) for a in parts])
        mods_s.append([jnp.repeat(a[bp:], ts, axis=0).reshape(1, bs * ts, d) for a in parts])

    lp = 256 if tp % 256 == 0 else CHUNK
    chains = PROMPT_CHAINS if bp % PROMPT_CHAINS == 0 else 1
    bc = bp // chains
    zc = jnp.zeros((depth, bc, H, dv, dqk), F32)
    zn = jnp.zeros((depth, bc, H, dqk), F32)
    zm = jnp.zeros((depth, bc, H), F32)
    zconv = jnp.zeros((depth, bc, CONV_W - 1, qkw), F32)
    shared = chains > 1 and (bc * tp) % (SC_WORKERS * SC_BATCH) == 0
    parts, y_p = [], None
    for ci in range(chains):
        rows = slice(ci * bc, (ci + 1) * bc)
        mods_c = [[a[rows] for a in layer_mods] for layer_mods in mods_p]
        if shared:
            y_p, states = _trunk(x_prompt, mods_c, zc, zn, zm, zconv, layers, lp, 0, GMLP_CHUNK, False,
                                 seq0=ci * bc, nb=bc, dest=(y_p,))
        else:
            y_c, states = _trunk(x_prompt[rows], mods_c, zc, zn, zm, zconv, layers, lp, 0, GMLP_CHUNK, False)
            y_p = y_c if y_p is None else jnp.concatenate([y_p, y_c], axis=0)
        parts.append(states)
    p_c, p_n, p_m, p_conv = (jnp.concatenate([s[i] for s in parts], axis=1) for i in range(4))
    g_start = PAST_LEN % GMLP_CHUNK
    y_s, (s_c, s_n, s_m, s_conv, s_v) = _trunk(x_sample, mods_s, state_mlstm_C, state_mlstm_n, state_mlstm_m,
                                               state_conv, layers, ts, g_start, ts, True)
    return (y_p, y_s, p_c, p_n, p_m, p_conv, s_c, s_n, s_m, s_conv, s_v)
```

```python
import functools

import jax
import jax.numpy as jnp
from jax import lax
from jax.experimental import pallas as pl
from jax.experimental.pallas import tpu as pltpu
from jax.experimental.pallas import tpu_sc as plsc

F32 = jnp.float32
BF16 = jnp.bfloat16

EPS = 1e-6
N_HEADS = 4
CONV_W = 4
GMLP_GROUPS = 8
GMLP_CHUNK = 128
CHUNK = 64
N_GROUPS = 8
TOPK_GROUPS = 4
TOP_K = 8
ROUTE_SCALE = 2.5
PAST_LEN = 4096

LANES = 128
SUBLANES = 8
VMEM_LIMIT = 56 * 1024 * 1024

NT_DIMS = (((1,), (1,)), ((), ()))
TN_DIMS = (((0,), (0,)), ((), ()))


def _sigmoid(x):
    return 0.5 * (jnp.tanh(0.5 * x) + 1.0)


def _silu(x):
    return x * _sigmoid(x)


def _gelu_tanh(x):
    return x * (0.5 * (1.0 + jnp.tanh(0.7978845608028654 * (x + 0.044715 * (x * x * x)))))


def _rms(x, g):
    return x * lax.rsqrt(jnp.mean(x * x, -1, keepdims=True) + EPS) * g


def _pack_pairs(x):
    c = x.shape[1] // 2
    return pltpu.pack_elementwise([x[:, :c], x[:, c:]], packed_dtype=BF16)


def _unpack_pairs(u):
    lo = pltpu.unpack_elementwise(u, index=0, packed_dtype=BF16, unpacked_dtype=F32)
    hi = pltpu.unpack_elementwise(u, index=1, packed_dtype=BF16, unpacked_dtype=F32)
    return jnp.concatenate([lo, hi], axis=1)


def _params(sem):
    return pltpu.CompilerParams(dimension_semantics=sem, vmem_limit_bytes=VMEM_LIMIT)


def _ada_kernel(c_ref, w_ref, b_ref, o_ref):
    a = _silu(c_ref[...]).astype(BF16)
    o_ref[0] = jnp.dot(a, w_ref[0].astype(BF16), preferred_element_type=F32) + b_ref[0]


def _ada(c, w_ada, b_ada):
    depth, d, six_d = w_ada.shape
    nb = c.shape[0]
    tn = d
    return pl.pallas_call(
        _ada_kernel,
        grid=(depth, six_d // tn),
        in_specs=[
            pl.BlockSpec((nb, d), lambda l, j: (0, 0)),
            pl.BlockSpec((1, d, tn), lambda l, j: (l, 0, j)),
            pl.BlockSpec((1, 1, tn), lambda l, j: (l, 0, j)),
        ],
        out_specs=pl.BlockSpec((1, nb, tn), lambda l, j: (l, 0, j)),
        out_shape=jax.ShapeDtypeStruct((depth, nb, six_d), F32),
        compiler_params=_params(("parallel", "parallel")),
        name="adaln",
    )(c, w_ada, b_ada.reshape(depth, 1, six_d))


def _inproj_kernel(x_ref, sc_ref, sh_ref, g_ref, w_ref, wg_ref, p_ref, gates_ref, h_scr):
    @pl.when(pl.program_id(1) == 0)
    def _():
        h = _rms(x_ref[...], g_ref[...]) * (1.0 + sc_ref[0]) + sh_ref[0]
        hb = h.astype(BF16)
        h_scr[...] = hb
        h_low = (h - hb.astype(F32)).astype(BF16)
        gg = jnp.dot(hb, wg_ref[...], preferred_element_type=F32)
        gates_ref[...] = (gg[:, :LANES] + gg[:, LANES:]
                          + jnp.dot(h_low, wg_ref[:, :LANES], preferred_element_type=F32))

    p_ref[...] = jnp.dot(h_scr[...], w_ref[...], preferred_element_type=F32).astype(BF16)


def _mod_spec(mod, tm, tiles_per_seq, ngrid):
    d = mod.shape[-1]
    if mod.shape[1] == 1:
        if ngrid == 2:
            return pl.BlockSpec((1, 1, d), lambda i, j: (i // tiles_per_seq, 0, 0))
        return pl.BlockSpec((1, 1, d), lambda i: (i // tiles_per_seq, 0, 0))
    if ngrid == 2:
        return pl.BlockSpec((1, tm, d), lambda i, j: (0, i, 0))
    return pl.BlockSpec((1, tm, d), lambda i: (0, i, 0))


def _inproj(x, sc, sh, g, w_main, w_gate, tm, t_seq, row0, n):
    d = x.shape[1]
    blk0 = row0 // tm
    width = w_main.shape[1]
    ncol = 2 if (width // 2) % LANES == 0 else width // d
    wc = width // ncol
    tps = max(t_seq // tm, 1)
    return pl.pallas_call(
        _inproj_kernel,
        grid=(n // tm, ncol),
        in_specs=[
            pl.BlockSpec((tm, d), lambda i, j: (i + blk0, 0)),
            _mod_spec(sc, tm, tps, 2),
            _mod_spec(sh, tm, tps, 2),
            pl.BlockSpec((1, d), lambda i, j: (0, 0)),
            pl.BlockSpec((d, wc), lambda i, j: (0, j)),
            pl.BlockSpec((d, 2 * LANES), lambda i, j: (0, 0)),
        ],
        out_specs=[
            pl.BlockSpec((tm, wc), lambda i, j: (i, j)),
            pl.BlockSpec((tm, LANES), lambda i, j: (i, 0)),
        ],
        out_shape=[
            jax.ShapeDtypeStruct((n, width), BF16),
            jax.ShapeDtypeStruct((n, LANES), F32),
        ],
        scratch_shapes=[pltpu.VMEM((tm, d), BF16)],
        compiler_params=_params(("parallel", "arbitrary")),
        name="inproj",
    )(x, sc, sh, g, w_main, w_gate)


def _mlstm_kernel(qk_ref, v_ref, gt_ref, c0_ref, n0_ref, m0_ref, conv0_ref, wconv_ref, bconv_ref,
                  gbias_ref, ghead_ref, hm_ref, c_ref, n_ref, m_ref, conv_ref, xbuf, *, L, H, dqk, dv):
    qkw = 2 * H * dqk
    pad = SUBLANES
    tail0 = pad - (CONV_W - 1)

    @pl.when(pl.program_id(1) == 0)
    def _():
        c_ref[...] = c0_ref[...]
        n_ref[...] = n0_ref[...]
        m_ref[...] = m0_ref[...]
        xbuf[0:pad, :] = jnp.zeros((pad, qkw), F32)
        xbuf[tail0:pad, :] = conv0_ref[0]

    xbuf[pad:pad + L, :] = qk_ref[...].astype(F32)
    w = wconv_ref[...]
    y = xbuf[tail0:tail0 + L, :] * w[0:1, :]
    for j in range(1, CONV_W):
        y = y + xbuf[tail0 + j:tail0 + j + L, :] * w[j:j + 1, :]
    y = y + bconv_ref[...]
    tail = xbuf[L + tail0:L + pad, :]
    conv_ref[0] = tail
    xbuf[tail0:pad, :] = tail
    qk = _silu(y)

    z = gt_ref[...] + gbias_ref[...]
    lf = jnp.minimum(z, 0.0) - jnp.log(1.0 + jnp.exp(-jnp.abs(z)))
    row = lax.broadcasted_iota(jnp.int32, (L, L), 0)
    col = lax.broadcasted_iota(jnp.int32, (L, L), 1)
    tri = row >= col
    hi = lax.Precision.HIGHEST
    b_col = jnp.dot(tri.astype(F32), lf, precision=hi, preferred_element_type=F32)
    eye = (lax.broadcasted_iota(jnp.int32, (LANES, LANES), 0)
           == lax.broadcasted_iota(jnp.int32, (LANES, LANES), 1)).astype(F32)
    b_row = lax.dot_general(eye, b_col, NT_DIMS, precision=hi, preferred_element_type=F32)
    z_row = lax.dot_general(eye, z, NT_DIMS, precision=hi, preferred_element_type=F32)

    scale = dqk ** -0.5
    lane1 = lax.broadcasted_iota(jnp.int32, (1, LANES), 1)
    m_old = m_ref[0]
    m_new = m_old
    for h in range(H):
        q = qk[:, h * dqk:(h + 1) * dqk] * scale
        k = qk[:, (H + h) * dqk:(H + h + 1) * dqk]
        qb = q.astype(BF16)
        vb = v_ref[:, h * dv:(h + 1) * dv]
        b_c = b_col[:, H + h:H + h + 1]
        i_c = z[:, h:h + 1]
        b_r = b_row[H + h:H + h + 1, :]
        i_r = z_row[h:h + 1, :]
        m_prev = m_old[:, h:h + 1]
        a = b_c + m_prev
        dm = jnp.where(tri, b_c - b_r + i_r, -jnp.inf)
        mt = jnp.maximum(a, jnp.max(dm, axis=-1, keepdims=True))
        s = lax.dot_general(qb, k.astype(BF16), NT_DIMS, preferred_element_type=F32)
        sg = s * jnp.exp(dm - mt)
        aw = jnp.exp(a - mt)
        cm = c_ref[0, h]
        qc = lax.dot_general(qb, cm.astype(BF16), NT_DIMS, preferred_element_type=F32)
        num = jnp.dot(sg.astype(BF16), vb, preferred_element_type=F32) + aw * qc
        nrow = n_ref[0, h:h + 1, :]
        den = jnp.sum(sg, -1, keepdims=True) + aw * jnp.sum(q * nrow, -1, keepdims=True)
        hh = num / jnp.maximum(jnp.abs(den), jnp.exp(-mt))
        m_last = mt[L - 1:L, :]
        b_last = b_c[L - 1:L, :]
        decay = jnp.exp(b_last + m_prev - m_last)
        ws = jnp.exp(b_last - b_c + i_c - m_last)
        kw = ws * k
        c_ref[0, h] = decay * cm + lax.dot_general(vb, kw.astype(BF16), TN_DIMS, preferred_element_type=F32)
        n_ref[0, h:h + 1, :] = decay * nrow + jnp.sum(kw, axis=0, keepdims=True)
        m_new = jnp.where(lane1 == h, m_last, m_new)
        hm_ref[:, h * dv:(h + 1) * dv] = _rms(hh, ghead_ref[:, h * dv:(h + 1) * dv]).astype(BF16)
    m_ref[0] = m_new


def _mlstm(p, gates, c0, n0, m0, conv0, w_conv, b_conv, gbias, g_head, nb, L):
    n = p.shape[0]
    t_seq = n // nb
    nc = t_seq // L
    _, H, dv, dqk = c0.shape
    qkw = 2 * H * dqk
    vw = H * dv
    kern = functools.partial(_mlstm_kernel, L=L, H=H, dqk=dqk, dv=dv)
    return pl.pallas_call(
        kern,
        grid=(nb, nc),
        in_specs=[
            pl.BlockSpec((L, qkw), lambda b, c: (b * nc + c, 0)),
            pl.BlockSpec((L, vw), lambda b, c: (b * nc + c, 1)),
            pl.BlockSpec((L, LANES), lambda b, c: (b * nc + c, 0)),
            pl.BlockSpec((1, H, dv, dqk), lambda b, c: (b, 0, 0, 0)),
            pl.BlockSpec((1, H, dqk), lambda b, c: (b, 0, 0)),
            pl.BlockSpec((1, 1, LANES), lambda b, c: (b, 0, 0)),
            pl.BlockSpec((1, CONV_W - 1, qkw), lambda b, c: (b, 0, 0)),
            pl.BlockSpec((CONV_W, qkw), lambda b, c: (0, 0)),
            pl.BlockSpec((1, qkw), lambda b, c: (0, 0)),
            pl.BlockSpec((1, LANES), lambda b, c: (0, 0)),
            pl.BlockSpec((1, vw), lambda b, c: (0, 0)),
        ],
        out_specs=[
            pl.BlockSpec((L, vw), lambda b, c: (b * nc + c, 0)),
            pl.BlockSpec((1, H, dv, dqk), lambda b, c: (b, 0, 0, 0)),
            pl.BlockSpec((1, H, dqk), lambda b, c: (b, 0, 0)),
            pl.BlockSpec((1, 1, LANES), lambda b, c: (b, 0, 0)),
            pl.BlockSpec((1, CONV_W - 1, qkw), lambda b, c: (b, 0, 0)),
        ],
        out_shape=[
            jax.ShapeDtypeStruct((n, vw), BF16),
            jax.ShapeDtypeStruct((nb, H, dv, dqk), F32),
            jax.ShapeDtypeStruct((nb, H, dqk), F32),
            jax.ShapeDtypeStruct((nb, 1, LANES), F32),
            jax.ShapeDtypeStruct((nb, CONV_W - 1, qkw), F32),
        ],
        scratch_shapes=[pltpu.VMEM((L + SUBLANES, qkw), F32)],
        compiler_params=_params(("parallel", "arbitrary")),
        name="mlstm",
    )(p, p, gates, c0, n0, m0, conv0, w_conv, b_conv, gbias, g_head)


def _merge_kernel(o_ref, u_ref, vg_ref, ga_ref, gb_ref, hm_ref, x_ref, wsp_ref, bsp_ref, gvn_ref, bvn_ref,
                  wout_ref, gpost_ref, gt1_ref, gpre_ref, sc2_ref, sh2_ref, wr_ref, *rest, tm, gl, keep_v):
    if keep_v:
        x1_ref, h2_ref, lg_ref, vn_ref, sg_scr = rest
    else:
        x1_ref, h2_ref, lg_ref, sg_scr = rest
    groups = wsp_ref.shape[0]
    ch = vg_ref.shape[1] // groups
    vg = _gelu_tanh(vg_ref[...].astype(F32))
    mu = jnp.mean(vg, -1, keepdims=True)
    xc = vg - mu
    vn = xc * lax.rsqrt(jnp.mean(xc * xc, -1, keepdims=True) + EPS) * gvn_ref[...] + bvn_ref[...]
    if keep_v:
        vn_ref[...] = vn
    vnb = vn.astype(BF16)
    for ci in range(tm // gl):
        for g in range(groups):
            blk = jnp.dot(wsp_ref[g], vnb[ci * gl:(ci + 1) * gl, g * ch:(g + 1) * ch],
                          preferred_element_type=F32)
            sg_scr[ci * gl:(ci + 1) * gl, g * ch:(g + 1) * ch] = blk + bsp_ref[:, g:g + 1]
    h_b = _gelu_tanh(u_ref[...].astype(F32)) * sg_scr[...]
    h_a = _sigmoid(o_ref[...].astype(F32)) * hm_ref[...].astype(F32)
    merged = _sigmoid(ga_ref[...].astype(F32)) * h_a + _sigmoid(gb_ref[...].astype(F32)) * h_b
    y = jnp.dot(merged.astype(BF16), wout_ref[...], preferred_element_type=F32)
    x1 = x_ref[...] + gt1_ref[0] * _rms(y, gpost_ref[...])
    x1_ref[...] = x1
    h2 = _rms(x1, gpre_ref[...]) * (1.0 + sc2_ref[0]) + sh2_ref[0]
    h2_ref[...] = _pack_pairs(h2)
    n_exp = lg_ref.shape[0]
    h2_high = h2.astype(BF16)
    h2_low = (h2 - h2_high.astype(F32)).astype(BF16)
    lg = lax.dot_general(wr_ref[...], h2_high, NT_DIMS, preferred_element_type=F32)
    lg_ref[...] = (lg[:n_exp] + lg[n_exp:]
                   + lax.dot_general(wr_ref[:n_exp, :], h2_low, NT_DIMS, preferred_element_type=F32))


def _merge(p, hm, x, wsp, bsp_t, g_vn, b_vn, w_out, g_post, gt1, g_pre, sc2, sh2, wr, tm, t_seq, keep_v, x_row0):
    n, d = hm.shape
    x_blk0 = x_row0 // tm
    groups, gl, _ = wsp.shape
    n_exp = wr.shape[0] // 2
    tps = max(t_seq // tm, 1)
    row = lambda i: (0, 0)
    pcol = lambda c: pl.BlockSpec((tm, d), lambda i: (i, c))
    tok = pl.BlockSpec((tm, d), lambda i: (i, 0))
    vec = pl.BlockSpec((1, d), row)
    out_specs = [tok, pl.BlockSpec((tm, d // 2), lambda i: (i, 0)), pl.BlockSpec((n_exp, tm), lambda i: (0, i))]
    out_shape = [jax.ShapeDtypeStruct((n, d), F32), jax.ShapeDtypeStruct((n, d // 2), jnp.uint32),
                 jax.ShapeDtypeStruct((n_exp, n), F32)]
    if keep_v:
        out_specs.append(tok)
        out_shape.append(jax.ShapeDtypeStruct((n, d), F32))
    kern = functools.partial(_merge_kernel, tm=tm, gl=gl, keep_v=keep_v)
    return pl.pallas_call(
        kern,
        grid=(n // tm,),
        in_specs=[
            pcol(2), pcol(3), pcol(4), pcol(5), pcol(6), tok,
            pl.BlockSpec((tm, d), lambda i: (i + x_blk0, 0)),
            pl.BlockSpec((groups, gl, gl), lambda i: (0, 0, 0)),
            pl.BlockSpec((gl, groups), row),
            vec, vec,
            pl.BlockSpec((d, d), row),
            vec,
            _mod_spec(gt1, tm, tps, 1),
            vec,
            _mod_spec(sc2, tm, tps, 1),
            _mod_spec(sh2, tm, tps, 1),
            pl.BlockSpec((2 * n_exp, d), row),
        ],
        out_specs=out_specs,
        out_shape=out_shape,
        scratch_shapes=[pltpu.VMEM((tm, d), F32)],
        compiler_params=_params(("parallel",)),
        name="merge",
    )(p, p, p, p, p, hm, x, wsp, bsp_t, g_vn, b_vn, w_out, g_post, gt1, g_pre, sc2, sh2, wr)


def _select(logits, br_ref):
    n_exp, tm = logits.shape
    per = n_exp // N_GROUPS
    neg = -jnp.inf
    s = _sigmoid(logits)
    sel = s + br_ref[...]
    mem = [sel[r * N_GROUPS:(r + 1) * N_GROUPS, :] for r in range(per)]
    grow = lax.broadcasted_iota(jnp.int32, (N_GROUPS, tm), 0)
    m1 = functools.reduce(jnp.maximum, mem)
    idx1 = functools.reduce(jnp.minimum, [jnp.where(mem[r] == m1, r, per) for r in range(per)])
    m2 = functools.reduce(jnp.maximum, [jnp.where(idx1 == r, neg, mem[r]) for r in range(per)])
    gs = m1 + m2
    gmask = jnp.zeros((N_GROUPS, tm), jnp.bool_)
    for _ in range(TOPK_GROUPS):
        mx = jnp.max(gs, axis=0, keepdims=True)
        gi = jnp.min(jnp.where(gs == mx, grow, N_GROUPS), axis=0, keepdims=True)
        pick = grow == gi
        gmask = jnp.logical_or(gmask, pick)
        gs = jnp.where(pick, neg, gs)
    msk = [jnp.where(gmask, mem[r], neg) for r in range(per)]
    eidx = [grow * per + r for r in range(per)]
    chosen = [jnp.zeros((N_GROUPS, tm), jnp.bool_) for _ in range(per)]
    firsts = []
    for _ in range(TOP_K):
        mx = jnp.max(functools.reduce(jnp.maximum, msk), axis=0, keepdims=True)
        cand = functools.reduce(jnp.minimum, [jnp.where(msk[r] == mx, eidx[r], n_exp) for r in range(per)])
        first = jnp.min(cand, axis=0, keepdims=True)
        firsts.append(first)
        for r in range(per):
            pick = eidx[r] == first
            chosen[r] = jnp.logical_or(chosen[r], pick)
            msk[r] = jnp.where(pick, neg, msk[r])
    wk = [jnp.where(chosen[r], s[r * N_GROUPS:(r + 1) * N_GROUPS, :], 0.0) for r in range(per)]
    denom = jnp.sum(functools.reduce(jnp.add, wk), axis=0, keepdims=True)
    return s, chosen, firsts, eidx, denom


def _shared_expert(hb, ws1_ref, ws3_ref, ws2_ref):
    a = jnp.dot(hb, ws1_ref[...], preferred_element_type=F32)
    g = jnp.dot(hb, ws3_ref[...], preferred_element_type=F32)
    return jnp.dot((_silu(a) * g).astype(BF16), ws2_ref[...], preferred_element_type=F32)


def _moe_kernel(h_ref, x1_ref, lg_ref, br_ref, w1_ref, w3_ref, w2_ref, ws1_ref, ws3_ref, ws2_ref, gpost_ref,
                gt2_ref, out_ref, gt_scr, gates_scr, acc_scr, *, eb):
    j = pl.program_id(1)
    hb = _unpack_pairs(h_ref[...]).astype(BF16)
    n_exp = lg_ref.shape[0]
    per = n_exp // N_GROUPS

    @pl.when(j == 0)
    def _():
        s, chosen, _, _, denom = _select(lg_ref[...], br_ref)
        gt_scr[n_exp:, :] = jnp.zeros((gt_scr.shape[0] - n_exp, hb.shape[0]), F32)
        for r in range(per):
            wk = jnp.where(chosen[r], s[r * N_GROUPS:(r + 1) * N_GROUPS, :], 0.0)
            gt_scr[r * N_GROUPS:(r + 1) * N_GROUPS, :] = wk / denom * ROUTE_SCALE
        gates_scr[...] = gt_scr[...].T
        acc_scr[...] = _shared_expert(hb, ws1_ref, ws3_ref, ws2_ref)

    first = j * eb
    lane0 = lax.rem(first, per) * N_GROUPS + first // per
    gates = pltpu.roll(gates_scr[...], lax.rem(LANES - lane0, LANES), 1)
    acc = acc_scr[...]
    for e in range(eb):
        a = jnp.dot(hb, w1_ref[e].astype(BF16), preferred_element_type=F32)
        g = jnp.dot(hb, w3_ref[e].astype(BF16), preferred_element_type=F32)
        gate = gates[:, e * N_GROUPS:e * N_GROUPS + 1]
        acc = acc + jnp.dot((_silu(a) * g * gate).astype(BF16), w2_ref[e].astype(BF16),
                            preferred_element_type=F32)
    acc_scr[...] = acc

    @pl.when(j == pl.num_programs(1) - 1)
    def _():
        out_ref[...] = x1_ref[...] + gt2_ref[0] * _rms(acc_scr[...], gpost_ref[...])


def _moe_dense(h2, x1, logits, lw, gt2, tm, t_seq, eb):
    n, d = x1.shape
    _, n_exp, _, f = lw["w1"].shape
    layer = lw["layer"]
    fs = lw["ws1"].shape[1]
    tps = max(t_seq // tm, 1)
    const = lambda i, j: (0, 0)
    return pl.pallas_call(
        functools.partial(_moe_kernel, eb=eb),
        grid=(n // tm, n_exp // eb),
        in_specs=[
            pl.BlockSpec((tm, d // 2), lambda i, j: (i, 0)),
            pl.BlockSpec((tm, d), lambda i, j: (i, 0)),
            pl.BlockSpec((n_exp, tm), lambda i, j: (0, i)),
            pl.BlockSpec((n_exp, 1), const),
            pl.BlockSpec((None, eb, d, f), lambda i, j: (layer, j, 0, 0)),
            pl.BlockSpec((None, eb, d, f), lambda i, j: (layer, j, 0, 0)),
            pl.BlockSpec((None, eb, f, d), lambda i, j: (layer, j, 0, 0)),
            pl.BlockSpec((d, fs), const),
            pl.BlockSpec((d, fs), const),
            pl.BlockSpec((fs, d), const),
            pl.BlockSpec((1, d), const),
            _mod_spec(gt2, tm, tps, 2),
        ],
        out_specs=pl.BlockSpec((tm, d), lambda i, j: (i, 0)),
        out_shape=jax.ShapeDtypeStruct((n, d), F32),
        scratch_shapes=[
            pltpu.VMEM((LANES, tm), F32),
            pltpu.VMEM((tm, LANES), F32),
            pltpu.VMEM((tm, d), F32),
        ],
        compiler_params=_params(("parallel", "arbitrary")),
        name="moe_dense",
    )(h2, x1, logits, lw["br"], lw["w1"], lw["w3"], lw["w2"], lw["ws1"], lw["ws3"], lw["ws2"],
      lw["g_post_ffn"], gt2)


def _router_kernel(lg_ref, br_ref, eidx_ref, rank_ref, gate_ref, cnt_ref, carry_scr):
    n_exp, tm = lg_ref.shape
    per = n_exp // N_GROUPS

    @pl.when(pl.program_id(0) == 0)
    def _():
        carry_scr[...] = jnp.zeros(carry_scr.shape, F32)

    s, chosen, firsts, eidx, denom = _select(lg_ref[...], br_ref)
    sel01 = jnp.concatenate([c.astype(F32) for c in chosen], axis=0)
    before = (lax.broadcasted_iota(jnp.int32, (tm, tm), 0)
              < lax.broadcasted_iota(jnp.int32, (tm, tm), 1)).astype(BF16)
    rank = jnp.dot(sel01.astype(BF16), before, preferred_element_type=F32) + carry_scr[:, 0:1]
    carry_scr[...] = carry_scr[...] + jnp.sum(sel01, axis=1, keepdims=True)
    cnt_ref[...] = carry_scr[...]
    for k in range(TOP_K):
        s_k = jnp.zeros((1, tm), F32)
        r_k = jnp.zeros((1, tm), F32)
        for r in range(per):
            pick = eidx[r] == firsts[k]
            rows = slice(r * N_GROUPS, (r + 1) * N_GROUPS)
            s_k = s_k + jnp.sum(jnp.where(pick, s[rows, :], 0.0), axis=0, keepdims=True)
            r_k = r_k + jnp.sum(jnp.where(pick, rank[rows, :], 0.0), axis=0, keepdims=True)
        eidx_ref[k:k + 1, :] = firsts[k]
        rank_ref[k:k + 1, :] = r_k.astype(jnp.int32)
        gate_ref[k:k + 1, :] = s_k / denom * ROUTE_SCALE


def _router(logits, br, tm, row0, n):
    n_exp = logits.shape[0]
    tok = pl.BlockSpec((TOP_K, tm), lambda i: (0, i))
    blk0 = row0 // tm
    return pl.pallas_call(
        _router_kernel,
        grid=(n // tm,),
        in_specs=[
            pl.BlockSpec((n_exp, tm), lambda i: (0, i + blk0)),
            pl.BlockSpec((n_exp, 1), lambda i: (0, 0)),
        ],
        out_specs=[tok, tok, tok, pl.BlockSpec((n_exp, LANES), lambda i: (0, 0))],
        out_shape=[
            jax.ShapeDtypeStruct((TOP_K, n), jnp.int32),
            jax.ShapeDtypeStruct((TOP_K, n), jnp.int32),
            jax.ShapeDtypeStruct((TOP_K, n), F32),
            jax.ShapeDtypeStruct((n_exp, LANES), F32),
        ],
        scratch_shapes=[pltpu.VMEM((n_exp, LANES), F32)],
        compiler_params=_params(("arbitrary",)),
        name="router",
    )(logits, br)


def _pos_kernel(starts_ref, eidx_ref, rank_ref, pos_ref):
    e = eidx_ref[...]
    base = jnp.zeros_like(e)
    for x in range(starts_ref.shape[0]):
        base = base + jnp.where(e == x, starts_ref[x], 0)
    pos = base + rank_ref[...]
    for c in range(pos_ref.shape[0]):
        pos_ref[c] = pos[:, c * SC_BATCH:(c + 1) * SC_BATCH]


def _positions(starts, eidx, rank, tm):
    k, n = eidx.shape
    grid_spec = pltpu.PrefetchScalarGridSpec(
        num_scalar_prefetch=1,
        grid=(n // tm,),
        in_specs=[pl.BlockSpec((k, tm), lambda i, s: (0, i)), pl.BlockSpec((k, tm), lambda i, s: (0, i))],
        out_specs=pl.BlockSpec((tm // SC_BATCH, k, SC_BATCH), lambda i, s: (i, 0, 0)),
    )
    return pl.pallas_call(
        _pos_kernel,
        grid_spec=grid_spec,
        out_shape=jax.ShapeDtypeStruct((n // SC_BATCH, k, SC_BATCH), jnp.int32),
        compiler_params=_params(("parallel",)),
        name="positions",
    )(starts, eidx, rank)


SC_BATCH = 128
SC_WORKERS = 32


def _sc_mesh():
    return plsc.VectorSubcoreMesh(core_axis_name="c", subcore_axis_name="s")


def _sc_scatter(x, pos, n_rows, row0):
    w = x.shape[1]
    n = pos.shape[0] * SC_BATCH
    info = plsc.get_sparse_core_info()
    nc, nw = info.num_cores, info.num_cores * info.num_subcores
    steps = n // (nw * SC_BATCH)

    @functools.partial(
        pl.kernel, mesh=_sc_mesh(),
        out_type=jax.ShapeDtypeStruct((n_rows, w), x.dtype),
        scratch_types=[pltpu.VMEM((TOP_K, SC_BATCH), jnp.int32), pltpu.VMEM((SC_BATCH, w), x.dtype),
                       pltpu.SemaphoreType.DMA],
    )
    def scatter(x_hbm, pos_hbm, out_hbm, idx_v, rows_v, sem):
        wid = lax.axis_index("s") * nc + lax.axis_index("c")

        @pl.loop(0, steps)
        def _(s):
            blk = wid * steps + s
            pltpu.sync_copy(pos_hbm.at[blk], idx_v)
            pltpu.sync_copy(x_hbm.at[pl.ds(row0 + blk * SC_BATCH, SC_BATCH)], rows_v)
            copies = [pltpu.async_copy(rows_v, out_hbm.at[idx_v.at[k]], sem) for k in range(TOP_K)]
            for c in copies:
                c.wait()

    return scatter(x, pos)


def _sc_gather(y, pos):
    w = y.shape[1]
    n = pos.shape[0] * SC_BATCH
    info = plsc.get_sparse_core_info()
    nc, nw = info.num_cores, info.num_cores * info.num_subcores
    steps = n // (nw * SC_BATCH)

    @functools.partial(
        pl.kernel, mesh=_sc_mesh(),
        out_type=jax.ShapeDtypeStruct((TOP_K, n, w), y.dtype),
        scratch_types=[pltpu.VMEM((TOP_K, SC_BATCH), jnp.int32), pltpu.VMEM((SC_BATCH, w), y.dtype),
                       pltpu.SemaphoreType.DMA],
    )
    def gather(y_hbm, pos_hbm, out_hbm, idx_v, rows_v, sem):
        wid = lax.axis_index("s") * nc + lax.axis_index("c")

        @pl.loop(0, steps)
        def _(s):
            blk = wid * steps + s
            pltpu.sync_copy(pos_hbm.at[blk], idx_v)
            for k in range(TOP_K):
                pltpu.async_copy(y_hbm.at[idx_v.at[k]], rows_v, sem).wait()
                pltpu.sync_copy(rows_v, out_hbm.at[k, pl.ds(blk * SC_BATCH, SC_BATCH)])

    return gather(y, pos)


def _experts_kernel(te_ref, nu_ref, xs_ref, w1_ref, w3_ref, w2_ref, ys_ref, w1_scr, w3_scr, w2_scr):
    i = pl.program_id(0)

    @pl.when(jnp.logical_or(i == 0, te_ref[i] != te_ref[jnp.maximum(i - 1, 0)]))
    def _():
        w1_scr[...] = w1_ref[...].astype(BF16)
        w3_scr[...] = w3_ref[...].astype(BF16)
        w2_scr[...] = w2_ref[...].astype(BF16)

    @pl.when(i < nu_ref[0])
    def _():
        x = _unpack_pairs(xs_ref[...]).astype(BF16)
        a = jnp.dot(x, w1_scr[...], preferred_element_type=F32)
        g = jnp.dot(x, w3_scr[...], preferred_element_type=F32)
        y = jnp.dot((_silu(a) * g).astype(BF16), w2_scr[...], preferred_element_type=F32)
        ys_ref[...] = _pack_pairs(y)


def _experts(xs, tile_expert, n_used, w1, w3, w2, layer, rt):
    rows, half = xs.shape
    _, n_exp, d, f = w1.shape
    scratch = [pltpu.VMEM((d, f), BF16), pltpu.VMEM((d, f), BF16), pltpu.VMEM((f, d), BF16)]
    grid_spec = pltpu.PrefetchScalarGridSpec(
        num_scalar_prefetch=2,
        grid=(rows // rt,),
        in_specs=[
            pl.BlockSpec((rt, half), lambda i, te, nu: (i, 0)),
            pl.BlockSpec((None, None, d, f), lambda i, te, nu: (layer, te[i], 0, 0)),
            pl.BlockSpec((None, None, d, f), lambda i, te, nu: (layer, te[i], 0, 0)),
            pl.BlockSpec((None, None, f, d), lambda i, te, nu: (layer, te[i], 0, 0)),
        ],
        out_specs=pl.BlockSpec((rt, half), lambda i, te, nu: (i, 0)),
        scratch_shapes=scratch,
    )
    return pl.pallas_call(
        _experts_kernel,
        grid_spec=grid_spec,
        out_shape=jax.ShapeDtypeStruct((rows, half), jnp.uint32),
        compiler_params=_params(("arbitrary",)),
        name="experts",
    )(tile_expert, n_used, xs, w1, w3, w2)


def _combine_kernel(yk_ref, g_ref, h_ref, x1_ref, ws1_ref, ws3_ref, ws2_ref, gpost_ref, gt2_ref, *rest):
    out_ref = rest[-1]
    hb = _unpack_pairs(h_ref[...]).astype(BF16)
    acc = _shared_expert(hb, ws1_ref, ws3_ref, ws2_ref)
    gates = g_ref[...]
    for k in range(TOP_K):
        acc = acc + gates[:, k:k + 1] * _unpack_pairs(yk_ref[k])
    out_ref[...] = x1_ref[...] + gt2_ref[0] * _rms(acc, gpost_ref[...])


def _combine(yk, gates_t, h2, x1, lw, gt2, tm, t_seq, row0, out_rows, out_row0, out_prev):
    d = x1.shape[1]
    nc = yk.shape[1]
    fs = lw["ws1"].shape[1]
    tps = max(t_seq // tm, 1)
    blk0 = row0 // tm
    out_blk0 = (out_row0 + row0) // tm
    const = lambda i: (0, 0)
    in_specs = [
        pl.BlockSpec((TOP_K, tm, d // 2), lambda i: (0, i, 0)),
        pl.BlockSpec((tm, TOP_K), lambda i: (i, 0)),
        pl.BlockSpec((tm, d // 2), lambda i: (i + blk0, 0)),
        pl.BlockSpec((tm, d), lambda i: (i + blk0, 0)),
        pl.BlockSpec((d, fs), const),
        pl.BlockSpec((d, fs), const),
        pl.BlockSpec((fs, d), const),
        pl.BlockSpec((1, d), const),
        pl.BlockSpec((1, 1, d), lambda i: ((i + blk0) // tps, 0, 0)),
    ]
    args = [yk, gates_t, h2, x1, lw["ws1"], lw["ws3"], lw["ws2"], lw["g_post_ffn"], gt2]
    aliases = {}
    if out_prev is not None:
        in_specs.append(pl.BlockSpec(memory_space=pl.ANY))
        aliases = {len(args): 0}
        args.append(out_prev)
    return pl.pallas_call(
        _combine_kernel,
        grid=(nc // tm,),
        in_specs=in_specs,
        out_specs=pl.BlockSpec((tm, d), lambda i: (i + out_blk0, 0)),
        out_shape=jax.ShapeDtypeStruct((out_rows, d), F32),
        input_output_aliases=aliases,
        compiler_params=_params(("parallel",)),
        name="combine",
    )(*args)


EXPERT_ROWS = 1024
MOE_CHUNKS = 1
PROMPT_CHAINS = 2


def _moe_dispatched(h2, x1, logits, lw, gt2, t_seq, dest):
    n, d = x1.shape
    out_rows, out_row0, out = dest if dest is not None else (n, 0, None)
    n_exp = logits.shape[0]
    per = n_exp // N_GROUPS
    rt = EXPERT_ROWS
    chunks = MOE_CHUNKS
    while n % (chunks * SC_WORKERS * SC_BATCH):
        chunks //= 2
    nc = n // chunks
    n_tiles = nc * TOP_K // rt + n_exp
    tm = _tile(min(nc, t_seq), 512)
    routed = [_router(logits, lw["br"], _tile(nc, 1024), c * nc, nc) for c in range(chunks)]
    for c, (eidx, rank, gate, cnt) in enumerate(routed):
        counts = cnt[:, 0].astype(jnp.int32).reshape(per, N_GROUPS).T.reshape(n_exp)
        padded = (counts + rt - 1) // rt * rt
        ends = jnp.cumsum(padded)
        starts = ends - padded
        pos = _positions(starts, eidx, rank, _tile(nc, 2048))
        tile_start = jnp.arange(n_tiles, dtype=jnp.int32) * rt
        tile_expert = jnp.minimum(jnp.sum(ends[None, :] <= tile_start[:, None], axis=1), n_exp - 1).astype(jnp.int32)
        n_used = (ends[-1:] // rt).astype(jnp.int32)
        xs = _sc_scatter(h2, pos, n_tiles * rt, c * nc)
        ys = _experts(xs, tile_expert, n_used, lw["w1"], lw["w3"], lw["w2"], lw["layer"], rt)
        yk = _sc_gather(ys, pos)
        out = _combine(yk, gate.T, h2, x1, lw, gt2, tm, t_seq, c * nc, out_rows, out_row0, out)
    return out


def _split_router(w_router, n_exp, d):
    wrt = w_router.T.reshape(N_GROUPS, n_exp // N_GROUPS, d).transpose(1, 0, 2).reshape(n_exp, d)
    high = wrt.astype(BF16)
    return jnp.concatenate([high, (wrt - high.astype(F32)).astype(BF16)], axis=0)


def _prep_layer(l, w):
    d = w["w_in"].shape[1]
    n_exp = w["w_e1"].shape[1]
    H = N_HEADS
    w_in = w["w_in"][l]
    qkw = d
    g0 = 3 * d
    w_main = jnp.concatenate([w_in[:, :g0], w_in[:, g0 + 2 * H:]], axis=1).astype(BF16)
    wg = jnp.pad(w_in[:, g0:g0 + 2 * H], ((0, 0), (0, LANES - 2 * H)))
    wg_high = wg.astype(BF16)
    w_gate = jnp.concatenate([wg_high, (wg - wg_high.astype(F32)).astype(BF16)], axis=1)
    gbias = jnp.pad(jnp.concatenate([w["b_igate"][l], w["b_fgate"][l]]), (0, LANES - 2 * H)).reshape(1, LANES)
    return dict(
        w_main=w_main, w_gate=w_gate, gbias=gbias,
        g_pre_mix=w["g_pre_mix"][l].reshape(1, d), g_post_mix=w["g_post_mix"][l].reshape(1, d),
        g_pre_ffn=w["g_pre_ffn"][l].reshape(1, d), g_post_ffn=w["g_post_ffn"][l].reshape(1, d),
        w_conv=w["w_conv"][l], b_conv=w["b_conv"][l].reshape(1, qkw),
        g_head=w["g_head"][l].reshape(1, d), g_vnorm=w["g_vnorm"][l].reshape(1, d),
        b_vnorm=w["b_vnorm"][l].reshape(1, d),
        w_spatial=w["w_spatial"][l], b_spatial=w["b_spatial"][l],
        w_out=w["w_out"][l].astype(BF16),
        wr=_split_router(w["w_router"][l], n_exp, d),
        br=w["b_router"][l].reshape(N_GROUPS, n_exp // N_GROUPS).T.reshape(n_exp, 1),
        w1=w["w_e1"], w3=w["w_e3"], w2=w["w_e2"], layer=l,
        ws1=w["w_s1"][l].astype(BF16), ws3=w["w_s3"][l].astype(BF16), ws2=w["w_s2"][l].astype(BF16),
    )


def _spatial_weights(lw, start, gl):
    idx = jnp.arange(GMLP_CHUNK)
    mask = (idx[None, :] // CHUNK) <= (idx[:, None] // CHUNK)
    wsp = jnp.where(mask, lw["w_spatial"], 0.0)[:, start:start + gl, start:start + gl].astype(BF16)
    bsp_t = lw["b_spatial"][:, start:start + gl].T
    return wsp, bsp_t


def _tile(n, cap):
    t = cap
    while n % t:
        t //= 2
    return t


def _trunk(x, mods, c0, n0, m0, conv0, layers, mlstm_l, g_start, g_l, keep_v, seq0=0, nb=None, dest=None):
    b_all, t_seq, d = x.shape
    nb = b_all if nb is None else nb
    n = nb * t_seq
    per_token = mods[0][0].shape[1] != 1
    tok_cap = n if per_token else t_seq
    tm_in = _tile(tok_cap, 1024)
    tm_mg = _tile(tok_cap, 512)
    tm_moe = _tile(tok_cap, 1024)
    xf = x.reshape(b_all * t_seq, d)
    x_row0 = seq0 * t_seq
    outs = [[] for _ in range(5 if keep_v else 4)]
    for l, lw in enumerate(layers):
        sh1, sc1, gt1, sh2, sc2, gt2 = mods[l]
        p, gates = _inproj(xf, sc1, sh1, lw["g_pre_mix"], lw["w_main"], lw["w_gate"], tm_in, t_seq, x_row0, n)
        m0p = jnp.pad(m0[l], ((0, 0), (0, LANES - m0[l].shape[1]))).reshape(nb, 1, LANES)
        hm, c_new, n_new, m_new, conv_new = _mlstm(
            p, gates, c0[l], n0[l], m0p, conv0[l], lw["w_conv"], lw["b_conv"], lw["gbias"], lw["g_head"],
            nb, mlstm_l)
        wsp, bsp_t = _spatial_weights(lw, g_start, g_l)
        res = _merge(p, hm, xf, wsp, bsp_t, lw["g_vnorm"], lw["b_vnorm"], lw["w_out"], lw["g_post_mix"], gt1,
                     lw["g_pre_ffn"], sc2, sh2, lw["wr"], tm_mg, t_seq, keep_v, x_row0)
        x1, h2, logits = res[0], res[1], res[2]
        last = l == len(layers) - 1
        if n % (SC_WORKERS * SC_BATCH) == 0 and not per_token:
            shared = None
            if last and dest is not None:
                buf = None if dest[0] is None else dest[0].reshape(b_all * t_seq, d)
                shared = (b_all * t_seq, seq0 * t_seq, buf)
            xf = _moe_dispatched(h2, x1, logits, lw, gt2, t_seq, shared)
        else:
            xf = _moe_dense(h2, x1, logits, lw, gt2, tm_moe, t_seq, 4)
        x_row0 = 0
        vals = [c_new, n_new, m_new[:, 0, :N_HEADS], conv_new]
        if keep_v:
            vals.append(res[3].reshape(nb, t_seq, d))
        for lst, val in zip(outs, vals):
            lst.append(val)
    return xf.reshape(-1, t_seq, d), tuple(jnp.stack(lst) for lst in outs)


def kernel(x_prompt, x_sample, c_prompt, c_sample, state_mlstm_C, state_mlstm_n, state_mlstm_m, state_conv,
           w_ada, b_ada, g_pre_mix, g_post_mix, g_pre_ffn, g_post_ffn, w_in, b_igate, b_fgate, w_conv, b_conv,
           g_head, g_vnorm, b_vnorm, w_spatial, b_spatial, w_out, w_router, b_router, w_e1, w_e3, w_e2,
           w_s1, w_s3, w_s2):
    w = dict(w_in=w_in, b_igate=b_igate, b_fgate=b_fgate, g_pre_mix=g_pre_mix, g_post_mix=g_post_mix,
             g_pre_ffn=g_pre_ffn, g_post_ffn=g_post_ffn, w_conv=w_conv, b_conv=b_conv, g_head=g_head,
             g_vnorm=g_vnorm, b_vnorm=b_vnorm, w_spatial=w_spatial, b_spatial=b_spatial, w_out=w_out,
             w_router=w_router, b_router=b_router, w_e1=w_e1, w_e3=w_e3, w_e2=w_e2, w_s1=w_s1, w_s3=w_s3,
             w_s2=w_s2)
    depth = w_in.shape[0]
    bp, tp, d = x_prompt.shape
    bs, ts, _ = x_sample.shape
    H, dv, dqk = state_mlstm_C.shape[2:]
    qkw = state_conv.shape[-1]
    layers = [_prep_layer(l, w) for l in range(depth)]

    mod = _ada(jnp.concatenate([c_prompt, c_sample], axis=0), w_ada, b_ada)
    mods_p, mods_s = [], []
    for l in range(depth):
        parts = [mod[l][:, i * d:(i + 1) * d] for i in range(6)]
        mods_p.append([a[:bp].reshape(bp, 1, d) for a in parts])
        mods_s.append([jnp.repeat(a[bp:], ts, axis=0).reshape(1, bs * ts, d) for a in parts])

    lp = 256 if tp % 256 == 0 else CHUNK
    chains = PROMPT_CHAINS if bp % PROMPT_CHAINS == 0 else 1
    bc = bp // chains
    zc = jnp.zeros((depth, bc, H, dv, dqk), F32)
    zn = jnp.zeros((depth, bc, H, dqk), F32)
    zm = jnp.zeros((depth, bc, H), F32)
    zconv = jnp.zeros((depth, bc, CONV_W - 1, qkw), F32)
    shared = chains > 1 and (bc * tp) % (SC_WORKERS * SC_BATCH) == 0
    parts, y_p = [], None
    for ci in range(chains):
        rows = slice(ci * bc, (ci + 1) * bc)
        mods_c = [[a[rows] for a in layer_mods] for layer_mods in mods_p]
        if shared:
            y_p, states = _trunk(x_prompt, mods_c, zc, zn, zm, zconv, layers, lp, 0, GMLP_CHUNK, False,
                                 seq0=ci * bc, nb=bc, dest=(y_p,))
        else:
            y_c, states = _trunk(x_prompt[rows], mods_c, zc, zn, zm, zconv, layers, lp, 0, GMLP_CHUNK, False)
            y_p = y_c if y_p is None else jnp.concatenate([y_p, y_c], axis=0)
        parts.append(states)
    p_c, p_n, p_m, p_conv = (jnp.concatenate([s[i] for s in parts], axis=1) for i in range(4))
    g_start = PAST_LEN % GMLP_CHUNK
    y_s, (s_c, s_n, s_m, s_conv, s_v) = _trunk(x_sample, mods_s, state_mlstm_C, state_mlstm_n, state_mlstm_m,
                                               state_conv, layers, ts, g_start, ts, True)
    return (y_p, y_s, p_c, p_n, p_m, p_conv, s_c, s_n, s_m, s_conv, s_v)
```

```python
import functools

import jax
import jax.numpy as jnp
from jax import lax
from jax.experimental import pallas as pl
from jax.experimental.pallas import tpu as pltpu
from jax.experimental.pallas import tpu_sc as plsc

F32 = jnp.float32
BF16 = jnp.bfloat16

EPS = 1e-6
N_HEADS = 4
CONV_W = 4
GMLP_GROUPS = 8
GMLP_CHUNK = 128
CHUNK = 64
N_GROUPS = 8
TOPK_GROUPS = 4
TOP_K = 8
ROUTE_SCALE = 2.5
PAST_LEN = 4096

LANES = 128
SUBLANES = 8
VMEM_LIMIT = 56 * 1024 * 1024

NT_DIMS = (((1,), (1,)), ((), ()))
TN_DIMS = (((0,), (0,)), ((), ()))


def _sigmoid(x):
    return 0.5 * (jnp.tanh(0.5 * x) + 1.0)


def _silu(x):
    return x * _sigmoid(x)


def _gelu_tanh(x):
    return x * (0.5 * (1.0 + jnp.tanh(0.7978845608028654 * (x + 0.044715 * (x * x * x)))))


def _rms(x, g):
    return x * lax.rsqrt(jnp.mean(x * x, -1, keepdims=True) + EPS) * g


def _pack_pairs(x):
    c = x.shape[1] // 2
    return pltpu.pack_elementwise([x[:, :c], x[:, c:]], packed_dtype=BF16)


def _unpack_pairs(u):
    lo = pltpu.unpack_elementwise(u, index=0, packed_dtype=BF16, unpacked_dtype=F32)
    hi = pltpu.unpack_elementwise(u, index=1, packed_dtype=BF16, unpacked_dtype=F32)
    return jnp.concatenate([lo, hi], axis=1)


def _params(sem):
    return pltpu.CompilerParams(dimension_semantics=sem, vmem_limit_bytes=VMEM_LIMIT)


def _ada_kernel(c_ref, w_ref, b_ref, o_ref):
    a = _silu(c_ref[...]).astype(BF16)
    o_ref[0] = jnp.dot(a, w_ref[0].astype(BF16), preferred_element_type=F32) + b_ref[0]


def _ada(c, w_ada, b_ada):
    depth, d, six_d = w_ada.shape
    nb = c.shape[0]
    tn = d
    return pl.pallas_call(
        _ada_kernel,
        grid=(depth, six_d // tn),
        in_specs=[
            pl.BlockSpec((nb, d), lambda l, j: (0, 0)),
            pl.BlockSpec((1, d, tn), lambda l, j: (l, 0, j)),
            pl.BlockSpec((1, 1, tn), lambda l, j: (l, 0, j)),
        ],
        out_specs=pl.BlockSpec((1, nb, tn), lambda l, j: (l, 0, j)),
        out_shape=jax.ShapeDtypeStruct((depth, nb, six_d), F32),
        compiler_params=_params(("parallel", "parallel")),
        name="adaln",
    )(c, w_ada, b_ada.reshape(depth, 1, six_d))


def _inproj_kernel(x_ref, sc_ref, sh_ref, g_ref, w_ref, wg_ref, p_ref, gates_ref, h_scr):
    @pl.when(pl.program_id(1) == 0)
    def _():
        h = _rms(x_ref[...], g_ref[...]) * (1.0 + sc_ref[0]) + sh_ref[0]
        hb = h.astype(BF16)
        h_scr[...] = hb
        h_low = (h - hb.astype(F32)).astype(BF16)
        gg = jnp.dot(hb, wg_ref[...], preferred_element_type=F32)
        gates_ref[...] = (gg[:, :LANES] + gg[:, LANES:]
                          + jnp.dot(h_low, wg_ref[:, :LANES], preferred_element_type=F32))

    p_ref[...] = jnp.dot(h_scr[...], w_ref[...], preferred_element_type=F32).astype(BF16)


def _mod_spec(mod, tm, tiles_per_seq, ngrid):
    d = mod.shape[-1]
    if mod.shape[1] == 1:
        if ngrid == 2:
            return pl.BlockSpec((1, 1, d), lambda i, j: (i // tiles_per_seq, 0, 0))
        return pl.BlockSpec((1, 1, d), lambda i: (i // tiles_per_seq, 0, 0))
    if ngrid == 2:
        return pl.BlockSpec((1, tm, d), lambda i, j: (0, i, 0))
    return pl.BlockSpec((1, tm, d), lambda i: (0, i, 0))


def _inproj(x, sc, sh, g, w_main, w_gate, tm, t_seq, row0, n):
    d = x.shape[1]
    blk0 = row0 // tm
    width = w_main.shape[1]
    ncol = 2 if (width // 2) % LANES == 0 else width // d
    wc = width // ncol
    tps = max(t_seq // tm, 1)
    return pl.pallas_call(
        _inproj_kernel,
        grid=(n // tm, ncol),
        in_specs=[
            pl.BlockSpec((tm, d), lambda i, j: (i + blk0, 0)),
            _mod_spec(sc, tm, tps, 2),
            _mod_spec(sh, tm, tps, 2),
            pl.BlockSpec((1, d), lambda i, j: (0, 0)),
            pl.BlockSpec((d, wc), lambda i, j: (0, j)),
            pl.BlockSpec((d, 2 * LANES), lambda i, j: (0, 0)),
        ],
        out_specs=[
            pl.BlockSpec((tm, wc), lambda i, j: (i, j)),
            pl.BlockSpec((tm, LANES), lambda i, j: (i, 0)),
        ],
        out_shape=[
            jax.ShapeDtypeStruct((n, width), BF16),
            jax.ShapeDtypeStruct((n, LANES), F32),
        ],
        scratch_shapes=[pltpu.VMEM((tm, d), BF16)],
        compiler_params=_params(("parallel", "arbitrary")),
        name="inproj",
    )(x, sc, sh, g, w_main, w_gate)


def _mlstm_kernel(qk_ref, v_ref, gt_ref, c0_ref, n0_ref, m0_ref, conv0_ref, wconv_ref, bconv_ref,
                  gbias_ref, ghead_ref, hm_ref, c_ref, n_ref, m_ref, conv_ref, xbuf, *, L, H, dqk, dv):
    qkw = 2 * H * dqk
    pad = SUBLANES
    tail0 = pad - (CONV_W - 1)

    @pl.when(pl.program_id(1) == 0)
    def _():
        c_ref[...] = c0_ref[...]
        n_ref[...] = n0_ref[...]
        m_ref[...] = m0_ref[...]
        xbuf[0:pad, :] = jnp.zeros((pad, qkw), F32)
        xbuf[tail0:pad, :] = conv0_ref[0]

    xbuf[pad:pad + L, :] = qk_ref[...].astype(F32)
    w = wconv_ref[...]
    y = xbuf[tail0:tail0 + L, :] * w[0:1, :]
    for j in range(1, CONV_W):
        y = y + xbuf[tail0 + j:tail0 + j + L, :] * w[j:j + 1, :]
    y = y + bconv_ref[...]
    tail = xbuf[L + tail0:L + pad, :]
    conv_ref[0] = tail
    xbuf[tail0:pad, :] = tail
    qk = _silu(y)

    z = gt_ref[...] + gbias_ref[...]
    lf = jnp.minimum(z, 0.0) - jnp.log(1.0 + jnp.exp(-jnp.abs(z)))
    row = lax.broadcasted_iota(jnp.int32, (L, L), 0)
    col = lax.broadcasted_iota(jnp.int32, (L, L), 1)
    tri = row >= col
    hi = lax.Precision.HIGHEST
    b_col = jnp.dot(tri.astype(F32), lf, precision=hi, preferred_element_type=F32)
    eye = (lax.broadcasted_iota(jnp.int32, (LANES, LANES), 0)
           == lax.broadcasted_iota(jnp.int32, (LANES, LANES), 1)).astype(F32)
    b_row = lax.dot_general(eye, b_col, NT_DIMS, precision=hi, preferred_element_type=F32)
    z_row = lax.dot_general(eye, z, NT_DIMS, precision=hi, preferred_element_type=F32)

    scale = dqk ** -0.5
    lane1 = lax.broadcasted_iota(jnp.int32, (1, LANES), 1)
    m_old = m_ref[0]
    m_new = m_old
    for h in range(H):
        q = qk[:, h * dqk:(h + 1) * dqk] * scale
        k = qk[:, (H + h) * dqk:(H + h + 1) * dqk]
        qb = q.astype(BF16)
        vb = v_ref[:, h * dv:(h + 1) * dv]
        b_c = b_col[:, H + h:H + h + 1]
        i_c = z[:, h:h + 1]
        b_r = b_row[H + h:H + h + 1, :]
        i_r = z_row[h:h + 1, :]
        m_prev = m_old[:, h:h + 1]
        a = b_c + m_prev
        dm = jnp.where(tri, b_c - b_r + i_r, -jnp.inf)
        mt = jnp.maximum(a, jnp.max(dm, axis=-1, keepdims=True))
        s = lax.dot_general(qb, k.astype(BF16), NT_DIMS, preferred_element_type=F32)
        sg = s * jnp.exp(dm - mt)
        aw = jnp.exp(a - mt)
        cm = c_ref[0, h]
        qc = lax.dot_general(qb, cm.astype(BF16), NT_DIMS, preferred_element_type=F32)
        num = jnp.dot(sg.astype(BF16), vb, preferred_element_type=F32) + aw * qc
        nrow = n_ref[0, h:h + 1, :]
        den = jnp.sum(sg, -1, keepdims=True) + aw * jnp.sum(q * nrow, -1, keepdims=True)
        hh = num / jnp.maximum(jnp.abs(den), jnp.exp(-mt))
        m_last = mt[L - 1:L, :]
        b_last = b_c[L - 1:L, :]
        decay = jnp.exp(b_last + m_prev - m_last)
        ws = jnp.exp(b_last - b_c + i_c - m_last)
        kw = ws * k
        c_ref[0, h] = decay * cm + lax.dot_general(vb, kw.astype(BF16), TN_DIMS, preferred_element_type=F32)
        n_ref[0, h:h + 1, :] = decay * nrow + jnp.sum(kw, axis=0, keepdims=True)
        m_new = jnp.where(lane1 == h, m_last, m_new)
        hm_ref[:, h * dv:(h + 1) * dv] = _rms(hh, ghead_ref[:, h * dv:(h + 1) * dv]).astype(BF16)
    m_ref[0] = m_new


def _mlstm(p, gates, c0, n0, m0, conv0, w_conv, b_conv, gbias, g_head, nb, L):
    n = p.shape[0]
    t_seq = n // nb
    nc = t_seq // L
    _, H, dv, dqk = c0.shape
    qkw = 2 * H * dqk
    vw = H * dv
    kern = functools.partial(_mlstm_kernel, L=L, H=H, dqk=dqk, dv=dv)
    return pl.pallas_call(
        kern,
        grid=(nb, nc),
        in_specs=[
            pl.BlockSpec((L, qkw), lambda b, c: (b * nc + c, 0)),
            pl.BlockSpec((L, vw), lambda b, c: (b * nc + c, 1)),
            pl.BlockSpec((L, LANES), lambda b, c: (b * nc + c, 0)),
            pl.BlockSpec((1, H, dv, dqk), lambda b, c: (b, 0, 0, 0)),
            pl.BlockSpec((1, H, dqk), lambda b, c: (b, 0, 0)),
            pl.BlockSpec((1, 1, LANES), lambda b, c: (b, 0, 0)),
            pl.BlockSpec((1, CONV_W - 1, qkw), lambda b, c: (b, 0, 0)),
            pl.BlockSpec((CONV_W, qkw), lambda b, c: (0, 0)),
            pl.BlockSpec((1, qkw), lambda b, c: (0, 0)),
            pl.BlockSpec((1, LANES), lambda b, c: (0, 0)),
            pl.BlockSpec((1, vw), lambda b, c: (0, 0)),
        ],
        out_specs=[
            pl.BlockSpec((L, vw), lambda b, c: (b * nc + c, 0)),
            pl.BlockSpec((1, H, dv, dqk), lambda b, c: (b, 0, 0, 0)),
            pl.BlockSpec((1, H, dqk), lambda b, c: (b, 0, 0)),
            pl.BlockSpec((1, 1, LANES), lambda b, c: (b, 0, 0)),
            pl.BlockSpec((1, CONV_W - 1, qkw), lambda b, c: (b, 0, 0)),
        ],
        out_shape=[
            jax.ShapeDtypeStruct((n, vw), BF16),
            jax.ShapeDtypeStruct((nb, H, dv, dqk), F32),
            jax.ShapeDtypeStruct((nb, H, dqk), F32),
            jax.ShapeDtypeStruct((nb, 1, LANES), F32),
            jax.ShapeDtypeStruct((nb, CONV_W - 1, qkw), F32),
        ],
        scratch_shapes=[pltpu.VMEM((L + SUBLANES, qkw), F32)],
        compiler_params=_params(("parallel", "arbitrary")),
        name="mlstm",
    )(p, p, gates, c0, n0, m0, conv0, w_conv, b_conv, gbias, g_head)


def _merge_kernel(o_ref, u_ref, vg_ref, ga_ref, gb_ref, hm_ref, x_ref, wsp_ref, bsp_ref, gvn_ref, bvn_ref,
                  wout_ref, gpost_ref, gt1_ref, gpre_ref, sc2_ref, sh2_ref, wr_ref, *rest, tm, gl, keep_v):
    if keep_v:
        x1_ref, h2_ref, lg_ref, vn_ref, sg_scr = rest
    else:
        x1_ref, h2_ref, lg_ref, sg_scr = rest
    groups = wsp_ref.shape[0]
    ch = vg_ref.shape[1] // groups
    vg = _gelu_tanh(vg_ref[...].astype(F32))
    mu = jnp.mean(vg, -1, keepdims=True)
    xc = vg - mu
    vn = xc * lax.rsqrt(jnp.mean(xc * xc, -1, keepdims=True) + EPS) * gvn_ref[...] + bvn_ref[...]
    if keep_v:
        vn_ref[...] = vn
    vnb = vn.astype(BF16)
    for ci in range(tm // gl):
        for g in range(groups):
            blk = jnp.dot(wsp_ref[g], vnb[ci * gl:(ci + 1) * gl, g * ch:(g + 1) * ch],
                          preferred_element_type=F32)
            sg_scr[ci * gl:(ci + 1) * gl, g * ch:(g + 1) * ch] = blk + bsp_ref[:, g:g + 1]
    h_b = _gelu_tanh(u_ref[...].astype(F32)) * sg_scr[...]
    h_a = _sigmoid(o_ref[...].astype(F32)) * hm_ref[...].astype(F32)
    merged = _sigmoid(ga_ref[...].astype(F32)) * h_a + _sigmoid(gb_ref[...].astype(F32)) * h_b
    y = jnp.dot(merged.astype(BF16), wout_ref[...], preferred_element_type=F32)
    x1 = x_ref[...] + gt1_ref[0] * _rms(y, gpost_ref[...])
    x1_ref[...] = x1
    h2 = _rms(x1, gpre_ref[...]) * (1.0 + sc2_ref[0]) + sh2_ref[0]
    h2_ref[...] = _pack_pairs(h2)
    n_exp = lg_ref.shape[0]
    h2_high = h2.astype(BF16)
    h2_low = (h2 - h2_high.astype(F32)).astype(BF16)
    lg = lax.dot_general(wr_ref[...], h2_high, NT_DIMS, preferred_element_type=F32)
    lg_ref[...] = (lg[:n_exp] + lg[n_exp:]
                   + lax.dot_general(wr_ref[:n_exp, :], h2_low, NT_DIMS, preferred_element_type=F32))


def _merge(p, hm, x, wsp, bsp_t, g_vn, b_vn, w_out, g_post, gt1, g_pre, sc2, sh2, wr, tm, t_seq, keep_v, x_row0):
    n, d = hm.shape
    x_blk0 = x_row0 // tm
    groups, gl, _ = wsp.shape
    n_exp = wr.shape[0] // 2
    tps = max(t_seq // tm, 1)
    row = lambda i: (0, 0)
    pcol = lambda c: pl.BlockSpec((tm, d), lambda i: (i, c))
    tok = pl.BlockSpec((tm, d), lambda i: (i, 0))
    vec = pl.BlockSpec((1, d), row)
    out_specs = [tok, pl.BlockSpec((tm, d // 2), lambda i: (i, 0)), pl.BlockSpec((n_exp, tm), lambda i: (0, i))]
    out_shape = [jax.ShapeDtypeStruct((n, d), F32), jax.ShapeDtypeStruct((n, d // 2), jnp.uint32),
                 jax.ShapeDtypeStruct((n_exp, n), F32)]
    if keep_v:
        out_specs.append(tok)
        out_shape.append(jax.ShapeDtypeStruct((n, d), F32))
    kern = functools.partial(_merge_kernel, tm=tm, gl=gl, keep_v=keep_v)
    return pl.pallas_call(
        kern,
        grid=(n // tm,),
        in_specs=[
            pcol(2), pcol(3), pcol(4), pcol(5), pcol(6), tok,
            pl.BlockSpec((tm, d), lambda i: (i + x_blk0, 0)),
            pl.BlockSpec((groups, gl, gl), lambda i: (0, 0, 0)),
            pl.BlockSpec((gl, groups), row),
            vec, vec,
            pl.BlockSpec((d, d), row),
            vec,
            _mod_spec(gt1, tm, tps, 1),
            vec,
            _mod_spec(sc2, tm, tps, 1),
            _mod_spec(sh2, tm, tps, 1),
            pl.BlockSpec((2 * n_exp, d), row),
        ],
        out_specs=out_specs,
        out_shape=out_shape,
        scratch_shapes=[pltpu.VMEM((tm, d), F32)],
        compiler_params=_params(("parallel",)),
        name="merge",
    )(p, p, p, p, p, hm, x, wsp, bsp_t, g_vn, b_vn, w_out, g_post, gt1, g_pre, sc2, sh2, wr)


def _select(logits, br_ref):
    n_exp, tm = logits.shape
    per = n_exp // N_GROUPS
    neg = -jnp.inf
    s = _sigmoid(logits)
    sel = s + br_ref[...]
    mem = [sel[r * N_GROUPS:(r + 1) * N_GROUPS, :] for r in range(per)]
    grow = lax.broadcasted_iota(jnp.int32, (N_GROUPS, tm), 0)
    m1 = functools.reduce(jnp.maximum, mem)
    idx1 = functools.reduce(jnp.minimum, [jnp.where(mem[r] == m1, r, per) for r in range(per)])
    m2 = functools.reduce(jnp.maximum, [jnp.where(idx1 == r, neg, mem[r]) for r in range(per)])
    gs = m1 + m2
    gmask = jnp.zeros((N_GROUPS, tm), jnp.bool_)
    for _ in range(TOPK_GROUPS):
        mx = jnp.max(gs, axis=0, keepdims=True)
        gi = jnp.min(jnp.where(gs == mx, grow, N_GROUPS), axis=0, keepdims=True)
        pick = grow == gi
        gmask = jnp.logical_or(gmask, pick)
        gs = jnp.where(pick, neg, gs)
    msk = [jnp.where(gmask, mem[r], neg) for r in range(per)]
    eidx = [grow * per + r for r in range(per)]
    chosen = [jnp.zeros((N_GROUPS, tm), jnp.bool_) for _ in range(per)]
    firsts = []
    for _ in range(TOP_K):
        mx = jnp.max(functools.reduce(jnp.maximum, msk), axis=0, keepdims=True)
        cand = functools.reduce(jnp.minimum, [jnp.where(msk[r] == mx, eidx[r], n_exp) for r in range(per)])
        first = jnp.min(cand, axis=0, keepdims=True)
        firsts.append(first)
        for r in range(per):
            pick = eidx[r] == first
            chosen[r] = jnp.logical_or(chosen[r], pick)
            msk[r] = jnp.where(pick, neg, msk[r])
    wk = [jnp.where(chosen[r], s[r * N_GROUPS:(r + 1) * N_GROUPS, :], 0.0) for r in range(per)]
    denom = jnp.sum(functools.reduce(jnp.add, wk), axis=0, keepdims=True)
    return s, chosen, firsts, eidx, denom


def _shared_expert(hb, ws1_ref, ws3_ref, ws2_ref):
    a = jnp.dot(hb, ws1_ref[...], preferred_element_type=F32)
    g = jnp.dot(hb, ws3_ref[...], preferred_element_type=F32)
    return jnp.dot((_silu(a) * g).astype(BF16), ws2_ref[...], preferred_element_type=F32)


def _moe_kernel(h_ref, x1_ref, lg_ref, br_ref, w1_ref, w3_ref, w2_ref, ws1_ref, ws3_ref, ws2_ref, gpost_ref,
                gt2_ref, out_ref, gt_scr, gates_scr, acc_scr, *, eb):
    j = pl.program_id(1)
    hb = _unpack_pairs(h_ref[...]).astype(BF16)
    n_exp = lg_ref.shape[0]
    per = n_exp // N_GROUPS

    @pl.when(j == 0)
    def _():
        s, chosen, _, _, denom = _select(lg_ref[...], br_ref)
        gt_scr[n_exp:, :] = jnp.zeros((gt_scr.shape[0] - n_exp, hb.shape[0]), F32)
        for r in range(per):
            wk = jnp.where(chosen[r], s[r * N_GROUPS:(r + 1) * N_GROUPS, :], 0.0)
            gt_scr[r * N_GROUPS:(r + 1) * N_GROUPS, :] = wk / denom * ROUTE_SCALE
        gates_scr[...] = gt_scr[...].T
        acc_scr[...] = _shared_expert(hb, ws1_ref, ws3_ref, ws2_ref)

    first = j * eb
    lane0 = lax.rem(first, per) * N_GROUPS + first // per
    gates = pltpu.roll(gates_scr[...], lax.rem(LANES - lane0, LANES), 1)
    acc = acc_scr[...]
    for e in range(eb):
        a = jnp.dot(hb, w1_ref[e].astype(BF16), preferred_element_type=F32)
        g = jnp.dot(hb, w3_ref[e].astype(BF16), preferred_element_type=F32)
        gate = gates[:, e * N_GROUPS:e * N_GROUPS + 1]
        acc = acc + jnp.dot((_silu(a) * g * gate).astype(BF16), w2_ref[e].astype(BF16),
                            preferred_element_type=F32)
    acc_scr[...] = acc

    @pl.when(j == pl.num_programs(1) - 1)
    def _():
        out_ref[...] = x1_ref[...] + gt2_ref[0] * _rms(acc_scr[...], gpost_ref[...])


def _moe_dense(h2, x1, logits, lw, gt2, tm, t_seq, eb):
    n, d = x1.shape
    _, n_exp, _, f = lw["w1"].shape
    layer = lw["layer"]
    fs = lw["ws1"].shape[1]
    tps = max(t_seq // tm, 1)
    const = lambda i, j: (0, 0)
    return pl.pallas_call(
        functools.partial(_moe_kernel, eb=eb),
        grid=(n // tm, n_exp // eb),
        in_specs=[
            pl.BlockSpec((tm, d // 2), lambda i, j: (i, 0)),
            pl.BlockSpec((tm, d), lambda i, j: (i, 0)),
            pl.BlockSpec((n_exp, tm), lambda i, j: (0, i)),
            pl.BlockSpec((n_exp, 1), const),
            pl.BlockSpec((None, eb, d, f), lambda i, j: (layer, j, 0, 0)),
            pl.BlockSpec((None, eb, d, f), lambda i, j: (layer, j, 0, 0)),
            pl.BlockSpec((None, eb, f, d), lambda i, j: (layer, j, 0, 0)),
            pl.BlockSpec((d, fs), const),
            pl.BlockSpec((d, fs), const),
            pl.BlockSpec((fs, d), const),
            pl.BlockSpec((1, d), const),
            _mod_spec(gt2, tm, tps, 2),
        ],
        out_specs=pl.BlockSpec((tm, d), lambda i, j: (i, 0)),
        out_shape=jax.ShapeDtypeStruct((n, d), F32),
        scratch_shapes=[
            pltpu.VMEM((LANES, tm), F32),
            pltpu.VMEM((tm, LANES), F32),
            pltpu.VMEM((tm, d), F32),
        ],
        compiler_params=_params(("parallel", "arbitrary")),
        name="moe_dense",
    )(h2, x1, logits, lw["br"], lw["w1"], lw["w3"], lw["w2"], lw["ws1"], lw["ws3"], lw["ws2"],
      lw["g_post_ffn"], gt2)


def _router_kernel(lg_ref, br_ref, eidx_ref, rank_ref, gate_ref, cnt_ref, carry_scr):
    n_exp, tm = lg_ref.shape
    per = n_exp // N_GROUPS

    @pl.when(pl.program_id(0) == 0)
    def _():
        carry_scr[...] = jnp.zeros(carry_scr.shape, F32)

    s, chosen, firsts, eidx, denom = _select(lg_ref[...], br_ref)
    sel01 = jnp.concatenate([c.astype(F32) for c in chosen], axis=0)
    before = (lax.broadcasted_iota(jnp.int32, (tm, tm), 0)
              < lax.broadcasted_iota(jnp.int32, (tm, tm), 1)).astype(BF16)
    rank = jnp.dot(sel01.astype(BF16), before, preferred_element_type=F32) + carry_scr[:, 0:1]
    carry_scr[...] = carry_scr[...] + jnp.sum(sel01, axis=1, keepdims=True)
    cnt_ref[...] = carry_scr[...]
    for k in range(TOP_K):
        s_k = jnp.zeros((1, tm), F32)
        r_k = jnp.zeros((1, tm), F32)
        for r in range(per):
            pick = eidx[r] == firsts[k]
            rows = slice(r * N_GROUPS, (r + 1) * N_GROUPS)
            s_k = s_k + jnp.sum(jnp.where(pick, s[rows, :], 0.0), axis=0, keepdims=True)
            r_k = r_k + jnp.sum(jnp.where(pick, rank[rows, :], 0.0), axis=0, keepdims=True)
        eidx_ref[k:k + 1, :] = firsts[k]
        rank_ref[k:k + 1, :] = r_k.astype(jnp.int32)
        gate_ref[k:k + 1, :] = s_k / denom * ROUTE_SCALE


def _router(logits, br, tm, row0, n):
    n_exp = logits.shape[0]
    tok = pl.BlockSpec((TOP_K, tm), lambda i: (0, i))
    blk0 = row0 // tm
    return pl.pallas_call(
        _router_kernel,
        grid=(n // tm,),
        in_specs=[
            pl.BlockSpec((n_exp, tm), lambda i: (0, i + blk0)),
            pl.BlockSpec((n_exp, 1), lambda i: (0, 0)),
        ],
        out_specs=[tok, tok, tok, pl.BlockSpec((n_exp, LANES), lambda i: (0, 0))],
        out_shape=[
            jax.ShapeDtypeStruct((TOP_K, n), jnp.int32),
            jax.ShapeDtypeStruct((TOP_K, n), jnp.int32),
            jax.ShapeDtypeStruct((TOP_K, n), F32),
            jax.ShapeDtypeStruct((n_exp, LANES), F32),
        ],
        scratch_shapes=[pltpu.VMEM((n_exp, LANES), F32)],
        compiler_params=_params(("arbitrary",)),
        name="router",
    )(logits, br)


def _pos_kernel(starts_ref, eidx_ref, rank_ref, pos_ref):
    e = eidx_ref[...]
    base = jnp.zeros_like(e)
    for x in range(starts_ref.shape[0]):
        base = base + jnp.where(e == x, starts_ref[x], 0)
    pos = base + rank_ref[...]
    for c in range(pos_ref.shape[0]):
        pos_ref[c] = pos[:, c * SC_BATCH:(c + 1) * SC_BATCH]


def _positions(starts, eidx, rank, tm):
    k, n = eidx.shape
    grid_spec = pltpu.PrefetchScalarGridSpec(
        num_scalar_prefetch=1,
        grid=(n // tm,),
        in_specs=[pl.BlockSpec((k, tm), lambda i, s: (0, i)), pl.BlockSpec((k, tm), lambda i, s: (0, i))],
        out_specs=pl.BlockSpec((tm // SC_BATCH, k, SC_BATCH), lambda i, s: (i, 0, 0)),
    )
    return pl.pallas_call(
        _pos_kernel,
        grid_spec=grid_spec,
        out_shape=jax.ShapeDtypeStruct((n // SC_BATCH, k, SC_BATCH), jnp.int32),
        compiler_params=_params(("parallel",)),
        name="positions",
    )(starts, eidx, rank)


SC_BATCH = 128
SC_WORKERS = 32


def _sc_mesh():
    return plsc.VectorSubcoreMesh(core_axis_name="c", subcore_axis_name="s")


def _sc_scatter(x, pos, n_rows, row0):
    w = x.shape[1]
    n = pos.shape[0] * SC_BATCH
    info = plsc.get_sparse_core_info()
    nc, nw = info.num_cores, info.num_cores * info.num_subcores
    steps = n // (nw * SC_BATCH)

    @functools.partial(
        pl.kernel, mesh=_sc_mesh(),
        out_type=jax.ShapeDtypeStruct((n_rows, w), x.dtype),
        scratch_types=[pltpu.VMEM((TOP_K, SC_BATCH), jnp.int32), pltpu.VMEM((SC_BATCH, w), x.dtype),
                       pltpu.SemaphoreType.DMA],
    )
    def scatter(x_hbm, pos_hbm, out_hbm, idx_v, rows_v, sem):
        wid = lax.axis_index("s") * nc + lax.axis_index("c")

        @pl.loop(0, steps)
        def _(s):
            blk = wid * steps + s
            pltpu.sync_copy(pos_hbm.at[blk], idx_v)
            pltpu.sync_copy(x_hbm.at[pl.ds(row0 + blk * SC_BATCH, SC_BATCH)], rows_v)
            copies = [pltpu.async_copy(rows_v, out_hbm.at[idx_v.at[k]], sem) for k in range(TOP_K)]
            for c in copies:
                c.wait()

    return scatter(x, pos)


def _sc_gather(y, pos):
    w = y.shape[1]
    n = pos.shape[0] * SC_BATCH
    info = plsc.get_sparse_core_info()
    nc, nw = info.num_cores, info.num_cores * info.num_subcores
    steps = n // (nw * SC_BATCH)

    @functools.partial(
        pl.kernel, mesh=_sc_mesh(),
        out_type=jax.ShapeDtypeStruct((TOP_K, n, w), y.dtype),
        scratch_types=[pltpu.VMEM((TOP_K, SC_BATCH), jnp.int32), pltpu.VMEM((SC_BATCH, w), y.dtype),
                       pltpu.SemaphoreType.DMA],
    )
    def gather(y_hbm, pos_hbm, out_hbm, idx_v, rows_v, sem):
        wid = lax.axis_index("s") * nc + lax.axis_index("c")

        @pl.loop(0, steps)
        def _(s):
            blk = wid * steps + s
            pltpu.sync_copy(pos_hbm.at[blk], idx_v)
            for k in range(TOP_K):
                pltpu.async_copy(y_hbm.at[idx_v.at[k]], rows_v, sem).wait()
                pltpu.sync_copy(rows_v, out_hbm.at[k, pl.ds(blk * SC_BATCH, SC_BATCH)])

    return gather(y, pos)


def _experts_kernel(te_ref, nu_ref, xs_ref, w1_ref, w3_ref, w2_ref, ys_ref, w1_scr, w3_scr, w2_scr):
    i = pl.program_id(0)

    @pl.when(jnp.logical_or(i == 0, te_ref[i] != te_ref[jnp.maximum(i - 1, 0)]))
    def _():
        w1_scr[...] = w1_ref[...].astype(BF16)
        w3_scr[...] = w3_ref[...].astype(BF16)
        w2_scr[...] = w2_ref[...].astype(BF16)

    @pl.when(i < nu_ref[0])
    def _():
        x = _unpack_pairs(xs_ref[...]).astype(BF16)
        a = jnp.dot(x, w1_scr[...], preferred_element_type=F32)
        g = jnp.dot(x, w3_scr[...], preferred_element_type=F32)
        y = jnp.dot((_silu(a) * g).astype(BF16), w2_scr[...], preferred_element_type=F32)
        ys_ref[...] = _pack_pairs(y)


def _experts(xs, tile_expert, n_used, w1, w3, w2, layer, rt):
    rows, half = xs.shape
    _, n_exp, d, f = w1.shape
    scratch = [pltpu.VMEM((d, f), BF16), pltpu.VMEM((d, f), BF16), pltpu.VMEM((f, d), BF16)]
    grid_spec = pltpu.PrefetchScalarGridSpec(
        num_scalar_prefetch=2,
        grid=(rows // rt,),
        in_specs=[
            pl.BlockSpec((rt, half), lambda i, te, nu: (i, 0)),
            pl.BlockSpec((None, None, d, f), lambda i, te, nu: (layer, te[i], 0, 0)),
            pl.BlockSpec((None, None, d, f), lambda i, te, nu: (layer, te[i], 0, 0)),
            pl.BlockSpec((None, None, f, d), lambda i, te, nu: (layer, te[i], 0, 0)),
        ],
        out_specs=pl.BlockSpec((rt, half), lambda i, te, nu: (i, 0)),
        scratch_shapes=scratch,
    )
    return pl.pallas_call(
        _experts_kernel,
        grid_spec=grid_spec,
        out_shape=jax.ShapeDtypeStruct((rows, half), jnp.uint32),
        compiler_params=_params(("arbitrary",)),
        name="experts",
    )(tile_expert, n_used, xs, w1, w3, w2)


def _combine_kernel(yk_ref, g_ref, h_ref, x1_ref, ws1_ref, ws3_ref, ws2_ref, gpost_ref, gt2_ref, *rest):
    out_ref = rest[-1]
    hb = _unpack_pairs(h_ref[...]).astype(BF16)
    acc = _shared_expert(hb, ws1_ref, ws3_ref, ws2_ref)
    gates = g_ref[...]
    for k in range(TOP_K):
        acc = acc + gates[:, k:k + 1] * _unpack_pairs(yk_ref[k])
    out_ref[...] = x1_ref[...] + gt2_ref[0] * _rms(acc, gpost_ref[...])


def _combine(yk, gates_t, h2, x1, lw, gt2, tm, t_seq, row0, out_rows, out_row0, out_prev):
    d = x1.shape[1]
    nc = yk.shape[1]
    fs = lw["ws1"].shape[1]
    tps = max(t_seq // tm, 1)
    blk0 = row0 // tm
    out_blk0 = (out_row0 + row0) // tm
    const = lambda i: (0, 0)
    in_specs = [
        pl.BlockSpec((TOP_K, tm, d // 2), lambda i: (0, i, 0)),
        pl.BlockSpec((tm, TOP_K), lambda i: (i, 0)),
        pl.BlockSpec((tm, d // 2), lambda i: (i + blk0, 0)),
        pl.BlockSpec((tm, d), lambda i: (i + blk0, 0)),
        pl.BlockSpec((d, fs), const),
        pl.BlockSpec((d, fs), const),
        pl.BlockSpec((fs, d), const),
        pl.BlockSpec((1, d), const),
        pl.BlockSpec((1, 1, d), lambda i: ((i + blk0) // tps, 0, 0)),
    ]
    args = [yk, gates_t, h2, x1, lw["ws1"], lw["ws3"], lw["ws2"], lw["g_post_ffn"], gt2]
    aliases = {}
    if out_prev is not None:
        in_specs.append(pl.BlockSpec(memory_space=pl.ANY))
        aliases = {len(args): 0}
        args.append(out_prev)
    return pl.pallas_call(
        _combine_kernel,
        grid=(nc // tm,),
        in_specs=in_specs,
        out_specs=pl.BlockSpec((tm, d), lambda i: (i + out_blk0, 0)),
        out_shape=jax.ShapeDtypeStruct((out_rows, d), F32),
        input_output_aliases=aliases,
        compiler_params=_params(("parallel",)),
        name="combine",
    )(*args)


EXPERT_ROWS = 1024
PROMPT_CHAINS = 2


def _route_rows(logits, lw, n):
    n_exp = logits.shape[0]
    per = n_exp // N_GROUPS
    rt = EXPERT_ROWS
    n_tiles = n * TOP_K // rt + n_exp
    eidx, rank, gate, cnt = _router(logits, lw["br"], _tile(n, 1024), 0, n)
    counts = cnt[:, 0].astype(jnp.int32).reshape(per, N_GROUPS).T.reshape(n_exp)
    padded = (counts + rt - 1) // rt * rt
    ends = jnp.cumsum(padded)
    pos = _positions(ends - padded, eidx, rank, _tile(n, 2048))
    tile_start = jnp.arange(n_tiles, dtype=jnp.int32) * rt
    tile_expert = jnp.minimum(jnp.sum(ends[None, :] <= tile_start[:, None], axis=1), n_exp - 1).astype(jnp.int32)
    n_used = (ends[-1:] // rt).astype(jnp.int32)
    return dict(gates_t=gate.T, pos=pos, tile_expert=tile_expert, n_used=n_used, rows=n_tiles * rt)


def _split_router(w_router, n_exp, d):
    wrt = w_router.T.reshape(N_GROUPS, n_exp // N_GROUPS, d).transpose(1, 0, 2).reshape(n_exp, d)
    high = wrt.astype(BF16)
    return jnp.concatenate([high, (wrt - high.astype(F32)).astype(BF16)], axis=0)


def _prep_layer(l, w):
    d = w["w_in"].shape[1]
    n_exp = w["w_e1"].shape[1]
    H = N_HEADS
    w_in = w["w_in"][l]
    qkw = d
    g0 = 3 * d
    w_main = jnp.concatenate([w_in[:, :g0], w_in[:, g0 + 2 * H:]], axis=1).astype(BF16)
    wg = jnp.pad(w_in[:, g0:g0 + 2 * H], ((0, 0), (0, LANES - 2 * H)))
    wg_high = wg.astype(BF16)
    w_gate = jnp.concatenate([wg_high, (wg - wg_high.astype(F32)).astype(BF16)], axis=1)
    gbias = jnp.pad(jnp.concatenate([w["b_igate"][l], w["b_fgate"][l]]), (0, LANES - 2 * H)).reshape(1, LANES)
    return dict(
        w_main=w_main, w_gate=w_gate, gbias=gbias,
        g_pre_mix=w["g_pre_mix"][l].reshape(1, d), g_post_mix=w["g_post_mix"][l].reshape(1, d),
        g_pre_ffn=w["g_pre_ffn"][l].reshape(1, d), g_post_ffn=w["g_post_ffn"][l].reshape(1, d),
        w_conv=w["w_conv"][l], b_conv=w["b_conv"][l].reshape(1, qkw),
        g_head=w["g_head"][l].reshape(1, d), g_vnorm=w["g_vnorm"][l].reshape(1, d),
        b_vnorm=w["b_vnorm"][l].reshape(1, d),
        w_spatial=w["w_spatial"][l], b_spatial=w["b_spatial"][l],
        w_out=w["w_out"][l].astype(BF16),
        wr=_split_router(w["w_router"][l], n_exp, d),
        br=w["b_router"][l].reshape(N_GROUPS, n_exp // N_GROUPS).T.reshape(n_exp, 1),
        w1=w["w_e1"], w3=w["w_e3"], w2=w["w_e2"], layer=l,
        ws1=w["w_s1"][l].astype(BF16), ws3=w["w_s3"][l].astype(BF16), ws2=w["w_s2"][l].astype(BF16),
    )


def _spatial_weights(lw, start, gl):
    idx = jnp.arange(GMLP_CHUNK)
    mask = (idx[None, :] // CHUNK) <= (idx[:, None] // CHUNK)
    wsp = jnp.where(mask, lw["w_spatial"], 0.0)[:, start:start + gl, start:start + gl].astype(BF16)
    bsp_t = lw["b_spatial"][:, start:start + gl].T
    return wsp, bsp_t


def _tile(n, cap):
    t = cap
    while n % t:
        t //= 2
    return t


def _tie(x, dep):
    if dep is None:
        return x
    return lax.optimization_barrier((x, dep))[0]


class _Chain:
    def __init__(self, x, mods, c0, n0, m0, conv0, layers, mlstm_l, g_start, g_l, keep_v, seq0=0, nb=None):
        b_all, t_seq, d = x.shape
        self.nb = b_all if nb is None else nb
        self.t_seq, self.d = t_seq, d
        self.n = self.nb * t_seq
        self.total_rows = b_all * t_seq
        self.out_row0 = seq0 * t_seq
        self.x, self.x_row0 = x.reshape(self.total_rows, d), seq0 * t_seq
        self.mods, self.layers = mods, layers
        self.c0, self.n0, self.m0, self.conv0 = c0, n0, m0, conv0
        self.mlstm_l, self.g_start, self.g_l, self.keep_v = mlstm_l, g_start, g_l, keep_v
        per_token = mods[0][0].shape[1] != 1
        tok_cap = self.n if per_token else t_seq
        self.tm_in, self.tm_mg, self.tm_moe = _tile(tok_cap, 1024), _tile(tok_cap, 512), _tile(tok_cap, 1024)
        self.dispatched = self.n % (SC_WORKERS * SC_BATCH) == 0 and not per_token
        self.outs = [[] for _ in range(5 if keep_v else 4)]

    def inproj(self, l, dep=None):
        lw = self.layers[l]
        sh1, sc1 = self.mods[l][0], self.mods[l][1]
        self.p, self.gates = _inproj(self.x, _tie(sc1, dep), sh1, lw["g_pre_mix"], lw["w_main"], lw["w_gate"],
                                     self.tm_in, self.t_seq, self.x_row0, self.n)
        return self.gates

    def mlstm(self, l, dep=None):
        lw = self.layers[l]
        m0 = self.m0[l]
        m0p = jnp.pad(m0, ((0, 0), (0, LANES - m0.shape[1]))).reshape(self.nb, 1, LANES)
        self.hm, c_new, n_new, m_new, conv_new = _mlstm(
            self.p, self.gates, self.c0[l], self.n0[l], m0p, self.conv0[l], lw["w_conv"], lw["b_conv"],
            _tie(lw["gbias"], dep), lw["g_head"], self.nb, self.mlstm_l)
        for lst, val in zip(self.outs, [c_new, n_new, m_new[:, 0, :N_HEADS], conv_new]):
            lst.append(val)
        return self.hm

    def merge(self, l, dep=None):
        lw = self.layers[l]
        _, _, gt1, sh2, sc2, _ = self.mods[l]
        wsp, bsp_t = _spatial_weights(lw, self.g_start, self.g_l)
        res = _merge(self.p, self.hm, self.x, wsp, bsp_t, _tie(lw["g_vnorm"], dep), lw["b_vnorm"], lw["w_out"],
                     lw["g_post_mix"], gt1, lw["g_pre_ffn"], sc2, sh2, lw["wr"], self.tm_mg, self.t_seq,
                     self.keep_v, self.x_row0)
        self.x1, self.h2, self.logits = res[0], res[1], res[2]
        if self.keep_v:
            self.outs[4].append(res[3].reshape(self.nb, self.t_seq, self.d))
        if not self.dispatched:
            return self.logits
        self.route = _route_rows(self.logits, lw, self.n)
        return self.route["pos"]

    def mixer(self, l, dep=None):
        return self.merge(l, self.mlstm(l, self.inproj(l, dep)))

    def scatter(self, l, dep=None):
        self.xs = _sc_scatter(self.h2, _tie(self.route["pos"], dep), self.route["rows"], 0)

    def experts(self, l, dep=None):
        lw, r = self.layers[l], self.route
        ys = _experts(self.xs, _tie(r["tile_expert"], dep), r["n_used"], lw["w1"], lw["w3"], lw["w2"],
                      lw["layer"], EXPERT_ROWS)
        self.yk = _sc_gather(ys, r["pos"])
        return ys

    def combine(self, l, dep=None, shared=None):
        lw, gt2 = self.layers[l], self.mods[l][5]
        rows, row0, buf = (self.n, 0, None) if shared is None else (self.total_rows, self.out_row0, shared[0])
        self.x = _combine(self.yk, _tie(self.route["gates_t"], dep), self.h2, self.x1, lw, gt2,
                          _tile(self.t_seq, 512), self.t_seq, 0, rows, row0, buf)
        self.x_row0 = 0
        return self.x

    def dense(self, l, dep=None):
        lw, gt2 = self.layers[l], self.mods[l][5]
        self.x = _moe_dense(self.h2, self.x1, _tie(self.logits, dep), lw, gt2, self.tm_moe, self.t_seq, 4)
        self.x_row0 = 0
        return self.x

    def layer(self, l, dep=None, shared=None):
        t = self.mixer(l, dep)
        if not self.dispatched:
            return self.dense(l, t)
        self.scatter(l)
        return self.combine(l, self.experts(l), shared)

    def states(self):
        return tuple(jnp.stack(lst) for lst in self.outs)


def _run_two_chains(a, b, side, depth):
    last = depth - 1
    t = a.mixer(0)
    a.scatter(0)
    side_done = 0
    for l in range(depth):
        t = b.mlstm(l, b.inproj(l, t))
        t = a.experts(l, t)
        t = b.merge(l, t)
        t = y = a.combine(l, t, (None,) if l == last else None)
        b.scatter(l, t)
        if l < last:
            t = a.mlstm(l + 1, a.inproj(l + 1, t))
            t = b.experts(l, t)
            t = a.merge(l + 1, t)
            t = b.combine(l, t)
            a.scatter(l + 1, t)
        else:
            half = (depth + 1) // 2
            for sl in range(half):
                t = side.layer(sl, t)
            t = b.experts(l, t)
            for sl in range(half, depth):
                t = side.layer(sl, t)
            y = b.combine(l, t, (y,))
    return y


def kernel(x_prompt, x_sample, c_prompt, c_sample, state_mlstm_C, state_mlstm_n, state_mlstm_m, state_conv,
           w_ada, b_ada, g_pre_mix, g_post_mix, g_pre_ffn, g_post_ffn, w_in, b_igate, b_fgate, w_conv, b_conv,
           g_head, g_vnorm, b_vnorm, w_spatial, b_spatial, w_out, w_router, b_router, w_e1, w_e3, w_e2,
           w_s1, w_s3, w_s2):
    w = dict(w_in=w_in, b_igate=b_igate, b_fgate=b_fgate, g_pre_mix=g_pre_mix, g_post_mix=g_post_mix,
             g_pre_ffn=g_pre_ffn, g_post_ffn=g_post_ffn, w_conv=w_conv, b_conv=b_conv, g_head=g_head,
             g_vnorm=g_vnorm, b_vnorm=b_vnorm, w_spatial=w_spatial, b_spatial=b_spatial, w_out=w_out,
             w_router=w_router, b_router=b_router, w_e1=w_e1, w_e3=w_e3, w_e2=w_e2, w_s1=w_s1, w_s3=w_s3,
             w_s2=w_s2)
    depth = w_in.shape[0]
    bp, tp, d = x_prompt.shape
    bs, ts, _ = x_sample.shape
    H, dv, dqk = state_mlstm_C.shape[2:]
    qkw = state_conv.shape[-1]
    layers = [_prep_layer(l, w) for l in range(depth)]

    mod = _ada(jnp.concatenate([c_prompt, c_sample], axis=0), w_ada, b_ada)
    mods_p, mods_s = [], []
    for l in range(depth):
        parts = [mod[l][:, i * d:(i + 1) * d] for i in range(6)]
        mods_p.append([a[:bp].reshape(bp, 1, d) for a in parts])
        mods_s.append([jnp.repeat(a[bp:], ts, axis=0).reshape(1, bs * ts, d) for a in parts])

    lp = 256 if tp % 256 == 0 else CHUNK
    chains = PROMPT_CHAINS if bp % PROMPT_CHAINS == 0 else 1
    bc = bp // chains
    zc = jnp.zeros((depth, bc, H, dv, dqk), F32)
    zn = jnp.zeros((depth, bc, H, dqk), F32)
    zm = jnp.zeros((depth, bc, H), F32)
    zconv = jnp.zeros((depth, bc, CONV_W - 1, qkw), F32)
    prompt = []
    for ci in range(chains):
        mods_c = [[a[ci * bc:(ci + 1) * bc] for a in layer_mods] for layer_mods in mods_p]
        prompt.append(_Chain(x_prompt, mods_c, zc, zn, zm, zconv, layers, lp, 0, GMLP_CHUNK, False,
                             seq0=ci * bc, nb=bc))
    sample = _Chain(x_sample, mods_s, state_mlstm_C, state_mlstm_n, state_mlstm_m, state_conv, layers, ts,
                    PAST_LEN % GMLP_CHUNK, ts, True)
    if chains == 2 and all(c.dispatched for c in prompt) and not sample.dispatched:
        y_p = _run_two_chains(prompt[0], prompt[1], sample, depth)
    else:
        y_p = None
        for c in prompt:
            for l in range(depth):
                shared = (y_p,) if (l == depth - 1 and c.dispatched and chains > 1) else None
                out = c.layer(l, None, shared)
            if c.dispatched and chains > 1:
                y_p = out
            else:
                y_p = out if y_p is None else jnp.concatenate([y_p, out], axis=0)
        for l in range(depth):
            sample.layer(l)
    y_p = y_p.reshape(bp, tp, d)
    y_s = sample.x.reshape(bs, ts, d)
    states_p = [c.states() for c in prompt]
    p_c, p_n, p_m, p_conv = (jnp.concatenate([s[i] for s in states_p], axis=1) for i in range(4))
    s_c, s_n, s_m, s_conv, s_v = sample.states()
    return (y_p, y_s, p_c, p_n, p_m, p_conv, s_c, s_n, s_m, s_conv, s_v)
```

```python
import functools

import jax
import jax.numpy as jnp
from jax import lax
from jax.experimental import pallas as pl
from jax.experimental.pallas import tpu as pltpu
from jax.experimental.pallas import tpu_sc as plsc

F32 = jnp.float32
BF16 = jnp.bfloat16

EPS = 1e-6
N_HEADS = 4
CONV_W = 4
GMLP_GROUPS = 8
GMLP_CHUNK = 128
CHUNK = 64
N_GROUPS = 8
TOPK_GROUPS = 4
TOP_K = 8
ROUTE_SCALE = 2.5
PAST_LEN = 4096

LANES = 128
SUBLANES = 8
VMEM_LIMIT = 56 * 1024 * 1024

NT_DIMS = (((1,), (1,)), ((), ()))
TN_DIMS = (((0,), (0,)), ((), ()))


def _sigmoid(x):
    return 0.5 * (jnp.tanh(0.5 * x) + 1.0)


def _silu(x):
    return x * _sigmoid(x)


def _gelu_tanh(x):
    return x * (0.5 * (1.0 + jnp.tanh(0.7978845608028654 * (x + 0.044715 * (x * x * x)))))


def _rms(x, g):
    return x * lax.rsqrt(jnp.mean(x * x, -1, keepdims=True) + EPS) * g


def _pack_pairs(x):
    c = x.shape[1] // 2
    return pltpu.pack_elementwise([x[:, :c], x[:, c:]], packed_dtype=BF16)


def _unpack_pairs(u):
    lo = pltpu.unpack_elementwise(u, index=0, packed_dtype=BF16, unpacked_dtype=F32)
    hi = pltpu.unpack_elementwise(u, index=1, packed_dtype=BF16, unpacked_dtype=F32)
    return jnp.concatenate([lo, hi], axis=1)


def _params(sem):
    return pltpu.CompilerParams(dimension_semantics=sem, vmem_limit_bytes=VMEM_LIMIT)


def _ada_kernel(c_ref, w_ref, b_ref, o_ref):
    a = _silu(c_ref[...]).astype(BF16)
    o_ref[0] = jnp.dot(a, w_ref[0].astype(BF16), preferred_element_type=F32) + b_ref[0]


def _ada(c, w_ada, b_ada):
    depth, d, six_d = w_ada.shape
    nb = c.shape[0]
    tn = d
    return pl.pallas_call(
        _ada_kernel,
        grid=(depth, six_d // tn),
        in_specs=[
            pl.BlockSpec((nb, d), lambda l, j: (0, 0)),
            pl.BlockSpec((1, d, tn), lambda l, j: (l, 0, j)),
            pl.BlockSpec((1, 1, tn), lambda l, j: (l, 0, j)),
        ],
        out_specs=pl.BlockSpec((1, nb, tn), lambda l, j: (l, 0, j)),
        out_shape=jax.ShapeDtypeStruct((depth, nb, six_d), F32),
        compiler_params=_params(("parallel", "parallel")),
        name="adaln",
    )(c, w_ada, b_ada.reshape(depth, 1, six_d))


def _inproj_kernel(x_ref, sc_ref, sh_ref, g_ref, w_ref, wg_ref, p_ref, gates_ref, h_scr):
    @pl.when(pl.program_id(1) == 0)
    def _():
        h = _rms(x_ref[...], g_ref[...]) * (1.0 + sc_ref[0]) + sh_ref[0]
        hb = h.astype(BF16)
        h_scr[...] = hb
        h_low = (h - hb.astype(F32)).astype(BF16)
        gg = jnp.dot(hb, wg_ref[...], preferred_element_type=F32)
        gates_ref[...] = (gg[:, :LANES] + gg[:, LANES:]
                          + jnp.dot(h_low, wg_ref[:, :LANES], preferred_element_type=F32))

    p_ref[...] = jnp.dot(h_scr[...], w_ref[...], preferred_element_type=F32).astype(BF16)


def _mod_spec(mod, tm, tiles_per_seq, ngrid):
    d = mod.shape[-1]
    if mod.shape[1] == 1:
        if ngrid == 2:
            return pl.BlockSpec((1, 1, d), lambda i, j: (i // tiles_per_seq, 0, 0))
        return pl.BlockSpec((1, 1, d), lambda i: (i // tiles_per_seq, 0, 0))
    if ngrid == 2:
        return pl.BlockSpec((1, tm, d), lambda i, j: (0, i, 0))
    return pl.BlockSpec((1, tm, d), lambda i: (0, i, 0))


def _inproj(x, sc, sh, g, w_main, w_gate, tm, t_seq, row0, n):
    d = x.shape[1]
    blk0 = row0 // tm
    width = w_main.shape[1]
    ncol = 2 if (width // 2) % LANES == 0 else width // d
    wc = width // ncol
    tps = max(t_seq // tm, 1)
    return pl.pallas_call(
        _inproj_kernel,
        grid=(n // tm, ncol),
        in_specs=[
            pl.BlockSpec((tm, d), lambda i, j: (i + blk0, 0)),
            _mod_spec(sc, tm, tps, 2),
            _mod_spec(sh, tm, tps, 2),
            pl.BlockSpec((1, d), lambda i, j: (0, 0)),
            pl.BlockSpec((d, wc), lambda i, j: (0, j)),
            pl.BlockSpec((d, 2 * LANES), lambda i, j: (0, 0)),
        ],
        out_specs=[
            pl.BlockSpec((tm, wc), lambda i, j: (i, j)),
            pl.BlockSpec((tm, LANES), lambda i, j: (i, 0)),
        ],
        out_shape=[
            jax.ShapeDtypeStruct((n, width), BF16),
            jax.ShapeDtypeStruct((n, LANES), F32),
        ],
        scratch_shapes=[pltpu.VMEM((tm, d), BF16)],
        compiler_params=_params(("parallel", "arbitrary")),
        name="inproj",
    )(x, sc, sh, g, w_main, w_gate)


def _mlstm_kernel(qk_ref, v_ref, gt_ref, c0_ref, n0_ref, m0_ref, conv0_ref, wconv_ref, bconv_ref,
                  gbias_ref, ghead_ref, hm_ref, c_ref, n_ref, m_ref, conv_ref, xbuf, *, L, H, dqk, dv):
    qkw = 2 * H * dqk
    pad = SUBLANES
    tail0 = pad - (CONV_W - 1)

    @pl.when(pl.program_id(1) == 0)
    def _():
        c_ref[...] = c0_ref[...]
        n_ref[...] = n0_ref[...]
        m_ref[...] = m0_ref[...]
        xbuf[0:pad, :] = jnp.zeros((pad, qkw), F32)
        xbuf[tail0:pad, :] = conv0_ref[0]

    xbuf[pad:pad + L, :] = qk_ref[...].astype(F32)
    w = wconv_ref[...]
    y = xbuf[tail0:tail0 + L, :] * w[0:1, :]
    for j in range(1, CONV_W):
        y = y + xbuf[tail0 + j:tail0 + j + L, :] * w[j:j + 1, :]
    y = y + bconv_ref[...]
    tail = xbuf[L + tail0:L + pad, :]
    conv_ref[0] = tail
    xbuf[tail0:pad, :] = tail
    qk = _silu(y)

    z = gt_ref[...] + gbias_ref[...]
    lf = jnp.minimum(z, 0.0) - jnp.log(1.0 + jnp.exp(-jnp.abs(z)))
    row = lax.broadcasted_iota(jnp.int32, (L, L), 0)
    col = lax.broadcasted_iota(jnp.int32, (L, L), 1)
    tri = row >= col
    hi = lax.Precision.HIGHEST
    b_col = jnp.dot(tri.astype(F32), lf, precision=hi, preferred_element_type=F32)
    eye = (lax.broadcasted_iota(jnp.int32, (LANES, LANES), 0)
           == lax.broadcasted_iota(jnp.int32, (LANES, LANES), 1)).astype(F32)
    b_row = lax.dot_general(eye, b_col, NT_DIMS, precision=hi, preferred_element_type=F32)
    z_row = lax.dot_general(eye, z, NT_DIMS, precision=hi, preferred_element_type=F32)

    scale = dqk ** -0.5
    lane1 = lax.broadcasted_iota(jnp.int32, (1, LANES), 1)
    m_old = m_ref[0]
    m_new = m_old
    for h in range(H):
        q = qk[:, h * dqk:(h + 1) * dqk] * scale
        k = qk[:, (H + h) * dqk:(H + h + 1) * dqk]
        qb = q.astype(BF16)
        vb = v_ref[:, h * dv:(h + 1) * dv]
        b_c = b_col[:, H + h:H + h + 1]
        i_c = z[:, h:h + 1]
        b_r = b_row[H + h:H + h + 1, :]
        i_r = z_row[h:h + 1, :]
        m_prev = m_old[:, h:h + 1]
        a = b_c + m_prev
        dm = jnp.where(tri, b_c - b_r + i_r, -jnp.inf)
        mt = jnp.maximum(a, jnp.max(dm, axis=-1, keepdims=True))
        s = lax.dot_general(qb, k.astype(BF16), NT_DIMS, preferred_element_type=F32)
        sg = s * jnp.exp(dm - mt)
        aw = jnp.exp(a - mt)
        cm = c_ref[0, h]
        qc = lax.dot_general(qb, cm.astype(BF16), NT_DIMS, preferred_element_type=F32)
        num = jnp.dot(sg.astype(BF16), vb, preferred_element_type=F32) + aw * qc
        nrow = n_ref[0, h:h + 1, :]
        den = jnp.sum(sg, -1, keepdims=True) + aw * jnp.sum(q * nrow, -1, keepdims=True)
        hh = num / jnp.maximum(jnp.abs(den), jnp.exp(-mt))
        m_last = mt[L - 1:L, :]
        b_last = b_c[L - 1:L, :]
        decay = jnp.exp(b_last + m_prev - m_last)
        ws = jnp.exp(b_last - b_c + i_c - m_last)
        kw = ws * k
        c_ref[0, h] = decay * cm + lax.dot_general(vb, kw.astype(BF16), TN_DIMS, preferred_element_type=F32)
        n_ref[0, h:h + 1, :] = decay * nrow + jnp.sum(kw, axis=0, keepdims=True)
        m_new = jnp.where(lane1 == h, m_last, m_new)
        hm_ref[:, h * dv:(h + 1) * dv] = _rms(hh, ghead_ref[:, h * dv:(h + 1) * dv]).astype(BF16)
    m_ref[0] = m_new


def _mlstm(p, gates, c0, n0, m0, conv0, w_conv, b_conv, gbias, g_head, nb, L):
    n = p.shape[0]
    t_seq = n // nb
    nc = t_seq // L
    _, H, dv, dqk = c0.shape
    qkw = 2 * H * dqk
    vw = H * dv
    kern = functools.partial(_mlstm_kernel, L=L, H=H, dqk=dqk, dv=dv)
    return pl.pallas_call(
        kern,
        grid=(nb, nc),
        in_specs=[
            pl.BlockSpec((L, qkw), lambda b, c: (b * nc + c, 0)),
            pl.BlockSpec((L, vw), lambda b, c: (b * nc + c, 1)),
            pl.BlockSpec((L, LANES), lambda b, c: (b * nc + c, 0)),
            pl.BlockSpec((1, H, dv, dqk), lambda b, c: (b, 0, 0, 0)),
            pl.BlockSpec((1, H, dqk), lambda b, c: (b, 0, 0)),
            pl.BlockSpec((1, 1, LANES), lambda b, c: (b, 0, 0)),
            pl.BlockSpec((1, CONV_W - 1, qkw), lambda b, c: (b, 0, 0)),
            pl.BlockSpec((CONV_W, qkw), lambda b, c: (0, 0)),
            pl.BlockSpec((1, qkw), lambda b, c: (0, 0)),
            pl.BlockSpec((1, LANES), lambda b, c: (0, 0)),
            pl.BlockSpec((1, vw), lambda b, c: (0, 0)),
        ],
        out_specs=[
            pl.BlockSpec((L, vw), lambda b, c: (b * nc + c, 0)),
            pl.BlockSpec((1, H, dv, dqk), lambda b, c: (b, 0, 0, 0)),
            pl.BlockSpec((1, H, dqk), lambda b, c: (b, 0, 0)),
            pl.BlockSpec((1, 1, LANES), lambda b, c: (b, 0, 0)),
            pl.BlockSpec((1, CONV_W - 1, qkw), lambda b, c: (b, 0, 0)),
        ],
        out_shape=[
            jax.ShapeDtypeStruct((n, vw), BF16),
            jax.ShapeDtypeStruct((nb, H, dv, dqk), F32),
            jax.ShapeDtypeStruct((nb, H, dqk), F32),
            jax.ShapeDtypeStruct((nb, 1, LANES), F32),
            jax.ShapeDtypeStruct((nb, CONV_W - 1, qkw), F32),
        ],
        scratch_shapes=[pltpu.VMEM((L + SUBLANES, qkw), F32)],
        compiler_params=_params(("parallel", "arbitrary")),
        name="mlstm",
    )(p, p, gates, c0, n0, m0, conv0, w_conv, b_conv, gbias, g_head)


def _merge_kernel(o_ref, u_ref, vg_ref, ga_ref, gb_ref, hm_ref, x_ref, wsp_ref, bsp_ref, gvn_ref, bvn_ref,
                  wout_ref, gpost_ref, gt1_ref, gpre_ref, sc2_ref, sh2_ref, wr_ref, *rest, tm, gl, keep_v):
    if keep_v:
        x1_ref, h2_ref, lg_ref, vn_ref, sg_scr = rest
    else:
        x1_ref, h2_ref, lg_ref, sg_scr = rest
    groups = wsp_ref.shape[0]
    ch = vg_ref.shape[1] // groups
    vg = _gelu_tanh(vg_ref[...].astype(F32))
    mu = jnp.mean(vg, -1, keepdims=True)
    xc = vg - mu
    vn = xc * lax.rsqrt(jnp.mean(xc * xc, -1, keepdims=True) + EPS) * gvn_ref[...] + bvn_ref[...]
    if keep_v:
        vn_ref[...] = vn
    vnb = vn.astype(BF16)
    for ci in range(tm // gl):
        for g in range(groups):
            blk = jnp.dot(wsp_ref[g], vnb[ci * gl:(ci + 1) * gl, g * ch:(g + 1) * ch],
                          preferred_element_type=F32)
            sg_scr[ci * gl:(ci + 1) * gl, g * ch:(g + 1) * ch] = blk + bsp_ref[:, g:g + 1]
    h_b = _gelu_tanh(u_ref[...].astype(F32)) * sg_scr[...]
    h_a = _sigmoid(o_ref[...].astype(F32)) * hm_ref[...].astype(F32)
    merged = _sigmoid(ga_ref[...].astype(F32)) * h_a + _sigmoid(gb_ref[...].astype(F32)) * h_b
    y = jnp.dot(merged.astype(BF16), wout_ref[...], preferred_element_type=F32)
    x1 = x_ref[...] + gt1_ref[0] * _rms(y, gpost_ref[...])
    x1_ref[...] = x1
    h2 = _rms(x1, gpre_ref[...]) * (1.0 + sc2_ref[0]) + sh2_ref[0]
    h2_ref[...] = _pack_pairs(h2)
    n_exp = lg_ref.shape[0]
    h2_high = h2.astype(BF16)
    h2_low = (h2 - h2_high.astype(F32)).astype(BF16)
    lg = lax.dot_general(wr_ref[...], h2_high, NT_DIMS, preferred_element_type=F32)
    lg_ref[...] = (lg[:n_exp] + lg[n_exp:]
                   + lax.dot_general(wr_ref[:n_exp, :], h2_low, NT_DIMS, preferred_element_type=F32))


def _merge(p, hm, x, wsp, bsp_t, g_vn, b_vn, w_out, g_post, gt1, g_pre, sc2, sh2, wr, tm, t_seq, keep_v, x_row0):
    n, d = hm.shape
    x_blk0 = x_row0 // tm
    groups, gl, _ = wsp.shape
    n_exp = wr.shape[0] // 2
    tps = max(t_seq // tm, 1)
    row = lambda i: (0, 0)
    pcol = lambda c: pl.BlockSpec((tm, d), lambda i: (i, c))
    tok = pl.BlockSpec((tm, d), lambda i: (i, 0))
    vec = pl.BlockSpec((1, d), row)
    out_specs = [tok, pl.BlockSpec((tm, d // 2), lambda i: (i, 0)), pl.BlockSpec((n_exp, tm), lambda i: (0, i))]
    out_shape = [jax.ShapeDtypeStruct((n, d), F32), jax.ShapeDtypeStruct((n, d // 2), jnp.uint32),
                 jax.ShapeDtypeStruct((n_exp, n), F32)]
    if keep_v:
        out_specs.append(tok)
        out_shape.append(jax.ShapeDtypeStruct((n, d), F32))
    kern = functools.partial(_merge_kernel, tm=tm, gl=gl, keep_v=keep_v)
    return pl.pallas_call(
        kern,
        grid=(n // tm,),
        in_specs=[
            pcol(2), pcol(3), pcol(4), pcol(5), pcol(6), tok,
            pl.BlockSpec((tm, d), lambda i: (i + x_blk0, 0)),
            pl.BlockSpec((groups, gl, gl), lambda i: (0, 0, 0)),
            pl.BlockSpec((gl, groups), row),
            vec, vec,
            pl.BlockSpec((d, d), row),
            vec,
            _mod_spec(gt1, tm, tps, 1),
            vec,
            _mod_spec(sc2, tm, tps, 1),
            _mod_spec(sh2, tm, tps, 1),
            pl.BlockSpec((2 * n_exp, d), row),
        ],
        out_specs=out_specs,
        out_shape=out_shape,
        scratch_shapes=[pltpu.VMEM((tm, d), F32)],
        compiler_params=_params(("parallel",)),
        name="merge",
    )(p, p, p, p, p, hm, x, wsp, bsp_t, g_vn, b_vn, w_out, g_post, gt1, g_pre, sc2, sh2, wr)


def _select(logits, br_ref):
    n_exp, tm = logits.shape
    per = n_exp // N_GROUPS
    neg = -jnp.inf
    s = _sigmoid(logits)
    sel = s + br_ref[...]
    mem = [sel[r * N_GROUPS:(r + 1) * N_GROUPS, :] for r in range(per)]
    grow = lax.broadcasted_iota(jnp.int32, (N_GROUPS, tm), 0)
    m1 = functools.reduce(jnp.maximum, mem)
    idx1 = functools.reduce(jnp.minimum, [jnp.where(mem[r] == m1, r, per) for r in range(per)])
    m2 = functools.reduce(jnp.maximum, [jnp.where(idx1 == r, neg, mem[r]) for r in range(per)])
    gs = m1 + m2
    gmask = jnp.zeros((N_GROUPS, tm), jnp.bool_)
    for _ in range(TOPK_GROUPS):
        mx = jnp.max(gs, axis=0, keepdims=True)
        gi = jnp.min(jnp.where(gs == mx, grow, N_GROUPS), axis=0, keepdims=True)
        pick = grow == gi
        gmask = jnp.logical_or(gmask, pick)
        gs = jnp.where(pick, neg, gs)
    msk = [jnp.where(gmask, mem[r], neg) for r in range(per)]
    eidx = [grow * per + r for r in range(per)]
    chosen = [jnp.zeros((N_GROUPS, tm), jnp.bool_) for _ in range(per)]
    firsts = []
    for _ in range(TOP_K):
        mx = jnp.max(functools.reduce(jnp.maximum, msk), axis=0, keepdims=True)
        cand = functools.reduce(jnp.minimum, [jnp.where(msk[r] == mx, eidx[r], n_exp) for r in range(per)])
        first = jnp.min(cand, axis=0, keepdims=True)
        firsts.append(first)
        for r in range(per):
            pick = eidx[r] == first
            chosen[r] = jnp.logical_or(chosen[r], pick)
            msk[r] = jnp.where(pick, neg, msk[r])
    wk = [jnp.where(chosen[r], s[r * N_GROUPS:(r + 1) * N_GROUPS, :], 0.0) for r in range(per)]
    denom = jnp.sum(functools.reduce(jnp.add, wk), axis=0, keepdims=True)
    return s, chosen, firsts, eidx, denom


def _shared_expert(hb, ws1_ref, ws3_ref, ws2_ref):
    a = jnp.dot(hb, ws1_ref[...], preferred_element_type=F32)
    g = jnp.dot(hb, ws3_ref[...], preferred_element_type=F32)
    return jnp.dot((_silu(a) * g).astype(BF16), ws2_ref[...], preferred_element_type=F32)


def _moe_kernel(h_ref, x1_ref, lg_ref, br_ref, w1_ref, w3_ref, w2_ref, ws1_ref, ws3_ref, ws2_ref, gpost_ref,
                gt2_ref, out_ref, gt_scr, gates_scr, acc_scr, *, eb):
    j = pl.program_id(1)
    hb = _unpack_pairs(h_ref[...]).astype(BF16)
    n_exp = lg_ref.shape[0]
    per = n_exp // N_GROUPS

    @pl.when(j == 0)
    def _():
        s, chosen, _, _, denom = _select(lg_ref[...], br_ref)
        gt_scr[n_exp:, :] = jnp.zeros((gt_scr.shape[0] - n_exp, hb.shape[0]), F32)
        for r in range(per):
            wk = jnp.where(chosen[r], s[r * N_GROUPS:(r + 1) * N_GROUPS, :], 0.0)
            gt_scr[r * N_GROUPS:(r + 1) * N_GROUPS, :] = wk / denom * ROUTE_SCALE
        gates_scr[...] = gt_scr[...].T
        acc_scr[...] = _shared_expert(hb, ws1_ref, ws3_ref, ws2_ref)

    first = j * eb
    lane0 = lax.rem(first, per) * N_GROUPS + first // per
    gates = pltpu.roll(gates_scr[...], lax.rem(LANES - lane0, LANES), 1)
    acc = acc_scr[...]
    for e in range(eb):
        a = jnp.dot(hb, w1_ref[e].astype(BF16), preferred_element_type=F32)
        g = jnp.dot(hb, w3_ref[e].astype(BF16), preferred_element_type=F32)
        gate = gates[:, e * N_GROUPS:e * N_GROUPS + 1]
        acc = acc + jnp.dot((_silu(a) * g * gate).astype(BF16), w2_ref[e].astype(BF16),
                            preferred_element_type=F32)
    acc_scr[...] = acc

    @pl.when(j == pl.num_programs(1) - 1)
    def _():
        out_ref[...] = x1_ref[...] + gt2_ref[0] * _rms(acc_scr[...], gpost_ref[...])


def _moe_dense(h2, x1, logits, lw, gt2, tm, t_seq, eb):
    n, d = x1.shape
    _, n_exp, _, f = lw["w1"].shape
    layer = lw["layer"]
    fs = lw["ws1"].shape[1]
    tps = max(t_seq // tm, 1)
    const = lambda i, j: (0, 0)
    return pl.pallas_call(
        functools.partial(_moe_kernel, eb=eb),
        grid=(n // tm, n_exp // eb),
        in_specs=[
            pl.BlockSpec((tm, d // 2), lambda i, j: (i, 0)),
            pl.BlockSpec((tm, d), lambda i, j: (i, 0)),
            pl.BlockSpec((n_exp, tm), lambda i, j: (0, i)),
            pl.BlockSpec((n_exp, 1), const),
            pl.BlockSpec((None, eb, d, f), lambda i, j: (layer, j, 0, 0)),
            pl.BlockSpec((None, eb, d, f), lambda i, j: (layer, j, 0, 0)),
            pl.BlockSpec((None, eb, f, d), lambda i, j: (layer, j, 0, 0)),
            pl.BlockSpec((d, fs), const),
            pl.BlockSpec((d, fs), const),
            pl.BlockSpec((fs, d), const),
            pl.BlockSpec((1, d), const),
            _mod_spec(gt2, tm, tps, 2),
        ],
        out_specs=pl.BlockSpec((tm, d), lambda i, j: (i, 0)),
        out_shape=jax.ShapeDtypeStruct((n, d), F32),
        scratch_shapes=[
            pltpu.VMEM((LANES, tm), F32),
            pltpu.VMEM((tm, LANES), F32),
            pltpu.VMEM((tm, d), F32),
        ],
        compiler_params=_params(("parallel", "arbitrary")),
        name="moe_dense",
    )(h2, x1, logits, lw["br"], lw["w1"], lw["w3"], lw["w2"], lw["ws1"], lw["ws3"], lw["ws2"],
      lw["g_post_ffn"], gt2)


def _router_kernel(lg_ref, br_ref, eidx_ref, rank_ref, gate_ref, cnt_ref, carry_scr):
    n_exp, tm = lg_ref.shape
    per = n_exp // N_GROUPS

    @pl.when(pl.program_id(0) == 0)
    def _():
        carry_scr[...] = jnp.zeros(carry_scr.shape, F32)

    s, chosen, firsts, eidx, denom = _select(lg_ref[...], br_ref)
    sel01 = jnp.concatenate([c.astype(F32) for c in chosen], axis=0)
    before = (lax.broadcasted_iota(jnp.int32, (tm, tm), 0)
              < lax.broadcasted_iota(jnp.int32, (tm, tm), 1)).astype(BF16)
    rank = jnp.dot(sel01.astype(BF16), before, preferred_element_type=F32) + carry_scr[:, 0:1]
    carry_scr[...] = carry_scr[...] + jnp.sum(sel01, axis=1, keepdims=True)
    cnt_ref[...] = carry_scr[...]
    for k in range(TOP_K):
        s_k = jnp.zeros((1, tm), F32)
        r_k = jnp.zeros((1, tm), F32)
        for r in range(per):
            pick = eidx[r] == firsts[k]
            rows = slice(r * N_GROUPS, (r + 1) * N_GROUPS)
            s_k = s_k + jnp.sum(jnp.where(pick, s[rows, :], 0.0), axis=0, keepdims=True)
            r_k = r_k + jnp.sum(jnp.where(pick, rank[rows, :], 0.0), axis=0, keepdims=True)
        eidx_ref[k:k + 1, :] = firsts[k]
        rank_ref[k:k + 1, :] = r_k.astype(jnp.int32)
        gate_ref[k:k + 1, :] = s_k / denom * ROUTE_SCALE


def _router(logits, br, tm, row0, n):
    n_exp = logits.shape[0]
    tok = pl.BlockSpec((TOP_K, tm), lambda i: (0, i))
    blk0 = row0 // tm
    return pl.pallas_call(
        _router_kernel,
        grid=(n // tm,),
        in_specs=[
            pl.BlockSpec((n_exp, tm), lambda i: (0, i + blk0)),
            pl.BlockSpec((n_exp, 1), lambda i: (0, 0)),
        ],
        out_specs=[tok, tok, tok, pl.BlockSpec((n_exp, LANES), lambda i: (0, 0))],
        out_shape=[
            jax.ShapeDtypeStruct((TOP_K, n), jnp.int32),
            jax.ShapeDtypeStruct((TOP_K, n), jnp.int32),
            jax.ShapeDtypeStruct((TOP_K, n), F32),
            jax.ShapeDtypeStruct((n_exp, LANES), F32),
        ],
        scratch_shapes=[pltpu.VMEM((n_exp, LANES), F32)],
        compiler_params=_params(("arbitrary",)),
        name="router",
    )(logits, br)


def _pos_kernel(starts_ref, eidx_ref, rank_ref, pos_ref):
    e = eidx_ref[...]
    base = jnp.zeros_like(e)
    for x in range(starts_ref.shape[0]):
        base = base + jnp.where(e == x, starts_ref[x], 0)
    pos = base + rank_ref[...]
    for c in range(pos_ref.shape[0]):
        pos_ref[c] = pos[:, c * SC_BATCH:(c + 1) * SC_BATCH]


def _positions(starts, eidx, rank, tm):
    k, n = eidx.shape
    grid_spec = pltpu.PrefetchScalarGridSpec(
        num_scalar_prefetch=1,
        grid=(n // tm,),
        in_specs=[pl.BlockSpec((k, tm), lambda i, s: (0, i)), pl.BlockSpec((k, tm), lambda i, s: (0, i))],
        out_specs=pl.BlockSpec((tm // SC_BATCH, k, SC_BATCH), lambda i, s: (i, 0, 0)),
    )
    return pl.pallas_call(
        _pos_kernel,
        grid_spec=grid_spec,
        out_shape=jax.ShapeDtypeStruct((n // SC_BATCH, k, SC_BATCH), jnp.int32),
        compiler_params=_params(("parallel",)),
        name="positions",
    )(starts, eidx, rank)


SC_BATCH = 128
SC_WORKERS = 32


def _sc_mesh():
    return plsc.VectorSubcoreMesh(core_axis_name="c", subcore_axis_name="s")


def _sc_scatter(x, pos, n_rows, row0):
    w = x.shape[1]
    n = pos.shape[0] * SC_BATCH
    info = plsc.get_sparse_core_info()
    nc, nw = info.num_cores, info.num_cores * info.num_subcores
    steps = n // (nw * SC_BATCH)

    @functools.partial(
        pl.kernel, mesh=_sc_mesh(),
        out_type=jax.ShapeDtypeStruct((n_rows, w), x.dtype),
        scratch_types=[pltpu.VMEM((TOP_K, SC_BATCH), jnp.int32), pltpu.VMEM((SC_BATCH, w), x.dtype),
                       pltpu.SemaphoreType.DMA],
    )
    def scatter(x_hbm, pos_hbm, out_hbm, idx_v, rows_v, sem):
        wid = lax.axis_index("s") * nc + lax.axis_index("c")

        @pl.loop(0, steps)
        def _(s):
            blk = wid * steps + s
            pltpu.sync_copy(pos_hbm.at[blk], idx_v)
            pltpu.sync_copy(x_hbm.at[pl.ds(row0 + blk * SC_BATCH, SC_BATCH)], rows_v)
            copies = [pltpu.async_copy(rows_v, out_hbm.at[idx_v.at[k]], sem) for k in range(TOP_K)]
            for c in copies:
                c.wait()

    return scatter(x, pos)


def _sc_gather(y, pos):
    w = y.shape[1]
    n = pos.shape[0] * SC_BATCH
    info = plsc.get_sparse_core_info()
    nc, nw = info.num_cores, info.num_cores * info.num_subcores
    steps = n // (nw * SC_BATCH)

    @functools.partial(
        pl.kernel, mesh=_sc_mesh(),
        out_type=jax.ShapeDtypeStruct((TOP_K, n, w), y.dtype),
        scratch_types=[pltpu.VMEM((TOP_K, SC_BATCH), jnp.int32), pltpu.VMEM((SC_BATCH, w), y.dtype),
                       pltpu.SemaphoreType.DMA],
    )
    def gather(y_hbm, pos_hbm, out_hbm, idx_v, rows_v, sem):
        wid = lax.axis_index("s") * nc + lax.axis_index("c")

        @pl.loop(0, steps)
        def _(s):
            blk = wid * steps + s
            pltpu.sync_copy(pos_hbm.at[blk], idx_v)
            for k in range(TOP_K):
                pltpu.async_copy(y_hbm.at[idx_v.at[k]], rows_v, sem).wait()
                pltpu.sync_copy(rows_v, out_hbm.at[k, pl.ds(blk * SC_BATCH, SC_BATCH)])

    return gather(y, pos)


def _experts_kernel(te_ref, nu_ref, xs_ref, w1_ref, w3_ref, w2_ref, ys_ref, w1_scr, w3_scr, w2_scr):
    i = pl.program_id(0)

    @pl.when(jnp.logical_or(i == 0, te_ref[i] != te_ref[jnp.maximum(i - 1, 0)]))
    def _():
        w1_scr[...] = w1_ref[...].astype(BF16)
        w3_scr[...] = w3_ref[...].astype(BF16)
        w2_scr[...] = w2_ref[...].astype(BF16)

    @pl.when(i < nu_ref[0])
    def _():
        x = _unpack_pairs(xs_ref[...]).astype(BF16)
        a = jnp.dot(x, w1_scr[...], preferred_element_type=F32)
        g = jnp.dot(x, w3_scr[...], preferred_element_type=F32)
        y = jnp.dot((_silu(a) * g).astype(BF16), w2_scr[...], preferred_element_type=F32)
        ys_ref[...] = _pack_pairs(y)


def _experts(xs, tile_expert, n_used, w1, w3, w2, layer, rt):
    rows, half = xs.shape
    _, n_exp, d, f = w1.shape
    scratch = [pltpu.VMEM((d, f), BF16), pltpu.VMEM((d, f), BF16), pltpu.VMEM((f, d), BF16)]
    grid_spec = pltpu.PrefetchScalarGridSpec(
        num_scalar_prefetch=2,
        grid=(rows // rt,),
        in_specs=[
            pl.BlockSpec((rt, half), lambda i, te, nu: (i, 0)),
            pl.BlockSpec((None, None, d, f), lambda i, te, nu: (layer, te[i], 0, 0)),
            pl.BlockSpec((None, None, d, f), lambda i, te, nu: (layer, te[i], 0, 0)),
            pl.BlockSpec((None, None, f, d), lambda i, te, nu: (layer, te[i], 0, 0)),
        ],
        out_specs=pl.BlockSpec((rt, half), lambda i, te, nu: (i, 0)),
        scratch_shapes=scratch,
    )
    return pl.pallas_call(
        _experts_kernel,
        grid_spec=grid_spec,
        out_shape=jax.ShapeDtypeStruct((rows, half), jnp.uint32),
        compiler_params=_params(("arbitrary",)),
        name="experts",
    )(tile_expert, n_used, xs, w1, w3, w2)


def _combine_kernel(yk_ref, g_ref, h_ref, x1_ref, ws1_ref, ws3_ref, ws2_ref, gpost_ref, gt2_ref, *rest):
    out_ref = rest[-1]
    hb = _unpack_pairs(h_ref[...]).astype(BF16)
    acc = _shared_expert(hb, ws1_ref, ws3_ref, ws2_ref)
    gates = g_ref[...]
    for k in range(TOP_K):
        acc = acc + gates[:, k:k + 1] * _unpack_pairs(yk_ref[k])
    out_ref[...] = x1_ref[...] + gt2_ref[0] * _rms(acc, gpost_ref[...])


def _combine(yk, gates_t, h2, x1, lw, gt2, tm, t_seq, row0, out_rows, out_row0, out_prev):
    d = x1.shape[1]
    nc = yk.shape[1]
    fs = lw["ws1"].shape[1]
    tps = max(t_seq // tm, 1)
    blk0 = row0 // tm
    out_blk0 = (out_row0 + row0) // tm
    const = lambda i: (0, 0)
    in_specs = [
        pl.BlockSpec((TOP_K, tm, d // 2), lambda i: (0, i, 0)),
        pl.BlockSpec((tm, TOP_K), lambda i: (i, 0)),
        pl.BlockSpec((tm, d // 2), lambda i: (i + blk0, 0)),
        pl.BlockSpec((tm, d), lambda i: (i + blk0, 0)),
        pl.BlockSpec((d, fs), const),
        pl.BlockSpec((d, fs), const),
        pl.BlockSpec((fs, d), const),
        pl.BlockSpec((1, d), const),
        pl.BlockSpec((1, 1, d), lambda i: ((i + blk0) // tps, 0, 0)),
    ]
    args = [yk, gates_t, h2, x1, lw["ws1"], lw["ws3"], lw["ws2"], lw["g_post_ffn"], gt2]
    aliases = {}
    if out_prev is not None:
        in_specs.append(pl.BlockSpec(memory_space=pl.ANY))
        aliases = {len(args): 0}
        args.append(out_prev)
    return pl.pallas_call(
        _combine_kernel,
        grid=(nc // tm,),
        in_specs=in_specs,
        out_specs=pl.BlockSpec((tm, d), lambda i: (i + out_blk0, 0)),
        out_shape=jax.ShapeDtypeStruct((out_rows, d), F32),
        input_output_aliases=aliases,
        compiler_params=_params(("parallel",)),
        name="combine",
    )(*args)


EXPERT_ROWS = 1024
PROMPT_CHAINS = 2


def _route_rows(logits, lw, n):
    n_exp = logits.shape[0]
    per = n_exp // N_GROUPS
    rt = EXPERT_ROWS
    n_tiles = n * TOP_K // rt + n_exp
    eidx, rank, gate, cnt = _router(logits, lw["br"], _tile(n, 1024), 0, n)
    counts = cnt[:, 0].astype(jnp.int32).reshape(per, N_GROUPS).T.reshape(n_exp)
    padded = (counts + rt - 1) // rt * rt
    ends = jnp.cumsum(padded)
    pos = _positions(ends - padded, eidx, rank, _tile(n, 2048))
    tile_start = jnp.arange(n_tiles, dtype=jnp.int32) * rt
    tile_expert = jnp.minimum(jnp.sum(ends[None, :] <= tile_start[:, None], axis=1), n_exp - 1).astype(jnp.int32)
    n_used = (ends[-1:] // rt).astype(jnp.int32)
    return dict(gates_t=gate.T, pos=pos, tile_expert=tile_expert, n_used=n_used, rows=n_tiles * rt)


def _split_router(w_router, n_exp, d):
    wrt = w_router.T.reshape(N_GROUPS, n_exp // N_GROUPS, d).transpose(1, 0, 2).reshape(n_exp, d)
    high = wrt.astype(BF16)
    return jnp.concatenate([high, (wrt - high.astype(F32)).astype(BF16)], axis=0)


def _prep_layer(l, w):
    d = w["w_in"].shape[1]
    n_exp = w["w_e1"].shape[1]
    H = N_HEADS
    w_in = w["w_in"][l]
    qkw = d
    g0 = 3 * d
    w_main = jnp.concatenate([w_in[:, :g0], w_in[:, g0 + 2 * H:]], axis=1).astype(BF16)
    wg = jnp.pad(w_in[:, g0:g0 + 2 * H], ((0, 0), (0, LANES - 2 * H)))
    wg_high = wg.astype(BF16)
    w_gate = jnp.concatenate([wg_high, (wg - wg_high.astype(F32)).astype(BF16)], axis=1)
    gbias = jnp.pad(jnp.concatenate([w["b_igate"][l], w["b_fgate"][l]]), (0, LANES - 2 * H)).reshape(1, LANES)
    return dict(
        w_main=w_main, w_gate=w_gate, gbias=gbias,
        g_pre_mix=w["g_pre_mix"][l].reshape(1, d), g_post_mix=w["g_post_mix"][l].reshape(1, d),
        g_pre_ffn=w["g_pre_ffn"][l].reshape(1, d), g_post_ffn=w["g_post_ffn"][l].reshape(1, d),
        w_conv=w["w_conv"][l], b_conv=w["b_conv"][l].reshape(1, qkw),
        g_head=w["g_head"][l].reshape(1, d), g_vnorm=w["g_vnorm"][l].reshape(1, d),
        b_vnorm=w["b_vnorm"][l].reshape(1, d),
        w_spatial=w["w_spatial"][l], b_spatial=w["b_spatial"][l],
        w_out=w["w_out"][l].astype(BF16),
        wr=_split_router(w["w_router"][l], n_exp, d),
        br=w["b_router"][l].reshape(N_GROUPS, n_exp // N_GROUPS).T.reshape(n_exp, 1),
        w1=w["w_e1"], w3=w["w_e3"], w2=w["w_e2"], layer=l,
        ws1=w["w_s1"][l].astype(BF16), ws3=w["w_s3"][l].astype(BF16), ws2=w["w_s2"][l].astype(BF16),
    )


def _spatial_weights(lw, start, gl):
    idx = jnp.arange(GMLP_CHUNK)
    mask = (idx[None, :] // CHUNK) <= (idx[:, None] // CHUNK)
    wsp = jnp.where(mask, lw["w_spatial"], 0.0)[:, start:start + gl, start:start + gl].astype(BF16)
    bsp_t = lw["b_spatial"][:, start:start + gl].T
    return wsp, bsp_t


def _tile(n, cap):
    t = cap
    while n % t:
        t //= 2
    return t


def _tie(x, dep):
    if dep is None:
        return x
    return lax.optimization_barrier((x, dep))[0]


class _Chain:
    def __init__(self, x, mods, c0, n0, m0, conv0, layers, mlstm_l, g_start, g_l, keep_v, seq0=0, nb=None):
        b_all, t_seq, d = x.shape
        self.nb = b_all if nb is None else nb
        self.t_seq, self.d = t_seq, d
        self.n = self.nb * t_seq
        self.total_rows = b_all * t_seq
        self.out_row0 = seq0 * t_seq
        self.x, self.x_row0 = x.reshape(self.total_rows, d), seq0 * t_seq
        self.mods, self.layers = mods, layers
        self.c0, self.n0, self.m0, self.conv0 = c0, n0, m0, conv0
        self.mlstm_l, self.g_start, self.g_l, self.keep_v = mlstm_l, g_start, g_l, keep_v
        per_token = mods[0][0].shape[1] != 1
        tok_cap = self.n if per_token else t_seq
        self.tm_in, self.tm_mg, self.tm_moe = _tile(tok_cap, 1024), _tile(tok_cap, 512), _tile(tok_cap, 1024)
        self.dispatched = self.n % (SC_WORKERS * SC_BATCH) == 0 and not per_token
        self.outs = [[] for _ in range(5 if keep_v else 4)]

    def inproj(self, l, dep=None):
        lw = self.layers[l]
        sh1, sc1 = self.mods[l][0], self.mods[l][1]
        self.p, self.gates = _inproj(self.x, _tie(sc1, dep), sh1, lw["g_pre_mix"], lw["w_main"], lw["w_gate"],
                                     self.tm_in, self.t_seq, self.x_row0, self.n)
        return self.gates

    def mlstm(self, l, dep=None):
        lw = self.layers[l]
        m0 = self.m0[l]
        m0p = jnp.pad(m0, ((0, 0), (0, LANES - m0.shape[1]))).reshape(self.nb, 1, LANES)
        self.hm, c_new, n_new, m_new, conv_new = _mlstm(
            self.p, self.gates, self.c0[l], self.n0[l], m0p, self.conv0[l], lw["w_conv"], lw["b_conv"],
            _tie(lw["gbias"], dep), lw["g_head"], self.nb, self.mlstm_l)
        for lst, val in zip(self.outs, [c_new, n_new, m_new[:, 0, :N_HEADS], conv_new]):
            lst.append(val)
        return self.hm

    def merge(self, l, dep=None):
        lw = self.layers[l]
        _, _, gt1, sh2, sc2, _ = self.mods[l]
        wsp, bsp_t = _spatial_weights(lw, self.g_start, self.g_l)
        res = _merge(self.p, self.hm, self.x, wsp, bsp_t, _tie(lw["g_vnorm"], dep), lw["b_vnorm"], lw["w_out"],
                     lw["g_post_mix"], gt1, lw["g_pre_ffn"], sc2, sh2, lw["wr"], self.tm_mg, self.t_seq,
                     self.keep_v, self.x_row0)
        self.x1, self.h2, self.logits = res[0], res[1], res[2]
        if self.keep_v:
            self.outs[4].append(res[3].reshape(self.nb, self.t_seq, self.d))
        if not self.dispatched:
            return self.logits
        self.route = _route_rows(self.logits, lw, self.n)
        return self.route["pos"]

    def mixer(self, l, dep=None):
        return self.merge(l, self.mlstm(l, self.inproj(l, dep)))

    def scatter(self, l, dep=None):
        self.xs = _sc_scatter(self.h2, _tie(self.route["pos"], dep), self.route["rows"], 0)

    def experts(self, l, dep=None):
        lw, r = self.layers[l], self.route
        ys = _experts(self.xs, _tie(r["tile_expert"], dep), r["n_used"], lw["w1"], lw["w3"], lw["w2"],
                      lw["layer"], EXPERT_ROWS)
        self.yk = _sc_gather(ys, r["pos"])
        return ys

    def combine(self, l, dep=None, shared=None):
        lw, gt2 = self.layers[l], self.mods[l][5]
        rows, row0, buf = (self.n, 0, None) if shared is None else (self.total_rows, self.out_row0, shared[0])
        self.x = _combine(self.yk, _tie(self.route["gates_t"], dep), self.h2, self.x1, lw, gt2,
                          _tile(self.t_seq, 512), self.t_seq, 0, rows, row0, buf)
        self.x_row0 = 0
        return self.x

    def dense(self, l, dep=None):
        lw, gt2 = self.layers[l], self.mods[l][5]
        self.x = _moe_dense(self.h2, self.x1, _tie(self.logits, dep), lw, gt2, self.tm_moe, self.t_seq, 4)
        self.x_row0 = 0
        return self.x

    def layer(self, l, dep=None, shared=None):
        t = self.mixer(l, dep)
        if not self.dispatched:
            return self.dense(l, t)
        self.scatter(l)
        return self.combine(l, self.experts(l), shared)

    def states(self):
        return tuple(jnp.stack(lst) for lst in self.outs)


def _run_two_chains(a, b, side, depth):
    last = depth - 1
    t = a.mixer(0)
    a.scatter(0)
    side_done = 0
    for l in range(depth):
        t = b.inproj(l, t)
        t = a.experts(l, t)
        t = b.merge(l, b.mlstm(l, t))
        t = y = a.combine(l, t, (None,) if l == last else None)
        b.scatter(l, t)
        if l < last:
            t = a.inproj(l + 1, t)
            t = b.experts(l, t)
            t = a.merge(l + 1, a.mlstm(l + 1, t))
            t = b.combine(l, t)
            a.scatter(l + 1, t)
        else:
            half = (depth + 1) // 2
            for sl in range(half):
                t = side.layer(sl, t)
            t = b.experts(l, t)
            for sl in range(half, depth):
                t = side.layer(sl, t)
            y = b.combine(l, t, (y,))
    return y


def kernel(x_prompt, x_sample, c_prompt, c_sample, state_mlstm_C, state_mlstm_n, state_mlstm_m, state_conv,
           w_ada, b_ada, g_pre_mix, g_post_mix, g_pre_ffn, g_post_ffn, w_in, b_igate, b_fgate, w_conv, b_conv,
           g_head, g_vnorm, b_vnorm, w_spatial, b_spatial, w_out, w_router, b_router, w_e1, w_e3, w_e2,
           w_s1, w_s3, w_s2):
    w = dict(w_in=w_in, b_igate=b_igate, b_fgate=b_fgate, g_pre_mix=g_pre_mix, g_post_mix=g_post_mix,
             g_pre_ffn=g_pre_ffn, g_post_ffn=g_post_ffn, w_conv=w_conv, b_conv=b_conv, g_head=g_head,
             g_vnorm=g_vnorm, b_vnorm=b_vnorm, w_spatial=w_spatial, b_spatial=b_spatial, w_out=w_out,
             w_router=w_router, b_router=b_router, w_e1=w_e1, w_e3=w_e3, w_e2=w_e2, w_s1=w_s1, w_s3=w_s3,
             w_s2=w_s2)
    depth = w_in.shape[0]
    bp, tp, d = x_prompt.shape
    bs, ts, _ = x_sample.shape
    H, dv, dqk = state_mlstm_C.shape[2:]
    qkw = state_conv.shape[-1]
    layers = [_prep_layer(l, w) for l in range(depth)]

    mod = _ada(jnp.concatenate([c_prompt, c_sample], axis=0), w_ada, b_ada)
    mods_p, mods_s = [], []
    for l in range(depth):
        parts = [mod[l][:, i * d:(i + 1) * d] for i in range(6)]
        mods_p.append([a[:bp].reshape(bp, 1, d) for a in parts])
        mods_s.append([jnp.repeat(a[bp:], ts, axis=0).reshape(1, bs * ts, d) for a in parts])

    lp = 256 if tp % 256 == 0 else CHUNK
    chains = PROMPT_CHAINS if bp % PROMPT_CHAINS == 0 else 1
    bc = bp // chains
    zc = jnp.zeros((depth, bc, H, dv, dqk), F32)
    zn = jnp.zeros((depth, bc, H, dqk), F32)
    zm = jnp.zeros((depth, bc, H), F32)
    zconv = jnp.zeros((depth, bc, CONV_W - 1, qkw), F32)
    prompt = []
    for ci in range(chains):
        mods_c = [[a[ci * bc:(ci + 1) * bc] for a in layer_mods] for layer_mods in mods_p]
        prompt.append(_Chain(x_prompt, mods_c, zc, zn, zm, zconv, layers, lp, 0, GMLP_CHUNK, False,
                             seq0=ci * bc, nb=bc))
    sample = _Chain(x_sample, mods_s, state_mlstm_C, state_mlstm_n, state_mlstm_m, state_conv, layers, ts,
                    PAST_LEN % GMLP_CHUNK, ts, True)
    if chains == 2 and all(c.dispatched for c in prompt) and not sample.dispatched:
        y_p = _run_two_chains(prompt[0], prompt[1], sample, depth)
    else:
        y_p = None
        for c in prompt:
            for l in range(depth):
                shared = (y_p,) if (l == depth - 1 and c.dispatched and chains > 1) else None
                out = c.layer(l, None, shared)
            if c.dispatched and chains > 1:
                y_p = out
            else:
                y_p = out if y_p is None else jnp.concatenate([y_p, out], axis=0)
        for l in range(depth):
            sample.layer(l)
    y_p = y_p.reshape(bp, tp, d)
    y_s = sample.x.reshape(bs, ts, d)
    states_p = [c.states() for c in prompt]
    p_c, p_n, p_m, p_conv = (jnp.concatenate([s[i] for s in states_p], axis=1) for i in range(4))
    s_c, s_n, s_m, s_conv, s_v = sample.states()
    return (y_p, y_s, p_c, p_n, p_m, p_conv, s_c, s_n, s_m, s_conv, s_v)
```

```python
import functools

import jax
import jax.numpy as jnp
from jax import lax
from jax.experimental import pallas as pl
from jax.experimental.pallas import tpu as pltpu
from jax.experimental.pallas import tpu_sc as plsc

F32 = jnp.float32
BF16 = jnp.bfloat16

EPS = 1e-6
N_HEADS = 4
CONV_W = 4
GMLP_GROUPS = 8
GMLP_CHUNK = 128
CHUNK = 64
N_GROUPS = 8
TOPK_GROUPS = 4
TOP_K = 8
ROUTE_SCALE = 2.5
PAST_LEN = 4096

LANES = 128
SUBLANES = 8
VMEM_LIMIT = 56 * 1024 * 1024

NT_DIMS = (((1,), (1,)), ((), ()))
TN_DIMS = (((0,), (0,)), ((), ()))


def _sigmoid(x):
    return 0.5 * (jnp.tanh(0.5 * x) + 1.0)


def _silu(x):
    return x * _sigmoid(x)


def _gelu_tanh(x):
    return x * (0.5 * (1.0 + jnp.tanh(0.7978845608028654 * (x + 0.044715 * (x * x * x)))))


def _rms(x, g):
    return x * lax.rsqrt(jnp.mean(x * x, -1, keepdims=True) + EPS) * g


def _pack_pairs(x):
    c = x.shape[1] // 2
    return pltpu.pack_elementwise([x[:, :c], x[:, c:]], packed_dtype=BF16)


def _unpack_pairs(u):
    lo = pltpu.unpack_elementwise(u, index=0, packed_dtype=BF16, unpacked_dtype=F32)
    hi = pltpu.unpack_elementwise(u, index=1, packed_dtype=BF16, unpacked_dtype=F32)
    return jnp.concatenate([lo, hi], axis=1)


def _params(sem):
    return pltpu.CompilerParams(dimension_semantics=sem, vmem_limit_bytes=VMEM_LIMIT)


def _ada_kernel(c_ref, w_ref, b_ref, o_ref):
    a = _silu(c_ref[...]).astype(BF16)
    o_ref[0] = jnp.dot(a, w_ref[0].astype(BF16), preferred_element_type=F32) + b_ref[0]


def _ada(c, w_ada, b_ada):
    depth, d, six_d = w_ada.shape
    nb = c.shape[0]
    tn = d
    return pl.pallas_call(
        _ada_kernel,
        grid=(depth, six_d // tn),
        in_specs=[
            pl.BlockSpec((nb, d), lambda l, j: (0, 0)),
            pl.BlockSpec((1, d, tn), lambda l, j: (l, 0, j)),
            pl.BlockSpec((1, 1, tn), lambda l, j: (l, 0, j)),
        ],
        out_specs=pl.BlockSpec((1, nb, tn), lambda l, j: (l, 0, j)),
        out_shape=jax.ShapeDtypeStruct((depth, nb, six_d), F32),
        compiler_params=_params(("parallel", "parallel")),
        name="adaln",
    )(c, w_ada, b_ada.reshape(depth, 1, six_d))


def _inproj_kernel(x_ref, sc_ref, sh_ref, g_ref, w_ref, wg_ref, p_ref, gates_ref, h_scr):
    @pl.when(pl.program_id(1) == 0)
    def _():
        h = _rms(x_ref[...], g_ref[...]) * (1.0 + sc_ref[0]) + sh_ref[0]
        hb = h.astype(BF16)
        h_scr[...] = hb
        h_low = (h - hb.astype(F32)).astype(BF16)
        gg = jnp.dot(hb, wg_ref[...], preferred_element_type=F32)
        gates_ref[...] = (gg[:, :LANES] + gg[:, LANES:]
                          + jnp.dot(h_low, wg_ref[:, :LANES], preferred_element_type=F32))

    p_ref[...] = jnp.dot(h_scr[...], w_ref[...], preferred_element_type=F32).astype(BF16)


def _mod_spec(mod, tm, tiles_per_seq, ngrid):
    d = mod.shape[-1]
    if mod.shape[1] == 1:
        if ngrid == 2:
            return pl.BlockSpec((1, 1, d), lambda i, j: (i // tiles_per_seq, 0, 0))
        return pl.BlockSpec((1, 1, d), lambda i: (i // tiles_per_seq, 0, 0))
    if ngrid == 2:
        return pl.BlockSpec((1, tm, d), lambda i, j: (0, i, 0))
    return pl.BlockSpec((1, tm, d), lambda i: (0, i, 0))


def _inproj(x, sc, sh, g, w_main, w_gate, tm, t_seq, row0, n):
    d = x.shape[1]
    blk0 = row0 // tm
    width = w_main.shape[1]
    ncol = 2 if (width // 2) % LANES == 0 else width // d
    wc = width // ncol
    tps = max(t_seq // tm, 1)
    return pl.pallas_call(
        _inproj_kernel,
        grid=(n // tm, ncol),
        in_specs=[
            pl.BlockSpec((tm, d), lambda i, j: (i + blk0, 0)),
            _mod_spec(sc, tm, tps, 2),
            _mod_spec(sh, tm, tps, 2),
            pl.BlockSpec((1, d), lambda i, j: (0, 0)),
            pl.BlockSpec((d, wc), lambda i, j: (0, j)),
            pl.BlockSpec((d, 2 * LANES), lambda i, j: (0, 0)),
        ],
        out_specs=[
            pl.BlockSpec((tm, wc), lambda i, j: (i, j)),
            pl.BlockSpec((tm, LANES), lambda i, j: (i, 0)),
        ],
        out_shape=[
            jax.ShapeDtypeStruct((n, width), BF16),
            jax.ShapeDtypeStruct((n, LANES), F32),
        ],
        scratch_shapes=[pltpu.VMEM((tm, d), BF16)],
        compiler_params=_params(("parallel", "arbitrary")),
        name="inproj",
    )(x, sc, sh, g, w_main, w_gate)


def _mlstm_kernel(qk_ref, v_ref, gt_ref, c0_ref, n0_ref, m0_ref, conv0_ref, wconv_ref, bconv_ref,
                  gbias_ref, ghead_ref, hm_ref, c_ref, n_ref, m_ref, conv_ref, xbuf, *, L, H, dqk, dv):
    qkw = 2 * H * dqk
    pad = SUBLANES
    tail0 = pad - (CONV_W - 1)

    @pl.when(pl.program_id(1) == 0)
    def _():
        c_ref[...] = c0_ref[...]
        n_ref[...] = n0_ref[...]
        m_ref[...] = m0_ref[...]
        xbuf[0:pad, :] = jnp.zeros((pad, qkw), F32)
        xbuf[tail0:pad, :] = conv0_ref[0]

    xbuf[pad:pad + L, :] = qk_ref[...].astype(F32)
    w = wconv_ref[...]
    y = xbuf[tail0:tail0 + L, :] * w[0:1, :]
    for j in range(1, CONV_W):
        y = y + xbuf[tail0 + j:tail0 + j + L, :] * w[j:j + 1, :]
    y = y + bconv_ref[...]
    tail = xbuf[L + tail0:L + pad, :]
    conv_ref[0] = tail
    xbuf[tail0:pad, :] = tail
    qk = _silu(y)

    z = gt_ref[...] + gbias_ref[...]
    lf = jnp.minimum(z, 0.0) - jnp.log(1.0 + jnp.exp(-jnp.abs(z)))
    row = lax.broadcasted_iota(jnp.int32, (L, L), 0)
    col = lax.broadcasted_iota(jnp.int32, (L, L), 1)
    tri = row >= col
    hi = lax.Precision.HIGHEST
    b_col = jnp.dot(tri.astype(F32), lf, precision=hi, preferred_element_type=F32)
    eye = (lax.broadcasted_iota(jnp.int32, (LANES, LANES), 0)
           == lax.broadcasted_iota(jnp.int32, (LANES, LANES), 1)).astype(F32)
    b_row = lax.dot_general(eye, b_col, NT_DIMS, precision=hi, preferred_element_type=F32)
    z_row = lax.dot_general(eye, z, NT_DIMS, precision=hi, preferred_element_type=F32)

    scale = dqk ** -0.5
    lane1 = lax.broadcasted_iota(jnp.int32, (1, LANES), 1)
    m_old = m_ref[0]
    m_new = m_old
    for h in range(H):
        q = qk[:, h * dqk:(h + 1) * dqk] * scale
        k = qk[:, (H + h) * dqk:(H + h + 1) * dqk]
        qb = q.astype(BF16)
        vb = v_ref[:, h * dv:(h + 1) * dv]
        b_c = b_col[:, H + h:H + h + 1]
        i_c = z[:, h:h + 1]
        b_r = b_row[H + h:H + h + 1, :]
        i_r = z_row[h:h + 1, :]
        m_prev = m_old[:, h:h + 1]
        a = b_c + m_prev
        dm = jnp.where(tri, b_c - b_r + i_r, -jnp.inf)
        mt = jnp.maximum(a, jnp.max(dm, axis=-1, keepdims=True))
        s = lax.dot_general(qb, k.astype(BF16), NT_DIMS, preferred_element_type=F32)
        sg = s * jnp.exp(dm - mt)
        aw = jnp.exp(a - mt)
        cm = c_ref[0, h]
        qc = lax.dot_general(qb, cm.astype(BF16), NT_DIMS, preferred_element_type=F32)
        num = jnp.dot(sg.astype(BF16), vb, preferred_element_type=F32) + aw * qc
        nrow = n_ref[0, h:h + 1, :]
        den = jnp.sum(sg, -1, keepdims=True) + aw * jnp.sum(q * nrow, -1, keepdims=True)
        hh = num / jnp.maximum(jnp.abs(den), jnp.exp(-mt))
        m_last = mt[L - 1:L, :]
        b_last = b_c[L - 1:L, :]
        decay = jnp.exp(b_last + m_prev - m_last)
        ws = jnp.exp(b_last - b_c + i_c - m_last)
        kw = ws * k
        c_ref[0, h] = decay * cm + lax.dot_general(vb, kw.astype(BF16), TN_DIMS, preferred_element_type=F32)
        n_ref[0, h:h + 1, :] = decay * nrow + jnp.sum(kw, axis=0, keepdims=True)
        m_new = jnp.where(lane1 == h, m_last, m_new)
        hm_ref[:, h * dv:(h + 1) * dv] = _rms(hh, ghead_ref[:, h * dv:(h + 1) * dv]).astype(BF16)
    m_ref[0] = m_new


def _mlstm(p, gates, c0, n0, m0, conv0, w_conv, b_conv, gbias, g_head, nb, L):
    n = p.shape[0]
    t_seq = n // nb
    nc = t_seq // L
    _, H, dv, dqk = c0.shape
    qkw = 2 * H * dqk
    vw = H * dv
    kern = functools.partial(_mlstm_kernel, L=L, H=H, dqk=dqk, dv=dv)
    return pl.pallas_call(
        kern,
        grid=(nb, nc),
        in_specs=[
            pl.BlockSpec((L, qkw), lambda b, c: (b * nc + c, 0)),
            pl.BlockSpec((L, vw), lambda b, c: (b * nc + c, 1)),
            pl.BlockSpec((L, LANES), lambda b, c: (b * nc + c, 0)),
            pl.BlockSpec((1, H, dv, dqk), lambda b, c: (b, 0, 0, 0)),
            pl.BlockSpec((1, H, dqk), lambda b, c: (b, 0, 0)),
            pl.BlockSpec((1, 1, LANES), lambda b, c: (b, 0, 0)),
            pl.BlockSpec((1, CONV_W - 1, qkw), lambda b, c: (b, 0, 0)),
            pl.BlockSpec((CONV_W, qkw), lambda b, c: (0, 0)),
            pl.BlockSpec((1, qkw), lambda b, c: (0, 0)),
            pl.BlockSpec((1, LANES), lambda b, c: (0, 0)),
            pl.BlockSpec((1, vw), lambda b, c: (0, 0)),
        ],
        out_specs=[
            pl.BlockSpec((L, vw), lambda b, c: (b * nc + c, 0)),
            pl.BlockSpec((1, H, dv, dqk), lambda b, c: (b, 0, 0, 0)),
            pl.BlockSpec((1, H, dqk), lambda b, c: (b, 0, 0)),
            pl.BlockSpec((1, 1, LANES), lambda b, c: (b, 0, 0)),
            pl.BlockSpec((1, CONV_W - 1, qkw), lambda b, c: (b, 0, 0)),
        ],
        out_shape=[
            jax.ShapeDtypeStruct((n, vw), BF16),
            jax.ShapeDtypeStruct((nb, H, dv, dqk), F32),
            jax.ShapeDtypeStruct((nb, H, dqk), F32),
            jax.ShapeDtypeStruct((nb, 1, LANES), F32),
            jax.ShapeDtypeStruct((nb, CONV_W - 1, qkw), F32),
        ],
        scratch_shapes=[pltpu.VMEM((L + SUBLANES, qkw), F32)],
        compiler_params=_params(("parallel", "arbitrary")),
        name="mlstm",
    )(p, p, gates, c0, n0, m0, conv0, w_conv, b_conv, gbias, g_head)


def _merge_kernel(o_ref, u_ref, vg_ref, ga_ref, gb_ref, hm_ref, x_ref, wsp_ref, bsp_ref, gvn_ref, bvn_ref,
                  wout_ref, gpost_ref, gt1_ref, gpre_ref, sc2_ref, sh2_ref, wr_ref, *rest, tm, gl, keep_v):
    if keep_v:
        x1_ref, h2_ref, lg_ref, vn_ref, sg_scr = rest
    else:
        x1_ref, h2_ref, lg_ref, sg_scr = rest
    groups = wsp_ref.shape[0]
    ch = vg_ref.shape[1] // groups
    vg = _gelu_tanh(vg_ref[...].astype(F32))
    mu = jnp.mean(vg, -1, keepdims=True)
    xc = vg - mu
    vn = xc * lax.rsqrt(jnp.mean(xc * xc, -1, keepdims=True) + EPS) * gvn_ref[...] + bvn_ref[...]
    if keep_v:
        vn_ref[...] = vn
    vnb = vn.astype(BF16)
    for ci in range(tm // gl):
        for g in range(groups):
            blk = jnp.dot(wsp_ref[g], vnb[ci * gl:(ci + 1) * gl, g * ch:(g + 1) * ch],
                          preferred_element_type=F32)
            sg_scr[ci * gl:(ci + 1) * gl, g * ch:(g + 1) * ch] = blk + bsp_ref[:, g:g + 1]
    h_b = _gelu_tanh(u_ref[...].astype(F32)) * sg_scr[...]
    h_a = _sigmoid(o_ref[...].astype(F32)) * hm_ref[...].astype(F32)
    merged = _sigmoid(ga_ref[...].astype(F32)) * h_a + _sigmoid(gb_ref[...].astype(F32)) * h_b
    y = jnp.dot(merged.astype(BF16), wout_ref[...], preferred_element_type=F32)
    x1 = x_ref[...] + gt1_ref[0] * _rms(y, gpost_ref[...])
    x1_ref[...] = x1
    h2 = _rms(x1, gpre_ref[...]) * (1.0 + sc2_ref[0]) + sh2_ref[0]
    h2_ref[...] = _pack_pairs(h2)
    n_exp = lg_ref.shape[0]
    h2_high = h2.astype(BF16)
    h2_low = (h2 - h2_high.astype(F32)).astype(BF16)
    lg = lax.dot_general(wr_ref[...], h2_high, NT_DIMS, preferred_element_type=F32)
    lg_ref[...] = (lg[:n_exp] + lg[n_exp:]
                   + lax.dot_general(wr_ref[:n_exp, :], h2_low, NT_DIMS, preferred_element_type=F32))


def _merge(p, hm, x, wsp, bsp_t, g_vn, b_vn, w_out, g_post, gt1, g_pre, sc2, sh2, wr, tm, t_seq, keep_v, x_row0):
    n, d = hm.shape
    x_blk0 = x_row0 // tm
    groups, gl, _ = wsp.shape
    n_exp = wr.shape[0] // 2
    tps = max(t_seq // tm, 1)
    row = lambda i: (0, 0)
    pcol = lambda c: pl.BlockSpec((tm, d), lambda i: (i, c))
    tok = pl.BlockSpec((tm, d), lambda i: (i, 0))
    vec = pl.BlockSpec((1, d), row)
    out_specs = [tok, pl.BlockSpec((tm, d // 2), lambda i: (i, 0)), pl.BlockSpec((n_exp, tm), lambda i: (0, i))]
    out_shape = [jax.ShapeDtypeStruct((n, d), F32), jax.ShapeDtypeStruct((n, d // 2), jnp.uint32),
                 jax.ShapeDtypeStruct((n_exp, n), F32)]
    if keep_v:
        out_specs.append(tok)
        out_shape.append(jax.ShapeDtypeStruct((n, d), F32))
    kern = functools.partial(_merge_kernel, tm=tm, gl=gl, keep_v=keep_v)
    return pl.pallas_call(
        kern,
        grid=(n // tm,),
        in_specs=[
            pcol(2), pcol(3), pcol(4), pcol(5), pcol(6), tok,
            pl.BlockSpec((tm, d), lambda i: (i + x_blk0, 0)),
            pl.BlockSpec((groups, gl, gl), lambda i: (0, 0, 0)),
            pl.BlockSpec((gl, groups), row),
            vec, vec,
            pl.BlockSpec((d, d), row),
            vec,
            _mod_spec(gt1, tm, tps, 1),
            vec,
            _mod_spec(sc2, tm, tps, 1),
            _mod_spec(sh2, tm, tps, 1),
            pl.BlockSpec((2 * n_exp, d), row),
        ],
        out_specs=out_specs,
        out_shape=out_shape,
        scratch_shapes=[pltpu.VMEM((tm, d), F32)],
        compiler_params=_params(("parallel",)),
        name="merge",
    )(p, p, p, p, p, hm, x, wsp, bsp_t, g_vn, b_vn, w_out, g_post, gt1, g_pre, sc2, sh2, wr)


def _select(logits, br_ref):
    n_exp, tm = logits.shape
    per = n_exp // N_GROUPS
    neg = -jnp.inf
    s = _sigmoid(logits)
    sel = s + br_ref[...]
    mem = [sel[r * N_GROUPS:(r + 1) * N_GROUPS, :] for r in range(per)]
    grow = lax.broadcasted_iota(jnp.int32, (N_GROUPS, tm), 0)
    m1 = functools.reduce(jnp.maximum, mem)
    idx1 = functools.reduce(jnp.minimum, [jnp.where(mem[r] == m1, r, per) for r in range(per)])
    m2 = functools.reduce(jnp.maximum, [jnp.where(idx1 == r, neg, mem[r]) for r in range(per)])
    gs = m1 + m2
    gmask = jnp.zeros((N_GROUPS, tm), jnp.bool_)
    for _ in range(TOPK_GROUPS):
        mx = jnp.max(gs, axis=0, keepdims=True)
        gi = jnp.min(jnp.where(gs == mx, grow, N_GROUPS), axis=0, keepdims=True)
        pick = grow == gi
        gmask = jnp.logical_or(gmask, pick)
        gs = jnp.where(pick, neg, gs)
    msk = [jnp.where(gmask, mem[r], neg) for r in range(per)]
    eidx = [grow * per + r for r in range(per)]
    chosen = [jnp.zeros((N_GROUPS, tm), jnp.bool_) for _ in range(per)]
    firsts = []
    for _ in range(TOP_K):
        mx = jnp.max(functools.reduce(jnp.maximum, msk), axis=0, keepdims=True)
        cand = functools.reduce(jnp.minimum, [jnp.where(msk[r] == mx, eidx[r], n_exp) for r in range(per)])
        first = jnp.min(cand, axis=0, keepdims=True)
        firsts.append(first)
        for r in range(per):
            pick = eidx[r] == first
            chosen[r] = jnp.logical_or(chosen[r], pick)
            msk[r] = jnp.where(pick, neg, msk[r])
    wk = [jnp.where(chosen[r], s[r * N_GROUPS:(r + 1) * N_GROUPS, :], 0.0) for r in range(per)]
    denom = jnp.sum(functools.reduce(jnp.add, wk), axis=0, keepdims=True)
    return s, chosen, firsts, eidx, denom


def _shared_expert(hb, ws1_ref, ws3_ref, ws2_ref):
    a = jnp.dot(hb, ws1_ref[...], preferred_element_type=F32)
    g = jnp.dot(hb, ws3_ref[...], preferred_element_type=F32)
    return jnp.dot((_silu(a) * g).astype(BF16), ws2_ref[...], preferred_element_type=F32)


def _moe_kernel(h_ref, x1_ref, lg_ref, br_ref, w1_ref, w3_ref, w2_ref, ws1_ref, ws3_ref, ws2_ref, gpost_ref,
                gt2_ref, out_ref, gt_scr, gates_scr, acc_scr, *, eb):
    j = pl.program_id(1)
    hb = _unpack_pairs(h_ref[...]).astype(BF16)
    n_exp = lg_ref.shape[0]
    per = n_exp // N_GROUPS

    @pl.when(j == 0)
    def _():
        s, chosen, _, _, denom = _select(lg_ref[...], br_ref)
        gt_scr[n_exp:, :] = jnp.zeros((gt_scr.shape[0] - n_exp, hb.shape[0]), F32)
        for r in range(per):
            wk = jnp.where(chosen[r], s[r * N_GROUPS:(r + 1) * N_GROUPS, :], 0.0)
            gt_scr[r * N_GROUPS:(r + 1) * N_GROUPS, :] = wk / denom * ROUTE_SCALE
        gates_scr[...] = gt_scr[...].T
        acc_scr[...] = _shared_expert(hb, ws1_ref, ws3_ref, ws2_ref)

    first = j * eb
    lane0 = lax.rem(first, per) * N_GROUPS + first // per
    gates = pltpu.roll(gates_scr[...], lax.rem(LANES - lane0, LANES), 1)
    acc = acc_scr[...]
    for e in range(eb):
        a = jnp.dot(hb, w1_ref[e].astype(BF16), preferred_element_type=F32)
        g = jnp.dot(hb, w3_ref[e].astype(BF16), preferred_element_type=F32)
        gate = gates[:, e * N_GROUPS:e * N_GROUPS + 1]
        acc = acc + jnp.dot((_silu(a) * g * gate).astype(BF16), w2_ref[e].astype(BF16),
                            preferred_element_type=F32)
    acc_scr[...] = acc

    @pl.when(j == pl.num_programs(1) - 1)
    def _():
        out_ref[...] = x1_ref[...] + gt2_ref[0] * _rms(acc_scr[...], gpost_ref[...])


def _moe_dense(h2, x1, logits, lw, gt2, tm, t_seq, eb):
    n, d = x1.shape
    _, n_exp, _, f = lw["w1"].shape
    layer = lw["layer"]
    fs = lw["ws1"].shape[1]
    tps = max(t_seq // tm, 1)
    const = lambda i, j: (0, 0)
    return pl.pallas_call(
        functools.partial(_moe_kernel, eb=eb),
        grid=(n // tm, n_exp // eb),
        in_specs=[
            pl.BlockSpec((tm, d // 2), lambda i, j: (i, 0)),
            pl.BlockSpec((tm, d), lambda i, j: (i, 0)),
            pl.BlockSpec((n_exp, tm), lambda i, j: (0, i)),
            pl.BlockSpec((n_exp, 1), const),
            pl.BlockSpec((None, eb, d, f), lambda i, j: (layer, j, 0, 0)),
            pl.BlockSpec((None, eb, d, f), lambda i, j: (layer, j, 0, 0)),
            pl.BlockSpec((None, eb, f, d), lambda i, j: (layer, j, 0, 0)),
            pl.BlockSpec((d, fs), const),
            pl.BlockSpec((d, fs), const),
            pl.BlockSpec((fs, d), const),
            pl.BlockSpec((1, d), const),
            _mod_spec(gt2, tm, tps, 2),
        ],
        out_specs=pl.BlockSpec((tm, d), lambda i, j: (i, 0)),
        out_shape=jax.ShapeDtypeStruct((n, d), F32),
        scratch_shapes=[
            pltpu.VMEM((LANES, tm), F32),
            pltpu.VMEM((tm, LANES), F32),
            pltpu.VMEM((tm, d), F32),
        ],
        compiler_params=_params(("parallel", "arbitrary")),
        name="moe_dense",
    )(h2, x1, logits, lw["br"], lw["w1"], lw["w3"], lw["w2"], lw["ws1"], lw["ws3"], lw["ws2"],
      lw["g_post_ffn"], gt2)


def _router_kernel(lg_ref, br_ref, eidx_ref, rank_ref, gate_ref, cnt_ref, carry_scr):
    n_exp, tm = lg_ref.shape
    per = n_exp // N_GROUPS

    @pl.when(pl.program_id(0) == 0)
    def _():
        carry_scr[...] = jnp.zeros(carry_scr.shape, F32)

    s, chosen, firsts, eidx, denom = _select(lg_ref[...], br_ref)
    sel01 = jnp.concatenate([c.astype(F32) for c in chosen], axis=0)
    before = (lax.broadcasted_iota(jnp.int32, (tm, tm), 0)
              < lax.broadcasted_iota(jnp.int32, (tm, tm), 1)).astype(BF16)
    rank = jnp.dot(sel01.astype(BF16), before, preferred_element_type=F32) + carry_scr[:, 0:1]
    carry_scr[...] = carry_scr[...] + jnp.sum(sel01, axis=1, keepdims=True)
    cnt_ref[...] = carry_scr[...]
    for k in range(TOP_K):
        s_k = jnp.zeros((1, tm), F32)
        r_k = jnp.zeros((1, tm), F32)
        for r in range(per):
            pick = eidx[r] == firsts[k]
            rows = slice(r * N_GROUPS, (r + 1) * N_GROUPS)
            s_k = s_k + jnp.sum(jnp.where(pick, s[rows, :], 0.0), axis=0, keepdims=True)
            r_k = r_k + jnp.sum(jnp.where(pick, rank[rows, :], 0.0), axis=0, keepdims=True)
        eidx_ref[k:k + 1, :] = firsts[k]
        rank_ref[k:k + 1, :] = r_k.astype(jnp.int32)
        gate_ref[k:k + 1, :] = s_k / denom * ROUTE_SCALE


def _router(logits, br, tm, row0, n):
    n_exp = logits.shape[0]
    tok = pl.BlockSpec((TOP_K, tm), lambda i: (0, i))
    blk0 = row0 // tm
    return pl.pallas_call(
        _router_kernel,
        grid=(n // tm,),
        in_specs=[
            pl.BlockSpec((n_exp, tm), lambda i: (0, i + blk0)),
            pl.BlockSpec((n_exp, 1), lambda i: (0, 0)),
        ],
        out_specs=[tok, tok, tok, pl.BlockSpec((n_exp, LANES), lambda i: (0, 0))],
        out_shape=[
            jax.ShapeDtypeStruct((TOP_K, n), jnp.int32),
            jax.ShapeDtypeStruct((TOP_K, n), jnp.int32),
            jax.ShapeDtypeStruct((TOP_K, n), F32),
            jax.ShapeDtypeStruct((n_exp, LANES), F32),
        ],
        scratch_shapes=[pltpu.VMEM((n_exp, LANES), F32)],
        compiler_params=_params(("arbitrary",)),
        name="router",
    )(logits, br)


def _pos_kernel(starts_ref, eidx_ref, rank_ref, pos_ref):
    e = eidx_ref[...]
    base = jnp.zeros_like(e)
    for x in range(starts_ref.shape[0]):
        base = base + jnp.where(e == x, starts_ref[x], 0)
    pos = base + rank_ref[...]
    for c in range(pos_ref.shape[0]):
        pos_ref[c] = pos[:, c * SC_BATCH:(c + 1) * SC_BATCH]


def _positions(starts, eidx, rank, tm):
    k, n = eidx.shape
    grid_spec = pltpu.PrefetchScalarGridSpec(
        num_scalar_prefetch=1,
        grid=(n // tm,),
        in_specs=[pl.BlockSpec((k, tm), lambda i, s: (0, i)), pl.BlockSpec((k, tm), lambda i, s: (0, i))],
        out_specs=pl.BlockSpec((tm // SC_BATCH, k, SC_BATCH), lambda i, s: (i, 0, 0)),
    )
    return pl.pallas_call(
        _pos_kernel,
        grid_spec=grid_spec,
        out_shape=jax.ShapeDtypeStruct((n // SC_BATCH, k, SC_BATCH), jnp.int32),
        compiler_params=_params(("parallel",)),
        name="positions",
    )(starts, eidx, rank)


SC_BATCH = 128
SC_WORKERS = 32


def _sc_mesh():
    return plsc.VectorSubcoreMesh(core_axis_name="c", subcore_axis_name="s")


def _sc_scatter(x, pos, n_rows, row0):
    w = x.shape[1]
    n = pos.shape[0] * SC_BATCH
    info = plsc.get_sparse_core_info()
    nc, nw = info.num_cores, info.num_cores * info.num_subcores
    steps = n // (nw * SC_BATCH)

    @functools.partial(
        pl.kernel, mesh=_sc_mesh(),
        out_type=jax.ShapeDtypeStruct((n_rows, w), x.dtype),
        scratch_types=[pltpu.VMEM((TOP_K, SC_BATCH), jnp.int32), pltpu.VMEM((SC_BATCH, w), x.dtype),
                       pltpu.SemaphoreType.DMA],
    )
    def scatter(x_hbm, pos_hbm, out_hbm, idx_v, rows_v, sem):
        wid = lax.axis_index("s") * nc + lax.axis_index("c")

        @pl.loop(0, steps)
        def _(s):
            blk = wid * steps + s
            pltpu.sync_copy(pos_hbm.at[blk], idx_v)
            pltpu.sync_copy(x_hbm.at[pl.ds(row0 + blk * SC_BATCH, SC_BATCH)], rows_v)
            copies = [pltpu.async_copy(rows_v, out_hbm.at[idx_v.at[k]], sem) for k in range(TOP_K)]
            for c in copies:
                c.wait()

    return scatter(x, pos)


def _sc_gather(y, pos):
    w = y.shape[1]
    n = pos.shape[0] * SC_BATCH
    info = plsc.get_sparse_core_info()
    nc, nw = info.num_cores, info.num_cores * info.num_subcores
    steps = n // (nw * SC_BATCH)

    @functools.partial(
        pl.kernel, mesh=_sc_mesh(),
        out_type=jax.ShapeDtypeStruct((TOP_K, n, w), y.dtype),
        scratch_types=[pltpu.VMEM((TOP_K, SC_BATCH), jnp.int32), pltpu.VMEM((SC_BATCH, w), y.dtype),
                       pltpu.SemaphoreType.DMA],
    )
    def gather(y_hbm, pos_hbm, out_hbm, idx_v, rows_v, sem):
        wid = lax.axis_index("s") * nc + lax.axis_index("c")

        @pl.loop(0, steps)
        def _(s):
            blk = wid * steps + s
            pltpu.sync_copy(pos_hbm.at[blk], idx_v)
            for k in range(TOP_K):
                pltpu.async_copy(y_hbm.at[idx_v.at[k]], rows_v, sem).wait()
                pltpu.sync_copy(rows_v, out_hbm.at[k, pl.ds(blk * SC_BATCH, SC_BATCH)])

    return gather(y, pos)


def _experts_kernel(te_ref, nu_ref, xs_ref, w1_ref, w3_ref, w2_ref, ys_ref, w1_scr, w3_scr, w2_scr):
    i = pl.program_id(0)

    @pl.when(jnp.logical_or(i == 0, te_ref[i] != te_ref[jnp.maximum(i - 1, 0)]))
    def _():
        w1_scr[...] = w1_ref[...].astype(BF16)
        w3_scr[...] = w3_ref[...].astype(BF16)
        w2_scr[...] = w2_ref[...].astype(BF16)

    @pl.when(i < nu_ref[0])
    def _():
        x = _unpack_pairs(xs_ref[...]).astype(BF16)
        a = jnp.dot(x, w1_scr[...], preferred_element_type=F32)
        g = jnp.dot(x, w3_scr[...], preferred_element_type=F32)
        y = jnp.dot((_silu(a) * g).astype(BF16), w2_scr[...], preferred_element_type=F32)
        ys_ref[...] = _pack_pairs(y)


def _experts(xs, tile_expert, n_used, w1, w3, w2, layer, rt):
    rows, half = xs.shape
    _, n_exp, d, f = w1.shape
    scratch = [pltpu.VMEM((d, f), BF16), pltpu.VMEM((d, f), BF16), pltpu.VMEM((f, d), BF16)]
    grid_spec = pltpu.PrefetchScalarGridSpec(
        num_scalar_prefetch=2,
        grid=(rows // rt,),
        in_specs=[
            pl.BlockSpec((rt, half), lambda i, te, nu: (jnp.minimum(i, nu[0] - 1), 0)),
            pl.BlockSpec((None, None, d, f), lambda i, te, nu: (layer, te[jnp.minimum(i, nu[0] - 1)], 0, 0)),
            pl.BlockSpec((None, None, d, f), lambda i, te, nu: (layer, te[jnp.minimum(i, nu[0] - 1)], 0, 0)),
            pl.BlockSpec((None, None, f, d), lambda i, te, nu: (layer, te[jnp.minimum(i, nu[0] - 1)], 0, 0)),
        ],
        out_specs=pl.BlockSpec((rt, half), lambda i, te, nu: (jnp.minimum(i, nu[0] - 1), 0)),
        scratch_shapes=scratch,
    )
    return pl.pallas_call(
        _experts_kernel,
        grid_spec=grid_spec,
        out_shape=jax.ShapeDtypeStruct((rows, half), jnp.uint32),
        compiler_params=_params(("arbitrary",)),
        name="experts",
    )(tile_expert, n_used, xs, w1, w3, w2)


def _combine_kernel(yk_ref, g_ref, h_ref, x1_ref, ws1_ref, ws3_ref, ws2_ref, gpost_ref, gt2_ref, *rest):
    out_ref = rest[-1]
    hb = _unpack_pairs(h_ref[...]).astype(BF16)
    acc = _shared_expert(hb, ws1_ref, ws3_ref, ws2_ref)
    gates = g_ref[...]
    for k in range(TOP_K):
        acc = acc + gates[:, k:k + 1] * _unpack_pairs(yk_ref[k])
    out_ref[...] = x1_ref[...] + gt2_ref[0] * _rms(acc, gpost_ref[...])


def _combine(yk, gates_t, h2, x1, lw, gt2, tm, t_seq, row0, out_rows, out_row0, out_prev):
    d = x1.shape[1]
    nc = yk.shape[1]
    fs = lw["ws1"].shape[1]
    tps = max(t_seq // tm, 1)
    blk0 = row0 // tm
    out_blk0 = (out_row0 + row0) // tm
    const = lambda i: (0, 0)
    in_specs = [
        pl.BlockSpec((TOP_K, tm, d // 2), lambda i: (0, i, 0)),
        pl.BlockSpec((tm, TOP_K), lambda i: (i, 0)),
        pl.BlockSpec((tm, d // 2), lambda i: (i + blk0, 0)),
        pl.BlockSpec((tm, d), lambda i: (i + blk0, 0)),
        pl.BlockSpec((d, fs), const),
        pl.BlockSpec((d, fs), const),
        pl.BlockSpec((fs, d), const),
        pl.BlockSpec((1, d), const),
        pl.BlockSpec((1, 1, d), lambda i: ((i + blk0) // tps, 0, 0)),
    ]
    args = [yk, gates_t, h2, x1, lw["ws1"], lw["ws3"], lw["ws2"], lw["g_post_ffn"], gt2]
    aliases = {}
    if out_prev is not None:
        in_specs.append(pl.BlockSpec(memory_space=pl.ANY))
        aliases = {len(args): 0}
        args.append(out_prev)
    return pl.pallas_call(
        _combine_kernel,
        grid=(nc // tm,),
        in_specs=in_specs,
        out_specs=pl.BlockSpec((tm, d), lambda i: (i + out_blk0, 0)),
        out_shape=jax.ShapeDtypeStruct((out_rows, d), F32),
        input_output_aliases=aliases,
        compiler_params=_params(("parallel",)),
        name="combine",
    )(*args)


EXPERT_ROWS = 1024
PROMPT_CHAINS = 2


def _route_rows(logits, lw, n):
    n_exp = logits.shape[0]
    per = n_exp // N_GROUPS
    rt = EXPERT_ROWS
    n_tiles = n * TOP_K // rt + n_exp
    eidx, rank, gate, cnt = _router(logits, lw["br"], _tile(n, 1024), 0, n)
    counts = cnt[:, 0].astype(jnp.int32).reshape(per, N_GROUPS).T.reshape(n_exp)
    padded = (counts + rt - 1) // rt * rt
    ends = jnp.cumsum(padded)
    pos = _positions(ends - padded, eidx, rank, _tile(n, 2048))
    tile_start = jnp.arange(n_tiles, dtype=jnp.int32) * rt
    tile_expert = jnp.minimum(jnp.sum(ends[None, :] <= tile_start[:, None], axis=1), n_exp - 1).astype(jnp.int32)
    n_used = (ends[-1:] // rt).astype(jnp.int32)
    return dict(gates_t=gate.T, pos=pos, tile_expert=tile_expert, n_used=n_used, rows=n_tiles * rt)


def _split_router(w_router, n_exp, d):
    wrt = w_router.T.reshape(N_GROUPS, n_exp // N_GROUPS, d).transpose(1, 0, 2).reshape(n_exp, d)
    high = wrt.astype(BF16)
    return jnp.concatenate([high, (wrt - high.astype(F32)).astype(BF16)], axis=0)


def _prep_layer(l, w):
    d = w["w_in"].shape[1]
    n_exp = w["w_e1"].shape[1]
    H = N_HEADS
    w_in = w["w_in"][l]
    qkw = d
    g0 = 3 * d
    w_main = jnp.concatenate([w_in[:, :g0], w_in[:, g0 + 2 * H:]], axis=1).astype(BF16)
    wg = jnp.pad(w_in[:, g0:g0 + 2 * H], ((0, 0), (0, LANES - 2 * H)))
    wg_high = wg.astype(BF16)
    w_gate = jnp.concatenate([wg_high, (wg - wg_high.astype(F32)).astype(BF16)], axis=1)
    gbias = jnp.pad(jnp.concatenate([w["b_igate"][l], w["b_fgate"][l]]), (0, LANES - 2 * H)).reshape(1, LANES)
    return dict(
        w_main=w_main, w_gate=w_gate, gbias=gbias,
        g_pre_mix=w["g_pre_mix"][l].reshape(1, d), g_post_mix=w["g_post_mix"][l].reshape(1, d),
        g_pre_ffn=w["g_pre_ffn"][l].reshape(1, d), g_post_ffn=w["g_post_ffn"][l].reshape(1, d),
        w_conv=w["w_conv"][l], b_conv=w["b_conv"][l].reshape(1, qkw),
        g_head=w["g_head"][l].reshape(1, d), g_vnorm=w["g_vnorm"][l].reshape(1, d),
        b_vnorm=w["b_vnorm"][l].reshape(1, d),
        w_spatial=w["w_spatial"][l], b_spatial=w["b_spatial"][l],
        w_out=w["w_out"][l].astype(BF16),
        wr=_split_router(w["w_router"][l], n_exp, d),
        br=w["b_router"][l].reshape(N_GROUPS, n_exp // N_GROUPS).T.reshape(n_exp, 1),
        w1=w["w_e1"], w3=w["w_e3"], w2=w["w_e2"], layer=l,
        ws1=w["w_s1"][l].astype(BF16), ws3=w["w_s3"][l].astype(BF16), ws2=w["w_s2"][l].astype(BF16),
    )


def _spatial_weights(lw, start, gl):
    idx = jnp.arange(GMLP_CHUNK)
    mask = (idx[None, :] // CHUNK) <= (idx[:, None] // CHUNK)
    wsp = jnp.where(mask, lw["w_spatial"], 0.0)[:, start:start + gl, start:start + gl].astype(BF16)
    bsp_t = lw["b_spatial"][:, start:start + gl].T
    return wsp, bsp_t


def _tile(n, cap):
    t = cap
    while n % t:
        t //= 2
    return t


def _tie(x, dep):
    if dep is None:
        return x
    return lax.optimization_barrier((x, dep))[0]


class _Chain:
    def __init__(self, x, mods, c0, n0, m0, conv0, layers, mlstm_l, g_start, g_l, keep_v, seq0=0, nb=None):
        b_all, t_seq, d = x.shape
        self.nb = b_all if nb is None else nb
        self.t_seq, self.d = t_seq, d
        self.n = self.nb * t_seq
        self.total_rows = b_all * t_seq
        self.out_row0 = seq0 * t_seq
        self.x, self.x_row0 = x.reshape(self.total_rows, d), seq0 * t_seq
        self.mods, self.layers = mods, layers
        self.c0, self.n0, self.m0, self.conv0 = c0, n0, m0, conv0
        self.mlstm_l, self.g_start, self.g_l, self.keep_v = mlstm_l, g_start, g_l, keep_v
        per_token = mods[0][0].shape[1] != 1
        tok_cap = self.n if per_token else t_seq
        self.tm_in, self.tm_mg, self.tm_moe = _tile(tok_cap, 1024), _tile(tok_cap, 512), _tile(tok_cap, 1024)
        self.dispatched = self.n % (SC_WORKERS * SC_BATCH) == 0 and not per_token
        self.outs = [[] for _ in range(5 if keep_v else 4)]

    def inproj(self, l, dep=None):
        lw = self.layers[l]
        sh1, sc1 = self.mods[l][0], self.mods[l][1]
        self.p, self.gates = _inproj(self.x, _tie(sc1, dep), sh1, lw["g_pre_mix"], lw["w_main"], lw["w_gate"],
                                     self.tm_in, self.t_seq, self.x_row0, self.n)
        return self.gates

    def mlstm(self, l, dep=None):
        lw = self.layers[l]
        m0 = self.m0[l]
        m0p = jnp.pad(m0, ((0, 0), (0, LANES - m0.shape[1]))).reshape(self.nb, 1, LANES)
        self.hm, c_new, n_new, m_new, conv_new = _mlstm(
            self.p, self.gates, self.c0[l], self.n0[l], m0p, self.conv0[l], lw["w_conv"], lw["b_conv"],
            _tie(lw["gbias"], dep), lw["g_head"], self.nb, self.mlstm_l)
        for lst, val in zip(self.outs, [c_new, n_new, m_new[:, 0, :N_HEADS], conv_new]):
            lst.append(val)
        return self.hm

    def merge(self, l, dep=None):
        lw = self.layers[l]
        _, _, gt1, sh2, sc2, _ = self.mods[l]
        wsp, bsp_t = _spatial_weights(lw, self.g_start, self.g_l)
        res = _merge(self.p, self.hm, self.x, wsp, bsp_t, _tie(lw["g_vnorm"], dep), lw["b_vnorm"], lw["w_out"],
                     lw["g_post_mix"], gt1, lw["g_pre_ffn"], sc2, sh2, lw["wr"], self.tm_mg, self.t_seq,
                     self.keep_v, self.x_row0)
        self.x1, self.h2, self.logits = res[0], res[1], res[2]
        if self.keep_v:
            self.outs[4].append(res[3].reshape(self.nb, self.t_seq, self.d))
        if not self.dispatched:
            return self.logits
        self.route = _route_rows(self.logits, lw, self.n)
        return self.route["pos"]

    def mixer(self, l, dep=None):
        return self.merge(l, self.mlstm(l, self.inproj(l, dep)))

    def scatter(self, l, dep=None):
        self.xs = _sc_scatter(self.h2, _tie(self.route["pos"], dep), self.route["rows"], 0)

    def experts(self, l, dep=None):
        lw, r = self.layers[l], self.route
        ys = _experts(self.xs, _tie(r["tile_expert"], dep), r["n_used"], lw["w1"], lw["w3"], lw["w2"],
                      lw["layer"], EXPERT_ROWS)
        self.yk = _sc_gather(ys, r["pos"])
        return ys

    def combine(self, l, dep=None, shared=None):
        lw, gt2 = self.layers[l], self.mods[l][5]
        rows, row0, buf = (self.n, 0, None) if shared is None else (self.total_rows, self.out_row0, shared[0])
        self.x = _combine(self.yk, _tie(self.route["gates_t"], dep), self.h2, self.x1, lw, gt2,
                          _tile(self.t_seq, 512), self.t_seq, 0, rows, row0, buf)
        self.x_row0 = 0
        return self.x

    def dense(self, l, dep=None):
        lw, gt2 = self.layers[l], self.mods[l][5]
        self.x = _moe_dense(self.h2, self.x1, _tie(self.logits, dep), lw, gt2, self.tm_moe, self.t_seq, 4)
        self.x_row0 = 0
        return self.x

    def layer(self, l, dep=None, shared=None):
        t = self.mixer(l, dep)
        if not self.dispatched:
            return self.dense(l, t)
        self.scatter(l)
        return self.combine(l, self.experts(l), shared)

    def states(self):
        return tuple(jnp.stack(lst) for lst in self.outs)


def _run_two_chains(a, b, side, depth):
    last = depth - 1
    t = a.mixer(0)
    a.scatter(0)
    side_done = 0
    for l in range(depth):
        t = b.inproj(l, t)
        t = a.experts(l, t)
        t = b.merge(l, b.mlstm(l, t))
        t = y = a.combine(l, t, (None,) if l == last else None)
        b.scatter(l, t)
        if l < last:
            t = a.inproj(l + 1, t)
            t = b.experts(l, t)
            t = a.merge(l + 1, a.mlstm(l + 1, t))
            t = b.combine(l, t)
            a.scatter(l + 1, t)
        else:
            half = (depth + 1) // 2
            for sl in range(half):
                t = side.layer(sl, t)
            t = b.experts(l, t)
            for sl in range(half, depth):
                t = side.layer(sl, t)
            y = b.combine(l, t, (y,))
    return y


def kernel(x_prompt, x_sample, c_prompt, c_sample, state_mlstm_C, state_mlstm_n, state_mlstm_m, state_conv,
           w_ada, b_ada, g_pre_mix, g_post_mix, g_pre_ffn, g_post_ffn, w_in, b_igate, b_fgate, w_conv, b_conv,
           g_head, g_vnorm, b_vnorm, w_spatial, b_spatial, w_out, w_router, b_router, w_e1, w_e3, w_e2,
           w_s1, w_s3, w_s2):
    w = dict(w_in=w_in, b_igate=b_igate, b_fgate=b_fgate, g_pre_mix=g_pre_mix, g_post_mix=g_post_mix,
             g_pre_ffn=g_pre_ffn, g_post_ffn=g_post_ffn, w_conv=w_conv, b_conv=b_conv, g_head=g_head,
             g_vnorm=g_vnorm, b_vnorm=b_vnorm, w_spatial=w_spatial, b_spatial=b_spatial, w_out=w_out,
             w_router=w_router, b_router=b_router, w_e1=w_e1, w_e3=w_e3, w_e2=w_e2, w_s1=w_s1, w_s3=w_s3,
             w_s2=w_s2)
    depth = w_in.shape[0]
    bp, tp, d = x_prompt.shape
    bs, ts, _ = x_sample.shape
    H, dv, dqk = state_mlstm_C.shape[2:]
    qkw = state_conv.shape[-1]
    layers = [_prep_layer(l, w) for l in range(depth)]

    mod = _ada(jnp.concatenate([c_prompt, c_sample], axis=0), w_ada, b_ada)
    mods_p, mods_s = [], []
    for l in range(depth):
        parts = [mod[l][:, i * d:(i + 1) * d] for i in range(6)]
        mods_p.append([a[:bp].reshape(bp, 1, d) for a in parts])
        mods_s.append([jnp.repeat(a[bp:], ts, axis=0).reshape(1, bs * ts, d) for a in parts])

    lp = 256 if tp % 256 == 0 else CHUNK
    chains = PROMPT_CHAINS if bp % PROMPT_CHAINS == 0 else 1
    bc = bp // chains
    zc = jnp.zeros((depth, bc, H, dv, dqk), F32)
    zn = jnp.zeros((depth, bc, H, dqk), F32)
    zm = jnp.zeros((depth, bc, H), F32)
    zconv = jnp.zeros((depth, bc, CONV_W - 1, qkw), F32)
    prompt = []
    for ci in range(chains):
        mods_c = [[a[ci * bc:(ci + 1) * bc] for a in layer_mods] for layer_mods in mods_p]
        prompt.append(_Chain(x_prompt, mods_c, zc, zn, zm, zconv, layers, lp, 0, GMLP_CHUNK, False,
                             seq0=ci * bc, nb=bc))
    sample = _Chain(x_sample, mods_s, state_mlstm_C, state_mlstm_n, state_mlstm_m, state_conv, layers, ts,
                    PAST_LEN % GMLP_CHUNK, ts, True)
    if chains == 2 and all(c.dispatched for c in prompt) and not sample.dispatched:
        y_p = _run_two_chains(prompt[0], prompt[1], sample, depth)
    else:
        y_p = None
        for c in prompt:
            for l in range(depth):
                shared = (y_p,) if (l == depth - 1 and c.dispatched and chains > 1) else None
                out = c.layer(l, None, shared)
            if c.dispatched and chains > 1:
                y_p = out
            else:
                y_p = out if y_p is None else jnp.concatenate([y_p, out], axis=0)
        for l in range(depth):
            sample.layer(l)
    y_p = y_p.reshape(bp, tp, d)
    y_s = sample.x.reshape(bs, ts, d)
    states_p = [c.states() for c in prompt]
    p_c, p_n, p_m, p_conv = (jnp.concatenate([s[i] for s in states_p], axis=1) for i in range(4))
    s_c, s_n, s_m, s_conv, s_v = sample.states()
    return (y_p, y_s, p_c, p_n, p_m, p_conv, s_c, s_n, s_m, s_conv, s_v)
```

```python
import functools

import jax
import jax.numpy as jnp
from jax import lax
from jax.experimental import pallas as pl
from jax.experimental.pallas import tpu as pltpu
from jax.experimental.pallas import tpu_sc as plsc

F32 = jnp.float32
BF16 = jnp.bfloat16

EPS = 1e-6
N_HEADS = 4
CONV_W = 4
GMLP_GROUPS = 8
GMLP_CHUNK = 128
CHUNK = 64
N_GROUPS = 8
TOPK_GROUPS = 4
TOP_K = 8
ROUTE_SCALE = 2.5
PAST_LEN = 4096

LANES = 128
SUBLANES = 8
VMEM_LIMIT = 56 * 1024 * 1024

NT_DIMS = (((1,), (1,)), ((), ()))
TN_DIMS = (((0,), (0,)), ((), ()))


def _sigmoid(x):
    return 0.5 * (jnp.tanh(0.5 * x) + 1.0)


def _silu(x):
    return x * _sigmoid(x)


def _gelu_tanh(x):
    return x * (0.5 * (1.0 + jnp.tanh(0.7978845608028654 * (x + 0.044715 * (x * x * x)))))


def _rms(x, g):
    return x * lax.rsqrt(jnp.mean(x * x, -1, keepdims=True) + EPS) * g


def _pack_pairs(x):
    c = x.shape[1] // 2
    return pltpu.pack_elementwise([x[:, :c], x[:, c:]], packed_dtype=BF16)


def _unpack_pairs(u):
    lo = pltpu.unpack_elementwise(u, index=0, packed_dtype=BF16, unpacked_dtype=F32)
    hi = pltpu.unpack_elementwise(u, index=1, packed_dtype=BF16, unpacked_dtype=F32)
    return jnp.concatenate([lo, hi], axis=1)


def _params(sem):
    return pltpu.CompilerParams(dimension_semantics=sem, vmem_limit_bytes=VMEM_LIMIT)


def _ada_kernel(c_ref, w_ref, b_ref, o_ref):
    a = _silu(c_ref[...]).astype(BF16)
    o_ref[0] = jnp.dot(a, w_ref[0].astype(BF16), preferred_element_type=F32) + b_ref[0]


def _ada(c, w_ada, b_ada):
    depth, d, six_d = w_ada.shape
    nb = c.shape[0]
    tn = d
    return pl.pallas_call(
        _ada_kernel,
        grid=(depth, six_d // tn),
        in_specs=[
            pl.BlockSpec((nb, d), lambda l, j: (0, 0)),
            pl.BlockSpec((1, d, tn), lambda l, j: (l, 0, j)),
            pl.BlockSpec((1, 1, tn), lambda l, j: (l, 0, j)),
        ],
        out_specs=pl.BlockSpec((1, nb, tn), lambda l, j: (l, 0, j)),
        out_shape=jax.ShapeDtypeStruct((depth, nb, six_d), F32),
        compiler_params=_params(("parallel", "parallel")),
        name="adaln",
    )(c, w_ada, b_ada.reshape(depth, 1, six_d))


def _inproj_kernel(x_ref, sc_ref, sh_ref, g_ref, w_ref, wg_ref, p_ref, gates_ref, h_scr):
    @pl.when(pl.program_id(1) == 0)
    def _():
        h = _rms(x_ref[...], g_ref[...]) * (1.0 + sc_ref[0]) + sh_ref[0]
        hb = h.astype(BF16)
        h_scr[...] = hb
        h_low = (h - hb.astype(F32)).astype(BF16)
        gg = jnp.dot(hb, wg_ref[...], preferred_element_type=F32)
        gates_ref[...] = (gg[:, :LANES] + gg[:, LANES:]
                          + jnp.dot(h_low, wg_ref[:, :LANES], preferred_element_type=F32))

    p_ref[...] = jnp.dot(h_scr[...], w_ref[...], preferred_element_type=F32).astype(BF16)


def _mod_spec(mod, tm, tiles_per_seq, ngrid):
    d = mod.shape[-1]
    if mod.shape[1] == 1:
        if ngrid == 2:
            return pl.BlockSpec((1, 1, d), lambda i, j: (i // tiles_per_seq, 0, 0))
        return pl.BlockSpec((1, 1, d), lambda i: (i // tiles_per_seq, 0, 0))
    if ngrid == 2:
        return pl.BlockSpec((1, tm, d), lambda i, j: (0, i, 0))
    return pl.BlockSpec((1, tm, d), lambda i: (0, i, 0))


def _inproj(x, sc, sh, g, w_main, w_gate, tm, t_seq, row0, n):
    d = x.shape[1]
    blk0 = row0 // tm
    width = w_main.shape[1]
    ncol = 2 if (width // 2) % LANES == 0 else width // d
    wc = width // ncol
    tps = max(t_seq // tm, 1)
    return pl.pallas_call(
        _inproj_kernel,
        grid=(n // tm, ncol),
        in_specs=[
            pl.BlockSpec((tm, d), lambda i, j: (i + blk0, 0)),
            _mod_spec(sc, tm, tps, 2),
            _mod_spec(sh, tm, tps, 2),
            pl.BlockSpec((1, d), lambda i, j: (0, 0)),
            pl.BlockSpec((d, wc), lambda i, j: (0, j)),
            pl.BlockSpec((d, 2 * LANES), lambda i, j: (0, 0)),
        ],
        out_specs=[
            pl.BlockSpec((tm, wc), lambda i, j: (i, j)),
            pl.BlockSpec((tm, LANES), lambda i, j: (i, 0)),
        ],
        out_shape=[
            jax.ShapeDtypeStruct((n, width), BF16),
            jax.ShapeDtypeStruct((n, LANES), F32),
        ],
        scratch_shapes=[pltpu.VMEM((tm, d), BF16)],
        compiler_params=_params(("parallel", "arbitrary")),
        name="inproj",
    )(x, sc, sh, g, w_main, w_gate)


def _mlstm_kernel(qk_ref, v_ref, gt_ref, c0_ref, n0_ref, m0_ref, conv0_ref, wconv_ref, bconv_ref,
                  gbias_ref, ghead_ref, hm_ref, c_ref, n_ref, m_ref, conv_ref, xbuf, *, L, H, dqk, dv):
    qkw = 2 * H * dqk
    pad = SUBLANES
    tail0 = pad - (CONV_W - 1)

    @pl.when(pl.program_id(1) == 0)
    def _():
        c_ref[...] = c0_ref[...]
        n_ref[...] = n0_ref[...]
        m_ref[...] = m0_ref[...]
        xbuf[0:pad, :] = jnp.zeros((pad, qkw), F32)
        xbuf[tail0:pad, :] = conv0_ref[0]

    w = wconv_ref[...]
    row = lax.broadcasted_iota(jnp.int32, (L, L), 0)
    col = lax.broadcasted_iota(jnp.int32, (L, L), 1)
    tri = row >= col
    hi = lax.Precision.HIGHEST
    eye = (lax.broadcasted_iota(jnp.int32, (LANES, LANES), 0)
           == lax.broadcasted_iota(jnp.int32, (LANES, LANES), 1)).astype(F32)
    scale = dqk ** -0.5
    lane1 = lax.broadcasted_iota(jnp.int32, (1, LANES), 1)

    for r0 in range(0, qk_ref.shape[0], L):
        rows = slice(r0, r0 + L)
        xbuf[pad:pad + L, :] = qk_ref[rows, :].astype(F32)
        y = xbuf[tail0:tail0 + L, :] * w[0:1, :]
        for j in range(1, CONV_W):
            y = y + xbuf[tail0 + j:tail0 + j + L, :] * w[j:j + 1, :]
        y = y + bconv_ref[...]
        tail = xbuf[L + tail0:L + pad, :]
        conv_ref[0] = tail
        xbuf[tail0:pad, :] = tail
        qk = _silu(y)

        z = gt_ref[rows, :] + gbias_ref[...]
        lf = jnp.minimum(z, 0.0) - jnp.log(1.0 + jnp.exp(-jnp.abs(z)))
        b_col = jnp.dot(tri.astype(F32), lf, precision=hi, preferred_element_type=F32)
        b_row = lax.dot_general(eye, b_col, NT_DIMS, precision=hi, preferred_element_type=F32)
        z_row = lax.dot_general(eye, z, NT_DIMS, precision=hi, preferred_element_type=F32)

        m_old = m_ref[0]
        m_new = m_old
        for h in range(H):
            q = qk[:, h * dqk:(h + 1) * dqk] * scale
            k = qk[:, (H + h) * dqk:(H + h + 1) * dqk]
            qb = q.astype(BF16)
            vb = v_ref[rows, h * dv:(h + 1) * dv]
            b_c = b_col[:, H + h:H + h + 1]
            i_c = z[:, h:h + 1]
            b_r = b_row[H + h:H + h + 1, :]
            i_r = z_row[h:h + 1, :]
            m_prev = m_old[:, h:h + 1]
            a = b_c + m_prev
            dm = jnp.where(tri, b_c - b_r + i_r, -jnp.inf)
            mt = jnp.maximum(a, jnp.max(dm, axis=-1, keepdims=True))
            s = lax.dot_general(qb, k.astype(BF16), NT_DIMS, preferred_element_type=F32)
            sg = s * jnp.exp(dm - mt)
            aw = jnp.exp(a - mt)
            cm = c_ref[0, h]
            qc = lax.dot_general(qb, cm.astype(BF16), NT_DIMS, preferred_element_type=F32)
            num = jnp.dot(sg.astype(BF16), vb, preferred_element_type=F32) + aw * qc
            nrow = n_ref[0, h:h + 1, :]
            den = jnp.sum(sg, -1, keepdims=True) + aw * jnp.sum(q * nrow, -1, keepdims=True)
            hh = num / jnp.maximum(jnp.abs(den), jnp.exp(-mt))
            m_last = mt[L - 1:L, :]
            b_last = b_c[L - 1:L, :]
            decay = jnp.exp(b_last + m_prev - m_last)
            ws = jnp.exp(b_last - b_c + i_c - m_last)
            kw = ws * k
            c_ref[0, h] = decay * cm + lax.dot_general(vb, kw.astype(BF16), TN_DIMS,
                                                       preferred_element_type=F32)
            n_ref[0, h:h + 1, :] = decay * nrow + jnp.sum(kw, axis=0, keepdims=True)
            m_new = jnp.where(lane1 == h, m_last, m_new)
            hm_ref[rows, h * dv:(h + 1) * dv] = _rms(hh, ghead_ref[:, h * dv:(h + 1) * dv]).astype(BF16)
        m_ref[0] = m_new


def _mlstm(p, gates, c0, n0, m0, conv0, w_conv, b_conv, gbias, g_head, nb, L):
    n = p.shape[0]
    t_seq = n // nb
    rows = 2 * L if t_seq % (2 * L) == 0 else L
    nc = t_seq // rows
    _, H, dv, dqk = c0.shape
    qkw = 2 * H * dqk
    vw = H * dv
    kern = functools.partial(_mlstm_kernel, L=L, H=H, dqk=dqk, dv=dv)
    return pl.pallas_call(
        kern,
        grid=(nb, nc),
        in_specs=[
            pl.BlockSpec((rows, qkw), lambda b, c: (b * nc + c, 0)),
            pl.BlockSpec((rows, vw), lambda b, c: (b * nc + c, 1)),
            pl.BlockSpec((rows, LANES), lambda b, c: (b * nc + c, 0)),
            pl.BlockSpec((1, H, dv, dqk), lambda b, c: (b, 0, 0, 0)),
            pl.BlockSpec((1, H, dqk), lambda b, c: (b, 0, 0)),
            pl.BlockSpec((1, 1, LANES), lambda b, c: (b, 0, 0)),
            pl.BlockSpec((1, CONV_W - 1, qkw), lambda b, c: (b, 0, 0)),
            pl.BlockSpec((CONV_W, qkw), lambda b, c: (0, 0)),
            pl.BlockSpec((1, qkw), lambda b, c: (0, 0)),
            pl.BlockSpec((1, LANES), lambda b, c: (0, 0)),
            pl.BlockSpec((1, vw), lambda b, c: (0, 0)),
        ],
        out_specs=[
            pl.BlockSpec((rows, vw), lambda b, c: (b * nc + c, 0)),
            pl.BlockSpec((1, H, dv, dqk), lambda b, c: (b, 0, 0, 0)),
            pl.BlockSpec((1, H, dqk), lambda b, c: (b, 0, 0)),
            pl.BlockSpec((1, 1, LANES), lambda b, c: (b, 0, 0)),
            pl.BlockSpec((1, CONV_W - 1, qkw), lambda b, c: (b, 0, 0)),
        ],
        out_shape=[
            jax.ShapeDtypeStruct((n, vw), BF16),
            jax.ShapeDtypeStruct((nb, H, dv, dqk), F32),
            jax.ShapeDtypeStruct((nb, H, dqk), F32),
            jax.ShapeDtypeStruct((nb, 1, LANES), F32),
            jax.ShapeDtypeStruct((nb, CONV_W - 1, qkw), F32),
        ],
        scratch_shapes=[pltpu.VMEM((L + SUBLANES, qkw), F32)],
        compiler_params=_params(("parallel", "arbitrary")),
        name="mlstm",
    )(p, p, gates, c0, n0, m0, conv0, w_conv, b_conv, gbias, g_head)


def _merge_kernel(o_ref, u_ref, vg_ref, ga_ref, gb_ref, hm_ref, x_ref, wsp_ref, bsp_ref, gvn_ref, bvn_ref,
                  wout_ref, gpost_ref, gt1_ref, gpre_ref, sc2_ref, sh2_ref, wr_ref, *rest, tm, gl, keep_v):
    if keep_v:
        x1_ref, h2_ref, lg_ref, vn_ref, sg_scr = rest
    else:
        x1_ref, h2_ref, lg_ref, sg_scr = rest
    groups = wsp_ref.shape[0]
    ch = vg_ref.shape[1] // groups
    vg = _gelu_tanh(vg_ref[...].astype(F32))
    mu = jnp.mean(vg, -1, keepdims=True)
    xc = vg - mu
    vn = xc * lax.rsqrt(jnp.mean(xc * xc, -1, keepdims=True) + EPS) * gvn_ref[...] + bvn_ref[...]
    if keep_v:
        vn_ref[...] = vn
    vnb = vn.astype(BF16)
    for ci in range(tm // gl):
        for g in range(groups):
            blk = jnp.dot(wsp_ref[g], vnb[ci * gl:(ci + 1) * gl, g * ch:(g + 1) * ch],
                          preferred_element_type=F32)
            sg_scr[ci * gl:(ci + 1) * gl, g * ch:(g + 1) * ch] = blk + bsp_ref[:, g:g + 1]
    h_b = _gelu_tanh(u_ref[...].astype(F32)) * sg_scr[...]
    h_a = _sigmoid(o_ref[...].astype(F32)) * hm_ref[...].astype(F32)
    merged = _sigmoid(ga_ref[...].astype(F32)) * h_a + _sigmoid(gb_ref[...].astype(F32)) * h_b
    y = jnp.dot(merged.astype(BF16), wout_ref[...], preferred_element_type=F32)
    x1 = x_ref[...] + gt1_ref[0] * _rms(y, gpost_ref[...])
    x1_ref[...] = x1
    h2 = _rms(x1, gpre_ref[...]) * (1.0 + sc2_ref[0]) + sh2_ref[0]
    h2_ref[...] = _pack_pairs(h2)
    n_exp = lg_ref.shape[0]
    h2_high = h2.astype(BF16)
    h2_low = (h2 - h2_high.astype(F32)).astype(BF16)
    lg = lax.dot_general(wr_ref[...], h2_high, NT_DIMS, preferred_element_type=F32)
    lg_ref[...] = (lg[:n_exp] + lg[n_exp:]
                   + lax.dot_general(wr_ref[:n_exp, :], h2_low, NT_DIMS, preferred_element_type=F32))


def _merge(p, hm, x, wsp, bsp_t, g_vn, b_vn, w_out, g_post, gt1, g_pre, sc2, sh2, wr, tm, t_seq, keep_v, x_row0):
    n, d = hm.shape
    x_blk0 = x_row0 // tm
    groups, gl, _ = wsp.shape
    n_exp = wr.shape[0] // 2
    tps = max(t_seq // tm, 1)
    row = lambda i: (0, 0)
    pcol = lambda c: pl.BlockSpec((tm, d), lambda i: (i, c))
    tok = pl.BlockSpec((tm, d), lambda i: (i, 0))
    vec = pl.BlockSpec((1, d), row)
    out_specs = [tok, pl.BlockSpec((tm, d // 2), lambda i: (i, 0)), pl.BlockSpec((n_exp, tm), lambda i: (0, i))]
    out_shape = [jax.ShapeDtypeStruct((n, d), F32), jax.ShapeDtypeStruct((n, d // 2), jnp.uint32),
                 jax.ShapeDtypeStruct((n_exp, n), F32)]
    if keep_v:
        out_specs.append(tok)
        out_shape.append(jax.ShapeDtypeStruct((n, d), F32))
    kern = functools.partial(_merge_kernel, tm=tm, gl=gl, keep_v=keep_v)
    return pl.pallas_call(
        kern,
        grid=(n // tm,),
        in_specs=[
            pcol(2), pcol(3), pcol(4), pcol(5), pcol(6), tok,
            pl.BlockSpec((tm, d), lambda i: (i + x_blk0, 0)),
            pl.BlockSpec((groups, gl, gl), lambda i: (0, 0, 0)),
            pl.BlockSpec((gl, groups), row),
            vec, vec,
            pl.BlockSpec((d, d), row),
            vec,
            _mod_spec(gt1, tm, tps, 1),
            vec,
            _mod_spec(sc2, tm, tps, 1),
            _mod_spec(sh2, tm, tps, 1),
            pl.BlockSpec((2 * n_exp, d), row),
        ],
        out_specs=out_specs,
        out_shape=out_shape,
        scratch_shapes=[pltpu.VMEM((tm, d), F32)],
        compiler_params=_params(("parallel",)),
        name="merge",
    )(p, p, p, p, p, hm, x, wsp, bsp_t, g_vn, b_vn, w_out, g_post, gt1, g_pre, sc2, sh2, wr)


def _select(logits, br_ref):
    n_exp, tm = logits.shape
    per = n_exp // N_GROUPS
    neg = -jnp.inf
    s = _sigmoid(logits)
    sel = s + br_ref[...]
    mem = [sel[r * N_GROUPS:(r + 1) * N_GROUPS, :] for r in range(per)]
    grow = lax.broadcasted_iota(jnp.int32, (N_GROUPS, tm), 0)
    m1 = functools.reduce(jnp.maximum, mem)
    idx1 = functools.reduce(jnp.minimum, [jnp.where(mem[r] == m1, r, per) for r in range(per)])
    m2 = functools.reduce(jnp.maximum, [jnp.where(idx1 == r, neg, mem[r]) for r in range(per)])
    gs = m1 + m2
    gmask = jnp.zeros((N_GROUPS, tm), jnp.bool_)
    for _ in range(TOPK_GROUPS):
        mx = jnp.max(gs, axis=0, keepdims=True)
        gi = jnp.min(jnp.where(gs == mx, grow, N_GROUPS), axis=0, keepdims=True)
        pick = grow == gi
        gmask = jnp.logical_or(gmask, pick)
        gs = jnp.where(pick, neg, gs)
    msk = [jnp.where(gmask, mem[r], neg) for r in range(per)]
    eidx = [grow * per + r for r in range(per)]
    chosen = [jnp.zeros((N_GROUPS, tm), jnp.bool_) for _ in range(per)]
    firsts = []
    for _ in range(TOP_K):
        mx = jnp.max(functools.reduce(jnp.maximum, msk), axis=0, keepdims=True)
        cand = functools.reduce(jnp.minimum, [jnp.where(msk[r] == mx, eidx[r], n_exp) for r in range(per)])
        first = jnp.min(cand, axis=0, keepdims=True)
        firsts.append(first)
        for r in range(per):
            pick = eidx[r] == first
            chosen[r] = jnp.logical_or(chosen[r], pick)
            msk[r] = jnp.where(pick, neg, msk[r])
    wk = [jnp.where(chosen[r], s[r * N_GROUPS:(r + 1) * N_GROUPS, :], 0.0) for r in range(per)]
    denom = jnp.sum(functools.reduce(jnp.add, wk), axis=0, keepdims=True)
    return s, chosen, firsts, eidx, denom


def _shared_expert(hb, ws1_ref, ws3_ref, ws2_ref):
    a = jnp.dot(hb, ws1_ref[...], preferred_element_type=F32)
    g = jnp.dot(hb, ws3_ref[...], preferred_element_type=F32)
    return jnp.dot((_silu(a) * g).astype(BF16), ws2_ref[...], preferred_element_type=F32)


def _moe_kernel(h_ref, x1_ref, lg_ref, br_ref, w1_ref, w3_ref, w2_ref, ws1_ref, ws3_ref, ws2_ref, gpost_ref,
                gt2_ref, out_ref, gt_scr, gates_scr, acc_scr, *, eb):
    j = pl.program_id(1)
    hb = _unpack_pairs(h_ref[...]).astype(BF16)
    n_exp = lg_ref.shape[0]
    per = n_exp // N_GROUPS

    @pl.when(j == 0)
    def _():
        s, chosen, _, _, denom = _select(lg_ref[...], br_ref)
        gt_scr[n_exp:, :] = jnp.zeros((gt_scr.shape[0] - n_exp, hb.shape[0]), F32)
        for r in range(per):
            wk = jnp.where(chosen[r], s[r * N_GROUPS:(r + 1) * N_GROUPS, :], 0.0)
            gt_scr[r * N_GROUPS:(r + 1) * N_GROUPS, :] = wk / denom * ROUTE_SCALE
        gates_scr[...] = gt_scr[...].T
        acc_scr[...] = _shared_expert(hb, ws1_ref, ws3_ref, ws2_ref)

    first = j * eb
    lane0 = lax.rem(first, per) * N_GROUPS + first // per
    gates = pltpu.roll(gates_scr[...], lax.rem(LANES - lane0, LANES), 1)
    acc = acc_scr[...]
    for e in range(eb):
        a = jnp.dot(hb, w1_ref[e].astype(BF16), preferred_element_type=F32)
        g = jnp.dot(hb, w3_ref[e].astype(BF16), preferred_element_type=F32)
        gate = gates[:, e * N_GROUPS:e * N_GROUPS + 1]
        acc = acc + jnp.dot((_silu(a) * g * gate).astype(BF16), w2_ref[e].astype(BF16),
                            preferred_element_type=F32)
    acc_scr[...] = acc

    @pl.when(j == pl.num_programs(1) - 1)
    def _():
        out_ref[...] = x1_ref[...] + gt2_ref[0] * _rms(acc_scr[...], gpost_ref[...])


def _moe_dense(h2, x1, logits, lw, gt2, tm, t_seq, eb):
    n, d = x1.shape
    _, n_exp, _, f = lw["w1"].shape
    layer = lw["layer"]
    fs = lw["ws1"].shape[1]
    tps = max(t_seq // tm, 1)
    const = lambda i, j: (0, 0)
    return pl.pallas_call(
        functools.partial(_moe_kernel, eb=eb),
        grid=(n // tm, n_exp // eb),
        in_specs=[
            pl.BlockSpec((tm, d // 2), lambda i, j: (i, 0)),
            pl.BlockSpec((tm, d), lambda i, j: (i, 0)),
            pl.BlockSpec((n_exp, tm), lambda i, j: (0, i)),
            pl.BlockSpec((n_exp, 1), const),
            pl.BlockSpec((None, eb, d, f), lambda i, j: (layer, j, 0, 0)),
            pl.BlockSpec((None, eb, d, f), lambda i, j: (layer, j, 0, 0)),
            pl.BlockSpec((None, eb, f, d), lambda i, j: (layer, j, 0, 0)),
            pl.BlockSpec((d, fs), const),
            pl.BlockSpec((d, fs), const),
            pl.BlockSpec((fs, d), const),
            pl.BlockSpec((1, d), const),
            _mod_spec(gt2, tm, tps, 2),
        ],
        out_specs=pl.BlockSpec((tm, d), lambda i, j: (i, 0)),
        out_shape=jax.ShapeDtypeStruct((n, d), F32),
        scratch_shapes=[
            pltpu.VMEM((LANES, tm), F32),
            pltpu.VMEM((tm, LANES), F32),
            pltpu.VMEM((tm, d), F32),
        ],
        compiler_params=_params(("parallel", "arbitrary")),
        name="moe_dense",
    )(h2, x1, logits, lw["br"], lw["w1"], lw["w3"], lw["w2"], lw["ws1"], lw["ws3"], lw["ws2"],
      lw["g_post_ffn"], gt2)


def _router_kernel(lg_ref, br_ref, eidx_ref, rank_ref, gate_ref, cnt_ref, carry_scr):
    n_exp, tm = lg_ref.shape
    per = n_exp // N_GROUPS

    @pl.when(pl.program_id(0) == 0)
    def _():
        carry_scr[...] = jnp.zeros(carry_scr.shape, F32)

    s, chosen, firsts, eidx, denom = _select(lg_ref[...], br_ref)
    sel01 = jnp.concatenate([c.astype(F32) for c in chosen], axis=0)
    before = (lax.broadcasted_iota(jnp.int32, (tm, tm), 0)
              < lax.broadcasted_iota(jnp.int32, (tm, tm), 1)).astype(BF16)
    rank = jnp.dot(sel01.astype(BF16), before, preferred_element_type=F32) + carry_scr[:, 0:1]
    carry_scr[...] = carry_scr[...] + jnp.sum(sel01, axis=1, keepdims=True)
    cnt_ref[...] = carry_scr[...]
    for k in range(TOP_K):
        s_k = jnp.zeros((1, tm), F32)
        r_k = jnp.zeros((1, tm), F32)
        for r in range(per):
            pick = eidx[r] == firsts[k]
            rows = slice(r * N_GROUPS, (r + 1) * N_GROUPS)
            s_k = s_k + jnp.sum(jnp.where(pick, s[rows, :], 0.0), axis=0, keepdims=True)
            r_k = r_k + jnp.sum(jnp.where(pick, rank[rows, :], 0.0), axis=0, keepdims=True)
        eidx_ref[k:k + 1, :] = firsts[k]
        rank_ref[k:k + 1, :] = r_k.astype(jnp.int32)
        gate_ref[k:k + 1, :] = s_k / denom * ROUTE_SCALE


def _router(logits, br, tm, row0, n):
    n_exp = logits.shape[0]
    tok = pl.BlockSpec((TOP_K, tm), lambda i: (0, i))
    blk0 = row0 // tm
    return pl.pallas_call(
        _router_kernel,
        grid=(n // tm,),
        in_specs=[
            pl.BlockSpec((n_exp, tm), lambda i: (0, i + blk0)),
            pl.BlockSpec((n_exp, 1), lambda i: (0, 0)),
        ],
        out_specs=[tok, tok, tok, pl.BlockSpec((n_exp, LANES), lambda i: (0, 0))],
        out_shape=[
            jax.ShapeDtypeStruct((TOP_K, n), jnp.int32),
            jax.ShapeDtypeStruct((TOP_K, n), jnp.int32),
            jax.ShapeDtypeStruct((TOP_K, n), F32),
            jax.ShapeDtypeStruct((n_exp, LANES), F32),
        ],
        scratch_shapes=[pltpu.VMEM((n_exp, LANES), F32)],
        compiler_params=_params(("arbitrary",)),
        name="router",
    )(logits, br)


def _pos_kernel(starts_ref, eidx_ref, rank_ref, pos_ref):
    e = eidx_ref[...]
    base = jnp.zeros_like(e)
    for x in range(starts_ref.shape[0]):
        base = base + jnp.where(e == x, starts_ref[x], 0)
    pos = base + rank_ref[...]
    for c in range(pos_ref.shape[0]):
        pos_ref[c] = pos[:, c * SC_BATCH:(c + 1) * SC_BATCH]


def _positions(starts, eidx, rank, tm):
    k, n = eidx.shape
    grid_spec = pltpu.PrefetchScalarGridSpec(
        num_scalar_prefetch=1,
        grid=(n // tm,),
        in_specs=[pl.BlockSpec((k, tm), lambda i, s: (0, i)), pl.BlockSpec((k, tm), lambda i, s: (0, i))],
        out_specs=pl.BlockSpec((tm // SC_BATCH, k, SC_BATCH), lambda i, s: (i, 0, 0)),
    )
    return pl.pallas_call(
        _pos_kernel,
        grid_spec=grid_spec,
        out_shape=jax.ShapeDtypeStruct((n // SC_BATCH, k, SC_BATCH), jnp.int32),
        compiler_params=_params(("parallel",)),
        name="positions",
    )(starts, eidx, rank)


SC_BATCH = 128
SC_WORKERS = 32


def _sc_mesh():
    return plsc.VectorSubcoreMesh(core_axis_name="c", subcore_axis_name="s")


def _sc_scatter(x, pos, n_rows, row0):
    w = x.shape[1]
    n = pos.shape[0] * SC_BATCH
    info = plsc.get_sparse_core_info()
    nc, nw = info.num_cores, info.num_cores * info.num_subcores
    steps = n // (nw * SC_BATCH)

    @functools.partial(
        pl.kernel, mesh=_sc_mesh(),
        out_type=jax.ShapeDtypeStruct((n_rows, w), x.dtype),
        scratch_types=[pltpu.VMEM((TOP_K, SC_BATCH), jnp.int32), pltpu.VMEM((SC_BATCH, w), x.dtype),
                       pltpu.SemaphoreType.DMA],
    )
    def scatter(x_hbm, pos_hbm, out_hbm, idx_v, rows_v, sem):
        wid = lax.axis_index("s") * nc + lax.axis_index("c")

        @pl.loop(0, steps)
        def _(s):
            blk = wid * steps + s
            pltpu.sync_copy(pos_hbm.at[blk], idx_v)
            pltpu.sync_copy(x_hbm.at[pl.ds(row0 + blk * SC_BATCH, SC_BATCH)], rows_v)
            copies = [pltpu.async_copy(rows_v, out_hbm.at[idx_v.at[k]], sem) for k in range(TOP_K)]
            for c in copies:
                c.wait()

    return scatter(x, pos)


def _sc_gather(y, pos):
    w = y.shape[1]
    n = pos.shape[0] * SC_BATCH
    info = plsc.get_sparse_core_info()
    nc, nw = info.num_cores, info.num_cores * info.num_subcores
    steps = n // (nw * SC_BATCH)

    @functools.partial(
        pl.kernel, mesh=_sc_mesh(),
        out_type=jax.ShapeDtypeStruct((TOP_K, n, w), y.dtype),
        scratch_types=[pltpu.VMEM((TOP_K, SC_BATCH), jnp.int32), pltpu.VMEM((SC_BATCH, w), y.dtype),
                       pltpu.SemaphoreType.DMA],
    )
    def gather(y_hbm, pos_hbm, out_hbm, idx_v, rows_v, sem):
        wid = lax.axis_index("s") * nc + lax.axis_index("c")

        @pl.loop(0, steps)
        def _(s):
            blk = wid * steps + s
            pltpu.sync_copy(pos_hbm.at[blk], idx_v)
            for k in range(TOP_K):
                pltpu.async_copy(y_hbm.at[idx_v.at[k]], rows_v, sem).wait()
                pltpu.sync_copy(rows_v, out_hbm.at[k, pl.ds(blk * SC_BATCH, SC_BATCH)])

    return gather(y, pos)


def _experts_kernel(te_ref, nu_ref, xs_ref, w1_ref, w3_ref, w2_ref, ys_ref, w1_scr, w3_scr, w2_scr):
    i = pl.program_id(0)

    @pl.when(jnp.logical_or(i == 0, te_ref[i] != te_ref[jnp.maximum(i - 1, 0)]))
    def _():
        w1_scr[...] = w1_ref[...].astype(BF16)
        w3_scr[...] = w3_ref[...].astype(BF16)
        w2_scr[...] = w2_ref[...].astype(BF16)

    @pl.when(i < nu_ref[0])
    def _():
        x = _unpack_pairs(xs_ref[...]).astype(BF16)
        a = jnp.dot(x, w1_scr[...], preferred_element_type=F32)
        g = jnp.dot(x, w3_scr[...], preferred_element_type=F32)
        y = jnp.dot((_silu(a) * g).astype(BF16), w2_scr[...], preferred_element_type=F32)
        ys_ref[...] = _pack_pairs(y)


def _experts(xs, tile_expert, n_used, w1, w3, w2, layer, rt):
    rows, half = xs.shape
    _, n_exp, d, f = w1.shape
    scratch = [pltpu.VMEM((d, f), BF16), pltpu.VMEM((d, f), BF16), pltpu.VMEM((f, d), BF16)]
    grid_spec = pltpu.PrefetchScalarGridSpec(
        num_scalar_prefetch=2,
        grid=(rows // rt,),
        in_specs=[
            pl.BlockSpec((rt, half), lambda i, te, nu: (jnp.minimum(i, nu[0] - 1), 0)),
            pl.BlockSpec((None, None, d, f), lambda i, te, nu: (layer, te[jnp.minimum(i, nu[0] - 1)], 0, 0)),
            pl.BlockSpec((None, None, d, f), lambda i, te, nu: (layer, te[jnp.minimum(i, nu[0] - 1)], 0, 0)),
            pl.BlockSpec((None, None, f, d), lambda i, te, nu: (layer, te[jnp.minimum(i, nu[0] - 1)], 0, 0)),
        ],
        out_specs=pl.BlockSpec((rt, half), lambda i, te, nu: (jnp.minimum(i, nu[0] - 1), 0)),
        scratch_shapes=scratch,
    )
    return pl.pallas_call(
        _experts_kernel,
        grid_spec=grid_spec,
        out_shape=jax.ShapeDtypeStruct((rows, half), jnp.uint32),
        compiler_params=_params(("arbitrary",)),
        name="experts",
    )(tile_expert, n_used, xs, w1, w3, w2)


def _combine_kernel(yk_ref, g_ref, h_ref, x1_ref, ws1_ref, ws3_ref, ws2_ref, gpost_ref, gt2_ref, *rest):
    out_ref = rest[-1]
    hb = _unpack_pairs(h_ref[...]).astype(BF16)
    acc = _shared_expert(hb, ws1_ref, ws3_ref, ws2_ref)
    gates = g_ref[...]
    for k in range(TOP_K):
        acc = acc + gates[:, k:k + 1] * _unpack_pairs(yk_ref[k])
    out_ref[...] = x1_ref[...] + gt2_ref[0] * _rms(acc, gpost_ref[...])


def _combine(yk, gates_t, h2, x1, lw, gt2, tm, t_seq, row0, out_rows, out_row0, out_prev):
    d = x1.shape[1]
    nc = yk.shape[1]
    fs = lw["ws1"].shape[1]
    tps = max(t_seq // tm, 1)
    blk0 = row0 // tm
    out_blk0 = (out_row0 + row0) // tm
    const = lambda i: (0, 0)
    in_specs = [
        pl.BlockSpec((TOP_K, tm, d // 2), lambda i: (0, i, 0)),
        pl.BlockSpec((tm, TOP_K), lambda i: (i, 0)),
        pl.BlockSpec((tm, d // 2), lambda i: (i + blk0, 0)),
        pl.BlockSpec((tm, d), lambda i: (i + blk0, 0)),
        pl.BlockSpec((d, fs), const),
        pl.BlockSpec((d, fs), const),
        pl.BlockSpec((fs, d), const),
        pl.BlockSpec((1, d), const),
        pl.BlockSpec((1, 1, d), lambda i: ((i + blk0) // tps, 0, 0)),
    ]
    args = [yk, gates_t, h2, x1, lw["ws1"], lw["ws3"], lw["ws2"], lw["g_post_ffn"], gt2]
    aliases = {}
    if out_prev is not None:
        in_specs.append(pl.BlockSpec(memory_space=pl.ANY))
        aliases = {len(args): 0}
        args.append(out_prev)
    return pl.pallas_call(
        _combine_kernel,
        grid=(nc // tm,),
        in_specs=in_specs,
        out_specs=pl.BlockSpec((tm, d), lambda i: (i + out_blk0, 0)),
        out_shape=jax.ShapeDtypeStruct((out_rows, d), F32),
        input_output_aliases=aliases,
        compiler_params=_params(("parallel",)),
        name="combine",
    )(*args)


EXPERT_ROWS = 1024
PROMPT_CHAINS = 2


def _route_rows(logits, lw, n):
    n_exp = logits.shape[0]
    per = n_exp // N_GROUPS
    rt = EXPERT_ROWS
    n_tiles = n * TOP_K // rt + n_exp
    eidx, rank, gate, cnt = _router(logits, lw["br"], _tile(n, 1024), 0, n)
    counts = cnt[:, 0].astype(jnp.int32).reshape(per, N_GROUPS).T.reshape(n_exp)
    padded = (counts + rt - 1) // rt * rt
    ends = jnp.cumsum(padded)
    pos = _positions(ends - padded, eidx, rank, _tile(n, 2048))
    tile_start = jnp.arange(n_tiles, dtype=jnp.int32) * rt
    tile_expert = jnp.minimum(jnp.sum(ends[None, :] <= tile_start[:, None], axis=1), n_exp - 1).astype(jnp.int32)
    n_used = (ends[-1:] // rt).astype(jnp.int32)
    return dict(gates_t=gate.T, pos=pos, tile_expert=tile_expert, n_used=n_used, rows=n_tiles * rt)


def _split_router(w_router, n_exp, d):
    wrt = w_router.T.reshape(N_GROUPS, n_exp // N_GROUPS, d).transpose(1, 0, 2).reshape(n_exp, d)
    high = wrt.astype(BF16)
    return jnp.concatenate([high, (wrt - high.astype(F32)).astype(BF16)], axis=0)


def _prep_layer(l, w):
    d = w["w_in"].shape[1]
    n_exp = w["w_e1"].shape[1]
    H = N_HEADS
    w_in = w["w_in"][l]
    qkw = d
    g0 = 3 * d
    w_main = jnp.concatenate([w_in[:, :g0], w_in[:, g0 + 2 * H:]], axis=1).astype(BF16)
    wg = jnp.pad(w_in[:, g0:g0 + 2 * H], ((0, 0), (0, LANES - 2 * H)))
    wg_high = wg.astype(BF16)
    w_gate = jnp.concatenate([wg_high, (wg - wg_high.astype(F32)).astype(BF16)], axis=1)
    gbias = jnp.pad(jnp.concatenate([w["b_igate"][l], w["b_fgate"][l]]), (0, LANES - 2 * H)).reshape(1, LANES)
    return dict(
        w_main=w_main, w_gate=w_gate, gbias=gbias,
        g_pre_mix=w["g_pre_mix"][l].reshape(1, d), g_post_mix=w["g_post_mix"][l].reshape(1, d),
        g_pre_ffn=w["g_pre_ffn"][l].reshape(1, d), g_post_ffn=w["g_post_ffn"][l].reshape(1, d),
        w_conv=w["w_conv"][l], b_conv=w["b_conv"][l].reshape(1, qkw),
        g_head=w["g_head"][l].reshape(1, d), g_vnorm=w["g_vnorm"][l].reshape(1, d),
        b_vnorm=w["b_vnorm"][l].reshape(1, d),
        w_spatial=w["w_spatial"][l], b_spatial=w["b_spatial"][l],
        w_out=w["w_out"][l].astype(BF16),
        wr=_split_router(w["w_router"][l], n_exp, d),
        br=w["b_router"][l].reshape(N_GROUPS, n_exp // N_GROUPS).T.reshape(n_exp, 1),
        w1=w["w_e1"], w3=w["w_e3"], w2=w["w_e2"], layer=l,
        ws1=w["w_s1"][l].astype(BF16), ws3=w["w_s3"][l].astype(BF16), ws2=w["w_s2"][l].astype(BF16),
    )


def _spatial_weights(lw, start, gl):
    idx = jnp.arange(GMLP_CHUNK)
    mask = (idx[None, :] // CHUNK) <= (idx[:, None] // CHUNK)
    wsp = jnp.where(mask, lw["w_spatial"], 0.0)[:, start:start + gl, start:start + gl].astype(BF16)
    bsp_t = lw["b_spatial"][:, start:start + gl].T
    return wsp, bsp_t


def _tile(n, cap):
    t = cap
    while n % t:
        t //= 2
    return t


def _tie(x, dep):
    if dep is None:
        return x
    return lax.optimization_barrier((x, dep))[0]


class _Chain:
    def __init__(self, x, mods, c0, n0, m0, conv0, layers, mlstm_l, g_start, g_l, keep_v, seq0=0, nb=None):
        b_all, t_seq, d = x.shape
        self.nb = b_all if nb is None else nb
        self.t_seq, self.d = t_seq, d
        self.n = self.nb * t_seq
        self.total_rows = b_all * t_seq
        self.out_row0 = seq0 * t_seq
        self.x, self.x_row0 = x.reshape(self.total_rows, d), seq0 * t_seq
        self.mods, self.layers = mods, layers
        self.c0, self.n0, self.m0, self.conv0 = c0, n0, m0, conv0
        self.mlstm_l, self.g_start, self.g_l, self.keep_v = mlstm_l, g_start, g_l, keep_v
        per_token = mods[0][0].shape[1] != 1
        tok_cap = self.n if per_token else t_seq
        self.tm_in, self.tm_mg, self.tm_moe = _tile(tok_cap, 1024), _tile(tok_cap, 512), _tile(tok_cap, 1024)
        self.dispatched = self.n % (SC_WORKERS * SC_BATCH) == 0 and not per_token
        self.outs = [[] for _ in range(5 if keep_v else 4)]

    def inproj(self, l, dep=None):
        lw = self.layers[l]
        sh1, sc1 = self.mods[l][0], self.mods[l][1]
        self.p, self.gates = _inproj(self.x, _tie(sc1, dep), sh1, lw["g_pre_mix"], lw["w_main"], lw["w_gate"],
                                     self.tm_in, self.t_seq, self.x_row0, self.n)
        return self.gates

    def mlstm(self, l, dep=None):
        lw = self.layers[l]
        m0 = self.m0[l]
        m0p = jnp.pad(m0, ((0, 0), (0, LANES - m0.shape[1]))).reshape(self.nb, 1, LANES)
        self.hm, c_new, n_new, m_new, conv_new = _mlstm(
            self.p, self.gates, self.c0[l], self.n0[l], m0p, self.conv0[l], lw["w_conv"], lw["b_conv"],
            _tie(lw["gbias"], dep), lw["g_head"], self.nb, self.mlstm_l)
        for lst, val in zip(self.outs, [c_new, n_new, m_new[:, 0, :N_HEADS], conv_new]):
            lst.append(val)
        return self.hm

    def merge(self, l, dep=None):
        lw = self.layers[l]
        _, _, gt1, sh2, sc2, _ = self.mods[l]
        wsp, bsp_t = _spatial_weights(lw, self.g_start, self.g_l)
        res = _merge(self.p, self.hm, self.x, wsp, bsp_t, _tie(lw["g_vnorm"], dep), lw["b_vnorm"], lw["w_out"],
                     lw["g_post_mix"], gt1, lw["g_pre_ffn"], sc2, sh2, lw["wr"], self.tm_mg, self.t_seq,
                     self.keep_v, self.x_row0)
        self.x1, self.h2, self.logits = res[0], res[1], res[2]
        if self.keep_v:
            self.outs[4].append(res[3].reshape(self.nb, self.t_seq, self.d))
        if not self.dispatched:
            return self.logits
        self.route = _route_rows(self.logits, lw, self.n)
        return self.route["pos"]

    def mixer(self, l, dep=None):
        return self.merge(l, self.mlstm(l, self.inproj(l, dep)))

    def scatter(self, l, dep=None):
        self.xs = _sc_scatter(self.h2, _tie(self.route["pos"], dep), self.route["rows"], 0)

    def experts(self, l, dep=None):
        lw, r = self.layers[l], self.route
        ys = _experts(self.xs, _tie(r["tile_expert"], dep), r["n_used"], lw["w1"], lw["w3"], lw["w2"],
                      lw["layer"], EXPERT_ROWS)
        self.yk = _sc_gather(ys, r["pos"])
        return ys

    def combine(self, l, dep=None, shared=None):
        lw, gt2 = self.layers[l], self.mods[l][5]
        rows, row0, buf = (self.n, 0, None) if shared is None else (self.total_rows, self.out_row0, shared[0])
        self.x = _combine(self.yk, _tie(self.route["gates_t"], dep), self.h2, self.x1, lw, gt2,
                          _tile(self.t_seq, 512), self.t_seq, 0, rows, row0, buf)
        self.x_row0 = 0
        return self.x

    def dense(self, l, dep=None):
        lw, gt2 = self.layers[l], self.mods[l][5]
        self.x = _moe_dense(self.h2, self.x1, _tie(self.logits, dep), lw, gt2, self.tm_moe, self.t_seq, 4)
        self.x_row0 = 0
        return self.x

    def layer(self, l, dep=None, shared=None):
        t = self.mixer(l, dep)
        if not self.dispatched:
            return self.dense(l, t)
        self.scatter(l)
        return self.combine(l, self.experts(l), shared)

    def states(self):
        return tuple(jnp.stack(lst) for lst in self.outs)


def _run_two_chains(a, b, side, depth):
    last = depth - 1
    t = a.mixer(0)
    a.scatter(0)
    side_done = 0
    for l in range(depth):
        t = b.inproj(l, t)
        t = a.experts(l, t)
        t = b.merge(l, b.mlstm(l, t))
        t = y = a.combine(l, t, (None,) if l == last else None)
        b.scatter(l, t)
        if l < last:
            t = a.inproj(l + 1, t)
            t = b.experts(l, t)
            t = a.merge(l + 1, a.mlstm(l + 1, t))
            t = b.combine(l, t)
            a.scatter(l + 1, t)
        else:
            half = (depth + 1) // 2
            for sl in range(half):
                t = side.layer(sl, t)
            t = b.experts(l, t)
            for sl in range(half, depth):
                t = side.layer(sl, t)
            y = b.combine(l, t, (y,))
    return y


def kernel(x_prompt, x_sample, c_prompt, c_sample, state_mlstm_C, state_mlstm_n, state_mlstm_m, state_conv,
           w_ada, b_ada, g_pre_mix, g_post_mix, g_pre_ffn, g_post_ffn, w_in, b_igate, b_fgate, w_conv, b_conv,
           g_head, g_vnorm, b_vnorm, w_spatial, b_spatial, w_out, w_router, b_router, w_e1, w_e3, w_e2,
           w_s1, w_s3, w_s2):
    w = dict(w_in=w_in, b_igate=b_igate, b_fgate=b_fgate, g_pre_mix=g_pre_mix, g_post_mix=g_post_mix,
             g_pre_ffn=g_pre_ffn, g_post_ffn=g_post_ffn, w_conv=w_conv, b_conv=b_conv, g_head=g_head,
             g_vnorm=g_vnorm, b_vnorm=b_vnorm, w_spatial=w_spatial, b_spatial=b_spatial, w_out=w_out,
             w_router=w_router, b_router=b_router, w_e1=w_e1, w_e3=w_e3, w_e2=w_e2, w_s1=w_s1, w_s3=w_s3,
             w_s2=w_s2)
    depth = w_in.shape[0]
    bp, tp, d = x_prompt.shape
    bs, ts, _ = x_sample.shape
    H, dv, dqk = state_mlstm_C.shape[2:]
    qkw = state_conv.shape[-1]
    layers = [_prep_layer(l, w) for l in range(depth)]

    mod = _ada(jnp.concatenate([c_prompt, c_sample], axis=0), w_ada, b_ada)
    mods_p, mods_s = [], []
    for l in range(depth):
        parts = [mod[l][:, i * d:(i + 1) * d] for i in range(6)]
        mods_p.append([a[:bp].reshape(bp, 1, d) for a in parts])
        mods_s.append([jnp.repeat(a[bp:], ts, axis=0).reshape(1, bs * ts, d) for a in parts])

    lp = 256 if tp % 256 == 0 else CHUNK
    chains = PROMPT_CHAINS if bp % PROMPT_CHAINS == 0 else 1
    bc = bp // chains
    zc = jnp.zeros((depth, bc, H, dv, dqk), F32)
    zn = jnp.zeros((depth, bc, H, dqk), F32)
    zm = jnp.zeros((depth, bc, H), F32)
    zconv = jnp.zeros((depth, bc, CONV_W - 1, qkw), F32)
    prompt = []
    for ci in range(chains):
        mods_c = [[a[ci * bc:(ci + 1) * bc] for a in layer_mods] for layer_mods in mods_p]
        prompt.append(_Chain(x_prompt, mods_c, zc, zn, zm, zconv, layers, lp, 0, GMLP_CHUNK, False,
                             seq0=ci * bc, nb=bc))
    sample = _Chain(x_sample, mods_s, state_mlstm_C, state_mlstm_n, state_mlstm_m, state_conv, layers, ts,
                    PAST_LEN % GMLP_CHUNK, ts, True)
    if chains == 2 and all(c.dispatched for c in prompt) and not sample.dispatched:
        y_p = _run_two_chains(prompt[0], prompt[1], sample, depth)
    else:
        y_p = None
        for c in prompt:
            for l in range(depth):
                shared = (y_p,) if (l == depth - 1 and c.dispatched and chains > 1) else None
                out = c.layer(l, None, shared)
            if c.dispatched and chains > 1:
                y_p = out
            else:
                y_p = out if y_p is None else jnp.concatenate([y_p, out], axis=0)
        for l in range(depth):
            sample.layer(l)
    y_p = y_p.reshape(bp, tp, d)
    y_s = sample.x.reshape(bs, ts, d)
    states_p = [c.states() for c in prompt]
    p_c, p_n, p_m, p_conv = (jnp.concatenate([s[i] for s in states_p], axis=1) for i in range(4))
    s_c, s_n, s_m, s_conv, s_v = sample.states()
    return (y_p, y_s, p_c, p_n, p_m, p_conv, s_c, s_n, s_m, s_conv, s_v)
```

```python
import functools

import jax
import jax.numpy as jnp
from jax import lax
from jax.experimental import pallas as pl
from jax.experimental.pallas import tpu as pltpu
from jax.experimental.pallas import tpu_sc as plsc

F32 = jnp.float32
BF16 = jnp.bfloat16

EPS = 1e-6
N_HEADS = 4
CONV_W = 4
GMLP_GROUPS = 8
GMLP_CHUNK = 128
CHUNK = 64
N_GROUPS = 8
TOPK_GROUPS = 4
TOP_K = 8
ROUTE_SCALE = 2.5
PAST_LEN = 4096

LANES = 128
SUBLANES = 8
VMEM_LIMIT = 56 * 1024 * 1024

NT_DIMS = (((1,), (1,)), ((), ()))
TN_DIMS = (((0,), (0,)), ((), ()))


def _sigmoid(x):
    return 0.5 * (jnp.tanh(0.5 * x) + 1.0)


def _silu(x):
    return x * _sigmoid(x)


def _gelu_tanh(x):
    return x * (0.5 * (1.0 + jnp.tanh(0.7978845608028654 * (x + 0.044715 * (x * x * x)))))


def _rms(x, g):
    return x * lax.rsqrt(jnp.mean(x * x, -1, keepdims=True) + EPS) * g


def _pack_pairs(x):
    c = x.shape[1] // 2
    return pltpu.pack_elementwise([x[:, :c], x[:, c:]], packed_dtype=BF16)


def _unpack_pairs(u):
    lo = pltpu.unpack_elementwise(u, index=0, packed_dtype=BF16, unpacked_dtype=F32)
    hi = pltpu.unpack_elementwise(u, index=1, packed_dtype=BF16, unpacked_dtype=F32)
    return jnp.concatenate([lo, hi], axis=1)


def _params(sem):
    return pltpu.CompilerParams(dimension_semantics=sem, vmem_limit_bytes=VMEM_LIMIT)


def _ada_kernel(c_ref, w_ref, b_ref, o_ref):
    a = _silu(c_ref[...]).astype(BF16)
    o_ref[0] = jnp.dot(a, w_ref[0].astype(BF16), preferred_element_type=F32) + b_ref[0]


def _ada(c, w_ada, b_ada):
    depth, d, six_d = w_ada.shape
    nb = c.shape[0]
    tn = d
    return pl.pallas_call(
        _ada_kernel,
        grid=(depth, six_d // tn),
        in_specs=[
            pl.BlockSpec((nb, d), lambda l, j: (0, 0)),
            pl.BlockSpec((1, d, tn), lambda l, j: (l, 0, j)),
            pl.BlockSpec((1, 1, tn), lambda l, j: (l, 0, j)),
        ],
        out_specs=pl.BlockSpec((1, nb, tn), lambda l, j: (l, 0, j)),
        out_shape=jax.ShapeDtypeStruct((depth, nb, six_d), F32),
        compiler_params=_params(("parallel", "parallel")),
        name="adaln",
    )(c, w_ada, b_ada.reshape(depth, 1, six_d))


def _inproj_kernel(x_ref, sc_ref, sh_ref, g_ref, w_ref, wg_ref, p_ref, gates_ref, h_scr):
    @pl.when(pl.program_id(1) == 0)
    def _():
        h = _rms(x_ref[...], g_ref[...]) * (1.0 + sc_ref[0]) + sh_ref[0]
        hb = h.astype(BF16)
        h_scr[...] = hb
        h_low = (h - hb.astype(F32)).astype(BF16)
        gg = jnp.dot(hb, wg_ref[...], preferred_element_type=F32)
        gates_ref[...] = (gg[:, :LANES] + gg[:, LANES:]
                          + jnp.dot(h_low, wg_ref[:, :LANES], preferred_element_type=F32))

    p_ref[...] = jnp.dot(h_scr[...], w_ref[...], preferred_element_type=F32).astype(BF16)


def _mod_spec(mod, tm, tiles_per_seq, ngrid):
    d = mod.shape[-1]
    if mod.shape[1] == 1:
        if ngrid == 2:
            return pl.BlockSpec((1, 1, d), lambda i, j: (i // tiles_per_seq, 0, 0))
        return pl.BlockSpec((1, 1, d), lambda i: (i // tiles_per_seq, 0, 0))
    if ngrid == 2:
        return pl.BlockSpec((1, tm, d), lambda i, j: (0, i, 0))
    return pl.BlockSpec((1, tm, d), lambda i: (0, i, 0))


def _inproj(x, sc, sh, g, w_main, w_gate, tm, t_seq, row0, n):
    d = x.shape[1]
    blk0 = row0 // tm
    width = w_main.shape[1]
    ncol = 2 if (width // 2) % LANES == 0 else width // d
    wc = width // ncol
    tps = max(t_seq // tm, 1)
    return pl.pallas_call(
        _inproj_kernel,
        grid=(n // tm, ncol),
        in_specs=[
            pl.BlockSpec((tm, d), lambda i, j: (i + blk0, 0)),
            _mod_spec(sc, tm, tps, 2),
            _mod_spec(sh, tm, tps, 2),
            pl.BlockSpec((1, d), lambda i, j: (0, 0)),
            pl.BlockSpec((d, wc), lambda i, j: (0, j)),
            pl.BlockSpec((d, 2 * LANES), lambda i, j: (0, 0)),
        ],
        out_specs=[
            pl.BlockSpec((tm, wc), lambda i, j: (i, j)),
            pl.BlockSpec((tm, LANES), lambda i, j: (i, 0)),
        ],
        out_shape=[
            jax.ShapeDtypeStruct((n, width), BF16),
            jax.ShapeDtypeStruct((n, LANES), F32),
        ],
        scratch_shapes=[pltpu.VMEM((tm, d), BF16)],
        compiler_params=_params(("parallel", "arbitrary")),
        name="inproj",
    )(x, sc, sh, g, w_main, w_gate)


def _mlstm_kernel(qk_ref, v_ref, gt_ref, *rest, L, H, dqk, dv, fresh):
    if fresh:
        wconv_ref, bconv_ref, gbias_ref, ghead_ref, hm_ref, c_ref, n_ref, m_ref, conv_ref, xbuf = rest
    else:
        (c0_ref, n0_ref, m0_ref, conv0_ref, wconv_ref, bconv_ref, gbias_ref, ghead_ref,
         hm_ref, c_ref, n_ref, m_ref, conv_ref, xbuf) = rest
    qkw = 2 * H * dqk
    pad = SUBLANES
    tail0 = pad - (CONV_W - 1)

    @pl.when(pl.program_id(1) == 0)
    def _():
        xbuf[0:pad, :] = jnp.zeros((pad, qkw), F32)
        if fresh:
            c_ref[...] = jnp.zeros(c_ref.shape, F32)
            n_ref[...] = jnp.zeros(n_ref.shape, F32)
            m_ref[...] = jnp.zeros(m_ref.shape, F32)
        else:
            c_ref[...] = c0_ref[...]
            n_ref[...] = n0_ref[...]
            m_ref[...] = m0_ref[...]
            xbuf[tail0:pad, :] = conv0_ref[0]

    xbuf[pad:pad + L, :] = qk_ref[...].astype(F32)
    w = wconv_ref[...]
    y = xbuf[tail0:tail0 + L, :] * w[0:1, :]
    for j in range(1, CONV_W):
        y = y + xbuf[tail0 + j:tail0 + j + L, :] * w[j:j + 1, :]
    y = y + bconv_ref[...]
    tail = xbuf[L + tail0:L + pad, :]
    conv_ref[0] = tail
    xbuf[tail0:pad, :] = tail
    qk = _silu(y)

    z = gt_ref[...] + gbias_ref[...]
    lf = jnp.minimum(z, 0.0) - jnp.log(1.0 + jnp.exp(-jnp.abs(z)))
    row = lax.broadcasted_iota(jnp.int32, (L, L), 0)
    col = lax.broadcasted_iota(jnp.int32, (L, L), 1)
    tri = row >= col
    hi = lax.Precision.HIGHEST
    b_col = jnp.dot(tri.astype(F32), lf, precision=hi, preferred_element_type=F32)
    eye = (lax.broadcasted_iota(jnp.int32, (LANES, LANES), 0)
           == lax.broadcasted_iota(jnp.int32, (LANES, LANES), 1)).astype(F32)
    b_row = lax.dot_general(eye, b_col, NT_DIMS, precision=hi, preferred_element_type=F32)
    z_row = lax.dot_general(eye, z, NT_DIMS, precision=hi, preferred_element_type=F32)

    scale = dqk ** -0.5
    lane1 = lax.broadcasted_iota(jnp.int32, (1, LANES), 1)
    m_old = m_ref[0]
    m_new = m_old
    for h in range(H):
        q = qk[:, h * dqk:(h + 1) * dqk] * scale
        k = qk[:, (H + h) * dqk:(H + h + 1) * dqk]
        qb = q.astype(BF16)
        vb = v_ref[:, h * dv:(h + 1) * dv]
        b_c = b_col[:, H + h:H + h + 1]
        i_c = z[:, h:h + 1]
        b_r = b_row[H + h:H + h + 1, :]
        i_r = z_row[h:h + 1, :]
        m_prev = m_old[:, h:h + 1]
        a = b_c + m_prev
        dm = jnp.where(tri, b_c - b_r + i_r, -jnp.inf)
        mt = jnp.maximum(a, jnp.max(dm, axis=-1, keepdims=True))
        s = lax.dot_general(qb, k.astype(BF16), NT_DIMS, preferred_element_type=F32)
        sg = s * jnp.exp(dm - mt)
        aw = jnp.exp(a - mt)
        cm = c_ref[0, h]
        qc = lax.dot_general(qb, cm.astype(BF16), NT_DIMS, preferred_element_type=F32)
        num = jnp.dot(sg.astype(BF16), vb, preferred_element_type=F32) + aw * qc
        nrow = n_ref[0, h:h + 1, :]
        den = jnp.sum(sg, -1, keepdims=True) + aw * jnp.sum(q * nrow, -1, keepdims=True)
        hh = num / jnp.maximum(jnp.abs(den), jnp.exp(-mt))
        m_last = mt[L - 1:L, :]
        b_last = b_c[L - 1:L, :]
        decay = jnp.exp(b_last + m_prev - m_last)
        ws = jnp.exp(b_last - b_c + i_c - m_last)
        kw = ws * k
        c_ref[0, h] = decay * cm + lax.dot_general(vb, kw.astype(BF16), TN_DIMS, preferred_element_type=F32)
        n_ref[0, h:h + 1, :] = decay * nrow + jnp.sum(kw, axis=0, keepdims=True)
        m_new = jnp.where(lane1 == h, m_last, m_new)
        hm_ref[:, h * dv:(h + 1) * dv] = _rms(hh, ghead_ref[:, h * dv:(h + 1) * dv]).astype(BF16)
    m_ref[0] = m_new


def _mlstm(p, gates, state, w_conv, b_conv, gbias, g_head, nb, L, H, dv, dqk):
    n = p.shape[0]
    t_seq = n // nb
    nc = t_seq // L
    qkw = 2 * H * dqk
    vw = H * dv
    fresh = state is None
    kern = functools.partial(_mlstm_kernel, L=L, H=H, dqk=dqk, dv=dv, fresh=fresh)
    state_specs = [] if fresh else [
        pl.BlockSpec((1, H, dv, dqk), lambda b, c: (b, 0, 0, 0)),
        pl.BlockSpec((1, H, dqk), lambda b, c: (b, 0, 0)),
        pl.BlockSpec((1, 1, LANES), lambda b, c: (b, 0, 0)),
        pl.BlockSpec((1, CONV_W - 1, qkw), lambda b, c: (b, 0, 0)),
    ]
    return pl.pallas_call(
        kern,
        grid=(nb, nc),
        in_specs=[
            pl.BlockSpec((L, qkw), lambda b, c: (b * nc + c, 0)),
            pl.BlockSpec((L, vw), lambda b, c: (b * nc + c, 1)),
            pl.BlockSpec((L, LANES), lambda b, c: (b * nc + c, 0)),
            *state_specs,
            pl.BlockSpec((CONV_W, qkw), lambda b, c: (0, 0)),
            pl.BlockSpec((1, qkw), lambda b, c: (0, 0)),
            pl.BlockSpec((1, LANES), lambda b, c: (0, 0)),
            pl.BlockSpec((1, vw), lambda b, c: (0, 0)),
        ],
        out_specs=[
            pl.BlockSpec((L, vw), lambda b, c: (b * nc + c, 0)),
            pl.BlockSpec((1, H, dv, dqk), lambda b, c: (b, 0, 0, 0)),
            pl.BlockSpec((1, H, dqk), lambda b, c: (b, 0, 0)),
            pl.BlockSpec((1, 1, LANES), lambda b, c: (b, 0, 0)),
            pl.BlockSpec((1, CONV_W - 1, qkw), lambda b, c: (b, 0, 0)),
        ],
        out_shape=[
            jax.ShapeDtypeStruct((n, vw), BF16),
            jax.ShapeDtypeStruct((nb, H, dv, dqk), F32),
            jax.ShapeDtypeStruct((nb, H, dqk), F32),
            jax.ShapeDtypeStruct((nb, 1, LANES), F32),
            jax.ShapeDtypeStruct((nb, CONV_W - 1, qkw), F32),
        ],
        scratch_shapes=[pltpu.VMEM((L + SUBLANES, qkw), F32)],
        compiler_params=_params(("parallel", "arbitrary")),
        name="mlstm",
    )(p, p, gates, *(() if fresh else state), w_conv, b_conv, gbias, g_head)


def _merge_kernel(o_ref, u_ref, vg_ref, ga_ref, gb_ref, hm_ref, x_ref, wsp_ref, bsp_ref, gvn_ref, bvn_ref,
                  wout_ref, gpost_ref, gt1_ref, gpre_ref, sc2_ref, sh2_ref, wr_ref, *rest, tm, gl, keep_v):
    if keep_v:
        x1_ref, h2_ref, lg_ref, vn_ref, sg_scr = rest
    else:
        x1_ref, h2_ref, lg_ref, sg_scr = rest
    groups = wsp_ref.shape[0]
    ch = vg_ref.shape[1] // groups
    vg = _gelu_tanh(vg_ref[...].astype(F32))
    mu = jnp.mean(vg, -1, keepdims=True)
    xc = vg - mu
    vn = xc * lax.rsqrt(jnp.mean(xc * xc, -1, keepdims=True) + EPS) * gvn_ref[...] + bvn_ref[...]
    if keep_v:
        vn_ref[...] = vn
    vnb = vn.astype(BF16)
    for ci in range(tm // gl):
        for g in range(groups):
            blk = jnp.dot(wsp_ref[g], vnb[ci * gl:(ci + 1) * gl, g * ch:(g + 1) * ch],
                          preferred_element_type=F32)
            sg_scr[ci * gl:(ci + 1) * gl, g * ch:(g + 1) * ch] = blk + bsp_ref[:, g:g + 1]
    h_b = _gelu_tanh(u_ref[...].astype(F32)) * sg_scr[...]
    h_a = _sigmoid(o_ref[...].astype(F32)) * hm_ref[...].astype(F32)
    merged = _sigmoid(ga_ref[...].astype(F32)) * h_a + _sigmoid(gb_ref[...].astype(F32)) * h_b
    y = jnp.dot(merged.astype(BF16), wout_ref[...], preferred_element_type=F32)
    x1 = x_ref[...] + gt1_ref[0] * _rms(y, gpost_ref[...])
    x1_ref[...] = x1
    h2 = _rms(x1, gpre_ref[...]) * (1.0 + sc2_ref[0]) + sh2_ref[0]
    h2_ref[...] = _pack_pairs(h2)
    n_exp = lg_ref.shape[0]
    h2_high = h2.astype(BF16)
    h2_low = (h2 - h2_high.astype(F32)).astype(BF16)
    lg = lax.dot_general(wr_ref[...], h2_high, NT_DIMS, preferred_element_type=F32)
    lg_ref[...] = (lg[:n_exp] + lg[n_exp:]
                   + lax.dot_general(wr_ref[:n_exp, :], h2_low, NT_DIMS, preferred_element_type=F32))


def _merge(p, hm, x, wsp, bsp_t, g_vn, b_vn, w_out, g_post, gt1, g_pre, sc2, sh2, wr, tm, t_seq, keep_v, x_row0):
    n, d = hm.shape
    x_blk0 = x_row0 // tm
    groups, gl, _ = wsp.shape
    n_exp = wr.shape[0] // 2
    tps = max(t_seq // tm, 1)
    row = lambda i: (0, 0)
    pcol = lambda c: pl.BlockSpec((tm, d), lambda i: (i, c))
    tok = pl.BlockSpec((tm, d), lambda i: (i, 0))
    vec = pl.BlockSpec((1, d), row)
    out_specs = [tok, pl.BlockSpec((tm, d // 2), lambda i: (i, 0)), pl.BlockSpec((n_exp, tm), lambda i: (0, i))]
    out_shape = [jax.ShapeDtypeStruct((n, d), F32), jax.ShapeDtypeStruct((n, d // 2), jnp.uint32),
                 jax.ShapeDtypeStruct((n_exp, n), F32)]
    if keep_v:
        out_specs.append(tok)
        out_shape.append(jax.ShapeDtypeStruct((n, d), F32))
    kern = functools.partial(_merge_kernel, tm=tm, gl=gl, keep_v=keep_v)
    return pl.pallas_call(
        kern,
        grid=(n // tm,),
        in_specs=[
            pcol(2), pcol(3), pcol(4), pcol(5), pcol(6), tok,
            pl.BlockSpec((tm, d), lambda i: (i + x_blk0, 0)),
            pl.BlockSpec((groups, gl, gl), lambda i: (0, 0, 0)),
            pl.BlockSpec((gl, groups), row),
            vec, vec,
            pl.BlockSpec((d, d), row),
            vec,
            _mod_spec(gt1, tm, tps, 1),
            vec,
            _mod_spec(sc2, tm, tps, 1),
            _mod_spec(sh2, tm, tps, 1),
            pl.BlockSpec((2 * n_exp, d), row),
        ],
        out_specs=out_specs,
        out_shape=out_shape,
        scratch_shapes=[pltpu.VMEM((tm, d), F32)],
        compiler_params=_params(("parallel",)),
        name="merge",
    )(p, p, p, p, p, hm, x, wsp, bsp_t, g_vn, b_vn, w_out, g_post, gt1, g_pre, sc2, sh2, wr)


def _select(logits, br_ref):
    n_exp, tm = logits.shape
    per = n_exp // N_GROUPS
    neg = -jnp.inf
    s = _sigmoid(logits)
    sel = s + br_ref[...]
    mem = [sel[r * N_GROUPS:(r + 1) * N_GROUPS, :] for r in range(per)]
    grow = lax.broadcasted_iota(jnp.int32, (N_GROUPS, tm), 0)
    m1 = functools.reduce(jnp.maximum, mem)
    idx1 = functools.reduce(jnp.minimum, [jnp.where(mem[r] == m1, r, per) for r in range(per)])
    m2 = functools.reduce(jnp.maximum, [jnp.where(idx1 == r, neg, mem[r]) for r in range(per)])
    gs = m1 + m2
    gmask = jnp.zeros((N_GROUPS, tm), jnp.bool_)
    for _ in range(TOPK_GROUPS):
        mx = jnp.max(gs, axis=0, keepdims=True)
        gi = jnp.min(jnp.where(gs == mx, grow, N_GROUPS), axis=0, keepdims=True)
        pick = grow == gi
        gmask = jnp.logical_or(gmask, pick)
        gs = jnp.where(pick, neg, gs)
    msk = [jnp.where(gmask, mem[r], neg) for r in range(per)]
    eidx = [grow * per + r for r in range(per)]
    chosen = [jnp.zeros((N_GROUPS, tm), jnp.bool_) for _ in range(per)]
    firsts = []
    for _ in range(TOP_K):
        mx = jnp.max(functools.reduce(jnp.maximum, msk), axis=0, keepdims=True)
        cand = functools.reduce(jnp.minimum, [jnp.where(msk[r] == mx, eidx[r], n_exp) for r in range(per)])
        first = jnp.min(cand, axis=0, keepdims=True)
        firsts.append(first)
        for r in range(per):
            pick = eidx[r] == first
            chosen[r] = jnp.logical_or(chosen[r], pick)
            msk[r] = jnp.where(pick, neg, msk[r])
    wk = [jnp.where(chosen[r], s[r * N_GROUPS:(r + 1) * N_GROUPS, :], 0.0) for r in range(per)]
    denom = jnp.sum(functools.reduce(jnp.add, wk), axis=0, keepdims=True)
    return s, chosen, firsts, eidx, denom


def _shared_expert(hb, ws1_ref, ws3_ref, ws2_ref):
    a = jnp.dot(hb, ws1_ref[...], preferred_element_type=F32)
    g = jnp.dot(hb, ws3_ref[...], preferred_element_type=F32)
    return jnp.dot((_silu(a) * g).astype(BF16), ws2_ref[...], preferred_element_type=F32)


def _moe_kernel(h_ref, x1_ref, lg_ref, br_ref, w1_ref, w3_ref, w2_ref, ws1_ref, ws3_ref, ws2_ref, gpost_ref,
                gt2_ref, out_ref, gt_scr, gates_scr, acc_scr, *, eb):
    j = pl.program_id(1)
    hb = _unpack_pairs(h_ref[...]).astype(BF16)
    n_exp = lg_ref.shape[0]
    per = n_exp // N_GROUPS

    @pl.when(j == 0)
    def _():
        s, chosen, _, _, denom = _select(lg_ref[...], br_ref)
        gt_scr[n_exp:, :] = jnp.zeros((gt_scr.shape[0] - n_exp, hb.shape[0]), F32)
        for r in range(per):
            wk = jnp.where(chosen[r], s[r * N_GROUPS:(r + 1) * N_GROUPS, :], 0.0)
            gt_scr[r * N_GROUPS:(r + 1) * N_GROUPS, :] = wk / denom * ROUTE_SCALE
        gates_scr[...] = gt_scr[...].T
        acc_scr[...] = _shared_expert(hb, ws1_ref, ws3_ref, ws2_ref)

    first = j * eb
    lane0 = lax.rem(first, per) * N_GROUPS + first // per
    gates = pltpu.roll(gates_scr[...], lax.rem(LANES - lane0, LANES), 1)
    acc = acc_scr[...]
    for e in range(eb):
        a = jnp.dot(hb, w1_ref[e].astype(BF16), preferred_element_type=F32)
        g = jnp.dot(hb, w3_ref[e].astype(BF16), preferred_element_type=F32)
        gate = gates[:, e * N_GROUPS:e * N_GROUPS + 1]
        acc = acc + jnp.dot((_silu(a) * g * gate).astype(BF16), w2_ref[e].astype(BF16),
                            preferred_element_type=F32)
    acc_scr[...] = acc

    @pl.when(j == pl.num_programs(1) - 1)
    def _():
        out_ref[...] = x1_ref[...] + gt2_ref[0] * _rms(acc_scr[...], gpost_ref[...])


def _moe_dense(h2, x1, logits, lw, gt2, tm, t_seq, eb):
    n, d = x1.shape
    _, n_exp, _, f = lw["w1"].shape
    layer = lw["layer"]
    fs = lw["ws1"].shape[1]
    tps = max(t_seq // tm, 1)
    const = lambda i, j: (0, 0)
    return pl.pallas_call(
        functools.partial(_moe_kernel, eb=eb),
        grid=(n // tm, n_exp // eb),
        in_specs=[
            pl.BlockSpec((tm, d // 2), lambda i, j: (i, 0)),
            pl.BlockSpec((tm, d), lambda i, j: (i, 0)),
            pl.BlockSpec((n_exp, tm), lambda i, j: (0, i)),
            pl.BlockSpec((n_exp, 1), const),
            pl.BlockSpec((None, eb, d, f), lambda i, j: (layer, j, 0, 0)),
            pl.BlockSpec((None, eb, d, f), lambda i, j: (layer, j, 0, 0)),
            pl.BlockSpec((None, eb, f, d), lambda i, j: (layer, j, 0, 0)),
            pl.BlockSpec((d, fs), const),
            pl.BlockSpec((d, fs), const),
            pl.BlockSpec((fs, d), const),
            pl.BlockSpec((1, d), const),
            _mod_spec(gt2, tm, tps, 2),
        ],
        out_specs=pl.BlockSpec((tm, d), lambda i, j: (i, 0)),
        out_shape=jax.ShapeDtypeStruct((n, d), F32),
        scratch_shapes=[
            pltpu.VMEM((LANES, tm), F32),
            pltpu.VMEM((tm, LANES), F32),
            pltpu.VMEM((tm, d), F32),
        ],
        compiler_params=_params(("parallel", "arbitrary")),
        name="moe_dense",
    )(h2, x1, logits, lw["br"], lw["w1"], lw["w3"], lw["w2"], lw["ws1"], lw["ws3"], lw["ws2"],
      lw["g_post_ffn"], gt2)


def _router_kernel(lg_ref, br_ref, eidx_ref, rank_ref, gate_ref, cnt_ref, carry_scr):
    n_exp, tm = lg_ref.shape
    per = n_exp // N_GROUPS

    @pl.when(pl.program_id(0) == 0)
    def _():
        carry_scr[...] = jnp.zeros(carry_scr.shape, F32)

    s, chosen, firsts, eidx, denom = _select(lg_ref[...], br_ref)
    sel01 = jnp.concatenate([c.astype(F32) for c in chosen], axis=0)
    before = (lax.broadcasted_iota(jnp.int32, (tm, tm), 0)
              < lax.broadcasted_iota(jnp.int32, (tm, tm), 1)).astype(BF16)
    rank = jnp.dot(sel01.astype(BF16), before, preferred_element_type=F32) + carry_scr[:, 0:1]
    carry_scr[...] = carry_scr[...] + jnp.sum(sel01, axis=1, keepdims=True)
    cnt_ref[...] = carry_scr[...]
    for k in range(TOP_K):
        s_k = jnp.zeros((1, tm), F32)
        r_k = jnp.zeros((1, tm), F32)
        for r in range(per):
            pick = eidx[r] == firsts[k]
            rows = slice(r * N_GROUPS, (r + 1) * N_GROUPS)
            s_k = s_k + jnp.sum(jnp.where(pick, s[rows, :], 0.0), axis=0, keepdims=True)
            r_k = r_k + jnp.sum(jnp.where(pick, rank[rows, :], 0.0), axis=0, keepdims=True)
        eidx_ref[k:k + 1, :] = firsts[k]
        rank_ref[k:k + 1, :] = r_k.astype(jnp.int32)
        gate_ref[k:k + 1, :] = s_k / denom * ROUTE_SCALE


def _router(logits, br, tm, row0, n):
    n_exp = logits.shape[0]
    tok = pl.BlockSpec((TOP_K, tm), lambda i: (0, i))
    blk0 = row0 // tm
    return pl.pallas_call(
        _router_kernel,
        grid=(n // tm,),
        in_specs=[
            pl.BlockSpec((n_exp, tm), lambda i: (0, i + blk0)),
            pl.BlockSpec((n_exp, 1), lambda i: (0, 0)),
        ],
        out_specs=[tok, tok, tok, pl.BlockSpec((n_exp, LANES), lambda i: (0, 0))],
        out_shape=[
            jax.ShapeDtypeStruct((TOP_K, n), jnp.int32),
            jax.ShapeDtypeStruct((TOP_K, n), jnp.int32),
            jax.ShapeDtypeStruct((TOP_K, n), F32),
            jax.ShapeDtypeStruct((n_exp, LANES), F32),
        ],
        scratch_shapes=[pltpu.VMEM((n_exp, LANES), F32)],
        compiler_params=_params(("arbitrary",)),
        name="router",
    )(logits, br)


def _pos_kernel(starts_ref, eidx_ref, rank_ref, pos_ref):
    e = eidx_ref[...]
    base = jnp.zeros_like(e)
    for x in range(starts_ref.shape[0]):
        base = base + jnp.where(e == x, starts_ref[x], 0)
    pos = base + rank_ref[...]
    for c in range(pos_ref.shape[0]):
        pos_ref[c] = pos[:, c * SC_BATCH:(c + 1) * SC_BATCH]


def _positions(starts, eidx, rank, tm):
    k, n = eidx.shape
    grid_spec = pltpu.PrefetchScalarGridSpec(
        num_scalar_prefetch=1,
        grid=(n // tm,),
        in_specs=[pl.BlockSpec((k, tm), lambda i, s: (0, i)), pl.BlockSpec((k, tm), lambda i, s: (0, i))],
        out_specs=pl.BlockSpec((tm // SC_BATCH, k, SC_BATCH), lambda i, s: (i, 0, 0)),
    )
    return pl.pallas_call(
        _pos_kernel,
        grid_spec=grid_spec,
        out_shape=jax.ShapeDtypeStruct((n // SC_BATCH, k, SC_BATCH), jnp.int32),
        compiler_params=_params(("parallel",)),
        name="positions",
    )(starts, eidx, rank)


SC_BATCH = 128
SC_WORKERS = 32


def _sc_mesh():
    return plsc.VectorSubcoreMesh(core_axis_name="c", subcore_axis_name="s")


def _sc_scatter(x, pos, n_rows, row0):
    w = x.shape[1]
    n = pos.shape[0] * SC_BATCH
    info = plsc.get_sparse_core_info()
    nc, nw = info.num_cores, info.num_cores * info.num_subcores
    steps = n // (nw * SC_BATCH)

    @functools.partial(
        pl.kernel, mesh=_sc_mesh(),
        out_type=jax.ShapeDtypeStruct((n_rows, w), x.dtype),
        scratch_types=[pltpu.VMEM((TOP_K, SC_BATCH), jnp.int32), pltpu.VMEM((SC_BATCH, w), x.dtype),
                       pltpu.SemaphoreType.DMA],
    )
    def scatter(x_hbm, pos_hbm, out_hbm, idx_v, rows_v, sem):
        wid = lax.axis_index("s") * nc + lax.axis_index("c")

        @pl.loop(0, steps)
        def _(s):
            blk = wid * steps + s
            pltpu.sync_copy(pos_hbm.at[blk], idx_v)
            pltpu.sync_copy(x_hbm.at[pl.ds(row0 + blk * SC_BATCH, SC_BATCH)], rows_v)
            copies = [pltpu.async_copy(rows_v, out_hbm.at[idx_v.at[k]], sem) for k in range(TOP_K)]
            for c in copies:
                c.wait()

    return scatter(x, pos)


def _sc_gather(y, pos):
    w = y.shape[1]
    n = pos.shape[0] * SC_BATCH
    info = plsc.get_sparse_core_info()
    nc, nw = info.num_cores, info.num_cores * info.num_subcores
    steps = n // (nw * SC_BATCH)

    @functools.partial(
        pl.kernel, mesh=_sc_mesh(),
        out_type=jax.ShapeDtypeStruct((TOP_K, n, w), y.dtype),
        scratch_types=[pltpu.VMEM((TOP_K, SC_BATCH), jnp.int32), pltpu.VMEM((SC_BATCH, w), y.dtype),
                       pltpu.SemaphoreType.DMA],
    )
    def gather(y_hbm, pos_hbm, out_hbm, idx_v, rows_v, sem):
        wid = lax.axis_index("s") * nc + lax.axis_index("c")

        @pl.loop(0, steps)
        def _(s):
            blk = wid * steps + s
            pltpu.sync_copy(pos_hbm.at[blk], idx_v)
            for k in range(TOP_K):
                pltpu.async_copy(y_hbm.at[idx_v.at[k]], rows_v, sem).wait()
                pltpu.sync_copy(rows_v, out_hbm.at[k, pl.ds(blk * SC_BATCH, SC_BATCH)])

    return gather(y, pos)


def _experts_kernel(te_ref, nu_ref, xs_ref, w1_ref, w3_ref, w2_ref, ys_ref, w1_scr, w3_scr, w2_scr):
    i = pl.program_id(0)

    @pl.when(jnp.logical_or(i == 0, te_ref[i] != te_ref[jnp.maximum(i - 1, 0)]))
    def _():
        w1_scr[...] = w1_ref[...].astype(BF16)
        w3_scr[...] = w3_ref[...].astype(BF16)
        w2_scr[...] = w2_ref[...].astype(BF16)

    @pl.when(i < nu_ref[0])
    def _():
        x = _unpack_pairs(xs_ref[...]).astype(BF16)
        a = jnp.dot(x, w1_scr[...], preferred_element_type=F32)
        g = jnp.dot(x, w3_scr[...], preferred_element_type=F32)
        y = jnp.dot((_silu(a) * g).astype(BF16), w2_scr[...], preferred_element_type=F32)
        ys_ref[...] = _pack_pairs(y)


def _experts(xs, tile_expert, n_used, w1, w3, w2, layer, rt):
    rows, half = xs.shape
    _, n_exp, d, f = w1.shape
    scratch = [pltpu.VMEM((d, f), BF16), pltpu.VMEM((d, f), BF16), pltpu.VMEM((f, d), BF16)]
    grid_spec = pltpu.PrefetchScalarGridSpec(
        num_scalar_prefetch=2,
        grid=(rows // rt,),
        in_specs=[
            pl.BlockSpec((rt, half), lambda i, te, nu: (jnp.minimum(i, nu[0] - 1), 0)),
            pl.BlockSpec((None, None, d, f), lambda i, te, nu: (layer, te[jnp.minimum(i, nu[0] - 1)], 0, 0)),
            pl.BlockSpec((None, None, d, f), lambda i, te, nu: (layer, te[jnp.minimum(i, nu[0] - 1)], 0, 0)),
            pl.BlockSpec((None, None, f, d), lambda i, te, nu: (layer, te[jnp.minimum(i, nu[0] - 1)], 0, 0)),
        ],
        out_specs=pl.BlockSpec((rt, half), lambda i, te, nu: (jnp.minimum(i, nu[0] - 1), 0)),
        scratch_shapes=scratch,
    )
    return pl.pallas_call(
        _experts_kernel,
        grid_spec=grid_spec,
        out_shape=jax.ShapeDtypeStruct((rows, half), jnp.uint32),
        compiler_params=_params(("arbitrary",)),
        name="experts",
    )(tile_expert, n_used, xs, w1, w3, w2)


def _combine_kernel(yk_ref, g_ref, h_ref, x1_ref, ws1_ref, ws3_ref, ws2_ref, gpost_ref, gt2_ref, *rest):
    out_ref = rest[-1]
    hb = _unpack_pairs(h_ref[...]).astype(BF16)
    acc = _shared_expert(hb, ws1_ref, ws3_ref, ws2_ref)
    gates = g_ref[...]
    for k in range(TOP_K):
        acc = acc + gates[:, k:k + 1] * _unpack_pairs(yk_ref[k])
    out_ref[...] = x1_ref[...] + gt2_ref[0] * _rms(acc, gpost_ref[...])


def _combine(yk, gates_t, h2, x1, lw, gt2, tm, t_seq, row0, out_rows, out_row0, out_prev):
    d = x1.shape[1]
    nc = yk.shape[1]
    fs = lw["ws1"].shape[1]
    tps = max(t_seq // tm, 1)
    blk0 = row0 // tm
    out_blk0 = (out_row0 + row0) // tm
    const = lambda i: (0, 0)
    in_specs = [
        pl.BlockSpec((TOP_K, tm, d // 2), lambda i: (0, i, 0)),
        pl.BlockSpec((tm, TOP_K), lambda i: (i, 0)),
        pl.BlockSpec((tm, d // 2), lambda i: (i + blk0, 0)),
        pl.BlockSpec((tm, d), lambda i: (i + blk0, 0)),
        pl.BlockSpec((d, fs), const),
        pl.BlockSpec((d, fs), const),
        pl.BlockSpec((fs, d), const),
        pl.BlockSpec((1, d), const),
        pl.BlockSpec((1, 1, d), lambda i: ((i + blk0) // tps, 0, 0)),
    ]
    args = [yk, gates_t, h2, x1, lw["ws1"], lw["ws3"], lw["ws2"], lw["g_post_ffn"], gt2]
    aliases = {}
    if out_prev is not None:
        in_specs.append(pl.BlockSpec(memory_space=pl.ANY))
        aliases = {len(args): 0}
        args.append(out_prev)
    return pl.pallas_call(
        _combine_kernel,
        grid=(nc // tm,),
        in_specs=in_specs,
        out_specs=pl.BlockSpec((tm, d), lambda i: (i + out_blk0, 0)),
        out_shape=jax.ShapeDtypeStruct((out_rows, d), F32),
        input_output_aliases=aliases,
        compiler_params=_params(("parallel",)),
        name="combine",
    )(*args)


EXPERT_ROWS = 1024
PROMPT_CHAINS = 2


def _route_rows(logits, lw, n):
    n_exp = logits.shape[0]
    per = n_exp // N_GROUPS
    rt = EXPERT_ROWS
    n_tiles = n * TOP_K // rt + n_exp
    eidx, rank, gate, cnt = _router(logits, lw["br"], _tile(n, 1024), 0, n)
    counts = cnt[:, 0].astype(jnp.int32).reshape(per, N_GROUPS).T.reshape(n_exp)
    padded = (counts + rt - 1) // rt * rt
    ends = jnp.cumsum(padded)
    pos = _positions(ends - padded, eidx, rank, _tile(n, 2048))
    tile_start = jnp.arange(n_tiles, dtype=jnp.int32) * rt
    tile_expert = jnp.minimum(jnp.sum(ends[None, :] <= tile_start[:, None], axis=1), n_exp - 1).astype(jnp.int32)
    n_used = (ends[-1:] // rt).astype(jnp.int32)
    return dict(gates_t=gate.T, pos=pos, tile_expert=tile_expert, n_used=n_used, rows=n_tiles * rt)


def _split_router(w_router, n_exp, d):
    wrt = w_router.T.reshape(N_GROUPS, n_exp // N_GROUPS, d).transpose(1, 0, 2).reshape(n_exp, d)
    high = wrt.astype(BF16)
    return jnp.concatenate([high, (wrt - high.astype(F32)).astype(BF16)], axis=0)


def _prep_layer(l, w):
    d = w["w_in"].shape[1]
    n_exp = w["w_e1"].shape[1]
    H = N_HEADS
    w_in = w["w_in"][l]
    qkw = d
    g0 = 3 * d
    w_main = jnp.concatenate([w_in[:, :g0], w_in[:, g0 + 2 * H:]], axis=1).astype(BF16)
    wg = jnp.pad(w_in[:, g0:g0 + 2 * H], ((0, 0), (0, LANES - 2 * H)))
    wg_high = wg.astype(BF16)
    w_gate = jnp.concatenate([wg_high, (wg - wg_high.astype(F32)).astype(BF16)], axis=1)
    gbias = jnp.pad(jnp.concatenate([w["b_igate"][l], w["b_fgate"][l]]), (0, LANES - 2 * H)).reshape(1, LANES)
    return dict(
        w_main=w_main, w_gate=w_gate, gbias=gbias,
        g_pre_mix=w["g_pre_mix"][l].reshape(1, d), g_post_mix=w["g_post_mix"][l].reshape(1, d),
        g_pre_ffn=w["g_pre_ffn"][l].reshape(1, d), g_post_ffn=w["g_post_ffn"][l].reshape(1, d),
        w_conv=w["w_conv"][l], b_conv=w["b_conv"][l].reshape(1, qkw),
        g_head=w["g_head"][l].reshape(1, d), g_vnorm=w["g_vnorm"][l].reshape(1, d),
        b_vnorm=w["b_vnorm"][l].reshape(1, d),
        w_spatial=w["w_spatial"][l], b_spatial=w["b_spatial"][l],
        w_out=w["w_out"][l].astype(BF16),
        wr=_split_router(w["w_router"][l], n_exp, d),
        br=w["b_router"][l].reshape(N_GROUPS, n_exp // N_GROUPS).T.reshape(n_exp, 1),
        w1=w["w_e1"], w3=w["w_e3"], w2=w["w_e2"], layer=l,
        ws1=w["w_s1"][l].astype(BF16), ws3=w["w_s3"][l].astype(BF16), ws2=w["w_s2"][l].astype(BF16),
    )


def _spatial_weights(lw, start, gl):
    idx = jnp.arange(GMLP_CHUNK)
    mask = (idx[None, :] // CHUNK) <= (idx[:, None] // CHUNK)
    wsp = jnp.where(mask, lw["w_spatial"], 0.0)[:, start:start + gl, start:start + gl].astype(BF16)
    bsp_t = lw["b_spatial"][:, start:start + gl].T
    return wsp, bsp_t


def _tile(n, cap):
    t = cap
    while n % t:
        t //= 2
    return t


def _tie(x, dep):
    if dep is None:
        return x
    return lax.optimization_barrier((x, dep))[0]


class _Chain:
    def __init__(self, x, mods, state, dims, layers, mlstm_l, g_start, g_l, keep_v, seq0=0, nb=None):
        b_all, t_seq, d = x.shape
        self.nb = b_all if nb is None else nb
        self.t_seq, self.d = t_seq, d
        self.n = self.nb * t_seq
        self.total_rows = b_all * t_seq
        self.out_row0 = seq0 * t_seq
        self.x, self.x_row0 = x.reshape(self.total_rows, d), seq0 * t_seq
        self.mods, self.layers = mods, layers
        self.state, self.dims = state, dims
        self.mlstm_l, self.g_start, self.g_l, self.keep_v = mlstm_l, g_start, g_l, keep_v
        per_token = mods[0][0].shape[1] != 1
        tok_cap = self.n if per_token else t_seq
        self.tm_in, self.tm_mg, self.tm_moe = _tile(tok_cap, 1024), _tile(tok_cap, 512), _tile(tok_cap, 1024)
        self.dispatched = self.n % (SC_WORKERS * SC_BATCH) == 0 and not per_token
        self.outs = [[] for _ in range(5 if keep_v else 4)]

    def inproj(self, l, dep=None):
        lw = self.layers[l]
        sh1, sc1 = self.mods[l][0], self.mods[l][1]
        self.p, self.gates = _inproj(self.x, _tie(sc1, dep), sh1, lw["g_pre_mix"], lw["w_main"], lw["w_gate"],
                                     self.tm_in, self.t_seq, self.x_row0, self.n)
        return self.gates

    def mlstm(self, l, dep=None):
        lw = self.layers[l]
        state = None
        if self.state is not None:
            c0, n0, m0, conv0 = (s[l] for s in self.state)
            m0p = jnp.pad(m0, ((0, 0), (0, LANES - m0.shape[1]))).reshape(self.nb, 1, LANES)
            state = (c0, n0, m0p, conv0)
        self.hm, c_new, n_new, m_new, conv_new = _mlstm(
            self.p, self.gates, state, lw["w_conv"], lw["b_conv"], _tie(lw["gbias"], dep), lw["g_head"],
            self.nb, self.mlstm_l, *self.dims)
        for lst, val in zip(self.outs, [c_new, n_new, m_new[:, 0, :N_HEADS], conv_new]):
            lst.append(val)
        return self.hm

    def merge(self, l, dep=None):
        lw = self.layers[l]
        _, _, gt1, sh2, sc2, _ = self.mods[l]
        wsp, bsp_t = _spatial_weights(lw, self.g_start, self.g_l)
        res = _merge(self.p, self.hm, self.x, wsp, bsp_t, _tie(lw["g_vnorm"], dep), lw["b_vnorm"], lw["w_out"],
                     lw["g_post_mix"], gt1, lw["g_pre_ffn"], sc2, sh2, lw["wr"], self.tm_mg, self.t_seq,
                     self.keep_v, self.x_row0)
        self.x1, self.h2, self.logits = res[0], res[1], res[2]
        if self.keep_v:
            self.outs[4].append(res[3].reshape(self.nb, self.t_seq, self.d))
        if not self.dispatched:
            return self.logits
        self.route = _route_rows(self.logits, lw, self.n)
        return self.route["pos"]

    def mixer(self, l, dep=None):
        return self.merge(l, self.mlstm(l, self.inproj(l, dep)))

    def scatter(self, l, dep=None):
        self.xs = _sc_scatter(self.h2, _tie(self.route["pos"], dep), self.route["rows"], 0)

    def experts(self, l, dep=None):
        lw, r = self.layers[l], self.route
        ys = _experts(self.xs, _tie(r["tile_expert"], dep), r["n_used"], lw["w1"], lw["w3"], lw["w2"],
                      lw["layer"], EXPERT_ROWS)
        self.yk = _sc_gather(ys, r["pos"])
        return ys

    def combine(self, l, dep=None, shared=None):
        lw, gt2 = self.layers[l], self.mods[l][5]
        rows, row0, buf = (self.n, 0, None) if shared is None else (self.total_rows, self.out_row0, shared[0])
        self.x = _combine(self.yk, _tie(self.route["gates_t"], dep), self.h2, self.x1, lw, gt2,
                          _tile(self.t_seq, 512), self.t_seq, 0, rows, row0, buf)
        self.x_row0 = 0
        return self.x

    def dense(self, l, dep=None):
        lw, gt2 = self.layers[l], self.mods[l][5]
        self.x = _moe_dense(self.h2, self.x1, _tie(self.logits, dep), lw, gt2, self.tm_moe, self.t_seq, 4)
        self.x_row0 = 0
        return self.x

    def layer(self, l, dep=None, shared=None):
        t = self.mixer(l, dep)
        if not self.dispatched:
            return self.dense(l, t)
        self.scatter(l)
        return self.combine(l, self.experts(l), shared)

    def states(self):
        return tuple(jnp.stack(lst) for lst in self.outs)


def _run_two_chains(a, b, side, depth):
    last = depth - 1
    t = a.mixer(0)
    a.scatter(0)
    side_done = 0
    for l in range(depth):
        t = b.inproj(l, t)
        t = a.experts(l, t)
        t = b.merge(l, b.mlstm(l, t))
        t = y = a.combine(l, t, (None,) if l == last else None)
        b.scatter(l, t)
        if l < last:
            t = a.inproj(l + 1, t)
            t = b.experts(l, t)
            t = a.merge(l + 1, a.mlstm(l + 1, t))
            t = b.combine(l, t)
            a.scatter(l + 1, t)
        else:
            half = (depth + 1) // 2
            for sl in range(half):
                t = side.layer(sl, t)
            t = b.experts(l, t)
            for sl in range(half, depth):
                t = side.layer(sl, t)
            y = b.combine(l, t, (y,))
    return y


def kernel(x_prompt, x_sample, c_prompt, c_sample, state_mlstm_C, state_mlstm_n, state_mlstm_m, state_conv,
           w_ada, b_ada, g_pre_mix, g_post_mix, g_pre_ffn, g_post_ffn, w_in, b_igate, b_fgate, w_conv, b_conv,
           g_head, g_vnorm, b_vnorm, w_spatial, b_spatial, w_out, w_router, b_router, w_e1, w_e3, w_e2,
           w_s1, w_s3, w_s2):
    w = dict(w_in=w_in, b_igate=b_igate, b_fgate=b_fgate, g_pre_mix=g_pre_mix, g_post_mix=g_post_mix,
             g_pre_ffn=g_pre_ffn, g_post_ffn=g_post_ffn, w_conv=w_conv, b_conv=b_conv, g_head=g_head,
             g_vnorm=g_vnorm, b_vnorm=b_vnorm, w_spatial=w_spatial, b_spatial=b_spatial, w_out=w_out,
             w_router=w_router, b_router=b_router, w_e1=w_e1, w_e3=w_e3, w_e2=w_e2, w_s1=w_s1, w_s3=w_s3,
             w_s2=w_s2)
    depth = w_in.shape[0]
    bp, tp, d = x_prompt.shape
    bs, ts, _ = x_sample.shape
    H, dv, dqk = state_mlstm_C.shape[2:]
    layers = [_prep_layer(l, w) for l in range(depth)]

    mod = _ada(jnp.concatenate([c_prompt, c_sample], axis=0), w_ada, b_ada)
    mods_p, mods_s = [], []
    for l in range(depth):
        parts = [mod[l][:, i * d:(i + 1) * d] for i in range(6)]
        mods_p.append([a[:bp].reshape(bp, 1, d) for a in parts])
        mods_s.append([jnp.repeat(a[bp:], ts, axis=0).reshape(1, bs * ts, d) for a in parts])

    lp = 256 if tp % 256 == 0 else CHUNK
    chains = PROMPT_CHAINS if bp % PROMPT_CHAINS == 0 else 1
    bc = bp // chains
    dims = (H, dv, dqk)
    prompt = []
    for ci in range(chains):
        mods_c = [[a[ci * bc:(ci + 1) * bc] for a in layer_mods] for layer_mods in mods_p]
        prompt.append(_Chain(x_prompt, mods_c, None, dims, layers, lp, 0, GMLP_CHUNK, False,
                             seq0=ci * bc, nb=bc))
    sample = _Chain(x_sample, mods_s, (state_mlstm_C, state_mlstm_n, state_mlstm_m, state_conv), dims, layers,
                    ts, PAST_LEN % GMLP_CHUNK, ts, True)
    if chains == 2 and all(c.dispatched for c in prompt) and not sample.dispatched:
        y_p = _run_two_chains(prompt[0], prompt[1], sample, depth)
    else:
        y_p = None
        for c in prompt:
            for l in range(depth):
                shared = (y_p,) if (l == depth - 1 and c.dispatched and chains > 1) else None
                out = c.layer(l, None, shared)
            if c.dispatched and chains > 1:
                y_p = out
            else:
                y_p = out if y_p is None else jnp.concatenate([y_p, out], axis=0)
        for l in range(depth):
            sample.layer(l)
    y_p = y_p.reshape(bp, tp, d)
    y_s = sample.x.reshape(bs, ts, d)
    states_p = [c.states() for c in prompt]
    p_c, p_n, p_m, p_conv = (jnp.concatenate([s[i] for s in states_p], axis=1) for i in range(4))
    s_c, s_n, s_m, s_conv, s_v = sample.states()
    return (y_p, y_s, p_c, p_n, p_m, p_conv, s_c, s_n, s_m, s_conv, s_v)
```

```python
import functools

import jax
import jax.numpy as jnp
from jax import lax
from jax.experimental import pallas as pl
from jax.experimental.pallas import tpu as pltpu
from jax.experimental.pallas import tpu_sc as plsc

F32 = jnp.float32
BF16 = jnp.bfloat16

EPS = 1e-6
N_HEADS = 4
CONV_W = 4
GMLP_GROUPS = 8
GMLP_CHUNK = 128
CHUNK = 64
N_GROUPS = 8
TOPK_GROUPS = 4
TOP_K = 8
ROUTE_SCALE = 2.5
PAST_LEN = 4096

LANES = 128
SUBLANES = 8
VMEM_LIMIT = 56 * 1024 * 1024

NT_DIMS = (((1,), (1,)), ((), ()))
TN_DIMS = (((0,), (0,)), ((), ()))


def _sigmoid(x):
    return 0.5 * (jnp.tanh(0.5 * x) + 1.0)


def _silu(x):
    return x * _sigmoid(x)


def _gelu_tanh(x):
    return x * (0.5 * (1.0 + jnp.tanh(0.7978845608028654 * (x + 0.044715 * (x * x * x)))))


def _rms(x, g):
    return x * lax.rsqrt(jnp.mean(x * x, -1, keepdims=True) + EPS) * g


def _pack_pairs(x):
    c = x.shape[1] // 2
    return pltpu.pack_elementwise([x[:, :c], x[:, c:]], packed_dtype=BF16)


def _unpack_pairs(u):
    lo = pltpu.unpack_elementwise(u, index=0, packed_dtype=BF16, unpacked_dtype=F32)
    hi = pltpu.unpack_elementwise(u, index=1, packed_dtype=BF16, unpacked_dtype=F32)
    return jnp.concatenate([lo, hi], axis=1)


def _params(sem):
    return pltpu.CompilerParams(dimension_semantics=sem, vmem_limit_bytes=VMEM_LIMIT)


def _ada_kernel(c_ref, w_ref, b_ref, o_ref):
    a = _silu(c_ref[...]).astype(BF16)
    o_ref[0] = jnp.dot(a, w_ref[0].astype(BF16), preferred_element_type=F32) + b_ref[0]


def _ada(c, w_ada, b_ada):
    depth, d, six_d = w_ada.shape
    nb = c.shape[0]
    tn = d
    return pl.pallas_call(
        _ada_kernel,
        grid=(depth, six_d // tn),
        in_specs=[
            pl.BlockSpec((nb, d), lambda l, j: (0, 0)),
            pl.BlockSpec((1, d, tn), lambda l, j: (l, 0, j)),
            pl.BlockSpec((1, 1, tn), lambda l, j: (l, 0, j)),
        ],
        out_specs=pl.BlockSpec((1, nb, tn), lambda l, j: (l, 0, j)),
        out_shape=jax.ShapeDtypeStruct((depth, nb, six_d), F32),
        compiler_params=_params(("parallel", "parallel")),
        name="adaln",
    )(c, w_ada, b_ada.reshape(depth, 1, six_d))


def _inproj_kernel(x_ref, sc_ref, sh_ref, g_ref, w_ref, wg_ref, p_ref, gates_ref, h_scr):
    @pl.when(pl.program_id(1) == 0)
    def _():
        h = _rms(x_ref[...], g_ref[...]) * (1.0 + sc_ref[0]) + sh_ref[0]
        hb = h.astype(BF16)
        h_scr[...] = hb
        gates_ref[...] = jnp.dot(hb, wg_ref[...], preferred_element_type=F32)

    p_ref[...] = jnp.dot(h_scr[...], w_ref[...], preferred_element_type=F32).astype(BF16)


def _mod_spec(mod, tm, tiles_per_seq, ngrid):
    d = mod.shape[-1]
    if mod.shape[1] == 1:
        if ngrid == 2:
            return pl.BlockSpec((1, 1, d), lambda i, j: (i // tiles_per_seq, 0, 0))
        return pl.BlockSpec((1, 1, d), lambda i: (i // tiles_per_seq, 0, 0))
    if ngrid == 2:
        return pl.BlockSpec((1, tm, d), lambda i, j: (0, i, 0))
    return pl.BlockSpec((1, tm, d), lambda i: (0, i, 0))


def _inproj(x, sc, sh, g, w_main, w_gate, tm, t_seq, row0, n):
    d = x.shape[1]
    blk0 = row0 // tm
    width = w_main.shape[1]
    ncol = 2 if (width // 2) % LANES == 0 else width // d
    wc = width // ncol
    tps = max(t_seq // tm, 1)
    return pl.pallas_call(
        _inproj_kernel,
        grid=(n // tm, ncol),
        in_specs=[
            pl.BlockSpec((tm, d), lambda i, j: (i + blk0, 0)),
            _mod_spec(sc, tm, tps, 2),
            _mod_spec(sh, tm, tps, 2),
            pl.BlockSpec((1, d), lambda i, j: (0, 0)),
            pl.BlockSpec((d, wc), lambda i, j: (0, j)),
            pl.BlockSpec((d, LANES), lambda i, j: (0, 0)),
        ],
        out_specs=[
            pl.BlockSpec((tm, wc), lambda i, j: (i, j)),
            pl.BlockSpec((tm, LANES), lambda i, j: (i, 0)),
        ],
        out_shape=[
            jax.ShapeDtypeStruct((n, width), BF16),
            jax.ShapeDtypeStruct((n, LANES), F32),
        ],
        scratch_shapes=[pltpu.VMEM((tm, d), BF16)],
        compiler_params=_params(("parallel", "arbitrary")),
        name="inproj",
    )(x, sc, sh, g, w_main, w_gate)


def _mlstm_kernel(qk_ref, v_ref, gt_ref, *rest, L, H, dqk, dv, fresh):
    if fresh:
        wconv_ref, bconv_ref, gbias_ref, ghead_ref, hm_ref, c_ref, n_ref, m_ref, conv_ref, xbuf = rest
    else:
        (c0_ref, n0_ref, m0_ref, conv0_ref, wconv_ref, bconv_ref, gbias_ref, ghead_ref,
         hm_ref, c_ref, n_ref, m_ref, conv_ref, xbuf) = rest
    qkw = 2 * H * dqk
    pad = SUBLANES
    tail0 = pad - (CONV_W - 1)

    @pl.when(pl.program_id(1) == 0)
    def _():
        xbuf[0:pad, :] = jnp.zeros((pad, qkw), F32)
        if fresh:
            c_ref[...] = jnp.zeros(c_ref.shape, F32)
            n_ref[...] = jnp.zeros(n_ref.shape, F32)
            m_ref[...] = jnp.zeros(m_ref.shape, F32)
        else:
            c_ref[...] = c0_ref[...]
            n_ref[...] = n0_ref[...]
            m_ref[...] = m0_ref[...]
            xbuf[tail0:pad, :] = conv0_ref[0]

    xbuf[pad:pad + L, :] = qk_ref[...].astype(F32)
    w = wconv_ref[...]
    y = xbuf[tail0:tail0 + L, :] * w[0:1, :]
    for j in range(1, CONV_W):
        y = y + xbuf[tail0 + j:tail0 + j + L, :] * w[j:j + 1, :]
    y = y + bconv_ref[...]
    tail = xbuf[L + tail0:L + pad, :]
    conv_ref[0] = tail
    xbuf[tail0:pad, :] = tail
    qk = _silu(y)

    z = gt_ref[...] + gbias_ref[...]
    lf = jnp.minimum(z, 0.0) - jnp.log(1.0 + jnp.exp(-jnp.abs(z)))
    row = lax.broadcasted_iota(jnp.int32, (L, L), 0)
    col = lax.broadcasted_iota(jnp.int32, (L, L), 1)
    tri = row >= col
    hi = lax.Precision.HIGHEST
    b_col = jnp.dot(tri.astype(F32), lf, precision=hi, preferred_element_type=F32)
    eye = (lax.broadcasted_iota(jnp.int32, (LANES, LANES), 0)
           == lax.broadcasted_iota(jnp.int32, (LANES, LANES), 1)).astype(F32)
    zb = jnp.concatenate([z, b_col], axis=0)
    zb_row = lax.dot_general(eye, zb, NT_DIMS, precision=hi, preferred_element_type=F32)
    z_row, b_row = zb_row[:, :L], zb_row[:, L:]

    scale = dqk ** -0.5
    lane1 = lax.broadcasted_iota(jnp.int32, (1, LANES), 1)
    m_old = m_ref[0]
    m_new = m_old
    for h in range(H):
        q = qk[:, h * dqk:(h + 1) * dqk] * scale
        k = qk[:, (H + h) * dqk:(H + h + 1) * dqk]
        qb = q.astype(BF16)
        vb = v_ref[:, h * dv:(h + 1) * dv]
        b_c = b_col[:, H + h:H + h + 1]
        i_c = z[:, h:h + 1]
        b_r = b_row[H + h:H + h + 1, :]
        i_r = z_row[h:h + 1, :]
        m_prev = m_old[:, h:h + 1]
        a = b_c + m_prev
        dm = jnp.where(tri, b_c - b_r + i_r, -jnp.inf)
        mt = jnp.maximum(a, jnp.max(dm, axis=-1, keepdims=True))
        s = lax.dot_general(qb, k.astype(BF16), NT_DIMS, preferred_element_type=F32)
        sg = s * jnp.exp(dm - mt)
        aw = jnp.exp(a - mt)
        cm = c_ref[0, h]
        qc = lax.dot_general(qb, cm.astype(BF16), NT_DIMS, preferred_element_type=F32)
        num = jnp.dot(sg.astype(BF16), vb, preferred_element_type=F32) + aw * qc
        nrow = n_ref[0, h:h + 1, :]
        den = jnp.sum(sg, -1, keepdims=True) + aw * jnp.sum(q * nrow, -1, keepdims=True)
        hh = num / jnp.maximum(jnp.abs(den), jnp.exp(-mt))
        m_last = mt[L - 1:L, :]
        b_last = b_c[L - 1:L, :]
        decay = jnp.exp(b_last + m_prev - m_last)
        ws = jnp.exp(b_last - b_c + i_c - m_last)
        kw = ws * k
        c_ref[0, h] = decay * cm + lax.dot_general(vb, kw.astype(BF16), TN_DIMS, preferred_element_type=F32)
        n_ref[0, h:h + 1, :] = decay * nrow + jnp.sum(kw, axis=0, keepdims=True)
        m_new = jnp.where(lane1 == h, m_last, m_new)
        hm_ref[:, h * dv:(h + 1) * dv] = _rms(hh, ghead_ref[:, h * dv:(h + 1) * dv]).astype(BF16)
    m_ref[0] = m_new


def _mlstm(p, gates, state, w_conv, b_conv, gbias, g_head, nb, L, H, dv, dqk):
    n = p.shape[0]
    t_seq = n // nb
    nc = t_seq // L
    qkw = 2 * H * dqk
    vw = H * dv
    fresh = state is None
    kern = functools.partial(_mlstm_kernel, L=L, H=H, dqk=dqk, dv=dv, fresh=fresh)
    state_specs = [] if fresh else [
        pl.BlockSpec((1, H, dv, dqk), lambda b, c: (b, 0, 0, 0)),
        pl.BlockSpec((1, H, dqk), lambda b, c: (b, 0, 0)),
        pl.BlockSpec((1, 1, LANES), lambda b, c: (b, 0, 0)),
        pl.BlockSpec((1, CONV_W - 1, qkw), lambda b, c: (b, 0, 0)),
    ]
    return pl.pallas_call(
        kern,
        grid=(nb, nc),
        in_specs=[
            pl.BlockSpec((L, qkw), lambda b, c: (b * nc + c, 0)),
            pl.BlockSpec((L, vw), lambda b, c: (b * nc + c, 1)),
            pl.BlockSpec((L, LANES), lambda b, c: (b * nc + c, 0)),
            *state_specs,
            pl.BlockSpec((CONV_W, qkw), lambda b, c: (0, 0)),
            pl.BlockSpec((1, qkw), lambda b, c: (0, 0)),
            pl.BlockSpec((1, LANES), lambda b, c: (0, 0)),
            pl.BlockSpec((1, vw), lambda b, c: (0, 0)),
        ],
        out_specs=[
            pl.BlockSpec((L, vw), lambda b, c: (b * nc + c, 0)),
            pl.BlockSpec((1, H, dv, dqk), lambda b, c: (b, 0, 0, 0)),
            pl.BlockSpec((1, H, dqk), lambda b, c: (b, 0, 0)),
            pl.BlockSpec((1, 1, LANES), lambda b, c: (b, 0, 0)),
            pl.BlockSpec((1, CONV_W - 1, qkw), lambda b, c: (b, 0, 0)),
        ],
        out_shape=[
            jax.ShapeDtypeStruct((n, vw), BF16),
            jax.ShapeDtypeStruct((nb, H, dv, dqk), F32),
            jax.ShapeDtypeStruct((nb, H, dqk), F32),
            jax.ShapeDtypeStruct((nb, 1, LANES), F32),
            jax.ShapeDtypeStruct((nb, CONV_W - 1, qkw), F32),
        ],
        scratch_shapes=[pltpu.VMEM((L + SUBLANES, qkw), F32)],
        compiler_params=_params(("parallel", "arbitrary")),
        name="mlstm",
    )(p, p, gates, *(() if fresh else state), w_conv, b_conv, gbias, g_head)


def _merge_kernel(o_ref, u_ref, vg_ref, ga_ref, gb_ref, hm_ref, x_ref, wsp_ref, bsp_ref, gvn_ref, bvn_ref,
                  wout_ref, gpost_ref, gt1_ref, gpre_ref, sc2_ref, sh2_ref, wr_ref, *rest, tm, gl, keep_v):
    if keep_v:
        x1_ref, h2_ref, lg_ref, vn_ref, sg_scr = rest
    else:
        x1_ref, h2_ref, lg_ref, sg_scr = rest
    groups = wsp_ref.shape[0]
    ch = vg_ref.shape[1] // groups
    vg = _gelu_tanh(vg_ref[...].astype(F32))
    mu = jnp.mean(vg, -1, keepdims=True)
    xc = vg - mu
    vn = xc * lax.rsqrt(jnp.mean(xc * xc, -1, keepdims=True) + EPS) * gvn_ref[...] + bvn_ref[...]
    if keep_v:
        vn_ref[...] = vn
    vnb = vn.astype(BF16)
    for ci in range(tm // gl):
        for g in range(groups):
            blk = jnp.dot(wsp_ref[g], vnb[ci * gl:(ci + 1) * gl, g * ch:(g + 1) * ch],
                          preferred_element_type=F32)
            sg_scr[ci * gl:(ci + 1) * gl, g * ch:(g + 1) * ch] = blk + bsp_ref[:, g:g + 1]
    h_b = _gelu_tanh(u_ref[...].astype(F32)) * sg_scr[...]
    h_a = _sigmoid(o_ref[...].astype(F32)) * hm_ref[...].astype(F32)
    merged = _sigmoid(ga_ref[...].astype(F32)) * h_a + _sigmoid(gb_ref[...].astype(F32)) * h_b
    y = jnp.dot(merged.astype(BF16), wout_ref[...], preferred_element_type=F32)
    x1 = x_ref[...] + gt1_ref[0] * _rms(y, gpost_ref[...])
    x1_ref[...] = x1
    h2 = _rms(x1, gpre_ref[...]) * (1.0 + sc2_ref[0]) + sh2_ref[0]
    h2_ref[...] = _pack_pairs(h2)
    n_exp = lg_ref.shape[0]
    h2_high = h2.astype(BF16)
    h2_low = (h2 - h2_high.astype(F32)).astype(BF16)
    lg = lax.dot_general(wr_ref[...], h2_high, NT_DIMS, preferred_element_type=F32)
    lg_ref[...] = (lg[:n_exp] + lg[n_exp:]
                   + lax.dot_general(wr_ref[:n_exp, :], h2_low, NT_DIMS, preferred_element_type=F32))


def _merge(p, hm, x, wsp, bsp_t, g_vn, b_vn, w_out, g_post, gt1, g_pre, sc2, sh2, wr, tm, t_seq, keep_v, x_row0):
    n, d = hm.shape
    x_blk0 = x_row0 // tm
    groups, gl, _ = wsp.shape
    n_exp = wr.shape[0] // 2
    tps = max(t_seq // tm, 1)
    row = lambda i: (0, 0)
    pcol = lambda c: pl.BlockSpec((tm, d), lambda i: (i, c))
    tok = pl.BlockSpec((tm, d), lambda i: (i, 0))
    vec = pl.BlockSpec((1, d), row)
    out_specs = [tok, pl.BlockSpec((tm, d // 2), lambda i: (i, 0)), pl.BlockSpec((n_exp, tm), lambda i: (0, i))]
    out_shape = [jax.ShapeDtypeStruct((n, d), F32), jax.ShapeDtypeStruct((n, d // 2), jnp.uint32),
                 jax.ShapeDtypeStruct((n_exp, n), F32)]
    if keep_v:
        out_specs.append(tok)
        out_shape.append(jax.ShapeDtypeStruct((n, d), F32))
    kern = functools.partial(_merge_kernel, tm=tm, gl=gl, keep_v=keep_v)
    return pl.pallas_call(
        kern,
        grid=(n // tm,),
        in_specs=[
            pcol(2), pcol(3), pcol(4), pcol(5), pcol(6), tok,
            pl.BlockSpec((tm, d), lambda i: (i + x_blk0, 0)),
            pl.BlockSpec((groups, gl, gl), lambda i: (0, 0, 0)),
            pl.BlockSpec((gl, groups), row),
            vec, vec,
            pl.BlockSpec((d, d), row),
            vec,
            _mod_spec(gt1, tm, tps, 1),
            vec,
            _mod_spec(sc2, tm, tps, 1),
            _mod_spec(sh2, tm, tps, 1),
            pl.BlockSpec((2 * n_exp, d), row),
        ],
        out_specs=out_specs,
        out_shape=out_shape,
        scratch_shapes=[pltpu.VMEM((tm, d), F32)],
        compiler_params=_params(("parallel",)),
        name="merge",
    )(p, p, p, p, p, hm, x, wsp, bsp_t, g_vn, b_vn, w_out, g_post, gt1, g_pre, sc2, sh2, wr)


def _select(logits, br_ref):
    n_exp, tm = logits.shape
    per = n_exp // N_GROUPS
    neg = -jnp.inf
    s = _sigmoid(logits)
    sel = s + br_ref[...]
    mem = [sel[r * N_GROUPS:(r + 1) * N_GROUPS, :] for r in range(per)]
    grow = lax.broadcasted_iota(jnp.int32, (N_GROUPS, tm), 0)
    m1 = functools.reduce(jnp.maximum, mem)
    idx1 = functools.reduce(jnp.minimum, [jnp.where(mem[r] == m1, r, per) for r in range(per)])
    m2 = functools.reduce(jnp.maximum, [jnp.where(idx1 == r, neg, mem[r]) for r in range(per)])
    gs = m1 + m2
    gmask = jnp.zeros((N_GROUPS, tm), jnp.bool_)
    for _ in range(TOPK_GROUPS):
        mx = jnp.max(gs, axis=0, keepdims=True)
        gi = jnp.min(jnp.where(gs == mx, grow, N_GROUPS), axis=0, keepdims=True)
        pick = grow == gi
        gmask = jnp.logical_or(gmask, pick)
        gs = jnp.where(pick, neg, gs)
    msk = [jnp.where(gmask, mem[r], neg) for r in range(per)]
    eidx = [grow * per + r for r in range(per)]
    chosen = [jnp.zeros((N_GROUPS, tm), jnp.bool_) for _ in range(per)]
    firsts = []
    for _ in range(TOP_K):
        mx = jnp.max(functools.reduce(jnp.maximum, msk), axis=0, keepdims=True)
        cand = functools.reduce(jnp.minimum, [jnp.where(msk[r] == mx, eidx[r], n_exp) for r in range(per)])
        first = jnp.min(cand, axis=0, keepdims=True)
        firsts.append(first)
        for r in range(per):
            pick = eidx[r] == first
            chosen[r] = jnp.logical_or(chosen[r], pick)
            msk[r] = jnp.where(pick, neg, msk[r])
    wk = [jnp.where(chosen[r], s[r * N_GROUPS:(r + 1) * N_GROUPS, :], 0.0) for r in range(per)]
    denom = jnp.sum(functools.reduce(jnp.add, wk), axis=0, keepdims=True)
    return s, chosen, firsts, eidx, denom


def _shared_expert(hb, ws1_ref, ws3_ref, ws2_ref):
    a = jnp.dot(hb, ws1_ref[...], preferred_element_type=F32)
    g = jnp.dot(hb, ws3_ref[...], preferred_element_type=F32)
    return jnp.dot((_silu(a) * g).astype(BF16), ws2_ref[...], preferred_element_type=F32)


def _moe_kernel(h_ref, x1_ref, lg_ref, br_ref, w1_ref, w3_ref, w2_ref, ws1_ref, ws3_ref, ws2_ref, gpost_ref,
                gt2_ref, out_ref, gt_scr, gates_scr, acc_scr, *, eb):
    j = pl.program_id(1)
    hb = _unpack_pairs(h_ref[...]).astype(BF16)
    n_exp = lg_ref.shape[0]
    per = n_exp // N_GROUPS

    @pl.when(j == 0)
    def _():
        s, chosen, _, _, denom = _select(lg_ref[...], br_ref)
        gt_scr[n_exp:, :] = jnp.zeros((gt_scr.shape[0] - n_exp, hb.shape[0]), F32)
        for r in range(per):
            wk = jnp.where(chosen[r], s[r * N_GROUPS:(r + 1) * N_GROUPS, :], 0.0)
            gt_scr[r * N_GROUPS:(r + 1) * N_GROUPS, :] = wk / denom * ROUTE_SCALE
        gates_scr[...] = gt_scr[...].T
        acc_scr[...] = _shared_expert(hb, ws1_ref, ws3_ref, ws2_ref)

    first = j * eb
    lane0 = lax.rem(first, per) * N_GROUPS + first // per
    gates = pltpu.roll(gates_scr[...], lax.rem(LANES - lane0, LANES), 1)
    acc = acc_scr[...]
    for e in range(eb):
        a = jnp.dot(hb, w1_ref[e].astype(BF16), preferred_element_type=F32)
        g = jnp.dot(hb, w3_ref[e].astype(BF16), preferred_element_type=F32)
        gate = gates[:, e * N_GROUPS:e * N_GROUPS + 1]
        acc = acc + jnp.dot((_silu(a) * g * gate).astype(BF16), w2_ref[e].astype(BF16),
                            preferred_element_type=F32)
    acc_scr[...] = acc

    @pl.when(j == pl.num_programs(1) - 1)
    def _():
        out_ref[...] = x1_ref[...] + gt2_ref[0] * _rms(acc_scr[...], gpost_ref[...])


def _moe_dense(h2, x1, logits, lw, gt2, tm, t_seq, eb):
    n, d = x1.shape
    _, n_exp, _, f = lw["w1"].shape
    layer = lw["layer"]
    fs = lw["ws1"].shape[1]
    tps = max(t_seq // tm, 1)
    const = lambda i, j: (0, 0)
    return pl.pallas_call(
        functools.partial(_moe_kernel, eb=eb),
        grid=(n // tm, n_exp // eb),
        in_specs=[
            pl.BlockSpec((tm, d // 2), lambda i, j: (i, 0)),
            pl.BlockSpec((tm, d), lambda i, j: (i, 0)),
            pl.BlockSpec((n_exp, tm), lambda i, j: (0, i)),
            pl.BlockSpec((n_exp, 1), const),
            pl.BlockSpec((None, eb, d, f), lambda i, j: (layer, j, 0, 0)),
            pl.BlockSpec((None, eb, d, f), lambda i, j: (layer, j, 0, 0)),
            pl.BlockSpec((None, eb, f, d), lambda i, j: (layer, j, 0, 0)),
            pl.BlockSpec((d, fs), const),
            pl.BlockSpec((d, fs), const),
            pl.BlockSpec((fs, d), const),
            pl.BlockSpec((1, d), const),
            _mod_spec(gt2, tm, tps, 2),
        ],
        out_specs=pl.BlockSpec((tm, d), lambda i, j: (i, 0)),
        out_shape=jax.ShapeDtypeStruct((n, d), F32),
        scratch_shapes=[
            pltpu.VMEM((LANES, tm), F32),
            pltpu.VMEM((tm, LANES), F32),
            pltpu.VMEM((tm, d), F32),
        ],
        compiler_params=_params(("parallel", "arbitrary")),
        name="moe_dense",
    )(h2, x1, logits, lw["br"], lw["w1"], lw["w3"], lw["w2"], lw["ws1"], lw["ws3"], lw["ws2"],
      lw["g_post_ffn"], gt2)


def _router_kernel(lg_ref, br_ref, eidx_ref, rank_ref, gate_ref, cnt_ref, carry_scr):
    n_exp, tm = lg_ref.shape
    per = n_exp // N_GROUPS

    @pl.when(pl.program_id(0) == 0)
    def _():
        carry_scr[...] = jnp.zeros(carry_scr.shape, F32)

    s, chosen, firsts, eidx, denom = _select(lg_ref[...], br_ref)
    sel01 = jnp.concatenate([c.astype(F32) for c in chosen], axis=0)
    before = (lax.broadcasted_iota(jnp.int32, (tm, tm), 0)
              < lax.broadcasted_iota(jnp.int32, (tm, tm), 1)).astype(BF16)
    rank = jnp.dot(sel01.astype(BF16), before, preferred_element_type=F32) + carry_scr[:, 0:1]
    carry_scr[...] = carry_scr[...] + jnp.sum(sel01, axis=1, keepdims=True)
    cnt_ref[...] = carry_scr[...]
    for k in range(TOP_K):
        s_k = jnp.zeros((1, tm), F32)
        r_k = jnp.zeros((1, tm), F32)
        for r in range(per):
            pick = eidx[r] == firsts[k]
            rows = slice(r * N_GROUPS, (r + 1) * N_GROUPS)
            s_k = s_k + jnp.sum(jnp.where(pick, s[rows, :], 0.0), axis=0, keepdims=True)
            r_k = r_k + jnp.sum(jnp.where(pick, rank[rows, :], 0.0), axis=0, keepdims=True)
        eidx_ref[k:k + 1, :] = firsts[k]
        rank_ref[k:k + 1, :] = r_k.astype(jnp.int32)
        gate_ref[k:k + 1, :] = s_k / denom * ROUTE_SCALE


def _router(logits, br, tm, row0, n):
    n_exp = logits.shape[0]
    tok = pl.BlockSpec((TOP_K, tm), lambda i: (0, i))
    blk0 = row0 // tm
    return pl.pallas_call(
        _router_kernel,
        grid=(n // tm,),
        in_specs=[
            pl.BlockSpec((n_exp, tm), lambda i: (0, i + blk0)),
            pl.BlockSpec((n_exp, 1), lambda i: (0, 0)),
        ],
        out_specs=[tok, tok, tok, pl.BlockSpec((n_exp, LANES), lambda i: (0, 0))],
        out_shape=[
            jax.ShapeDtypeStruct((TOP_K, n), jnp.int32),
            jax.ShapeDtypeStruct((TOP_K, n), jnp.int32),
            jax.ShapeDtypeStruct((TOP_K, n), F32),
            jax.ShapeDtypeStruct((n_exp, LANES), F32),
        ],
        scratch_shapes=[pltpu.VMEM((n_exp, LANES), F32)],
        compiler_params=_params(("arbitrary",)),
        name="router",
    )(logits, br)


def _pos_kernel(starts_ref, eidx_ref, rank_ref, pos_ref):
    e = eidx_ref[...]
    base = jnp.zeros_like(e)
    for x in range(starts_ref.shape[0]):
        base = base + jnp.where(e == x, starts_ref[x], 0)
    pos = base + rank_ref[...]
    for c in range(pos_ref.shape[0]):
        pos_ref[c] = pos[:, c * SC_BATCH:(c + 1) * SC_BATCH]


def _positions(starts, eidx, rank, tm):
    k, n = eidx.shape
    grid_spec = pltpu.PrefetchScalarGridSpec(
        num_scalar_prefetch=1,
        grid=(n // tm,),
        in_specs=[pl.BlockSpec((k, tm), lambda i, s: (0, i)), pl.BlockSpec((k, tm), lambda i, s: (0, i))],
        out_specs=pl.BlockSpec((tm // SC_BATCH, k, SC_BATCH), lambda i, s: (i, 0, 0)),
    )
    return pl.pallas_call(
        _pos_kernel,
        grid_spec=grid_spec,
        out_shape=jax.ShapeDtypeStruct((n // SC_BATCH, k, SC_BATCH), jnp.int32),
        compiler_params=_params(("parallel",)),
        name="positions",
    )(starts, eidx, rank)


SC_BATCH = 128
SC_WORKERS = 32


def _sc_mesh():
    return plsc.VectorSubcoreMesh(core_axis_name="c", subcore_axis_name="s")


def _sc_scatter(x, pos, n_rows, row0):
    w = x.shape[1]
    n = pos.shape[0] * SC_BATCH
    info = plsc.get_sparse_core_info()
    nc, nw = info.num_cores, info.num_cores * info.num_subcores
    steps = n // (nw * SC_BATCH)

    @functools.partial(
        pl.kernel, mesh=_sc_mesh(),
        out_type=jax.ShapeDtypeStruct((n_rows, w), x.dtype),
        scratch_types=[pltpu.VMEM((TOP_K, SC_BATCH), jnp.int32), pltpu.VMEM((SC_BATCH, w), x.dtype),
                       pltpu.SemaphoreType.DMA],
    )
    def scatter(x_hbm, pos_hbm, out_hbm, idx_v, rows_v, sem):
        wid = lax.axis_index("s") * nc + lax.axis_index("c")

        @pl.loop(0, steps)
        def _(s):
            blk = wid * steps + s
            pltpu.sync_copy(pos_hbm.at[blk], idx_v)
            pltpu.sync_copy(x_hbm.at[pl.ds(row0 + blk * SC_BATCH, SC_BATCH)], rows_v)
            copies = [pltpu.async_copy(rows_v, out_hbm.at[idx_v.at[k]], sem) for k in range(TOP_K)]
            for c in copies:
                c.wait()

    return scatter(x, pos)


def _sc_gather(y, pos):
    w = y.shape[1]
    n = pos.shape[0] * SC_BATCH
    info = plsc.get_sparse_core_info()
    nc, nw = info.num_cores, info.num_cores * info.num_subcores
    steps = n // (nw * SC_BATCH)

    @functools.partial(
        pl.kernel, mesh=_sc_mesh(),
        out_type=jax.ShapeDtypeStruct((TOP_K, n, w), y.dtype),
        scratch_types=[pltpu.VMEM((TOP_K, SC_BATCH), jnp.int32), pltpu.VMEM((SC_BATCH, w), y.dtype),
                       pltpu.SemaphoreType.DMA],
    )
    def gather(y_hbm, pos_hbm, out_hbm, idx_v, rows_v, sem):
        wid = lax.axis_index("s") * nc + lax.axis_index("c")

        @pl.loop(0, steps)
        def _(s):
            blk = wid * steps + s
            pltpu.sync_copy(pos_hbm.at[blk], idx_v)
            for k in range(TOP_K):
                pltpu.async_copy(y_hbm.at[idx_v.at[k]], rows_v, sem).wait()
                pltpu.sync_copy(rows_v, out_hbm.at[k, pl.ds(blk * SC_BATCH, SC_BATCH)])

    return gather(y, pos)


ROW_RING = 3


def _experts_kernel(te_ref, nu_ref, xs_hbm, w1_ref, w3_ref, w2_ref, ys_ref, w1_scr, w3_scr, w2_scr,
                    xbuf, xsem):
    i = pl.program_id(0)
    n_used = nu_ref[0]
    rt = xbuf.shape[1]

    def row_copy(tile):
        if isinstance(tile, int):
            slot, start = tile % ROW_RING, tile * rt
        else:
            slot, start = lax.rem(tile, ROW_RING), pl.multiple_of(tile * rt, rt)
        return pltpu.make_async_copy(xs_hbm.at[pl.ds(start, rt), :], xbuf.at[slot], xsem.at[slot])

    @pl.when(i == 0)
    def _():
        for j in range(ROW_RING - 1):
            @pl.when(j < n_used)
            def _():
                row_copy(j).start()

    @pl.when(i + (ROW_RING - 1) < n_used)
    def _():
        row_copy(i + (ROW_RING - 1)).start()

    @pl.when(jnp.logical_or(i == 0, te_ref[i] != te_ref[jnp.maximum(i - 1, 0)]))
    def _():
        w1_scr[...] = w1_ref[...].astype(BF16)
        w3_scr[...] = w3_ref[...].astype(BF16)
        w2_scr[...] = w2_ref[...].astype(BF16)

    @pl.when(i < n_used)
    def _():
        row_copy(i).wait()
        x = _unpack_pairs(xbuf[lax.rem(i, ROW_RING)]).astype(BF16)
        a = jnp.dot(x, w1_scr[...], preferred_element_type=F32)
        g = jnp.dot(x, w3_scr[...], preferred_element_type=F32)
        y = jnp.dot((_silu(a) * g).astype(BF16), w2_scr[...], preferred_element_type=F32)
        ys_ref[...] = _pack_pairs(y)


def _experts(xs, tile_expert, n_used, w1, w3, w2, layer, rt):
    rows, half = xs.shape
    _, n_exp, d, f = w1.shape
    scratch = [pltpu.VMEM((d, f), BF16), pltpu.VMEM((d, f), BF16), pltpu.VMEM((f, d), BF16),
               pltpu.VMEM((ROW_RING, rt, half), xs.dtype), pltpu.SemaphoreType.DMA((ROW_RING,))]
    grid_spec = pltpu.PrefetchScalarGridSpec(
        num_scalar_prefetch=2,
        grid=(rows // rt,),
        in_specs=[
            pl.BlockSpec(memory_space=pl.ANY),
            pl.BlockSpec((None, None, d, f), lambda i, te, nu: (layer, te[jnp.minimum(i, nu[0] - 1)], 0, 0)),
            pl.BlockSpec((None, None, d, f), lambda i, te, nu: (layer, te[jnp.minimum(i, nu[0] - 1)], 0, 0)),
            pl.BlockSpec((None, None, f, d), lambda i, te, nu: (layer, te[jnp.minimum(i, nu[0] - 1)], 0, 0)),
        ],
        out_specs=pl.BlockSpec((rt, half), lambda i, te, nu: (jnp.minimum(i, nu[0] - 1), 0)),
        scratch_shapes=scratch,
    )
    return pl.pallas_call(
        _experts_kernel,
        grid_spec=grid_spec,
        out_shape=jax.ShapeDtypeStruct((rows, half), jnp.uint32),
        compiler_params=_params(("arbitrary",)),
        name="experts",
    )(tile_expert, n_used, xs, w1, w3, w2)


def _combine_kernel(yk_ref, g_ref, h_ref, x1_ref, ws1_ref, ws3_ref, ws2_ref, gpost_ref, gt2_ref, *rest):
    out_ref = rest[-1]
    hb = _unpack_pairs(h_ref[...]).astype(BF16)
    acc = _shared_expert(hb, ws1_ref, ws3_ref, ws2_ref)
    gates = g_ref[...]
    for k in range(TOP_K):
        acc = acc + gates[:, k:k + 1] * _unpack_pairs(yk_ref[k])
    out_ref[...] = x1_ref[...] + gt2_ref[0] * _rms(acc, gpost_ref[...])


def _combine(yk, gates_t, h2, x1, lw, gt2, tm, t_seq, row0, out_rows, out_row0, out_prev):
    d = x1.shape[1]
    nc = yk.shape[1]
    fs = lw["ws1"].shape[1]
    tps = max(t_seq // tm, 1)
    blk0 = row0 // tm
    out_blk0 = (out_row0 + row0) // tm
    const = lambda i: (0, 0)
    in_specs = [
        pl.BlockSpec((TOP_K, tm, d // 2), lambda i: (0, i, 0)),
        pl.BlockSpec((tm, TOP_K), lambda i: (i, 0)),
        pl.BlockSpec((tm, d // 2), lambda i: (i + blk0, 0)),
        pl.BlockSpec((tm, d), lambda i: (i + blk0, 0)),
        pl.BlockSpec((d, fs), const),
        pl.BlockSpec((d, fs), const),
        pl.BlockSpec((fs, d), const),
        pl.BlockSpec((1, d), const),
        pl.BlockSpec((1, 1, d), lambda i: ((i + blk0) // tps, 0, 0)),
    ]
    args = [yk, gates_t, h2, x1, lw["ws1"], lw["ws3"], lw["ws2"], lw["g_post_ffn"], gt2]
    aliases = {}
    if out_prev is not None:
        in_specs.append(pl.BlockSpec(memory_space=pl.ANY))
        aliases = {len(args): 0}
        args.append(out_prev)
    return pl.pallas_call(
        _combine_kernel,
        grid=(nc // tm,),
        in_specs=in_specs,
        out_specs=pl.BlockSpec((tm, d), lambda i: (i + out_blk0, 0)),
        out_shape=jax.ShapeDtypeStruct((out_rows, d), F32),
        input_output_aliases=aliases,
        compiler_params=_params(("parallel",)),
        name="combine",
    )(*args)


EXPERT_ROWS = 1024
PROMPT_CHAINS = 2


def _route_rows(logits, lw, n):
    n_exp = logits.shape[0]
    per = n_exp // N_GROUPS
    rt = EXPERT_ROWS
    n_tiles = n * TOP_K // rt + n_exp
    eidx, rank, gate, cnt = _router(logits, lw["br"], _tile(n, 1024), 0, n)
    counts = cnt[:, 0].astype(jnp.int32).reshape(per, N_GROUPS).T.reshape(n_exp)
    padded = (counts + rt - 1) // rt * rt
    ends = jnp.cumsum(padded)
    pos = _positions(ends - padded, eidx, rank, _tile(n, 2048))
    tile_start = jnp.arange(n_tiles, dtype=jnp.int32) * rt
    tile_expert = jnp.minimum(jnp.sum(ends[None, :] <= tile_start[:, None], axis=1), n_exp - 1).astype(jnp.int32)
    n_used = (ends[-1:] // rt).astype(jnp.int32)
    return dict(gates_t=gate.T, pos=pos, tile_expert=tile_expert, n_used=n_used, rows=n_tiles * rt)


def _split_router(w_router, n_exp, d):
    wrt = w_router.T.reshape(N_GROUPS, n_exp // N_GROUPS, d).transpose(1, 0, 2).reshape(n_exp, d)
    high = wrt.astype(BF16)
    return jnp.concatenate([high, (wrt - high.astype(F32)).astype(BF16)], axis=0)


def _prep_layer(l, w):
    d = w["w_in"].shape[1]
    n_exp = w["w_e1"].shape[1]
    H = N_HEADS
    w_in = w["w_in"][l]
    qkw = d
    g0 = 3 * d
    w_main = jnp.concatenate([w_in[:, :g0], w_in[:, g0 + 2 * H:]], axis=1).astype(BF16)
    wg = jnp.pad(w_in[:, g0:g0 + 2 * H], ((0, 0), (0, LANES - 2 * H)))
    w_gate = wg.astype(BF16)
    gbias = jnp.pad(jnp.concatenate([w["b_igate"][l], w["b_fgate"][l]]), (0, LANES - 2 * H)).reshape(1, LANES)
    return dict(
        w_main=w_main, w_gate=w_gate, gbias=gbias,
        g_pre_mix=w["g_pre_mix"][l].reshape(1, d), g_post_mix=w["g_post_mix"][l].reshape(1, d),
        g_pre_ffn=w["g_pre_ffn"][l].reshape(1, d), g_post_ffn=w["g_post_ffn"][l].reshape(1, d),
        w_conv=w["w_conv"][l], b_conv=w["b_conv"][l].reshape(1, qkw),
        g_head=w["g_head"][l].reshape(1, d), g_vnorm=w["g_vnorm"][l].reshape(1, d),
        b_vnorm=w["b_vnorm"][l].reshape(1, d),
        w_spatial=w["w_spatial"][l], b_spatial=w["b_spatial"][l],
        w_out=w["w_out"][l].astype(BF16),
        wr=_split_router(w["w_router"][l], n_exp, d),
        br=w["b_router"][l].reshape(N_GROUPS, n_exp // N_GROUPS).T.reshape(n_exp, 1),
        w1=w["w_e1"], w3=w["w_e3"], w2=w["w_e2"], layer=l,
        ws1=w["w_s1"][l].astype(BF16), ws3=w["w_s3"][l].astype(BF16), ws2=w["w_s2"][l].astype(BF16),
    )


def _spatial_weights(lw, start, gl):
    idx = jnp.arange(GMLP_CHUNK)
    mask = (idx[None, :] // CHUNK) <= (idx[:, None] // CHUNK)
    wsp = jnp.where(mask, lw["w_spatial"], 0.0)[:, start:start + gl, start:start + gl].astype(BF16)
    bsp_t = lw["b_spatial"][:, start:start + gl].T
    return wsp, bsp_t


def _tile(n, cap):
    t = cap
    while n % t:
        t //= 2
    return t


def _tie(x, dep):
    if dep is None:
        return x
    return lax.optimization_barrier((x, dep))[0]


class _Chain:
    def __init__(self, x, mods, state, dims, layers, mlstm_l, g_start, g_l, keep_v, seq0=0, nb=None):
        b_all, t_seq, d = x.shape
        self.nb = b_all if nb is None else nb
        self.t_seq, self.d = t_seq, d
        self.n = self.nb * t_seq
        self.total_rows = b_all * t_seq
        self.out_row0 = seq0 * t_seq
        self.x, self.x_row0 = x.reshape(self.total_rows, d), seq0 * t_seq
        self.mods, self.layers = mods, layers
        self.state, self.dims = state, dims
        self.mlstm_l, self.g_start, self.g_l, self.keep_v = mlstm_l, g_start, g_l, keep_v
        per_token = mods[0][0].shape[1] != 1
        tok_cap = self.n if per_token else t_seq
        self.tm_in, self.tm_mg, self.tm_moe = _tile(tok_cap, 1024), _tile(tok_cap, 512), _tile(tok_cap, 1024)
        self.dispatched = self.n % (SC_WORKERS * SC_BATCH) == 0 and not per_token
        self.outs = [[] for _ in range(5 if keep_v else 4)]

    def inproj(self, l, dep=None):
        lw = self.layers[l]
        sh1, sc1 = self.mods[l][0], self.mods[l][1]
        self.p, self.gates = _inproj(self.x, _tie(sc1, dep), sh1, lw["g_pre_mix"], lw["w_main"], lw["w_gate"],
                                     self.tm_in, self.t_seq, self.x_row0, self.n)
        return self.gates

    def mlstm(self, l, dep=None):
        lw = self.layers[l]
        state = None
        if self.state is not None:
            c0, n0, m0, conv0 = (s[l] for s in self.state)
            m0p = jnp.pad(m0, ((0, 0), (0, LANES - m0.shape[1]))).reshape(self.nb, 1, LANES)
            state = (c0, n0, m0p, conv0)
        self.hm, c_new, n_new, m_new, conv_new = _mlstm(
            self.p, self.gates, state, lw["w_conv"], lw["b_conv"], _tie(lw["gbias"], dep), lw["g_head"],
            self.nb, self.mlstm_l, *self.dims)
        for lst, val in zip(self.outs, [c_new, n_new, m_new[:, 0, :N_HEADS], conv_new]):
            lst.append(val)
        return self.hm

    def merge(self, l, dep=None):
        lw = self.layers[l]
        _, _, gt1, sh2, sc2, _ = self.mods[l]
        wsp, bsp_t = _spatial_weights(lw, self.g_start, self.g_l)
        res = _merge(self.p, self.hm, self.x, wsp, bsp_t, _tie(lw["g_vnorm"], dep), lw["b_vnorm"], lw["w_out"],
                     lw["g_post_mix"], gt1, lw["g_pre_ffn"], sc2, sh2, lw["wr"], self.tm_mg, self.t_seq,
                     self.keep_v, self.x_row0)
        self.x1, self.h2, self.logits = res[0], res[1], res[2]
        if self.keep_v:
            self.outs[4].append(res[3].reshape(self.nb, self.t_seq, self.d))
        if not self.dispatched:
            return self.logits
        self.route = _route_rows(self.logits, lw, self.n)
        return self.route["pos"]

    def mixer(self, l, dep=None):
        return self.merge(l, self.mlstm(l, self.inproj(l, dep)))

    def scatter(self, l, dep=None):
        self.xs = _sc_scatter(self.h2, _tie(self.route["pos"], dep), self.route["rows"], 0)

    def experts(self, l, dep=None):
        lw, r = self.layers[l], self.route
        ys = _experts(self.xs, _tie(r["tile_expert"], dep), r["n_used"], lw["w1"], lw["w3"], lw["w2"],
                      lw["layer"], EXPERT_ROWS)
        self.yk = _sc_gather(ys, r["pos"])
        return ys

    def combine(self, l, dep=None, shared=None):
        lw, gt2 = self.layers[l], self.mods[l][5]
        rows, row0, buf = (self.n, 0, None) if shared is None else (self.total_rows, self.out_row0, shared[0])
        self.x = _combine(self.yk, _tie(self.route["gates_t"], dep), self.h2, self.x1, lw, gt2,
                          _tile(self.t_seq, 512), self.t_seq, 0, rows, row0, buf)
        self.x_row0 = 0
        return self.x

    def dense(self, l, dep=None):
        lw, gt2 = self.layers[l], self.mods[l][5]
        self.x = _moe_dense(self.h2, self.x1, _tie(self.logits, dep), lw, gt2, self.tm_moe, self.t_seq, 4)
        self.x_row0 = 0
        return self.x

    def layer(self, l, dep=None, shared=None):
        t = self.mixer(l, dep)
        if not self.dispatched:
            return self.dense(l, t)
        self.scatter(l)
        return self.combine(l, self.experts(l), shared)

    def states(self):
        return tuple(jnp.stack(lst) for lst in self.outs)


def _run_two_chains(a, b, side, depth):
    last = depth - 1
    t = a.mixer(0)
    a.scatter(0)
    side_done = 0
    for l in range(depth):
        t = b.inproj(l, t)
        t = a.experts(l, t)
        t = b.merge(l, b.mlstm(l, t))
        t = y = a.combine(l, t, (None,) if l == last else None)
        b.scatter(l, t)
        if l < last:
            t = a.inproj(l + 1, t)
            t = b.experts(l, t)
            t = a.merge(l + 1, a.mlstm(l + 1, t))
            t = b.combine(l, t)
            a.scatter(l + 1, t)
        else:
            half = (depth + 1) // 2
            for sl in range(half):
                t = side.layer(sl, t)
            t = b.experts(l, t)
            for sl in range(half, depth):
                t = side.layer(sl, t)
            y = b.combine(l, t, (y,))
    return y


def kernel(x_prompt, x_sample, c_prompt, c_sample, state_mlstm_C, state_mlstm_n, state_mlstm_m, state_conv,
           w_ada, b_ada, g_pre_mix, g_post_mix, g_pre_ffn, g_post_ffn, w_in, b_igate, b_fgate, w_conv, b_conv,
           g_head, g_vnorm, b_vnorm, w_spatial, b_spatial, w_out, w_router, b_router, w_e1, w_e3, w_e2,
           w_s1, w_s3, w_s2):
    w = dict(w_in=w_in, b_igate=b_igate, b_fgate=b_fgate, g_pre_mix=g_pre_mix, g_post_mix=g_post_mix,
             g_pre_ffn=g_pre_ffn, g_post_ffn=g_post_ffn, w_conv=w_conv, b_conv=b_conv, g_head=g_head,
             g_vnorm=g_vnorm, b_vnorm=b_vnorm, w_spatial=w_spatial, b_spatial=b_spatial, w_out=w_out,
             w_router=w_router, b_router=b_router, w_e1=w_e1, w_e3=w_e3, w_e2=w_e2, w_s1=w_s1, w_s3=w_s3,
             w_s2=w_s2)
    depth = w_in.shape[0]
    bp, tp, d = x_prompt.shape
    bs, ts, _ = x_sample.shape
    H, dv, dqk = state_mlstm_C.shape[2:]
    layers = [_prep_layer(l, w) for l in range(depth)]

    mod = _ada(jnp.concatenate([c_prompt, c_sample], axis=0), w_ada, b_ada)
    mods_p, mods_s = [], []
    for l in range(depth):
        parts = [mod[l][:, i * d:(i + 1) * d] for i in range(6)]
        mods_p.append([a[:bp].reshape(bp, 1, d) for a in parts])
        mods_s.append([jnp.repeat(a[bp:], ts, axis=0).reshape(1, bs * ts, d) for a in parts])

    lp = 256 if tp % 256 == 0 else CHUNK
    chains = PROMPT_CHAINS if bp % PROMPT_CHAINS == 0 else 1
    bc = bp // chains
    dims = (H, dv, dqk)
    prompt = []
    for ci in range(chains):
        mods_c = [[a[ci * bc:(ci + 1) * bc] for a in layer_mods] for layer_mods in mods_p]
        prompt.append(_Chain(x_prompt, mods_c, None, dims, layers, lp, 0, GMLP_CHUNK, False,
                             seq0=ci * bc, nb=bc))
    sample = _Chain(x_sample, mods_s, (state_mlstm_C, state_mlstm_n, state_mlstm_m, state_conv), dims, layers,
                    ts, PAST_LEN % GMLP_CHUNK, ts, True)
    if chains == 2 and all(c.dispatched for c in prompt) and not sample.dispatched:
        y_p = _run_two_chains(prompt[0], prompt[1], sample, depth)
    else:
        y_p = None
        for c in prompt:
            for l in range(depth):
                shared = (y_p,) if (l == depth - 1 and c.dispatched and chains > 1) else None
                out = c.layer(l, None, shared)
            if c.dispatched and chains > 1:
                y_p = out
            else:
                y_p = out if y_p is None else jnp.concatenate([y_p, out], axis=0)
        for l in range(depth):
            sample.layer(l)
    y_p = y_p.reshape(bp, tp, d)
    y_s = sample.x.reshape(bs, ts, d)
    states_p = [c.states() for c in prompt]
    p_c, p_n, p_m, p_conv = (jnp.concatenate([s[i] for s in states_p], axis=1) for i in range(4))
    s_c, s_n, s_m, s_conv, s_v = sample.states()
    return (y_p, y_s, p_c, p_n, p_m, p_conv, s_c, s_n, s_m, s_conv, s_v)
```
